```python
import jax, jax.numpy as jnp
from jax import lax
import numpy as np

D_MODEL = 1024
BATCH = 4
SEQ = 4096
DEPTH = 2

GRID_W = 64
PLE_DIM = 256
HEAD_DIM = 64
ATTN_HEADS = 8
ATTN_KV_HEADS = 2
GLA_HEADS = 4
MLSTM_HEADS = 4
ATTN_WIDTH = ATTN_HEADS * HEAD_DIM
KV_WIDTH = ATTN_KV_HEADS * HEAD_DIM
GLA_WIDTH = GLA_HEADS * HEAD_DIM
MLSTM_WIDTH = MLSTM_HEADS * HEAD_DIM
MIX_WIDTH = ATTN_WIDTH + GLA_WIDTH + MLSTM_WIDTH
GLA_RANK = 16
GLA_TAU = 16.0
CHUNK = 64
Q_BLOCK = 128
ROPE_THETA = 10000.0
ROPE_AXIS_DIM = HEAD_DIM // 2
CONV_W = 3
N_GROUPS = 4
EXPERTS_PER_GROUP = 4
N_EXPERTS = N_GROUPS * EXPERTS_PER_GROUP
TOP_K = 2
D_EXPERT = 512
EPS = 1e-6
NEG = -1e30
IN_SIZES = (ATTN_WIDTH, KV_WIDTH, KV_WIDTH,
            GLA_WIDTH, GLA_WIDTH, GLA_WIDTH, GLA_WIDTH, 2 * GLA_RANK,
            MLSTM_WIDTH, MLSTM_WIDTH, MLSTM_WIDTH, MLSTM_WIDTH, 4 * MLSTM_HEADS)
IN_WIDTH = sum(IN_SIZES)

kernel_name = "hybrid_gqa_gla_mlstm_hiermoe_encoder"

F32 = jnp.float32


def rms_norm(x, g):
    xf = x.astype(F32)
    y = xf * lax.rsqrt(jnp.mean(xf * xf, axis=-1, keepdims=True) + EPS)
    return (y * g.astype(F32)).astype(x.dtype)


def split_cols(z, sizes):
    out, off = [], 0
    for s in sizes:
        out.append(z[..., off:off + s])
        off += s
    return out


def axial_rope_tables(T):
    rows = T // GRID_W
    row = jnp.repeat(jnp.arange(rows, dtype=F32), GRID_W)
    col = jnp.tile(jnp.arange(GRID_W, dtype=F32), rows)
    inv = ROPE_THETA ** (-jnp.arange(0, ROPE_AXIS_DIM, 2, dtype=F32) / ROPE_AXIS_DIM)
    ang_r = row[:, None] * inv[None, :]
    ang_c = col[:, None] * inv[None, :]
    return (jnp.cos(ang_r), jnp.sin(ang_r), jnp.cos(ang_c), jnp.sin(ang_c))


def rotate(xa, cos, sin):
    half = ROPE_AXIS_DIM // 2
    x1, x2 = xa[..., :half], xa[..., half:]
    c = cos[:, None, :]
    s = sin[:, None, :]
    return jnp.concatenate([x1 * c - x2 * s, x2 * c + x1 * s], axis=-1)


def apply_axial_rope(x, tables):
    cr, sr, cc, sc = tables
    xf = x.astype(F32)
    out = jnp.concatenate([rotate(xf[..., :ROPE_AXIS_DIM], cr, sr),
                           rotate(xf[..., ROPE_AXIS_DIM:], cc, sc)], axis=-1)
    return out.astype(x.dtype)


def grouped_query_attention(q, k, v):
    B, T = q.shape[0], q.shape[1]
    nb = T // Q_BLOCK
    G = ATTN_HEADS // ATTN_KV_HEADS
    qb = q.reshape(B, nb, Q_BLOCK, ATTN_KV_HEADS, G, HEAD_DIM).transpose(1, 0, 3, 4, 2, 5)
    kt = k.transpose(0, 2, 1, 3)
    vt = v.transpose(0, 2, 1, 3)
    scale = HEAD_DIM ** -0.5

    def block(qblk):
        s = jnp.einsum('bkgqd,bksd->bkgqs', qblk, kt, preferred_element_type=F32) * scale
        pr = jax.nn.softmax(s, axis=-1).astype(vt.dtype)
        return jnp.einsum('bkgqs,bksd->bkgqd', pr, vt)

    o = lax.map(block, qb)
    return o.transpose(1, 0, 4, 2, 3, 5).reshape(B, T, ATTN_WIDTH)


def gla_causal(q, k, v, la):
    B, H, T, dk = q.shape
    n = T // CHUNK
    q, k, v, la = [a.reshape(B, H, n, CHUNK, a.shape[-1]) for a in (q, k, v, la)]
    b = jnp.cumsum(la, axis=-2)
    b_mid = b[..., CHUNK // 2 - 1:CHUNK // 2, :]
    b_last = b[..., -1:, :]
    mask = jnp.tril(jnp.ones((CHUNK, CHUNK), dtype=bool))
    a = jnp.einsum('bhncd,bhnsd->bhncs', q * jnp.exp(b - b_mid), k * jnp.exp(b_mid - b))
    a = jnp.where(mask, a, 0.0)
    o_intra = jnp.einsum('bhncs,bhnse->bhnce', a, v)
    kv = jnp.einsum('bhncd,bhnce->bhnde', k * jnp.exp(b_last - b), v)
    decay = jnp.exp(b_last[..., 0, :])

    def step(S, inp):
        dec, kvc = inp
        return dec[..., None] * S + kvc, S

    S0 = jnp.zeros((B, H, dk, v.shape[-1]), F32)
    _, S_prev = lax.scan(step, S0, (jnp.moveaxis(decay, 2, 0), jnp.moveaxis(kv, 2, 0)))
    S_prev = jnp.moveaxis(S_prev, 0, 2)
    o_inter = jnp.einsum('bhncd,bhnde->bhnce', q * jnp.exp(b), S_prev)
    return (o_intra + o_inter).reshape(B, H, T, -1)


def mlstm_causal(q, k, v, logi, logf):
    B, H, T, d = q.shape
    n = T // CHUNK
    q, k, v = [a.reshape(B, H, n, CHUNK, d) for a in (q, k, v)]
    logi = logi.reshape(B, H, n, CHUNK)
    b = jnp.cumsum(logf.reshape(B, H, n, CHUNK), axis=-1)
    b_last = b[..., -1]
    g = b_last[..., None] - b + logi

    def step(carry, inp):
        C_s, n_s, m_s = carry
        a_c, g_c, k_c, v_c = inp
        m_new = jnp.maximum(a_c + m_s, jnp.max(g_c, axis=-1))
        w_prev = jnp.exp(a_c + m_s - m_new)
        w_k = jnp.exp(g_c - m_new[..., None])
        C_new = w_prev[..., None, None] * C_s + jnp.einsum('bhc,bhcd,bhce->bhde', w_k, k_c, v_c)
        n_new = w_prev[..., None] * n_s + jnp.einsum('bhc,bhcd->bhd', w_k, k_c)
        return (C_new, n_new, m_new), (C_s, n_s, m_s)

    init = (jnp.zeros((B, H, d, d), F32), jnp.zeros((B, H, d), F32), jnp.full((B, H), NEG, F32))
    xs = (jnp.moveaxis(b_last, 2, 0), jnp.moveaxis(g, 2, 0),
          jnp.moveaxis(k, 2, 0), jnp.moveaxis(v, 2, 0))
    _, (Cp, Np, Mp) = lax.scan(step, init, xs)
    Cp = jnp.moveaxis(Cp, 0, 2)
    Np = jnp.moveaxis(Np, 0, 2)
    Mp = jnp.moveaxis(Mp, 0, 2)
    mask = jnp.tril(jnp.ones((CHUNK, CHUNK), dtype=bool))
    inter = b + Mp[..., None]
    logD = b[..., :, None] - b[..., None, :] + logi[..., None, :]
    logD = jnp.where(mask, logD, NEG)
    m = jnp.maximum(inter, jnp.max(logD, axis=-1))
    w_inter = jnp.exp(inter - m)
    s = jnp.einsum('bhncd,bhnsd->bhncs', q, k) * jnp.exp(logD - m[..., None])
    num = w_inter[..., None] * jnp.einsum('bhncd,bhnde->bhnce', q, Cp) \
        + jnp.einsum('bhncs,bhnse->bhnce', s, v)
    den = w_inter * jnp.einsum('bhncd,bhnd->bhnc', q, Np) + jnp.sum(s, axis=-1)
    h = num / jnp.maximum(jnp.abs(den), jnp.exp(-m))[..., None]
    return h.reshape(B, H, T, d)


def to_heads(a, n_heads):
    B, T = a.shape[0], a.shape[1]
    return a.astype(F32).reshape(B, T, n_heads, HEAD_DIM).transpose(0, 2, 1, 3)


def flip_t(a):
    return jnp.flip(a, axis=2)


def gla_branch(gq, gk, gv, gg, glr, w_dec, b_dec, norm_g):
    B, T = gq.shape[0], gq.shape[1]
    q = to_heads(gq, GLA_HEADS) * HEAD_DIM ** -0.5
    k = to_heads(gk, GLA_HEADS)
    v = to_heads(gv, GLA_HEADS)
    lr = glr.astype(F32)
    wd = w_dec.astype(F32)
    bd = b_dec.astype(F32)
    la_f = to_heads(jax.nn.log_sigmoid(lr[..., :GLA_RANK] @ wd[0] + bd[0]) / GLA_TAU, GLA_HEADS)
    la_b = to_heads(jax.nn.log_sigmoid(lr[..., GLA_RANK:] @ wd[1] + bd[1]) / GLA_TAU, GLA_HEADS)
    o = gla_causal(q, k, v, la_f) \
        + flip_t(gla_causal(flip_t(q), flip_t(k), flip_t(v), flip_t(la_b)))
    o = rms_norm(o.transpose(0, 2, 1, 3), norm_g).reshape(B, T, GLA_WIDTH)
    return (o * jax.nn.silu(gg.astype(F32))).astype(gq.dtype)


def centred_conv(x, w, b):
    T = x.shape[1]
    pad = CONV_W // 2
    xp = jnp.pad(x, ((0, 0), (pad, pad), (0, 0)))
    return sum(xp[:, j:j + T] * w[j] for j in range(CONV_W)) + b


def mlstm_branch(mq, mk, mv, mo, mg, conv_w, conv_b, b_in, b_fg, norm_g):
    B, T = mq.shape[0], mq.shape[1]
    qk = jax.nn.silu(centred_conv(jnp.concatenate([mq, mk], axis=-1), conv_w, conv_b))
    q = to_heads(qk[..., :MLSTM_WIDTH], MLSTM_HEADS)
    k = to_heads(qk[..., MLSTM_WIDTH:], MLSTM_HEADS) * HEAD_DIM ** -0.5
    v = to_heads(mv, MLSTM_HEADS)
    gates = mg.astype(F32).transpose(0, 2, 1)
    Hm = MLSTM_HEADS
    bi = b_in.astype(F32)
    bf = b_fg.astype(F32)
    logi_f = gates[:, 0:Hm] + bi[0][None, :, None]
    logf_f = jax.nn.log_sigmoid(gates[:, Hm:2 * Hm] + bf[0][None, :, None])
    logi_b = gates[:, 2 * Hm:3 * Hm] + bi[1][None, :, None]
    logf_b = jax.nn.log_sigmoid(gates[:, 3 * Hm:4 * Hm] + bf[1][None, :, None])
    h = mlstm_causal(q, k, v, logi_f, logf_f) \
        + flip_t(mlstm_causal(flip_t(q), flip_t(k), flip_t(v), flip_t(logi_b), flip_t(logf_b)))
    h = rms_norm(h.transpose(0, 2, 1, 3), norm_g).reshape(B, T, MLSTM_WIDTH)
    return (h * jax.nn.sigmoid(mo.astype(F32))).astype(mq.dtype)


def hier_moe(h, w_group, b_group, w_router, b_router, w_gate, w_up, w_down):
    B, T, D = h.shape
    t = h.reshape(-1, D)
    gl = (t @ w_group + b_group).astype(F32)
    gi = jnp.argmax(gl, axis=-1)
    g_prob = jnp.take_along_axis(jax.nn.softmax(gl, axis=-1), gi[:, None], axis=-1)
    el = (t @ w_router + b_router).astype(F32).reshape(-1, N_GROUPS, EXPERTS_PER_GROUP)
    el_sel = jnp.take_along_axis(el, gi[:, None, None], axis=1)[:, 0]
    top_v, top_i = lax.top_k(el_sel, TOP_K)
    top_w = jax.nn.softmax(top_v, axis=-1) * g_prob
    eid = gi[:, None] * EXPERTS_PER_GROUP + top_i
    comb = jnp.sum(jax.nn.one_hot(eid, N_EXPERTS, dtype=F32) * top_w[..., None], axis=1)
    y = jnp.zeros((t.shape[0], D), F32)
    for e in range(N_EXPERTS):
        a = jax.nn.silu(t @ w_gate[e]) * (t @ w_up[e])
        y = y + comb[:, e:e + 1] * (a @ w_down[e]).astype(F32)
    return y.astype(h.dtype).reshape(B, T, D)


def setup_inputs(seed: int = 0) -> dict:
    key = jax.random.key(seed)
    ks = jax.random.split(key, 32)
    L = DEPTH

    def nrm(k, shape, scale):
        return jax.random.normal(k, shape, F32) * scale

    def gain(k, shape):
        return 1.0 + 0.02 * jax.random.normal(k, shape, F32)

    return {
        "x": nrm(ks[0], (BATCH, SEQ, D_MODEL), 1.0),
        "p": nrm(ks[1], (DEPTH, BATCH, SEQ, PLE_DIM), 1.0),
        "norm_mix_g": gain(ks[2], (L, D_MODEL)),
        "w_in": nrm(ks[3], (L, D_MODEL, IN_WIDTH), D_MODEL ** -0.5),
        "attn_q_norm_g": gain(ks[4], (L, HEAD_DIM)),
        "attn_k_norm_g": gain(ks[5], (L, HEAD_DIM)),
        "gla_w_decay": nrm(ks[6], (L, 2, GLA_RANK, GLA_WIDTH), GLA_RANK ** -0.5),
        "gla_b_decay": nrm(ks[7], (L, 2, GLA_WIDTH), 0.1),
        "gla_out_norm_g": gain(ks[8], (L, HEAD_DIM)),
        "mlstm_conv_w": nrm(ks[9], (L, CONV_W, 2 * MLSTM_WIDTH), CONV_W ** -0.5),
        "mlstm_conv_b": nrm(ks[10], (L, 2 * MLSTM_WIDTH), 0.02),
        "mlstm_b_input": nrm(ks[11], (L, 2, MLSTM_HEADS), 0.1),
        "mlstm_b_forget": jnp.linspace(3.0, 6.0, MLSTM_HEADS, dtype=F32)
                          + nrm(ks[12], (L, 2, MLSTM_HEADS), 0.1),
        "mlstm_out_norm_g": gain(ks[13], (L, HEAD_DIM)),
        "w_out": nrm(ks[14], (L, MIX_WIDTH, D_MODEL), MIX_WIDTH ** -0.5),
        "norm_ffn_g": gain(ks[15], (L, D_MODEL)),
        "w_group": nrm(ks[16], (L, D_MODEL, N_GROUPS), D_MODEL ** -0.5),
        "b_group": nrm(ks[17], (L, N_GROUPS), 0.01),
        "w_router": nrm(ks[18], (L, D_MODEL, N_EXPERTS), D_MODEL ** -0.5),
        "b_router": nrm(ks[19], (L, N_EXPERTS), 0.01),
        "w_expert_gate": nrm(ks[20], (L, N_EXPERTS, D_MODEL, D_EXPERT), D_MODEL ** -0.5),
        "w_expert_up": nrm(ks[21], (L, N_EXPERTS, D_MODEL, D_EXPERT), D_MODEL ** -0.5),
        "w_expert_down": nrm(ks[22], (L, N_EXPERTS, D_EXPERT, D_MODEL), D_EXPERT ** -0.5),
        "norm_ple_g": gain(ks[23], (L, D_MODEL)),
        "w_ple_gate": nrm(ks[24], (L, D_MODEL, D_MODEL), D_MODEL ** -0.5),
        "w_ple_proj": nrm(ks[25], (L, PLE_DIM, D_MODEL), PLE_DIM ** -0.5),
        "final_norm_g": gain(ks[26], (D_MODEL,)),
    }


def reference(x, p, norm_mix_g, w_in, attn_q_norm_g, attn_k_norm_g, gla_w_decay, gla_b_decay,
              gla_out_norm_g, mlstm_conv_w, mlstm_conv_b, mlstm_b_input, mlstm_b_forget,
              mlstm_out_norm_g, w_out, norm_ffn_g, w_group, b_group, w_router, b_router,
              w_expert_gate, w_expert_up, w_expert_down, norm_ple_g, w_ple_gate, w_ple_proj,
              final_norm_g):
    B, T, _ = x.shape
    rope = axial_rope_tables(T)
    for i in range(DEPTH):
        h = rms_norm(x, norm_mix_g[i])
        z = h @ w_in[i]
        (aq, ak, av, gq, gk, gv, gg, glr, mq, mk, mv, mo, mg) = split_cols(z, IN_SIZES)
        aq = rms_norm(aq.reshape(B, T, ATTN_HEADS, HEAD_DIM), attn_q_norm_g[i])
        ak = rms_norm(ak.reshape(B, T, ATTN_KV_HEADS, HEAD_DIM), attn_k_norm_g[i])
        aq = apply_axial_rope(aq, rope)
        ak = apply_axial_rope(ak, rope)
        av = av.reshape(B, T, ATTN_KV_HEADS, HEAD_DIM)
        attn_out = grouped_query_attention(aq, ak, av)
        gla_out = gla_branch(gq, gk, gv, gg, glr, gla_w_decay[i], gla_b_decay[i],
                             gla_out_norm_g[i])
        ml_out = mlstm_branch(mq, mk, mv, mo, mg, mlstm_conv_w[i], mlstm_conv_b[i],
                              mlstm_b_input[i], mlstm_b_forget[i],
                              mlstm_out_norm_g[i])
        mix = jnp.concatenate([attn_out.astype(x.dtype), gla_out, ml_out], axis=-1)
        x = x + mix @ w_out[i]
        x = x + hier_moe(rms_norm(x, norm_ffn_g[i]), w_group[i], b_group[i], w_router[i],
                         b_router[i], w_expert_gate[i], w_expert_up[i], w_expert_down[i])
        gate = jax.nn.sigmoid(rms_norm(x, norm_ple_g[i]) @ w_ple_gate[i])
        x = x + gate * (p[i] @ w_ple_proj[i])
    return rms_norm(x, final_norm_g)
```

```python
import functools

import jax
import jax.numpy as jnp
import numpy as np
from jax import lax
from jax.experimental import pallas as pl
from jax.experimental.pallas import tpu as pltpu

F32 = jnp.float32
BF16 = jnp.bfloat16

GRID_W = 64
HEAD_DIM = 64
ATTN_HEADS = 8
ATTN_KV_HEADS = 2
GLA_HEADS = 4
MLSTM_HEADS = 4
ATTN_WIDTH = ATTN_HEADS * HEAD_DIM
KV_WIDTH = ATTN_KV_HEADS * HEAD_DIM
GLA_WIDTH = GLA_HEADS * HEAD_DIM
MLSTM_WIDTH = MLSTM_HEADS * HEAD_DIM
GLA_RANK = 16
GLA_TAU = 16.0
CHUNK = 64
ROPE_THETA = 10000.0
ROPE_AXIS_DIM = HEAD_DIM // 2
N_GROUPS = 4
EXPERTS_PER_GROUP = 4
N_EXPERTS = N_GROUPS * EXPERTS_PER_GROUP
D_EXPERT = 512
EPS = 1e-6
NEG = -1e30

LANES = 128
SUBLANES = 8
VMEM_LIMIT_BYTES = 56 * 1024 * 1024

QK_WIDTH = ATTN_WIDTH + KV_WIDTH
OFF_V = QK_WIDTH
OFF_GLA = OFF_V + KV_WIDTH
OFF_ML = OFF_GLA + 4 * GLA_WIDTH
OFF_SMALL = OFF_ML + 4 * MLSTM_WIDTH
IN_PERM_WIDTH = OFF_SMALL + LANES
SMALL_GATE_LANE = 2 * GLA_RANK

ROUTE_BLOCK = 1024
ROUTE_TILE = 128
ROUTE_ROWS = ROUTE_BLOCK + N_GROUPS * ROUTE_TILE
TILES_PER_BLOCK = ROUTE_ROWS // ROUTE_TILE
ROUTE_W_LANE = 4
ROUTE_LO_SHIFT = 32


def _dot(a, b):
    return jnp.dot(a, b, preferred_element_type=F32)


def _dot_nt(a, b):
    return lax.dot_general(a, b, (((1,), (1,)), ((), ())), preferred_element_type=F32)


def _dot_tn(a, b):
    return lax.dot_general(a, b, (((0,), (0,)), ((), ())), preferred_element_type=F32)


def _split(a):
    hi = a.astype(BF16)
    lo = (a - hi.astype(F32)).astype(BF16)
    return hi, lo


def _dot3(a, w_hi, w_lo):
    a_hi, a_lo = _split(a)
    return _dot(a_hi, w_hi) + _dot(a_lo, w_hi) + _dot(a_hi, w_lo)


def _log_sigmoid(x):
    return jnp.minimum(x, 0.0) - jnp.log1p(jnp.exp(-jnp.abs(x)))


def _sigmoid(x):
    return 1.0 / (1.0 + jnp.exp(-x))


def _rms(x, g):
    return x * lax.rsqrt(jnp.mean(x * x, axis=-1, keepdims=True) + EPS) * g


def _params(*semantics):
    return pltpu.CompilerParams(dimension_semantics=semantics, vmem_limit_bytes=VMEM_LIMIT_BYTES)


def _in_proj_kernel(x_ref, g_ref, w_ref, cos_ref, sin_ref, gain_ref, eh_ref, eht_ref,
                    qt_ref, k_ref, vt_ref, gla_ref, ml_ref, small_ref):
    tm = x_ref.shape[0]
    h = _rms(x_ref[...], g_ref[...])
    z = _dot(h.astype(BF16), w_ref[...])
    qk = z[:, :QK_WIDTH]
    sq_hi, sq_lo = _split(qk * qk)
    ssq = _dot(sq_hi, eh_ref[...]) + _dot(sq_lo, eh_ref[...])
    inv_hi, inv_lo = _split(lax.rsqrt(ssq * (1.0 / HEAD_DIM) + EPS))
    scale = _dot(inv_hi, eht_ref[...]) + _dot(inv_lo, eht_ref[...])
    y = qk * scale * gain_ref[...]
    lane = lax.broadcasted_iota(jnp.int32, (tm, LANES), 1)
    first_half = (lane % ROPE_AXIS_DIM) < (ROPE_AXIS_DIM // 2)
    cos = cos_ref[...]
    sin = sin_ref[...]
    for c in range(QK_WIDTH // LANES):
        yc = y[:, c * LANES:(c + 1) * LANES]
        partner = jnp.where(first_half,
                            pltpu.roll(yc, LANES - ROPE_AXIS_DIM // 2, 1),
                            pltpu.roll(yc, ROPE_AXIS_DIM // 2, 1))
        rc = yc * cos + partner * sin
        if c < ATTN_WIDTH // LANES:
            qt_ref[0, c * LANES:(c + 1) * LANES, :] = rc.T.astype(BF16)
        else:
            k_ref[0] = rc.astype(BF16)
    vt_ref[0] = z[:, OFF_V:OFF_V + KV_WIDTH].T.astype(BF16)
    gla_ref[0] = z[:, OFF_GLA:OFF_ML]
    ml_ref[0] = z[:, OFF_ML:OFF_SMALL]
    small_ref[0] = z[:, OFF_SMALL:IN_PERM_WIDTH]


def _in_proj(x2, g, w_perm, cos, sin, gain, eh, eht, B, T, tm=256):
    N, D = x2.shape
    tpb = T // tm
    const = lambda i: (0, 0)
    tok3 = lambda i: (i // tpb, i % tpb, 0)
    tokT = lambda i: (i // tpb, 0, i % tpb)
    return pl.pallas_call(
        _in_proj_kernel,
        grid=(N // tm,),
        in_specs=[
            pl.BlockSpec((tm, D), lambda i: (i, 0)),
            pl.BlockSpec((1, D), const),
            pl.BlockSpec((D, IN_PERM_WIDTH), const),
            pl.BlockSpec((tm, LANES), lambda i: (i % tpb, 0)),
            pl.BlockSpec((tm, LANES), lambda i: (i % tpb, 0)),
            pl.BlockSpec((1, QK_WIDTH), const),
            pl.BlockSpec((QK_WIDTH, LANES), const),
            pl.BlockSpec((LANES, QK_WIDTH), const),
        ],
        out_specs=[
            pl.BlockSpec((1, ATTN_WIDTH, tm), tokT),
            pl.BlockSpec((1, tm, KV_WIDTH), tok3),
            pl.BlockSpec((1, KV_WIDTH, tm), tokT),
            pl.BlockSpec((1, tm, 4 * GLA_WIDTH), tok3),
            pl.BlockSpec((1, tm, 4 * MLSTM_WIDTH), tok3),
            pl.BlockSpec((1, tm, LANES), tok3),
        ],
        out_shape=[
            jax.ShapeDtypeStruct((B, ATTN_WIDTH, T), BF16),
            jax.ShapeDtypeStruct((B, T, KV_WIDTH), BF16),
            jax.ShapeDtypeStruct((B, KV_WIDTH, T), BF16),
            jax.ShapeDtypeStruct((B, T, 4 * GLA_WIDTH), F32),
            jax.ShapeDtypeStruct((B, T, 4 * MLSTM_WIDTH), F32),
            jax.ShapeDtypeStruct((B, T, LANES), F32),
        ],
        compiler_params=_params("arbitrary"),
        name="in_proj",
    )(x2, g, w_perm, cos, sin, gain, eh, eht)


def _attn_kernel(qt_ref, k_ref, vt_ref, o_ref, *, tk):
    tq = qt_ref.shape[2]
    T = k_ref.shape[1]
    G = ATTN_HEADS // ATTN_KV_HEADS
    zeros = jnp.zeros((HEAD_DIM, G * tq), BF16)
    for j in range(ATTN_KV_HEADS):
        base = j * G * HEAD_DIM
        qs = jnp.concatenate(
            [qt_ref[0, base + h * HEAD_DIM:base + (h + 1) * HEAD_DIM, :] for h in range(G)], axis=1)
        qp = jnp.concatenate([qs, zeros] if j == 0 else [zeros, qs], axis=0)

        def body(c, carry, qp=qp, j=j):
            m, l, acc = carry
            off = pl.multiple_of(c * tk, tk)
            st = _dot(k_ref[0, pl.ds(off, tk), :], qp)
            m_new = jnp.maximum(m, jnp.max(st, axis=0, keepdims=True))
            alpha = jnp.exp(m - m_new)
            p = jnp.exp(st - m_new)
            l = alpha * l + jnp.sum(p, axis=0, keepdims=True)
            vc = vt_ref[0, j * HEAD_DIM:(j + 1) * HEAD_DIM, pl.ds(off, tk)]
            acc = alpha * acc + _dot(vc, p.astype(BF16))
            return m_new, l, acc

        init = (jnp.full((1, G * tq), NEG, F32), jnp.zeros((1, G * tq), F32),
                jnp.zeros((HEAD_DIM, G * tq), F32))
        m, l, acc = lax.fori_loop(0, T // tk, body, init)
        o = acc * (1.0 / l)
        ot = jnp.concatenate([o[:, h * tq:(h + 1) * tq] for h in range(G)], axis=0)
        o_ref[0, :, base:base + G * HEAD_DIM] = ot.T.astype(BF16)


def _attention(qt, k, vt, tq=128, tk=512):
    B, _, T = qt.shape
    tk = min(tk, T)
    return pl.pallas_call(
        functools.partial(_attn_kernel, tk=tk),
        grid=(B, T // tq),
        in_specs=[
            pl.BlockSpec((1, ATTN_WIDTH, tq), lambda b, i: (b, 0, i)),
            pl.BlockSpec((1, T, KV_WIDTH), lambda b, i: (b, 0, 0)),
            pl.BlockSpec((1, KV_WIDTH, T), lambda b, i: (b, 0, 0)),
        ],
        out_specs=pl.BlockSpec((1, tq, ATTN_WIDTH), lambda b, i: (b, i, 0)),
        out_shape=jax.ShapeDtypeStruct((B, T, ATTN_WIDTH), BF16),
        compiler_params=_params("arbitrary", "arbitrary"),
        name="attention",
    )(qt, k, vt)


def _chunk_scan(x, pos, op, fill, reverse):
    rows = x.shape[0]
    s = 1
    while s < CHUNK:
        if reverse:
            shifted = jnp.where(pos < CHUNK - s, pltpu.roll(x, rows - s, 0), fill)
        else:
            shifted = jnp.where(pos >= s, pltpu.roll(x, s, 0), fill)
        x = op(x, shifted)
        s *= 2
    return x


def _chunk_pos(rows):
    return lax.broadcasted_iota(jnp.int32, (rows, LANES), 0) % CHUNK


def _chunk_row(a, reverse_dir, idx_fwd, idx_bwd):
    rows = a.shape[0]
    a3 = a.reshape(rows // CHUNK, CHUNK, LANES)
    i = idx_bwd if reverse_dir else idx_fwd
    return jnp.broadcast_to(a3[:, i:i + 1, :], a3.shape).reshape(rows, LANES)


def _head_rms(o, gain):
    lane = lax.broadcasted_iota(jnp.int32, o.shape, 1)
    lo = lane < HEAD_DIM
    sq = o * o
    s_lo = jnp.sum(jnp.where(lo, sq, 0.0), axis=-1, keepdims=True)
    s_hi = jnp.sum(jnp.where(lo, 0.0, sq), axis=-1, keepdims=True)
    ms = jnp.where(lo, s_lo, s_hi) * (1.0 / HEAD_DIM)
    return o * lax.rsqrt(ms + EPS) * gain


PREP_ROWS = 512


def _gla_kernel(q_ref, k_ref, v_ref, g_ref, small_ref, wdh_ref, wdl_ref, bd_ref, gain_ref,
                o_ref, qe_s, ke_s, kd_s, qb_s, dec_s, of_s, ob_s):
    T = q_ref.shape[1]
    nc = T // CHUNK
    R = min(PREP_ROWS, T)
    pos = _chunk_pos(R)

    def prep(t, _):
        r0 = pl.multiple_of(t * R, R)
        sm = small_ref[0, pl.ds(r0, R), :]
        q = q_ref[0, pl.ds(r0, R), :] * HEAD_DIM ** -0.5
        k = k_ref[0, pl.ds(r0, R), :]
        for d in range(2):
            la = _log_sigmoid(_dot3(sm, wdh_ref[d], wdl_ref[d]) + bd_ref[d]) * (1.0 / GLA_TAU)
            b = _chunk_scan(la, pos, jnp.add, 0.0, reverse=bool(d))
            b_mid = _chunk_row(b, d, CHUNK // 2 - 1, CHUNK // 2)
            b_last = _chunk_row(b, d, CHUNK - 1, 0)
            qe_s[d, pl.ds(r0, R), :] = (q * jnp.exp(b - b_mid)).astype(BF16)
            ke_s[d, pl.ds(r0, R), :] = (k * jnp.exp(b_mid - b)).astype(BF16)
            kd_s[d, pl.ds(r0, R), :] = (k * jnp.exp(b_last - b)).astype(BF16)
            qb_s[d, pl.ds(r0, R), :] = (q * jnp.exp(b)).astype(BF16)
            dec = jnp.exp(b_last).reshape(R // CHUNK, CHUNK, LANES)[:, :SUBLANES, :]
            dec_s[d, pl.ds(pl.multiple_of(t * (R // CHUNK) * SUBLANES, SUBLANES), (R // CHUNK) * SUBLANES), :] = (
                dec.reshape((R // CHUNK) * SUBLANES, LANES))
        return 0

    lax.fori_loop(0, T // R, prep, 0)

    ci = lax.broadcasted_iota(jnp.int32, (CHUNK, CHUNK), 0)
    si = lax.broadcasted_iota(jnp.int32, (CHUNK, CHUNK), 1)
    masks = (ci >= si, ci <= si)
    outs = (of_s, ob_s)

    def step(n, states):
        new_states = []
        for d in range(2):
            c = n if d == 0 else nc - 1 - n
            r0 = pl.multiple_of(c * CHUNK, CHUNK)
            qe = qe_s[d, pl.ds(r0, CHUNK), :]
            ke = ke_s[d, pl.ds(r0, CHUNK), :]
            kd = kd_s[d, pl.ds(r0, CHUNK), :]
            qb = qb_s[d, pl.ds(r0, CHUNK), :]
            v = v_ref[0, pl.ds(r0, CHUNK), :].astype(BF16)
            dec = dec_s[d, pl.ds(pl.multiple_of(c * SUBLANES, SUBLANES), SUBLANES), :][0:1, :]
            st = states[d]
            o_heads, kv_heads = [], []
            for hh in range(2):
                sl = slice(hh * HEAD_DIM, (hh + 1) * HEAD_DIM)
                a = jnp.where(masks[d], _dot_nt(qe[:, sl], ke[:, sl]), 0.0).astype(BF16)
                o_heads.append(_dot(a, v[:, sl]) + _dot_nt(qb[:, sl], st[:, sl].astype(BF16)))
                kv_heads.append(_dot_tn(v[:, sl], kd[:, sl]))
            outs[d][pl.ds(r0, CHUNK), :] = jnp.concatenate(o_heads, axis=1)
            new_states.append(st * dec + jnp.concatenate(kv_heads, axis=1))
        return tuple(new_states)

    zero = jnp.zeros((HEAD_DIM, LANES), F32)
    lax.fori_loop(0, nc, step, (zero, zero))

    def post(t, _):
        r0 = pl.multiple_of(t * R, R)
        o = _head_rms(of_s[pl.ds(r0, R), :] + ob_s[pl.ds(r0, R), :], gain_ref[...])
        g = g_ref[0, pl.ds(r0, R), :]
        o_ref[0, pl.ds(r0, R), :] = (o * (g * _sigmoid(g))).astype(BF16)
        return 0

    lax.fori_loop(0, T // R, post, 0)


def _gla(gla, small, lp):
    B, T, _ = gla.shape
    pairs = GLA_WIDTH // LANES
    col = lambda off: (lambda b, hp: (b, 0, off * pairs + hp))
    return pl.pallas_call(
        _gla_kernel,
        grid=(B, pairs),
        in_specs=[
            pl.BlockSpec((1, T, LANES), col(0)),
            pl.BlockSpec((1, T, LANES), col(1)),
            pl.BlockSpec((1, T, LANES), col(2)),
            pl.BlockSpec((1, T, LANES), col(3)),
            pl.BlockSpec((1, T, LANES), lambda b, hp: (b, 0, 0)),
            pl.BlockSpec((2, LANES, LANES), lambda b, hp: (0, 0, hp)),
            pl.BlockSpec((2, LANES, LANES), lambda b, hp: (0, 0, hp)),
            pl.BlockSpec((2, 1, LANES), lambda b, hp: (0, 0, hp)),
            pl.BlockSpec((1, LANES), lambda b, hp: (0, 0)),
        ],
        out_specs=pl.BlockSpec((1, T, LANES), lambda b, hp: (b, 0, hp)),
        out_shape=jax.ShapeDtypeStruct((B, T, GLA_WIDTH), BF16),
        scratch_shapes=[
            pltpu.VMEM((2, T, LANES), BF16), pltpu.VMEM((2, T, LANES), BF16),
            pltpu.VMEM((2, T, LANES), BF16), pltpu.VMEM((2, T, LANES), BF16),
            pltpu.VMEM((2, (T // CHUNK) * SUBLANES, LANES), F32),
            pltpu.VMEM((T, LANES), F32), pltpu.VMEM((T, LANES), F32),
        ],
        compiler_params=_params("arbitrary", "arbitrary"),
        name="gla",
    )(gla, gla, gla, gla, small, lp["wd_hi"], lp["wd_lo"], lp["bd"], lp["gla_gain"])


GATE_I = SMALL_GATE_LANE
GATE_F = SMALL_GATE_LANE + MLSTM_HEADS


def _gate_lane(d, hh):
    return GATE_F + 2 * MLSTM_HEADS * d + hh


def _mlstm_kernel(q_ref, k_ref, v_ref, og_ref, small_ref, wq_ref, wk_ref, bq_ref, bk_ref,
                  gbias_ref, gain_ref, o_ref, q_s, k_s, b_s, r_s, cm_s, rt_s, of_s, ob_s):
    T = q_ref.shape[1]
    nc = T // CHUNK
    R = min(PREP_ROWS, T)
    hp = pl.program_id(1)

    row = lax.broadcasted_iota(jnp.int32, (T, LANES), 0)
    for src, w_ref, b_ref, dst, scale in ((q_ref, wq_ref, bq_ref, q_s, 1.0),
                                          (k_ref, wk_ref, bk_ref, k_s, HEAD_DIM ** -0.5)):
        xc = src[0]
        prev = jnp.where(row >= 1, pltpu.roll(xc, 1, 0), 0.0)
        nxt = jnp.where(row < T - 1, pltpu.roll(xc, T - 1, 0), 0.0)
        y = prev * w_ref[0:1, :] + xc * w_ref[1:2, :] + nxt * w_ref[2:3, :] + b_ref[...]
        dst[...] = (y * _sigmoid(y) * scale).astype(BF16)

    pos = _chunk_pos(R)
    lane = lax.broadcasted_iota(jnp.int32, (R, LANES), 1)
    is_bwd = lane >= GATE_I + 2 * MLSTM_HEADS
    heads_per_pair = LANES // HEAD_DIM
    shift = (LANES - heads_per_pair * hp) % LANES

    def prep(t, _):
        r0 = pl.multiple_of(t * R, R)
        gc = pltpu.roll(small_ref[0, pl.ds(r0, R), :] + gbias_ref[...], shift, 1)
        logf = _log_sigmoid(gc)
        b = jnp.where(is_bwd, _chunk_scan(logf, pos, jnp.add, 0.0, True),
                      _chunk_scan(logf, pos, jnp.add, 0.0, False))
        r = pltpu.roll(gc, MLSTM_HEADS, 1) - b
        cm = jnp.where(is_bwd, _chunk_scan(r, pos, jnp.maximum, NEG, True),
                       _chunk_scan(r, pos, jnp.maximum, NEG, False))
        b_s[pl.ds(r0, R), :] = b
        r_s[pl.ds(r0, R), :] = r
        cm_s[pl.ds(r0, R), :] = cm
        rt_s[:, pl.ds(r0, R)] = r.T
        return 0

    lax.fori_loop(0, T // R, prep, 0)

    ci = lax.broadcasted_iota(jnp.int32, (CHUNK, CHUNK), 0)
    si = lax.broadcasted_iota(jnp.int32, (CHUNK, CHUNK), 1)
    masks = (ci >= si, ci <= si)
    one_col = (si == 0).astype(BF16)
    outs = (of_s, ob_s)

    def chunk(d, c, half, rt, m_row, states):
        r0 = pl.multiple_of(c * CHUNK, CHUNK)
        last = 0 if d else CHUNK - 1
        b_t = b_s[pl.ds(r0, CHUNK), :]
        r_t = r_s[pl.ds(r0, CHUNK), :]
        cm_t = cm_s[pl.ds(r0, CHUNK), :]
        m_t = jnp.maximum(b_t + m_row, b_t + cm_t)
        w_inter_t = jnp.exp(b_t + m_row - m_t)
        e_t = b_t - m_t
        floor_t = jnp.exp(-m_t)
        bl = b_t[last:last + 1, :]
        m_new = bl + jnp.maximum(m_row, cm_t[last:last + 1, :])
        w_prev = jnp.exp(bl + m_row - m_new)
        wk_t = jnp.exp(bl + r_t - m_new)
        q = q_s[pl.ds(r0, CHUNK), :]
        k = k_s[pl.ds(r0, CHUNK), :]
        v = v_ref[0, pl.ds(r0, CHUNK), :].astype(BF16)
        o_heads, new_states = [], []
        for hh in range(heads_per_pair):
            sl = slice(hh * HEAD_DIM, (hh + 1) * HEAD_DIM)
            L = _gate_lane(d, hh)
            r_row = rt[L:L + 1, half * CHUNK:(half + 1) * CHUNK]
            dmat = jnp.where(masks[d], jnp.exp(e_t[:, L:L + 1] + r_row), 0.0)
            smat = (_dot_nt(q[:, sl], k[:, sl]) * dmat).astype(BF16)
            v_aug = jnp.concatenate([v[:, sl], one_col], axis=1)
            st = states[hh]
            num = w_inter_t[:, L:L + 1] * _dot(q[:, sl], st.astype(BF16)) + _dot(smat, v_aug)
            den = jnp.maximum(jnp.abs(num[:, HEAD_DIM:HEAD_DIM + 1]), floor_t[:, L:L + 1])
            o_heads.append(num[:, :HEAD_DIM] / den)
            kw = (k[:, sl].astype(F32) * wk_t[:, L:L + 1]).astype(BF16)
            new_states.append(w_prev[:, L:L + 1] * st + _dot_tn(kw, v_aug))
        outs[d][pl.ds(r0, CHUNK), :] = jnp.concatenate(o_heads, axis=1)
        return m_new, tuple(new_states)

    def step(i, carry):
        m_rows, states = carry
        new_m, new_states = [], []
        for d in range(2):
            j = (nc // 2 - 1 - i) if d else i
            rt = rt_s[:, pl.ds(pl.multiple_of(j * LANES, LANES), LANES)]
            m_row, st = m_rows[d], states[d]
            for half in ((1, 0) if d else (0, 1)):
                m_row, st = chunk(d, 2 * j + half, half, rt, m_row, st)
            new_m.append(m_row)
            new_states.append(st)
        return tuple(new_m), tuple(new_states)

    zero = jnp.zeros((HEAD_DIM, LANES), F32)
    m0 = jnp.full((1, LANES), NEG, F32)
    lax.fori_loop(0, nc // 2, step, ((m0, m0), ((zero, zero), (zero, zero))))

    def post(t, _):
        r0 = pl.multiple_of(t * R, R)
        h = _head_rms(of_s[pl.ds(r0, R), :] + ob_s[pl.ds(r0, R), :], gain_ref[...])
        o_ref[0, pl.ds(r0, R), :] = (h * _sigmoid(og_ref[0, pl.ds(r0, R), :])).astype(BF16)
        return 0

    lax.fori_loop(0, T // R, post, 0)


def _mlstm(ml, small, lp):
    B, T, _ = ml.shape
    pairs = MLSTM_WIDTH // LANES
    col = lambda off: (lambda b, hp: (b, 0, off * pairs + hp))
    const = lambda b, hp: (0, 0)
    return pl.pallas_call(
        _mlstm_kernel,
        grid=(B, pairs),
        in_specs=[
            pl.BlockSpec((1, T, LANES), col(0)),
            pl.BlockSpec((1, T, LANES), col(1)),
            pl.BlockSpec((1, T, LANES), col(2)),
            pl.BlockSpec((1, T, LANES), col(3)),
            pl.BlockSpec((1, T, LANES), lambda b, hp: (b, 0, 0)),
            pl.BlockSpec((3, LANES), lambda b, hp: (0, hp)),
            pl.BlockSpec((3, LANES), lambda b, hp: (0, pairs + hp)),
            pl.BlockSpec((1, LANES), lambda b, hp: (0, hp)),
            pl.BlockSpec((1, LANES), lambda b, hp: (0, pairs + hp)),
            pl.BlockSpec((1, LANES), const),
            pl.BlockSpec((1, LANES), const),
        ],
        out_specs=pl.BlockSpec((1, T, LANES), lambda b, hp: (b, 0, hp)),
        out_shape=jax.ShapeDtypeStruct((B, T, MLSTM_WIDTH), BF16),
        scratch_shapes=[
            pltpu.VMEM((T, LANES), BF16), pltpu.VMEM((T, LANES), BF16),
            pltpu.VMEM((T, LANES), F32), pltpu.VMEM((T, LANES), F32), pltpu.VMEM((T, LANES), F32),
            pltpu.VMEM((LANES, T), F32),
            pltpu.VMEM((T, LANES), F32), pltpu.VMEM((T, LANES), F32),
        ],
        compiler_params=_params("arbitrary", "arbitrary"),
        name="mlstm",
    )(ml, ml, ml, ml, small, lp["conv_w"], lp["conv_w"], lp["conv_b"], lp["conv_b"],
      lp["gate_bias"], lp["ml_gain"])


def _first_argmax(vals, lane):
    mx = jnp.max(vals, axis=-1, keepdims=True)
    idx = jnp.min(jnp.where(vals == mx, lane, LANES), axis=-1, keepdims=True)
    return mx, idx


def _out_route_kernel(x_ref, a_ref, gl_ref, ml_ref, w_ref, g_ref, wrh_ref, wrl_ref, br_ref,
                      x1_ref, t_ref, route_ref, routet_ref):
    tm = x_ref.shape[0]
    x1 = (x_ref[...]
          + _dot(a_ref[...], w_ref[0:ATTN_WIDTH, :])
          + _dot(gl_ref[...], w_ref[ATTN_WIDTH:ATTN_WIDTH + GLA_WIDTH, :])
          + _dot(ml_ref[...], w_ref[ATTN_WIDTH + GLA_WIDTH:, :]))
    x1_ref[...] = x1
    t = _rms(x1, g_ref[...])
    t_ref[...] = t.astype(BF16)
    logits = _dot3(t, wrh_ref[...], wrl_ref[...]) + br_ref[...]
    lane = lax.broadcasted_iota(jnp.int32, (tm, LANES), 1)
    gl = jnp.where(lane < N_GROUPS, logits, -jnp.inf)
    gmax, gi = _first_argmax(gl, lane)
    g_prob = 1.0 / jnp.sum(jnp.exp(gl - gmax), axis=-1, keepdims=True)
    lo = ROUTE_W_LANE + gi * EXPERTS_PER_GROUP
    el = jnp.where((lane >= lo) & (lane < lo + EXPERTS_PER_GROUP), logits, -jnp.inf)
    v1, i1 = _first_argmax(el, lane)
    v2, i2 = _first_argmax(jnp.where(lane == i1, -jnp.inf, el), lane)
    e2 = jnp.exp(v2 - v1)
    w1 = g_prob / (1.0 + e2)
    w2 = g_prob * e2 / (1.0 + e2)
    comb = jnp.where(lane == i1, w1, jnp.where(lane == i2, w2, 0.0))
    route = jnp.where(lane == 0, gi.astype(F32), comb)
    route_ref[...] = route
    routet_ref[...] = route.T[0:SUBLANES, :]


def _out_route(x2, attn, gla_o, ml_o, lp, tm=256):
    N, D = x2.shape
    const = lambda i: (0, 0)
    tok = lambda i: (i, 0)
    return pl.pallas_call(
        _out_route_kernel,
        grid=(N // tm,),
        in_specs=[
            pl.BlockSpec((tm, D), tok),
            pl.BlockSpec((tm, ATTN_WIDTH), tok),
            pl.BlockSpec((tm, GLA_WIDTH), tok),
            pl.BlockSpec((tm, MLSTM_WIDTH), tok),
            pl.BlockSpec((ATTN_WIDTH + GLA_WIDTH + MLSTM_WIDTH, D), const),
            pl.BlockSpec((1, D), const),
            pl.BlockSpec((D, LANES), const),
            pl.BlockSpec((D, LANES), const),
            pl.BlockSpec((1, LANES), const),
        ],
        out_specs=[
            pl.BlockSpec((tm, D), tok),
            pl.BlockSpec((tm, D), tok),
            pl.BlockSpec((tm, LANES), tok),
            pl.BlockSpec((SUBLANES, tm), lambda i: (0, i)),
        ],
        out_shape=[
            jax.ShapeDtypeStruct((N, D), F32),
            jax.ShapeDtypeStruct((N, D), BF16),
            jax.ShapeDtypeStruct((N, LANES), F32),
            jax.ShapeDtypeStruct((SUBLANES, N), F32),
        ],
        compiler_params=_params("arbitrary"),
        name="out_route",
    )(x2, attn, gla_o, ml_o, lp["w_out"], lp["g_ffn"], lp["wr_hi"], lp["wr_lo"], lp["b_route"])


def _dispatch_kernel(t_ref, route_ref, routet_ref, xs_ref, cws_ref, pos_ref, cnt_ref):
    nb = ROUTE_BLOCK
    gi_row = routet_ref[0:1, :]
    sub = lax.broadcasted_iota(jnp.int32, (SUBLANES, nb), 0).astype(F32)
    onehot = (sub == gi_row)
    ri = lax.broadcasted_iota(jnp.int32, (nb, nb), 0)
    cj = lax.broadcasted_iota(jnp.int32, (nb, nb), 1)
    before = (ri < cj).astype(BF16)
    rank = _dot(onehot.astype(BF16), before)
    counts = jnp.broadcast_to(jnp.sum(onehot.astype(F32), axis=-1, keepdims=True), (SUBLANES, LANES))
    padded = jnp.ceil(counts * (1.0 / ROUTE_TILE)) * ROUTE_TILE
    srow = lax.broadcasted_iota(jnp.int32, (SUBLANES, LANES), 0)
    incl = padded
    s = 1
    while s < SUBLANES:
        incl = incl + jnp.where(srow >= s, pltpu.roll(incl, s, 0), 0.0)
        s *= 2
    start = incl - padded
    pos_row = jnp.sum(jnp.where(onehot, start[:, 0:1] + rank, 0.0), axis=0, keepdims=True)
    cnt_ref[0] = counts.astype(jnp.int32)

    route = route_ref[...]
    lane = lax.broadcasted_iota(jnp.int32, (nb, LANES), 1)
    onehot_c = lane.astype(F32) == route[:, 0:1]
    after = (cj < ri).astype(BF16)
    rank_c = _dot(after, onehot_c.astype(BF16))
    slane = lax.broadcasted_iota(jnp.int32, (SUBLANES, LANES), 1)
    start_c = jnp.sum(jnp.where(srow == slane, start, 0.0), axis=0, keepdims=True)
    pos_col = jnp.sum(jnp.where(onehot_c, start_c + rank_c, 0.0), axis=-1, keepdims=True)
    pos_ref[...] = jnp.broadcast_to(pos_col, (nb, LANES))

    comb = jnp.where(lane >= ROUTE_W_LANE, route, 0.0)
    c_hi, c_lo = _split(comb)
    c_lo2 = (comb - c_hi.astype(F32) - c_lo.astype(F32)).astype(BF16)
    cw = (c_hi.astype(F32) + pltpu.roll(c_lo.astype(F32), ROUTE_LO_SHIFT, 1)
          + pltpu.roll(c_lo2.astype(F32), 2 * ROUTE_LO_SHIFT, 1)).astype(BF16)
    tb = t_ref[...]
    for r in range(TILES_PER_BLOCK):
        rows = (lax.broadcasted_iota(jnp.int32, (ROUTE_TILE, nb), 0) + r * ROUTE_TILE).astype(F32)
        perm = (rows == pos_row).astype(BF16)
        xs_ref[r * ROUTE_TILE:(r + 1) * ROUTE_TILE, :] = _dot(perm, tb).astype(BF16)
        cws_ref[r * ROUTE_TILE:(r + 1) * ROUTE_TILE, :] = _dot(perm, cw)


def _dispatch(t, route, routet):
    N, D = t.shape
    nblk = N // ROUTE_BLOCK
    return pl.pallas_call(
        _dispatch_kernel,
        grid=(nblk,),
        in_specs=[
            pl.BlockSpec((ROUTE_BLOCK, D), lambda i: (i, 0)),
            pl.BlockSpec((ROUTE_BLOCK, LANES), lambda i: (i, 0)),
            pl.BlockSpec((SUBLANES, ROUTE_BLOCK), lambda i: (0, i)),
        ],
        out_specs=[
            pl.BlockSpec((ROUTE_ROWS, D), lambda i: (i, 0)),
            pl.BlockSpec((ROUTE_ROWS, LANES), lambda i: (i, 0)),
            pl.BlockSpec((ROUTE_BLOCK, LANES), lambda i: (i, 0)),
            pl.BlockSpec((1, SUBLANES, LANES), lambda i: (i, 0, 0)),
        ],
        out_shape=[
            jax.ShapeDtypeStruct((nblk * ROUTE_ROWS, D), BF16),
            jax.ShapeDtypeStruct((nblk * ROUTE_ROWS, LANES), F32),
            jax.ShapeDtypeStruct((N, LANES), F32),
            jax.ShapeDtypeStruct((nblk, SUBLANES, LANES), jnp.int32),
        ],
        compiler_params=_params("arbitrary"),
        name="dispatch",
    )(t, route, routet)


def _tile_schedule(cnt):
    nblk = cnt.shape[0]
    ntile = (cnt + ROUTE_TILE - 1) // ROUTE_TILE
    end = jnp.cumsum(ntile, axis=1)
    r = jnp.arange(TILES_PER_BLOCK, dtype=jnp.int32)
    grp = jnp.sum(r[None, :, None] >= end[:, None, :], axis=-1)
    grp = grp.reshape(-1).astype(jnp.int32)
    tile = jnp.arange(nblk * TILES_PER_BLOCK, dtype=jnp.int32)
    order = jnp.argsort(grp * (nblk * TILES_PER_BLOCK) + tile).astype(jnp.int32)
    n_active = jnp.sum(grp < N_GROUPS).astype(jnp.int32)
    g_sorted = grp[order]
    last_group = g_sorted[jnp.maximum(n_active - 1, 0)]
    g_sorted = jnp.where(g_sorted < N_GROUPS, g_sorted, last_group)
    return order, g_sorted, n_active[None]


def _expert_kernel(trow_ref, tgrp_ref, nact_ref, xs_ref, cws_ref, wg_ref, wu_ref, wd_ref, ys_ref):
    i = pl.program_id(0)

    @pl.when(i < nact_ref[0])
    def _():
        x = xs_ref[...]
        cws = cws_ref[...]
        lane = lax.broadcasted_iota(jnp.int32, cws.shape, 1)
        first = ROUTE_W_LANE + tgrp_ref[i] * EXPERTS_PER_GROUP
        y = jnp.zeros(ys_ref.shape, F32)
        for j in range(EXPERTS_PER_GROUP):
            off = lane - (first + j)
            sel = (off == 0) | (off == ROUTE_LO_SHIFT) | (off == 2 * ROUTE_LO_SHIFT)
            wj = jnp.sum(jnp.where(sel, cws, 0.0), axis=-1, keepdims=True)
            h1 = _dot(x, wg_ref[j])
            a = (h1 * _sigmoid(h1) * _dot(x, wu_ref[j])).astype(BF16)
            y = y + wj * _dot(a, wd_ref[j])
        ys_ref[...] = y.astype(BF16)

    @pl.when(i >= nact_ref[0])
    def _():
        ys_ref[...] = jnp.zeros(ys_ref.shape, BF16)


def _experts(xs, cws, order, grp, n_active, lp):
    rows, D = xs.shape
    n_tiles = rows // ROUTE_TILE
    tile = lambda i, trow, tgrp, nact: (trow[i], 0)
    wsel = lambda i, trow, tgrp, nact: (tgrp[i], 0, 0)
    return pl.pallas_call(
        _expert_kernel,
        grid_spec=pltpu.PrefetchScalarGridSpec(
            num_scalar_prefetch=3,
            grid=(n_tiles,),
            in_specs=[
                pl.BlockSpec((ROUTE_TILE, D), tile),
                pl.BlockSpec((ROUTE_TILE, LANES), tile),
                pl.BlockSpec((EXPERTS_PER_GROUP, D, D_EXPERT), wsel),
                pl.BlockSpec((EXPERTS_PER_GROUP, D, D_EXPERT), wsel),
                pl.BlockSpec((EXPERTS_PER_GROUP, D_EXPERT, D), wsel),
            ],
            out_specs=pl.BlockSpec((ROUTE_TILE, D), tile),
        ),
        out_shape=jax.ShapeDtypeStruct((rows, D), BF16),
        compiler_params=_params("arbitrary"),
        name="experts",
    )(order, grp, n_active, xs, cws, lp["w_gate"], lp["w_up"], lp["w_down"])


def _combine_kernel(ys_ref, pos_ref, x1_ref, p_ref, g_ref, wpg_ref, wpp_ref, gfin_ref, o_ref,
                    *, embed, final):
    tm = x1_ref.shape[0]
    pos = pos_ref[...]
    lane = lax.broadcasted_iota(jnp.int32, (tm, LANES), 1).astype(F32)
    perm_t = jnp.concatenate(
        [(lane + r * LANES == pos).astype(BF16) for r in range(ROUTE_ROWS // LANES)], axis=1)
    x = x1_ref[...] + _dot(perm_t, ys_ref[...])
    if embed:
        gate = _sigmoid(_dot(_rms(x, g_ref[...]).astype(BF16), wpg_ref[...]))
        x = x + gate * _dot(p_ref[...].astype(BF16), wpp_ref[...])
    if final:
        x = _rms(x, gfin_ref[...])
    o_ref[...] = x


def _combine(ys, pos, x1, p2, lp, g_final, embed=True, final=False, tm=256):
    N, D = x1.shape
    inner = ROUTE_BLOCK // tm
    tok = lambda b, i: (b * inner + i, 0)
    const = lambda b, i: (0, 0)
    return pl.pallas_call(
        functools.partial(_combine_kernel, embed=embed, final=final),
        grid=(N // ROUTE_BLOCK, inner),
        in_specs=[
            pl.BlockSpec((ROUTE_ROWS, D), lambda b, i: (b, 0)),
            pl.BlockSpec((tm, LANES), tok),
            pl.BlockSpec((tm, D), tok),
            pl.BlockSpec((tm, p2.shape[1]), tok),
            pl.BlockSpec((1, D), const),
            pl.BlockSpec((D, D), const),
            pl.BlockSpec((p2.shape[1], D), const),
            pl.BlockSpec((1, D), const),
        ],
        out_specs=pl.BlockSpec((tm, D), tok),
        out_shape=jax.ShapeDtypeStruct((N, D), F32),
        compiler_params=_params("arbitrary", "arbitrary"),
        name="combine",
    )(ys, pos, x1, p2, lp["g_ple"], lp["w_pg"], lp["w_pp"], g_final)


def _moe(t, route, routet, lp):
    xs, cws, pos, cnt = _dispatch(t, route, routet)
    order, grp, n_active = _tile_schedule(cnt[:, :N_GROUPS, 0])
    return _experts(xs, cws, order, grp, n_active, lp), pos


def _moe_debug(x2, lp):
    N, D = x2.shape
    zeros = lambda w: jnp.zeros((N, w), BF16)
    x1, t, route, routet = _out_route(x2, zeros(ATTN_WIDTH), zeros(GLA_WIDTH), zeros(MLSTM_WIDTH), lp)
    ys, pos = _moe(t, route, routet, lp)
    p2 = jnp.zeros((N, lp["w_pp"].shape[0]), F32)
    return _combine(ys, pos, x1, p2, lp, lp["g_ple"], embed=False) - x1


def _in_perm():
    sizes = (ATTN_WIDTH, KV_WIDTH, KV_WIDTH, GLA_WIDTH, GLA_WIDTH, GLA_WIDTH, GLA_WIDTH,
             2 * GLA_RANK, MLSTM_WIDTH, MLSTM_WIDTH, MLSTM_WIDTH, MLSTM_WIDTH, 4 * MLSTM_HEADS)
    offs = np.concatenate([[0], np.cumsum(sizes)])
    seg = lambda i: np.arange(offs[i], offs[i + 1])
    order = [0, 1, 2, 3, 4, 5, 6, 8, 9, 10, 11, 7, 12]
    return np.concatenate([seg(i) for i in order])


def _rope_tables(T):
    t = jnp.arange(T, dtype=F32)
    row = jnp.floor(t / GRID_W)
    col = t - row * GRID_W
    half = ROPE_AXIS_DIM // 2
    inv = ROPE_THETA ** (-jnp.arange(0, ROPE_AXIS_DIM, 2, dtype=F32) / ROPE_AXIS_DIM)
    ang_r = row[:, None] * inv[None, :]
    ang_c = col[:, None] * inv[None, :]
    cos_h = jnp.concatenate([jnp.cos(ang_r), jnp.cos(ang_r), jnp.cos(ang_c), jnp.cos(ang_c)], axis=1)
    sin_h = jnp.concatenate([-jnp.sin(ang_r), jnp.sin(ang_r), -jnp.sin(ang_c), jnp.sin(ang_c)], axis=1)
    reps = LANES // HEAD_DIM
    return jnp.tile(cos_h, (1, reps)), jnp.tile(sin_h, (1, reps))


def _head_onehots():
    n_heads = QK_WIDTH // HEAD_DIM
    eh = np.zeros((QK_WIDTH, LANES), np.float32)
    eh[np.arange(QK_WIDTH), np.arange(QK_WIDTH) // HEAD_DIM] = 1.0
    assert n_heads <= LANES
    return jnp.asarray(eh, BF16), jnp.asarray(eh.T, BF16)


def kernel(x, p, norm_mix_g, w_in, attn_q_norm_g, attn_k_norm_g, gla_w_decay, gla_b_decay,
           gla_out_norm_g, mlstm_conv_w, mlstm_conv_b, mlstm_b_input, mlstm_b_forget,
           mlstm_out_norm_g, w_out, norm_ffn_g, w_group, b_group, w_router, b_router,
           w_expert_gate, w_expert_up, w_expert_down, norm_ple_g, w_ple_gate, w_ple_proj,
           final_norm_g):
    params = dict(
        norm_mix_g=norm_mix_g, w_in=w_in, attn_q_norm_g=attn_q_norm_g, attn_k_norm_g=attn_k_norm_g,
        gla_w_decay=gla_w_decay, gla_b_decay=gla_b_decay, gla_out_norm_g=gla_out_norm_g,
        mlstm_conv_w=mlstm_conv_w, mlstm_conv_b=mlstm_conv_b, mlstm_b_input=mlstm_b_input,
        mlstm_b_forget=mlstm_b_forget, mlstm_out_norm_g=mlstm_out_norm_g, w_out=w_out,
        norm_ffn_g=norm_ffn_g, w_group=w_group, b_group=b_group, w_router=w_router,
        b_router=b_router, w_expert_gate=w_expert_gate, w_expert_up=w_expert_up,
        w_expert_down=w_expert_down, norm_ple_g=norm_ple_g, w_ple_gate=w_ple_gate,
        w_ple_proj=w_ple_proj)
    B, T, D = x.shape
    cos, sin = _rope_tables(T)
    eh, eht = _head_onehots()
    N = B * T
    depth = w_in.shape[0]
    x2 = x.reshape(N, D)
    g_final = final_norm_g[None, :]
    for i in range(depth):
        lp = _layer_params(params, i)
        qt, k, vt, gla, ml, small = _in_proj(x2, lp["g_mix"], lp["w_in"], cos, sin, lp["qk_gain"],
                                             eh, eht, B, T)
        attn = _attention(qt, k, vt).reshape(N, ATTN_WIDTH)
        gla_o = _gla(gla, small, lp).reshape(N, GLA_WIDTH)
        ml_o = _mlstm(ml, small, lp).reshape(N, MLSTM_WIDTH)
        x1, t, route, routet = _out_route(x2, attn, gla_o, ml_o, lp)
        ys, pos = _moe(t, route, routet, lp)
        x2 = _combine(ys, pos, x1, p[i].reshape(N, -1), lp, g_final, final=(i == depth - 1))
    return x2.reshape(B, T, D)


def _split_w(w):
    hi = w.astype(BF16)
    return hi, (w - hi.astype(F32)).astype(BF16)


def _layer_params(p, i):
    D = p["w_in"].shape[1]
    w_in = jnp.pad(p["w_in"][i][:, _in_perm()], ((0, 0), (0, IN_PERM_WIDTH - p["w_in"].shape[2])))
    qk_gain = jnp.concatenate([jnp.tile(p["attn_q_norm_g"][i], ATTN_HEADS) * HEAD_DIM ** -0.5,
                               jnp.tile(p["attn_k_norm_g"][i], ATTN_KV_HEADS)])
    wd = jnp.zeros((2, LANES, GLA_WIDTH), F32)
    wd = wd.at[0, :GLA_RANK].set(p["gla_w_decay"][i, 0]).at[1, GLA_RANK:2 * GLA_RANK].set(p["gla_w_decay"][i, 1])
    wd_hi, wd_lo = _split_w(wd)
    gate_bias = jnp.zeros((LANES,), F32).at[SMALL_GATE_LANE:SMALL_GATE_LANE + 4 * MLSTM_HEADS].set(
        jnp.concatenate([p["mlstm_b_input"][i, 0], p["mlstm_b_forget"][i, 0],
                         p["mlstm_b_input"][i, 1], p["mlstm_b_forget"][i, 1]]))
    w_route = jnp.zeros((D, LANES), F32)
    w_route = w_route.at[:, :N_GROUPS].set(p["w_group"][i])
    w_route = w_route.at[:, ROUTE_W_LANE:ROUTE_W_LANE + N_EXPERTS].set(p["w_router"][i])
    wr_hi, wr_lo = _split_w(w_route)
    b_route = jnp.zeros((LANES,), F32).at[:N_GROUPS].set(p["b_group"][i])
    b_route = b_route.at[ROUTE_W_LANE:ROUTE_W_LANE + N_EXPERTS].set(p["b_router"][i])
    return dict(
        g_mix=p["norm_mix_g"][i][None, :],
        w_in=w_in.astype(BF16),
        qk_gain=qk_gain[None, :],
        wd_hi=wd_hi, wd_lo=wd_lo,
        bd=p["gla_b_decay"][i][:, None, :],
        gla_gain=jnp.tile(p["gla_out_norm_g"][i], LANES // HEAD_DIM)[None, :],
        conv_w=p["mlstm_conv_w"][i],
        conv_b=p["mlstm_conv_b"][i][None, :],
        gate_bias=gate_bias[None, :],
        ml_gain=jnp.tile(p["mlstm_out_norm_g"][i], LANES // HEAD_DIM)[None, :],
        w_out=p["w_out"][i].astype(BF16),
        g_ffn=p["norm_ffn_g"][i][None, :],
        wr_hi=wr_hi, wr_lo=wr_lo, b_route=b_route[None, :],
        w_gate=p["w_expert_gate"][i].astype(BF16),
        w_up=p["w_expert_up"][i].astype(BF16),
        w_down=p["w_expert_down"][i].astype(BF16),
        g_ple=p["norm_ple_g"][i][None, :],
        w_pg=p["w_ple_gate"][i].astype(BF16),
        w_pp=p["w_ple_proj"][i].astype(BF16),
    )
```

```python
import functools

import jax
import jax.numpy as jnp
import numpy as np
from jax import lax
from jax.experimental import pallas as pl
from jax.experimental.pallas import tpu as pltpu

F32 = jnp.float32
BF16 = jnp.bfloat16

GRID_W = 64
HEAD_DIM = 64
ATTN_HEADS = 8
ATTN_KV_HEADS = 2
GLA_HEADS = 4
MLSTM_HEADS = 4
ATTN_WIDTH = ATTN_HEADS * HEAD_DIM
KV_WIDTH = ATTN_KV_HEADS * HEAD_DIM
GLA_WIDTH = GLA_HEADS * HEAD_DIM
MLSTM_WIDTH = MLSTM_HEADS * HEAD_DIM
GLA_RANK = 16
GLA_TAU = 16.0
CHUNK = 64
ROPE_THETA = 10000.0
ROPE_AXIS_DIM = HEAD_DIM // 2
N_GROUPS = 4
EXPERTS_PER_GROUP = 4
N_EXPERTS = N_GROUPS * EXPERTS_PER_GROUP
D_EXPERT = 512
EPS = 1e-6
NEG = -1e30

LANES = 128
SUBLANES = 8
VMEM_LIMIT_BYTES = 56 * 1024 * 1024

QK_WIDTH = ATTN_WIDTH + KV_WIDTH
OFF_V = QK_WIDTH
OFF_GLA = OFF_V + KV_WIDTH
OFF_ML = OFF_GLA + 4 * GLA_WIDTH
OFF_SMALL = OFF_ML + 4 * MLSTM_WIDTH
IN_PERM_WIDTH = OFF_SMALL + LANES
SMALL_GATE_LANE = 2 * GLA_RANK

ROUTE_BLOCK = 1024
ROUTE_TILE = 128
ROUTE_ROWS = ROUTE_BLOCK + N_GROUPS * ROUTE_TILE
TILES_PER_BLOCK = ROUTE_ROWS // ROUTE_TILE
ROUTE_W_LANE = 4
ROUTE_LO_SHIFT = 32


def _dot(a, b):
    return jnp.dot(a, b, preferred_element_type=F32)


def _dot_nt(a, b):
    return lax.dot_general(a, b, (((1,), (1,)), ((), ())), preferred_element_type=F32)


def _dot_tn(a, b):
    return lax.dot_general(a, b, (((0,), (0,)), ((), ())), preferred_element_type=F32)


def _split(a):
    hi = a.astype(BF16)
    lo = (a - hi.astype(F32)).astype(BF16)
    return hi, lo


def _dot3(a, w_hi, w_lo):
    a_hi, a_lo = _split(a)
    return _dot(a_hi, w_hi) + _dot(a_lo, w_hi) + _dot(a_hi, w_lo)


def _log_sigmoid(x):
    return jnp.minimum(x, 0.0) - jnp.log1p(jnp.exp(-jnp.abs(x)))


def _sigmoid(x):
    return 1.0 / (1.0 + jnp.exp(-x))


def _rms(x, g):
    return x * lax.rsqrt(jnp.mean(x * x, axis=-1, keepdims=True) + EPS) * g


def _params(*semantics):
    return pltpu.CompilerParams(dimension_semantics=semantics, vmem_limit_bytes=VMEM_LIMIT_BYTES)


def _in_proj_kernel(x_ref, g_ref, w_ref, cos_ref, sin_ref, gain_ref, eh_ref, eht_ref,
                    qt_ref, k_ref, vt_ref, gla_ref, ml_ref, small_ref):
    tm = x_ref.shape[0]
    h = _rms(x_ref[...], g_ref[...])
    z = _dot(h.astype(BF16), w_ref[...])
    qk = z[:, :QK_WIDTH]
    sq_hi, sq_lo = _split(qk * qk)
    ssq = _dot(sq_hi, eh_ref[...]) + _dot(sq_lo, eh_ref[...])
    inv_hi, inv_lo = _split(lax.rsqrt(ssq * (1.0 / HEAD_DIM) + EPS))
    scale = _dot(inv_hi, eht_ref[...]) + _dot(inv_lo, eht_ref[...])
    y = qk * scale * gain_ref[...]
    lane = lax.broadcasted_iota(jnp.int32, (tm, LANES), 1)
    first_half = (lane % ROPE_AXIS_DIM) < (ROPE_AXIS_DIM // 2)
    cos = cos_ref[...]
    sin = sin_ref[...]
    for c in range(QK_WIDTH // LANES):
        yc = y[:, c * LANES:(c + 1) * LANES]
        partner = jnp.where(first_half,
                            pltpu.roll(yc, LANES - ROPE_AXIS_DIM // 2, 1),
                            pltpu.roll(yc, ROPE_AXIS_DIM // 2, 1))
        rc = yc * cos + partner * sin
        if c < ATTN_WIDTH // LANES:
            qt_ref[0, c * LANES:(c + 1) * LANES, :] = rc.T.astype(BF16)
        else:
            k_ref[0] = rc.astype(BF16)
    vt_ref[0] = z[:, OFF_V:OFF_V + KV_WIDTH].T.astype(BF16)
    gla_ref[0] = z[:, OFF_GLA:OFF_ML]
    ml_ref[0] = z[:, OFF_ML:OFF_SMALL]
    small_ref[0] = z[:, OFF_SMALL:IN_PERM_WIDTH]


def _in_proj(x2, g, w_perm, cos, sin, gain, eh, eht, B, T, tm=256):
    N, D = x2.shape
    tpb = T // tm
    const = lambda i: (0, 0)
    tok3 = lambda i: (i // tpb, i % tpb, 0)
    tokT = lambda i: (i // tpb, 0, i % tpb)
    return pl.pallas_call(
        _in_proj_kernel,
        grid=(N // tm,),
        in_specs=[
            pl.BlockSpec((tm, D), lambda i: (i, 0)),
            pl.BlockSpec((1, D), const),
            pl.BlockSpec((D, IN_PERM_WIDTH), const),
            pl.BlockSpec((tm, LANES), lambda i: (i % tpb, 0)),
            pl.BlockSpec((tm, LANES), lambda i: (i % tpb, 0)),
            pl.BlockSpec((1, QK_WIDTH), const),
            pl.BlockSpec((QK_WIDTH, LANES), const),
            pl.BlockSpec((LANES, QK_WIDTH), const),
        ],
        out_specs=[
            pl.BlockSpec((1, ATTN_WIDTH, tm), tokT),
            pl.BlockSpec((1, tm, KV_WIDTH), tok3),
            pl.BlockSpec((1, KV_WIDTH, tm), tokT),
            pl.BlockSpec((1, tm, 4 * GLA_WIDTH), tok3),
            pl.BlockSpec((1, tm, 4 * MLSTM_WIDTH), tok3),
            pl.BlockSpec((1, tm, LANES), tok3),
        ],
        out_shape=[
            jax.ShapeDtypeStruct((B, ATTN_WIDTH, T), BF16),
            jax.ShapeDtypeStruct((B, T, KV_WIDTH), BF16),
            jax.ShapeDtypeStruct((B, KV_WIDTH, T), BF16),
            jax.ShapeDtypeStruct((B, T, 4 * GLA_WIDTH), F32),
            jax.ShapeDtypeStruct((B, T, 4 * MLSTM_WIDTH), F32),
            jax.ShapeDtypeStruct((B, T, LANES), F32),
        ],
        compiler_params=_params("arbitrary"),
        name="in_proj",
    )(x2, g, w_perm, cos, sin, gain, eh, eht)


ATTN_SAFE_LOGIT = 40.0
LOG2E = 1.4426950408889634


def _attn_kernel(safe_ref, qt_ref, k_ref, vt_ref, o_ref, *, tk):
    tq = qt_ref.shape[2]
    T = k_ref.shape[1]
    G = ATTN_HEADS // ATTN_KV_HEADS
    n = G * tq
    zeros = jnp.zeros((HEAD_DIM, n), BF16)

    def q_operand(j):
        base = j * G * HEAD_DIM
        qs = jnp.concatenate(
            [qt_ref[0, base + h * HEAD_DIM:base + (h + 1) * HEAD_DIM, :] for h in range(G)], axis=1)
        return jnp.concatenate([qs, zeros] if j == 0 else [zeros, qs], axis=0)

    def finish(j, acc, l):
        base = j * G * HEAD_DIM
        o = acc * (1.0 / l)
        ot = jnp.concatenate([o[:, h * tq:(h + 1) * tq] for h in range(G)], axis=0)
        o_ref[0, :, base:base + G * HEAD_DIM] = ot.T.astype(BF16)

    @pl.when(safe_ref[0] == 1)
    def _():
        qps = [q_operand(j) for j in range(ATTN_KV_HEADS)]
        units = [(c, j) for c in range(T // tk) for j in range(ATTN_KV_HEADS)]

        def scores(u):
            c, j = units[u]
            return _dot(k_ref[0, c * tk:(c + 1) * tk, :], qps[j])

        l8 = [jnp.zeros((SUBLANES, n), F32)] * ATTN_KV_HEADS
        acc = [jnp.zeros((HEAD_DIM, n), F32)] * ATTN_KV_HEADS
        st = scores(0)
        for u, (c, j) in enumerate(units):
            st_next = scores(u + 1) if u + 1 < len(units) else None
            p = jnp.exp2(st)
            l8[j] = l8[j] + jnp.sum(p.reshape(tk // SUBLANES, SUBLANES, n), axis=0)
            vc = vt_ref[0, j * HEAD_DIM:(j + 1) * HEAD_DIM, c * tk:(c + 1) * tk]
            acc[j] = acc[j] + _dot(vc, p.astype(BF16))
            st = st_next
        for j in range(ATTN_KV_HEADS):
            finish(j, acc[j], jnp.sum(l8[j], axis=0, keepdims=True))

    @pl.when(safe_ref[0] == 0)
    def _():
        for j in range(ATTN_KV_HEADS):
            qp = q_operand(j)

            def body(c, carry, qp=qp, j=j):
                m, l, acc = carry
                off = pl.multiple_of(c * tk, tk)
                st = _dot(k_ref[0, pl.ds(off, tk), :], qp)
                m_new = jnp.maximum(m, jnp.max(st, axis=0, keepdims=True))
                alpha = jnp.exp2(m - m_new)
                p = jnp.exp2(st - m_new)
                l = alpha * l + jnp.sum(p, axis=0, keepdims=True)
                vc = vt_ref[0, j * HEAD_DIM:(j + 1) * HEAD_DIM, pl.ds(off, tk)]
                return m_new, l, alpha * acc + _dot(vc, p.astype(BF16))

            init = (jnp.full((1, n), NEG, F32), jnp.zeros((1, n), F32), jnp.zeros((HEAD_DIM, n), F32))
            _, l, acc = lax.fori_loop(0, T // tk, body, init)
            finish(j, acc, l)


def _attention(safe, qt, k, vt, tq=128, tk=256):
    B, _, T = qt.shape
    tk = min(tk, T)
    return pl.pallas_call(
        functools.partial(_attn_kernel, tk=tk),
        grid_spec=pltpu.PrefetchScalarGridSpec(
            num_scalar_prefetch=1,
            grid=(B, T // tq),
            in_specs=[
                pl.BlockSpec((1, ATTN_WIDTH, tq), lambda b, i, s: (b, 0, i)),
                pl.BlockSpec((1, T, KV_WIDTH), lambda b, i, s: (b, 0, 0)),
                pl.BlockSpec((1, KV_WIDTH, T), lambda b, i, s: (b, 0, 0)),
            ],
            out_specs=pl.BlockSpec((1, tq, ATTN_WIDTH), lambda b, i, s: (b, i, 0)),
        ),
        out_shape=jax.ShapeDtypeStruct((B, T, ATTN_WIDTH), BF16),
        compiler_params=_params("arbitrary", "arbitrary"),
        name="attention",
    )(safe, qt, k, vt)


def _chunk_scan(x, pos, op, fill, reverse):
    rows = x.shape[0]
    s = 1
    while s < CHUNK:
        if reverse:
            shifted = jnp.where(pos < CHUNK - s, pltpu.roll(x, rows - s, 0), fill)
        else:
            shifted = jnp.where(pos >= s, pltpu.roll(x, s, 0), fill)
        x = op(x, shifted)
        s *= 2
    return x


def _chunk_pos(rows):
    return lax.broadcasted_iota(jnp.int32, (rows, LANES), 0) % CHUNK


def _chunk_row(a, reverse_dir, idx_fwd, idx_bwd):
    rows = a.shape[0]
    a3 = a.reshape(rows // CHUNK, CHUNK, LANES)
    i = idx_bwd if reverse_dir else idx_fwd
    return jnp.broadcast_to(a3[:, i:i + 1, :], a3.shape).reshape(rows, LANES)


def _head_rms(o, gain):
    lane = lax.broadcasted_iota(jnp.int32, o.shape, 1)
    lo = lane < HEAD_DIM
    sq = o * o
    s_lo = jnp.sum(jnp.where(lo, sq, 0.0), axis=-1, keepdims=True)
    s_hi = jnp.sum(jnp.where(lo, 0.0, sq), axis=-1, keepdims=True)
    ms = jnp.where(lo, s_lo, s_hi) * (1.0 / HEAD_DIM)
    return o * lax.rsqrt(ms + EPS) * gain


PREP_ROWS = 512


def _gla_kernel(q_ref, k_ref, v_ref, g_ref, small_ref, wdh_ref, wdl_ref, bd_ref, gain_ref,
                o_ref, qe_s, ke_s, kd_s, qb_s, dec_s, of_s, ob_s):
    T = q_ref.shape[1]
    nc = T // CHUNK
    R = min(PREP_ROWS, T)
    pos = _chunk_pos(R)

    def prep(t, _):
        r0 = pl.multiple_of(t * R, R)
        sm = small_ref[0, pl.ds(r0, R), :]
        q = q_ref[0, pl.ds(r0, R), :] * HEAD_DIM ** -0.5
        k = k_ref[0, pl.ds(r0, R), :]
        for d in range(2):
            la = _log_sigmoid(_dot3(sm, wdh_ref[d], wdl_ref[d]) + bd_ref[d]) * (1.0 / GLA_TAU)
            b = _chunk_scan(la, pos, jnp.add, 0.0, reverse=bool(d))
            b_mid = _chunk_row(b, d, CHUNK // 2 - 1, CHUNK // 2)
            b_last = _chunk_row(b, d, CHUNK - 1, 0)
            qe_s[d, pl.ds(r0, R), :] = (q * jnp.exp(b - b_mid)).astype(BF16)
            ke_s[d, pl.ds(r0, R), :] = (k * jnp.exp(b_mid - b)).astype(BF16)
            kd_s[d, pl.ds(r0, R), :] = (k * jnp.exp(b_last - b)).astype(BF16)
            qb_s[d, pl.ds(r0, R), :] = (q * jnp.exp(b)).astype(BF16)
            dec = jnp.exp(b_last).reshape(R // CHUNK, CHUNK, LANES)[:, :SUBLANES, :]
            dec_s[d, pl.ds(pl.multiple_of(t * (R // CHUNK) * SUBLANES, SUBLANES), (R // CHUNK) * SUBLANES), :] = (
                dec.reshape((R // CHUNK) * SUBLANES, LANES))
        return 0

    lax.fori_loop(0, T // R, prep, 0)

    ci = lax.broadcasted_iota(jnp.int32, (CHUNK, CHUNK), 0)
    si = lax.broadcasted_iota(jnp.int32, (CHUNK, CHUNK), 1)
    masks = (ci >= si, ci <= si)
    outs = (of_s, ob_s)

    def step(n, states):
        new_states = []
        for d in range(2):
            c = n if d == 0 else nc - 1 - n
            r0 = pl.multiple_of(c * CHUNK, CHUNK)
            qe = qe_s[d, pl.ds(r0, CHUNK), :]
            ke = ke_s[d, pl.ds(r0, CHUNK), :]
            kd = kd_s[d, pl.ds(r0, CHUNK), :]
            qb = qb_s[d, pl.ds(r0, CHUNK), :]
            v = v_ref[0, pl.ds(r0, CHUNK), :].astype(BF16)
            dec = dec_s[d, pl.ds(pl.multiple_of(c * SUBLANES, SUBLANES), SUBLANES), :][0:1, :]
            st = states[d]
            o_heads, kv_heads = [], []
            for hh in range(2):
                sl = slice(hh * HEAD_DIM, (hh + 1) * HEAD_DIM)
                a = jnp.where(masks[d], _dot_nt(qe[:, sl], ke[:, sl]), 0.0).astype(BF16)
                o_heads.append(_dot(a, v[:, sl]) + _dot_nt(qb[:, sl], st[:, sl].astype(BF16)))
                kv_heads.append(_dot_tn(v[:, sl], kd[:, sl]))
            outs[d][pl.ds(r0, CHUNK), :] = jnp.concatenate(o_heads, axis=1)
            new_states.append(st * dec + jnp.concatenate(kv_heads, axis=1))
        return tuple(new_states)

    zero = jnp.zeros((HEAD_DIM, LANES), F32)
    lax.fori_loop(0, nc, step, (zero, zero))

    def post(t, _):
        r0 = pl.multiple_of(t * R, R)
        o = _head_rms(of_s[pl.ds(r0, R), :] + ob_s[pl.ds(r0, R), :], gain_ref[...])
        g = g_ref[0, pl.ds(r0, R), :]
        o_ref[0, pl.ds(r0, R), :] = (o * (g * _sigmoid(g))).astype(BF16)
        return 0

    lax.fori_loop(0, T // R, post, 0)


def _gla(gla, small, lp):
    B, T, _ = gla.shape
    pairs = GLA_WIDTH // LANES
    col = lambda off: (lambda b, hp: (b, 0, off * pairs + hp))
    return pl.pallas_call(
        _gla_kernel,
        grid=(B, pairs),
        in_specs=[
            pl.BlockSpec((1, T, LANES), col(0)),
            pl.BlockSpec((1, T, LANES), col(1)),
            pl.BlockSpec((1, T, LANES), col(2)),
            pl.BlockSpec((1, T, LANES), col(3)),
            pl.BlockSpec((1, T, LANES), lambda b, hp: (b, 0, 0)),
            pl.BlockSpec((2, LANES, LANES), lambda b, hp: (0, 0, hp)),
            pl.BlockSpec((2, LANES, LANES), lambda b, hp: (0, 0, hp)),
            pl.BlockSpec((2, 1, LANES), lambda b, hp: (0, 0, hp)),
            pl.BlockSpec((1, LANES), lambda b, hp: (0, 0)),
        ],
        out_specs=pl.BlockSpec((1, T, LANES), lambda b, hp: (b, 0, hp)),
        out_shape=jax.ShapeDtypeStruct((B, T, GLA_WIDTH), BF16),
        scratch_shapes=[
            pltpu.VMEM((2, T, LANES), BF16), pltpu.VMEM((2, T, LANES), BF16),
            pltpu.VMEM((2, T, LANES), BF16), pltpu.VMEM((2, T, LANES), BF16),
            pltpu.VMEM((2, (T // CHUNK) * SUBLANES, LANES), F32),
            pltpu.VMEM((T, LANES), F32), pltpu.VMEM((T, LANES), F32),
        ],
        compiler_params=_params("arbitrary", "arbitrary"),
        name="gla",
    )(gla, gla, gla, gla, small, lp["wd_hi"], lp["wd_lo"], lp["bd"], lp["gla_gain"])


GATE_I = SMALL_GATE_LANE
GATE_F = SMALL_GATE_LANE + MLSTM_HEADS


def _gate_lane(d, hh):
    return GATE_F + 2 * MLSTM_HEADS * d + hh


def _mlstm_kernel(q_ref, k_ref, v_ref, og_ref, small_ref, wq_ref, wk_ref, bq_ref, bk_ref,
                  gbias_ref, gain_ref, o_ref, q_s, k_s, b_s, r_s, cm_s, rt_s, of_s, ob_s):
    T = q_ref.shape[1]
    nc = T // CHUNK
    R = min(PREP_ROWS, T)
    hp = pl.program_id(1)

    row = lax.broadcasted_iota(jnp.int32, (T, LANES), 0)
    for src, w_ref, b_ref, dst, scale in ((q_ref, wq_ref, bq_ref, q_s, 1.0),
                                          (k_ref, wk_ref, bk_ref, k_s, HEAD_DIM ** -0.5)):
        xc = src[0]
        prev = jnp.where(row >= 1, pltpu.roll(xc, 1, 0), 0.0)
        nxt = jnp.where(row < T - 1, pltpu.roll(xc, T - 1, 0), 0.0)
        y = prev * w_ref[0:1, :] + xc * w_ref[1:2, :] + nxt * w_ref[2:3, :] + b_ref[...]
        dst[...] = (y * _sigmoid(y) * scale).astype(BF16)

    pos = _chunk_pos(R)
    lane = lax.broadcasted_iota(jnp.int32, (R, LANES), 1)
    is_bwd = lane >= GATE_I + 2 * MLSTM_HEADS
    heads_per_pair = LANES // HEAD_DIM
    shift = (LANES - heads_per_pair * hp) % LANES

    def prep(t, _):
        r0 = pl.multiple_of(t * R, R)
        gc = pltpu.roll(small_ref[0, pl.ds(r0, R), :] + gbias_ref[...], shift, 1)
        logf = _log_sigmoid(gc)
        b = jnp.where(is_bwd, _chunk_scan(logf, pos, jnp.add, 0.0, True),
                      _chunk_scan(logf, pos, jnp.add, 0.0, False))
        r = pltpu.roll(gc, MLSTM_HEADS, 1) - b
        cm = jnp.where(is_bwd, _chunk_scan(r, pos, jnp.maximum, NEG, True),
                       _chunk_scan(r, pos, jnp.maximum, NEG, False))
        b_s[pl.ds(r0, R), :] = b
        r_s[pl.ds(r0, R), :] = r
        cm_s[pl.ds(r0, R), :] = cm
        rt_s[:, pl.ds(r0, R)] = r.T
        return 0

    lax.fori_loop(0, T // R, prep, 0)

    ci = lax.broadcasted_iota(jnp.int32, (CHUNK, CHUNK), 0)
    si = lax.broadcasted_iota(jnp.int32, (CHUNK, CHUNK), 1)
    masks = (ci >= si, ci <= si)
    one_col = (si == 0).astype(BF16)
    outs = (of_s, ob_s)

    def chunk(d, c, half, rt, m_row, states):
        r0 = pl.multiple_of(c * CHUNK, CHUNK)
        last = 0 if d else CHUNK - 1
        b_t = b_s[pl.ds(r0, CHUNK), :]
        r_t = r_s[pl.ds(r0, CHUNK), :]
        cm_t = cm_s[pl.ds(r0, CHUNK), :]
        m_t = jnp.maximum(b_t + m_row, b_t + cm_t)
        w_inter_t = jnp.exp(b_t + m_row - m_t)
        e_t = b_t - m_t
        floor_t = jnp.exp(-m_t)
        bl = b_t[last:last + 1, :]
        m_new = bl + jnp.maximum(m_row, cm_t[last:last + 1, :])
        w_prev = jnp.exp(bl + m_row - m_new)
        wk_t = jnp.exp(bl + r_t - m_new)
        q = q_s[pl.ds(r0, CHUNK), :]
        k = k_s[pl.ds(r0, CHUNK), :]
        v = v_ref[0, pl.ds(r0, CHUNK), :].astype(BF16)
        o_heads, new_states = [], []
        for hh in range(heads_per_pair):
            sl = slice(hh * HEAD_DIM, (hh + 1) * HEAD_DIM)
            L = _gate_lane(d, hh)
            r_row = rt[L:L + 1, half * CHUNK:(half + 1) * CHUNK]
            dmat = jnp.where(masks[d], jnp.exp(e_t[:, L:L + 1] + r_row), 0.0)
            smat = (_dot_nt(q[:, sl], k[:, sl]) * dmat).astype(BF16)
            v_aug = jnp.concatenate([v[:, sl], one_col], axis=1)
            st = states[hh]
            num = w_inter_t[:, L:L + 1] * _dot(q[:, sl], st.astype(BF16)) + _dot(smat, v_aug)
            den = jnp.maximum(jnp.abs(num[:, HEAD_DIM:HEAD_DIM + 1]), floor_t[:, L:L + 1])
            o_heads.append(num[:, :HEAD_DIM] / den)
            kw = (k[:, sl].astype(F32) * wk_t[:, L:L + 1]).astype(BF16)
            new_states.append(w_prev[:, L:L + 1] * st + _dot_tn(kw, v_aug))
        outs[d][pl.ds(r0, CHUNK), :] = jnp.concatenate(o_heads, axis=1)
        return m_new, tuple(new_states)

    def step(i, carry):
        m_rows, states = carry
        new_m, new_states = [], []
        for d in range(2):
            j = (nc // 2 - 1 - i) if d else i
            rt = rt_s[:, pl.ds(pl.multiple_of(j * LANES, LANES), LANES)]
            m_row, st = m_rows[d], states[d]
            for half in ((1, 0) if d else (0, 1)):
                m_row, st = chunk(d, 2 * j + half, half, rt, m_row, st)
            new_m.append(m_row)
            new_states.append(st)
        return tuple(new_m), tuple(new_states)

    zero = jnp.zeros((HEAD_DIM, LANES), F32)
    m0 = jnp.full((1, LANES), NEG, F32)
    lax.fori_loop(0, nc // 2, step, ((m0, m0), ((zero, zero), (zero, zero))))

    def post(t, _):
        r0 = pl.multiple_of(t * R, R)
        h = _head_rms(of_s[pl.ds(r0, R), :] + ob_s[pl.ds(r0, R), :], gain_ref[...])
        o_ref[0, pl.ds(r0, R), :] = (h * _sigmoid(og_ref[0, pl.ds(r0, R), :])).astype(BF16)
        return 0

    lax.fori_loop(0, T // R, post, 0)


def _mlstm(ml, small, lp):
    B, T, _ = ml.shape
    pairs = MLSTM_WIDTH // LANES
    col = lambda off: (lambda b, hp: (b, 0, off * pairs + hp))
    const = lambda b, hp: (0, 0)
    return pl.pallas_call(
        _mlstm_kernel,
        grid=(B, pairs),
        in_specs=[
            pl.BlockSpec((1, T, LANES), col(0)),
            pl.BlockSpec((1, T, LANES), col(1)),
            pl.BlockSpec((1, T, LANES), col(2)),
            pl.BlockSpec((1, T, LANES), col(3)),
            pl.BlockSpec((1, T, LANES), lambda b, hp: (b, 0, 0)),
            pl.BlockSpec((3, LANES), lambda b, hp: (0, hp)),
            pl.BlockSpec((3, LANES), lambda b, hp: (0, pairs + hp)),
            pl.BlockSpec((1, LANES), lambda b, hp: (0, hp)),
            pl.BlockSpec((1, LANES), lambda b, hp: (0, pairs + hp)),
            pl.BlockSpec((1, LANES), const),
            pl.BlockSpec((1, LANES), const),
        ],
        out_specs=pl.BlockSpec((1, T, LANES), lambda b, hp: (b, 0, hp)),
        out_shape=jax.ShapeDtypeStruct((B, T, MLSTM_WIDTH), BF16),
        scratch_shapes=[
            pltpu.VMEM((T, LANES), BF16), pltpu.VMEM((T, LANES), BF16),
            pltpu.VMEM((T, LANES), F32), pltpu.VMEM((T, LANES), F32), pltpu.VMEM((T, LANES), F32),
            pltpu.VMEM((LANES, T), F32),
            pltpu.VMEM((T, LANES), F32), pltpu.VMEM((T, LANES), F32),
        ],
        compiler_params=_params("arbitrary", "arbitrary"),
        name="mlstm",
    )(ml, ml, ml, ml, small, lp["conv_w"], lp["conv_w"], lp["conv_b"], lp["conv_b"],
      lp["gate_bias"], lp["ml_gain"])


def _first_argmax(vals, lane):
    mx = jnp.max(vals, axis=-1, keepdims=True)
    idx = jnp.min(jnp.where(vals == mx, lane, LANES), axis=-1, keepdims=True)
    return mx, idx


def _out_route_kernel(x_ref, a_ref, gl_ref, ml_ref, w_ref, g_ref, wrh_ref, wrl_ref, br_ref,
                      x1_ref, t_ref, route_ref, routet_ref):
    tm = x_ref.shape[0]
    x1 = (x_ref[...]
          + _dot(a_ref[...], w_ref[0:ATTN_WIDTH, :])
          + _dot(gl_ref[...], w_ref[ATTN_WIDTH:ATTN_WIDTH + GLA_WIDTH, :])
          + _dot(ml_ref[...], w_ref[ATTN_WIDTH + GLA_WIDTH:, :]))
    x1_ref[...] = x1
    t = _rms(x1, g_ref[...])
    t_ref[...] = t.astype(BF16)
    logits = _dot3(t, wrh_ref[...], wrl_ref[...]) + br_ref[...]
    lane = lax.broadcasted_iota(jnp.int32, (tm, LANES), 1)
    gl = jnp.where(lane < N_GROUPS, logits, -jnp.inf)
    gmax, gi = _first_argmax(gl, lane)
    g_prob = 1.0 / jnp.sum(jnp.exp(gl - gmax), axis=-1, keepdims=True)
    lo = ROUTE_W_LANE + gi * EXPERTS_PER_GROUP
    el = jnp.where((lane >= lo) & (lane < lo + EXPERTS_PER_GROUP), logits, -jnp.inf)
    v1, i1 = _first_argmax(el, lane)
    v2, i2 = _first_argmax(jnp.where(lane == i1, -jnp.inf, el), lane)
    e2 = jnp.exp(v2 - v1)
    w1 = g_prob / (1.0 + e2)
    w2 = g_prob * e2 / (1.0 + e2)
    comb = jnp.where(lane == i1, w1, jnp.where(lane == i2, w2, 0.0))
    route = jnp.where(lane == 0, gi.astype(F32), comb)
    route_ref[...] = route
    routet_ref[...] = route.T[0:SUBLANES, :]


def _out_route(x2, attn, gla_o, ml_o, lp, tm=256):
    N, D = x2.shape
    const = lambda i: (0, 0)
    tok = lambda i: (i, 0)
    return pl.pallas_call(
        _out_route_kernel,
        grid=(N // tm,),
        in_specs=[
            pl.BlockSpec((tm, D), tok),
            pl.BlockSpec((tm, ATTN_WIDTH), tok),
            pl.BlockSpec((tm, GLA_WIDTH), tok),
            pl.BlockSpec((tm, MLSTM_WIDTH), tok),
            pl.BlockSpec((ATTN_WIDTH + GLA_WIDTH + MLSTM_WIDTH, D), const),
            pl.BlockSpec((1, D), const),
            pl.BlockSpec((D, LANES), const),
            pl.BlockSpec((D, LANES), const),
            pl.BlockSpec((1, LANES), const),
        ],
        out_specs=[
            pl.BlockSpec((tm, D), tok),
            pl.BlockSpec((tm, D), tok),
            pl.BlockSpec((tm, LANES), tok),
            pl.BlockSpec((SUBLANES, tm), lambda i: (0, i)),
        ],
        out_shape=[
            jax.ShapeDtypeStruct((N, D), F32),
            jax.ShapeDtypeStruct((N, D), BF16),
            jax.ShapeDtypeStruct((N, LANES), F32),
            jax.ShapeDtypeStruct((SUBLANES, N), F32),
        ],
        compiler_params=_params("arbitrary"),
        name="out_route",
    )(x2, attn, gla_o, ml_o, lp["w_out"], lp["g_ffn"], lp["wr_hi"], lp["wr_lo"], lp["b_route"])


def _dispatch_kernel(t_ref, route_ref, routet_ref, xs_ref, cws_ref, pos_ref, cnt_ref):
    nb = ROUTE_BLOCK
    gi_row = routet_ref[0:1, :]
    sub = lax.broadcasted_iota(jnp.int32, (SUBLANES, nb), 0).astype(F32)
    onehot = (sub == gi_row)
    ri = lax.broadcasted_iota(jnp.int32, (nb, nb), 0)
    cj = lax.broadcasted_iota(jnp.int32, (nb, nb), 1)
    before = (ri < cj).astype(BF16)
    rank = _dot(onehot.astype(BF16), before)
    counts = jnp.broadcast_to(jnp.sum(onehot.astype(F32), axis=-1, keepdims=True), (SUBLANES, LANES))
    padded = jnp.ceil(counts * (1.0 / ROUTE_TILE)) * ROUTE_TILE
    srow = lax.broadcasted_iota(jnp.int32, (SUBLANES, LANES), 0)
    incl = padded
    s = 1
    while s < SUBLANES:
        incl = incl + jnp.where(srow >= s, pltpu.roll(incl, s, 0), 0.0)
        s *= 2
    start = incl - padded
    pos_row = jnp.sum(jnp.where(onehot, start[:, 0:1] + rank, 0.0), axis=0, keepdims=True)
    cnt_ref[0] = counts.astype(jnp.int32)

    route = route_ref[...]
    lane = lax.broadcasted_iota(jnp.int32, (nb, LANES), 1)
    onehot_c = lane.astype(F32) == route[:, 0:1]
    after = (cj < ri).astype(BF16)
    rank_c = _dot(after, onehot_c.astype(BF16))
    slane = lax.broadcasted_iota(jnp.int32, (SUBLANES, LANES), 1)
    start_c = jnp.sum(jnp.where(srow == slane, start, 0.0), axis=0, keepdims=True)
    pos_col = jnp.sum(jnp.where(onehot_c, start_c + rank_c, 0.0), axis=-1, keepdims=True)
    pos_ref[...] = jnp.broadcast_to(pos_col, (nb, LANES))

    comb = jnp.where(lane >= ROUTE_W_LANE, route, 0.0)
    c_hi, c_lo = _split(comb)
    c_lo2 = (comb - c_hi.astype(F32) - c_lo.astype(F32)).astype(BF16)
    cw = (c_hi.astype(F32) + pltpu.roll(c_lo.astype(F32), ROUTE_LO_SHIFT, 1)
          + pltpu.roll(c_lo2.astype(F32), 2 * ROUTE_LO_SHIFT, 1)).astype(BF16)
    tb = t_ref[...]
    for r in range(TILES_PER_BLOCK):
        rows = (lax.broadcasted_iota(jnp.int32, (ROUTE_TILE, nb), 0) + r * ROUTE_TILE).astype(F32)
        perm = (rows == pos_row).astype(BF16)
        xs_ref[r * ROUTE_TILE:(r + 1) * ROUTE_TILE, :] = _dot(perm, tb).astype(BF16)
        cws_ref[r * ROUTE_TILE:(r + 1) * ROUTE_TILE, :] = _dot(perm, cw)


def _dispatch(t, route, routet):
    N, D = t.shape
    nblk = N // ROUTE_BLOCK
    return pl.pallas_call(
        _dispatch_kernel,
        grid=(nblk,),
        in_specs=[
            pl.BlockSpec((ROUTE_BLOCK, D), lambda i: (i, 0)),
            pl.BlockSpec((ROUTE_BLOCK, LANES), lambda i: (i, 0)),
            pl.BlockSpec((SUBLANES, ROUTE_BLOCK), lambda i: (0, i)),
        ],
        out_specs=[
            pl.BlockSpec((ROUTE_ROWS, D), lambda i: (i, 0)),
            pl.BlockSpec((ROUTE_ROWS, LANES), lambda i: (i, 0)),
            pl.BlockSpec((ROUTE_BLOCK, LANES), lambda i: (i, 0)),
            pl.BlockSpec((1, SUBLANES, LANES), lambda i: (i, 0, 0)),
        ],
        out_shape=[
            jax.ShapeDtypeStruct((nblk * ROUTE_ROWS, D), BF16),
            jax.ShapeDtypeStruct((nblk * ROUTE_ROWS, LANES), F32),
            jax.ShapeDtypeStruct((N, LANES), F32),
            jax.ShapeDtypeStruct((nblk, SUBLANES, LANES), jnp.int32),
        ],
        compiler_params=_params("arbitrary"),
        name="dispatch",
    )(t, route, routet)


def _tile_schedule(cnt):
    nblk = cnt.shape[0]
    ntile = (cnt + ROUTE_TILE - 1) // ROUTE_TILE
    end = jnp.cumsum(ntile, axis=1)
    r = jnp.arange(TILES_PER_BLOCK, dtype=jnp.int32)
    grp = jnp.sum(r[None, :, None] >= end[:, None, :], axis=-1)
    grp = grp.reshape(-1).astype(jnp.int32)
    tile = jnp.arange(nblk * TILES_PER_BLOCK, dtype=jnp.int32)
    order = jnp.argsort(grp * (nblk * TILES_PER_BLOCK) + tile).astype(jnp.int32)
    n_active = jnp.sum(grp < N_GROUPS).astype(jnp.int32)
    g_sorted = grp[order]
    last_group = g_sorted[jnp.maximum(n_active - 1, 0)]
    g_sorted = jnp.where(g_sorted < N_GROUPS, g_sorted, last_group)
    return order, g_sorted, n_active[None]


def _expert_kernel(trow_ref, tgrp_ref, nact_ref, xs_ref, cws_ref, wg_ref, wu_ref, wd_ref, ys_ref):
    i = pl.program_id(0)

    @pl.when(i < nact_ref[0])
    def _():
        x = xs_ref[...]
        cws = cws_ref[...]
        lane = lax.broadcasted_iota(jnp.int32, cws.shape, 1)
        first = ROUTE_W_LANE + tgrp_ref[i] * EXPERTS_PER_GROUP
        y = jnp.zeros(ys_ref.shape, F32)
        for j in range(EXPERTS_PER_GROUP):
            off = lane - (first + j)
            sel = (off == 0) | (off == ROUTE_LO_SHIFT) | (off == 2 * ROUTE_LO_SHIFT)
            wj = jnp.sum(jnp.where(sel, cws, 0.0), axis=-1, keepdims=True)
            h1 = _dot(x, wg_ref[j])
            a = (h1 * _sigmoid(h1) * _dot(x, wu_ref[j])).astype(BF16)
            y = y + wj * _dot(a, wd_ref[j])
        ys_ref[...] = y.astype(BF16)

    @pl.when(i >= nact_ref[0])
    def _():
        ys_ref[...] = jnp.zeros(ys_ref.shape, BF16)


def _experts(xs, cws, order, grp, n_active, lp):
    rows, D = xs.shape
    n_tiles = rows // ROUTE_TILE
    tile = lambda i, trow, tgrp, nact: (trow[i], 0)
    wsel = lambda i, trow, tgrp, nact: (tgrp[i], 0, 0)
    return pl.pallas_call(
        _expert_kernel,
        grid_spec=pltpu.PrefetchScalarGridSpec(
            num_scalar_prefetch=3,
            grid=(n_tiles,),
            in_specs=[
                pl.BlockSpec((ROUTE_TILE, D), tile),
                pl.BlockSpec((ROUTE_TILE, LANES), tile),
                pl.BlockSpec((EXPERTS_PER_GROUP, D, D_EXPERT), wsel),
                pl.BlockSpec((EXPERTS_PER_GROUP, D, D_EXPERT), wsel),
                pl.BlockSpec((EXPERTS_PER_GROUP, D_EXPERT, D), wsel),
            ],
            out_specs=pl.BlockSpec((ROUTE_TILE, D), tile),
        ),
        out_shape=jax.ShapeDtypeStruct((rows, D), BF16),
        compiler_params=_params("arbitrary"),
        name="experts",
    )(order, grp, n_active, xs, cws, lp["w_gate"], lp["w_up"], lp["w_down"])


def _combine_kernel(ys_ref, pos_ref, x1_ref, p_ref, g_ref, wpg_ref, wpp_ref, gfin_ref, o_ref,
                    *, embed, final):
    tm = x1_ref.shape[0]
    pos = pos_ref[...]
    lane = lax.broadcasted_iota(jnp.int32, (tm, LANES), 1).astype(F32)
    perm_t = jnp.concatenate(
        [(lane + r * LANES == pos).astype(BF16) for r in range(ROUTE_ROWS // LANES)], axis=1)
    x = x1_ref[...] + _dot(perm_t, ys_ref[...])
    if embed:
        gate = _sigmoid(_dot(_rms(x, g_ref[...]).astype(BF16), wpg_ref[...]))
        x = x + gate * _dot(p_ref[...].astype(BF16), wpp_ref[...])
    if final:
        x = _rms(x, gfin_ref[...])
    o_ref[...] = x


def _combine(ys, pos, x1, p2, lp, g_final, embed=True, final=False, tm=256):
    N, D = x1.shape
    inner = ROUTE_BLOCK // tm
    tok = lambda b, i: (b * inner + i, 0)
    const = lambda b, i: (0, 0)
    return pl.pallas_call(
        functools.partial(_combine_kernel, embed=embed, final=final),
        grid=(N // ROUTE_BLOCK, inner),
        in_specs=[
            pl.BlockSpec((ROUTE_ROWS, D), lambda b, i: (b, 0)),
            pl.BlockSpec((tm, LANES), tok),
            pl.BlockSpec((tm, D), tok),
            pl.BlockSpec((tm, p2.shape[1]), tok),
            pl.BlockSpec((1, D), const),
            pl.BlockSpec((D, D), const),
            pl.BlockSpec((p2.shape[1], D), const),
            pl.BlockSpec((1, D), const),
        ],
        out_specs=pl.BlockSpec((tm, D), tok),
        out_shape=jax.ShapeDtypeStruct((N, D), F32),
        compiler_params=_params("arbitrary", "arbitrary"),
        name="combine",
    )(ys, pos, x1, p2, lp["g_ple"], lp["w_pg"], lp["w_pp"], g_final)


def _moe(t, route, routet, lp):
    xs, cws, pos, cnt = _dispatch(t, route, routet)
    order, grp, n_active = _tile_schedule(cnt[:, :N_GROUPS, 0])
    return _experts(xs, cws, order, grp, n_active, lp), pos


def _moe_debug(x2, lp):
    N, D = x2.shape
    zeros = lambda w: jnp.zeros((N, w), BF16)
    x1, t, route, routet = _out_route(x2, zeros(ATTN_WIDTH), zeros(GLA_WIDTH), zeros(MLSTM_WIDTH), lp)
    ys, pos = _moe(t, route, routet, lp)
    p2 = jnp.zeros((N, lp["w_pp"].shape[0]), F32)
    return _combine(ys, pos, x1, p2, lp, lp["g_ple"], embed=False) - x1


def _in_perm():
    sizes = (ATTN_WIDTH, KV_WIDTH, KV_WIDTH, GLA_WIDTH, GLA_WIDTH, GLA_WIDTH, GLA_WIDTH,
             2 * GLA_RANK, MLSTM_WIDTH, MLSTM_WIDTH, MLSTM_WIDTH, MLSTM_WIDTH, 4 * MLSTM_HEADS)
    offs = np.concatenate([[0], np.cumsum(sizes)])
    seg = lambda i: np.arange(offs[i], offs[i + 1])
    order = [0, 1, 2, 3, 4, 5, 6, 8, 9, 10, 11, 7, 12]
    return np.concatenate([seg(i) for i in order])


def _rope_tables(T):
    t = jnp.arange(T, dtype=F32)
    row = jnp.floor(t / GRID_W)
    col = t - row * GRID_W
    inv = ROPE_THETA ** (-jnp.arange(0, ROPE_AXIS_DIM, 2, dtype=F32) / ROPE_AXIS_DIM)
    ang_r = row[:, None] * inv[None, :]
    ang_c = col[:, None] * inv[None, :]
    cos_h = jnp.concatenate([jnp.cos(ang_r), jnp.cos(ang_r), jnp.cos(ang_c), jnp.cos(ang_c)], axis=1)
    sin_h = jnp.concatenate([-jnp.sin(ang_r), jnp.sin(ang_r), -jnp.sin(ang_c), jnp.sin(ang_c)], axis=1)
    reps = LANES // HEAD_DIM
    return jnp.tile(cos_h, (1, reps)), jnp.tile(sin_h, (1, reps))


def _head_onehots():
    n_heads = QK_WIDTH // HEAD_DIM
    eh = np.zeros((QK_WIDTH, LANES), np.float32)
    eh[np.arange(QK_WIDTH), np.arange(QK_WIDTH) // HEAD_DIM] = 1.0
    assert n_heads <= LANES
    return jnp.asarray(eh, BF16), jnp.asarray(eh.T, BF16)


def kernel(x, p, norm_mix_g, w_in, attn_q_norm_g, attn_k_norm_g, gla_w_decay, gla_b_decay,
           gla_out_norm_g, mlstm_conv_w, mlstm_conv_b, mlstm_b_input, mlstm_b_forget,
           mlstm_out_norm_g, w_out, norm_ffn_g, w_group, b_group, w_router, b_router,
           w_expert_gate, w_expert_up, w_expert_down, norm_ple_g, w_ple_gate, w_ple_proj,
           final_norm_g):
    params = dict(
        norm_mix_g=norm_mix_g, w_in=w_in, attn_q_norm_g=attn_q_norm_g, attn_k_norm_g=attn_k_norm_g,
        gla_w_decay=gla_w_decay, gla_b_decay=gla_b_decay, gla_out_norm_g=gla_out_norm_g,
        mlstm_conv_w=mlstm_conv_w, mlstm_conv_b=mlstm_conv_b, mlstm_b_input=mlstm_b_input,
        mlstm_b_forget=mlstm_b_forget, mlstm_out_norm_g=mlstm_out_norm_g, w_out=w_out,
        norm_ffn_g=norm_ffn_g, w_group=w_group, b_group=b_group, w_router=w_router,
        b_router=b_router, w_expert_gate=w_expert_gate, w_expert_up=w_expert_up,
        w_expert_down=w_expert_down, norm_ple_g=norm_ple_g, w_ple_gate=w_ple_gate,
        w_ple_proj=w_ple_proj)
    B, T, D = x.shape
    cos, sin = _rope_tables(T)
    eh, eht = _head_onehots()
    N = B * T
    depth = w_in.shape[0]
    x2 = x.reshape(N, D)
    g_final = final_norm_g[None, :]
    for i in range(depth):
        lp = _layer_params(params, i)
        qt, k, vt, gla, ml, small = _in_proj(x2, lp["g_mix"], lp["w_in"], cos, sin, lp["qk_gain"],
                                             eh, eht, B, T)
        attn = _attention(lp["attn_safe"], qt, k, vt).reshape(N, ATTN_WIDTH)
        gla_o = _gla(gla, small, lp).reshape(N, GLA_WIDTH)
        ml_o = _mlstm(ml, small, lp).reshape(N, MLSTM_WIDTH)
        x1, t, route, routet = _out_route(x2, attn, gla_o, ml_o, lp)
        ys, pos = _moe(t, route, routet, lp)
        x2 = _combine(ys, pos, x1, p[i].reshape(N, -1), lp, g_final, final=(i == depth - 1))
    return x2.reshape(B, T, D)


def _split_w(w):
    hi = w.astype(BF16)
    return hi, (w - hi.astype(F32)).astype(BF16)


def _layer_params(p, i):
    D = p["w_in"].shape[1]
    w_in = jnp.pad(p["w_in"][i][:, _in_perm()], ((0, 0), (0, IN_PERM_WIDTH - p["w_in"].shape[2])))
    gq, gk = p["attn_q_norm_g"][i], p["attn_k_norm_g"][i]
    qk_gain = jnp.concatenate([jnp.tile(gq, ATTN_HEADS) * (HEAD_DIM ** -0.5 * LOG2E),
                               jnp.tile(gk, ATTN_KV_HEADS)])
    logit_bound = HEAD_DIM ** 0.5 * jnp.max(jnp.abs(gq)) * jnp.max(jnp.abs(gk))
    attn_safe = (logit_bound <= ATTN_SAFE_LOGIT).astype(jnp.int32)[None]
    wd = jnp.zeros((2, LANES, GLA_WIDTH), F32)
    wd = wd.at[0, :GLA_RANK].set(p["gla_w_decay"][i, 0]).at[1, GLA_RANK:2 * GLA_RANK].set(p["gla_w_decay"][i, 1])
    wd_hi, wd_lo = _split_w(wd)
    gate_bias = jnp.zeros((LANES,), F32).at[SMALL_GATE_LANE:SMALL_GATE_LANE + 4 * MLSTM_HEADS].set(
        jnp.concatenate([p["mlstm_b_input"][i, 0], p["mlstm_b_forget"][i, 0],
                         p["mlstm_b_input"][i, 1], p["mlstm_b_forget"][i, 1]]))
    w_route = jnp.zeros((D, LANES), F32)
    w_route = w_route.at[:, :N_GROUPS].set(p["w_group"][i])
    w_route = w_route.at[:, ROUTE_W_LANE:ROUTE_W_LANE + N_EXPERTS].set(p["w_router"][i])
    wr_hi, wr_lo = _split_w(w_route)
    b_route = jnp.zeros((LANES,), F32).at[:N_GROUPS].set(p["b_group"][i])
    b_route = b_route.at[ROUTE_W_LANE:ROUTE_W_LANE + N_EXPERTS].set(p["b_router"][i])
    return dict(
        g_mix=p["norm_mix_g"][i][None, :],
        w_in=w_in.astype(BF16),
        qk_gain=qk_gain[None, :],
        attn_safe=attn_safe,
        wd_hi=wd_hi, wd_lo=wd_lo,
        bd=p["gla_b_decay"][i][:, None, :],
        gla_gain=jnp.tile(p["gla_out_norm_g"][i], LANES // HEAD_DIM)[None, :],
        conv_w=p["mlstm_conv_w"][i],
        conv_b=p["mlstm_conv_b"][i][None, :],
        gate_bias=gate_bias[None, :],
        ml_gain=jnp.tile(p["mlstm_out_norm_g"][i], LANES // HEAD_DIM)[None, :],
        w_out=p["w_out"][i].astype(BF16),
        g_ffn=p["norm_ffn_g"][i][None, :],
        wr_hi=wr_hi, wr_lo=wr_lo, b_route=b_route[None, :],
        w_gate=p["w_expert_gate"][i].astype(BF16),
        w_up=p["w_expert_up"][i].astype(BF16),
        w_down=p["w_expert_down"][i].astype(BF16),
        g_ple=p["norm_ple_g"][i][None, :],
        w_pg=p["w_ple_gate"][i].astype(BF16),
        w_pp=p["w_ple_proj"][i].astype(BF16),
    )
```

```python
import functools

import jax
import jax.numpy as jnp
import numpy as np
from jax import lax
from jax.experimental import pallas as pl
from jax.experimental.pallas import tpu as pltpu

F32 = jnp.float32
BF16 = jnp.bfloat16

GRID_W = 64
HEAD_DIM = 64
ATTN_HEADS = 8
ATTN_KV_HEADS = 2
GLA_HEADS = 4
MLSTM_HEADS = 4
ATTN_WIDTH = ATTN_HEADS * HEAD_DIM
KV_WIDTH = ATTN_KV_HEADS * HEAD_DIM
GLA_WIDTH = GLA_HEADS * HEAD_DIM
MLSTM_WIDTH = MLSTM_HEADS * HEAD_DIM
GLA_RANK = 16
GLA_TAU = 16.0
CHUNK = 64
ROPE_THETA = 10000.0
ROPE_AXIS_DIM = HEAD_DIM // 2
N_GROUPS = 4
EXPERTS_PER_GROUP = 4
N_EXPERTS = N_GROUPS * EXPERTS_PER_GROUP
D_EXPERT = 512
EPS = 1e-6
NEG = -1e30

LANES = 128
SUBLANES = 8
VMEM_LIMIT_BYTES = 56 * 1024 * 1024

QK_WIDTH = ATTN_WIDTH + KV_WIDTH
OFF_V = QK_WIDTH
OFF_GLA = OFF_V + KV_WIDTH
OFF_ML = OFF_GLA + 4 * GLA_WIDTH
OFF_SMALL = OFF_ML + 4 * MLSTM_WIDTH
IN_PERM_WIDTH = OFF_SMALL + LANES
SMALL_GATE_LANE = 2 * GLA_RANK

ROUTE_BLOCK = 1024
ROUTE_TILE = 128
ROUTE_ROWS = ROUTE_BLOCK + N_GROUPS * ROUTE_TILE
TILES_PER_BLOCK = ROUTE_ROWS // ROUTE_TILE
ROUTE_W_LANE = 4
ROUTE_LO_SHIFT = 32


def _dot(a, b):
    return jnp.dot(a, b, preferred_element_type=F32)


def _dot_nt(a, b):
    return lax.dot_general(a, b, (((1,), (1,)), ((), ())), preferred_element_type=F32)


def _dot_tn(a, b):
    return lax.dot_general(a, b, (((0,), (0,)), ((), ())), preferred_element_type=F32)


def _split(a):
    hi = a.astype(BF16)
    lo = (a - hi.astype(F32)).astype(BF16)
    return hi, lo


def _dot3(a, w_hi, w_lo):
    a_hi, a_lo = _split(a)
    return _dot(a_hi, w_hi) + _dot(a_lo, w_hi) + _dot(a_hi, w_lo)


def _log_sigmoid(x):
    return jnp.minimum(x, 0.0) - jnp.log1p(jnp.exp(-jnp.abs(x)))


def _sigmoid(x):
    return 1.0 / (1.0 + jnp.exp(-x))


def _rms(x, g):
    return x * lax.rsqrt(jnp.mean(x * x, axis=-1, keepdims=True) + EPS) * g


def _params(*semantics):
    return pltpu.CompilerParams(dimension_semantics=semantics, vmem_limit_bytes=VMEM_LIMIT_BYTES)


def _in_proj_kernel(x_ref, g_ref, w_ref, cos_ref, sin_ref, gain_ref, eh_ref, eht_ref,
                    qt_ref, k_ref, vt_ref, gla_ref, ml_ref, small_ref):
    tm = x_ref.shape[0]
    h = _rms(x_ref[...], g_ref[...])
    z = _dot(h.astype(BF16), w_ref[...])
    qk = z[:, :QK_WIDTH]
    sq_hi, sq_lo = _split(qk * qk)
    ssq = _dot(sq_hi, eh_ref[...]) + _dot(sq_lo, eh_ref[...])
    inv_hi, inv_lo = _split(lax.rsqrt(ssq * (1.0 / HEAD_DIM) + EPS))
    scale = _dot(inv_hi, eht_ref[...]) + _dot(inv_lo, eht_ref[...])
    y = qk * scale * gain_ref[...]
    lane = lax.broadcasted_iota(jnp.int32, (tm, LANES), 1)
    first_half = (lane % ROPE_AXIS_DIM) < (ROPE_AXIS_DIM // 2)
    cos = cos_ref[...]
    sin = sin_ref[...]
    for c in range(QK_WIDTH // LANES):
        yc = y[:, c * LANES:(c + 1) * LANES]
        partner = jnp.where(first_half,
                            pltpu.roll(yc, LANES - ROPE_AXIS_DIM // 2, 1),
                            pltpu.roll(yc, ROPE_AXIS_DIM // 2, 1))
        rc = yc * cos + partner * sin
        if c < ATTN_WIDTH // LANES:
            qt_ref[0, c * LANES:(c + 1) * LANES, :] = rc.T.astype(BF16)
        else:
            k_ref[0] = rc.astype(BF16)
    vt_ref[0] = z[:, OFF_V:OFF_V + KV_WIDTH].T.astype(BF16)
    gla_ref[0] = z[:, OFF_GLA:OFF_ML]
    ml_ref[0] = z[:, OFF_ML:OFF_SMALL]
    small_ref[0] = z[:, OFF_SMALL:IN_PERM_WIDTH]


def _in_proj(x2, g, w_perm, cos, sin, gain, eh, eht, B, T, tm=256):
    N, D = x2.shape
    tpb = T // tm
    const = lambda i: (0, 0)
    tok3 = lambda i: (i // tpb, i % tpb, 0)
    tokT = lambda i: (i // tpb, 0, i % tpb)
    return pl.pallas_call(
        _in_proj_kernel,
        grid=(N // tm,),
        in_specs=[
            pl.BlockSpec((tm, D), lambda i: (i, 0)),
            pl.BlockSpec((1, D), const),
            pl.BlockSpec((D, IN_PERM_WIDTH), const),
            pl.BlockSpec((tm, LANES), lambda i: (i % tpb, 0)),
            pl.BlockSpec((tm, LANES), lambda i: (i % tpb, 0)),
            pl.BlockSpec((1, QK_WIDTH), const),
            pl.BlockSpec((QK_WIDTH, LANES), const),
            pl.BlockSpec((LANES, QK_WIDTH), const),
        ],
        out_specs=[
            pl.BlockSpec((1, ATTN_WIDTH, tm), tokT),
            pl.BlockSpec((1, tm, KV_WIDTH), tok3),
            pl.BlockSpec((1, KV_WIDTH, tm), tokT),
            pl.BlockSpec((1, tm, 4 * GLA_WIDTH), tok3),
            pl.BlockSpec((1, tm, 4 * MLSTM_WIDTH), tok3),
            pl.BlockSpec((1, tm, LANES), tok3),
        ],
        out_shape=[
            jax.ShapeDtypeStruct((B, ATTN_WIDTH, T), BF16),
            jax.ShapeDtypeStruct((B, T, KV_WIDTH), BF16),
            jax.ShapeDtypeStruct((B, KV_WIDTH, T), BF16),
            jax.ShapeDtypeStruct((B, T, 4 * GLA_WIDTH), F32),
            jax.ShapeDtypeStruct((B, T, 4 * MLSTM_WIDTH), F32),
            jax.ShapeDtypeStruct((B, T, LANES), F32),
        ],
        compiler_params=_params("arbitrary"),
        name="in_proj",
    )(x2, g, w_perm, cos, sin, gain, eh, eht)


ATTN_SAFE_LOGIT = 40.0
LOG2E = 1.4426950408889634


def _attn_kernel(safe_ref, qt_ref, k_ref, vt_ref, o_ref, *, tk):
    tq = qt_ref.shape[2]
    T = k_ref.shape[1]
    G = ATTN_HEADS // ATTN_KV_HEADS
    n = G * tq
    zeros = jnp.zeros((HEAD_DIM, n), BF16)

    def q_operand(j):
        base = j * G * HEAD_DIM
        qs = jnp.concatenate(
            [qt_ref[0, base + h * HEAD_DIM:base + (h + 1) * HEAD_DIM, :] for h in range(G)], axis=1)
        return jnp.concatenate([qs, zeros] if j == 0 else [zeros, qs], axis=0)

    def finish(j, acc, l):
        base = j * G * HEAD_DIM
        o = acc * (1.0 / l)
        ot = jnp.concatenate([o[:, h * tq:(h + 1) * tq] for h in range(G)], axis=0)
        o_ref[0, :, base:base + G * HEAD_DIM] = ot.T.astype(BF16)

    @pl.when(safe_ref[0] == 1)
    def _():
        qps = [q_operand(j) for j in range(ATTN_KV_HEADS)]
        units = [(c, j) for c in range(T // tk) for j in range(ATTN_KV_HEADS)]

        def scores(u):
            c, j = units[u]
            return _dot(k_ref[0, c * tk:(c + 1) * tk, :], qps[j])

        l8 = [jnp.zeros((SUBLANES, n), F32)] * ATTN_KV_HEADS
        acc = [jnp.zeros((HEAD_DIM, n), F32)] * ATTN_KV_HEADS
        st = scores(0)
        for u, (c, j) in enumerate(units):
            st_next = scores(u + 1) if u + 1 < len(units) else None
            p = jnp.exp2(st)
            l8[j] = l8[j] + jnp.sum(p.reshape(tk // SUBLANES, SUBLANES, n), axis=0)
            vc = vt_ref[0, j * HEAD_DIM:(j + 1) * HEAD_DIM, c * tk:(c + 1) * tk]
            acc[j] = acc[j] + _dot(vc, p.astype(BF16))
            st = st_next
        for j in range(ATTN_KV_HEADS):
            finish(j, acc[j], jnp.sum(l8[j], axis=0, keepdims=True))

    @pl.when(safe_ref[0] == 0)
    def _():
        for j in range(ATTN_KV_HEADS):
            qp = q_operand(j)

            def body(c, carry, qp=qp, j=j):
                m, l, acc = carry
                off = pl.multiple_of(c * tk, tk)
                st = _dot(k_ref[0, pl.ds(off, tk), :], qp)
                m_new = jnp.maximum(m, jnp.max(st, axis=0, keepdims=True))
                alpha = jnp.exp2(m - m_new)
                p = jnp.exp2(st - m_new)
                l = alpha * l + jnp.sum(p, axis=0, keepdims=True)
                vc = vt_ref[0, j * HEAD_DIM:(j + 1) * HEAD_DIM, pl.ds(off, tk)]
                return m_new, l, alpha * acc + _dot(vc, p.astype(BF16))

            init = (jnp.full((1, n), NEG, F32), jnp.zeros((1, n), F32), jnp.zeros((HEAD_DIM, n), F32))
            _, l, acc = lax.fori_loop(0, T // tk, body, init)
            finish(j, acc, l)


def _attention(safe, qt, k, vt, tq=128, tk=256):
    B, _, T = qt.shape
    tk = min(tk, T)
    return pl.pallas_call(
        functools.partial(_attn_kernel, tk=tk),
        grid_spec=pltpu.PrefetchScalarGridSpec(
            num_scalar_prefetch=1,
            grid=(B, T // tq),
            in_specs=[
                pl.BlockSpec((1, ATTN_WIDTH, tq), lambda b, i, s: (b, 0, i)),
                pl.BlockSpec((1, T, KV_WIDTH), lambda b, i, s: (b, 0, 0)),
                pl.BlockSpec((1, KV_WIDTH, T), lambda b, i, s: (b, 0, 0)),
            ],
            out_specs=pl.BlockSpec((1, tq, ATTN_WIDTH), lambda b, i, s: (b, i, 0)),
        ),
        out_shape=jax.ShapeDtypeStruct((B, T, ATTN_WIDTH), BF16),
        compiler_params=_params("arbitrary", "arbitrary"),
        name="attention",
    )(safe, qt, k, vt)


def _chunk_scan(x, pos, op, fill, reverse):
    rows = x.shape[0]
    s = 1
    while s < CHUNK:
        if reverse:
            shifted = jnp.where(pos < CHUNK - s, pltpu.roll(x, rows - s, 0), fill)
        else:
            shifted = jnp.where(pos >= s, pltpu.roll(x, s, 0), fill)
        x = op(x, shifted)
        s *= 2
    return x


def _chunk_pos(rows):
    return lax.broadcasted_iota(jnp.int32, (rows, LANES), 0) % CHUNK


def _chunk_row(a, reverse_dir, idx_fwd, idx_bwd):
    rows = a.shape[0]
    a3 = a.reshape(rows // CHUNK, CHUNK, LANES)
    i = idx_bwd if reverse_dir else idx_fwd
    return jnp.broadcast_to(a3[:, i:i + 1, :], a3.shape).reshape(rows, LANES)


def _stack_heads(x):
    lane = lax.broadcasted_iota(jnp.int32, x.shape, 1)
    zero = jnp.zeros_like(x)
    return jnp.concatenate([jnp.where(lane < HEAD_DIM, x, zero), jnp.where(lane >= HEAD_DIM, x, zero)], axis=0)


def _select_heads(x):
    c = x.shape[0] // 2
    lane = lax.broadcasted_iota(jnp.int32, (c, x.shape[1]), 1)
    return jnp.where(lane < HEAD_DIM, x[:c], x[c:])


def _stacked_causal_masks():
    ci = lax.broadcasted_iota(jnp.int32, (2 * CHUNK, CHUNK), 0) % CHUNK
    si = lax.broadcasted_iota(jnp.int32, (2 * CHUNK, CHUNK), 1)
    return ci >= si, ci <= si


def _pair_blockdiag(width):
    r = lax.broadcasted_iota(jnp.int32, (LANES, width), 0) // HEAD_DIM
    c = lax.broadcasted_iota(jnp.int32, (LANES, width), 1) // (width // 2)
    return r == c


def _head_rms(o, gain):
    lane = lax.broadcasted_iota(jnp.int32, o.shape, 1)
    lo = lane < HEAD_DIM
    sq = o * o
    s_lo = jnp.sum(jnp.where(lo, sq, 0.0), axis=-1, keepdims=True)
    s_hi = jnp.sum(jnp.where(lo, 0.0, sq), axis=-1, keepdims=True)
    ms = jnp.where(lo, s_lo, s_hi) * (1.0 / HEAD_DIM)
    return o * lax.rsqrt(ms + EPS) * gain


PREP_ROWS = 512
GLA_CHUNKS_PER_STEP = 4


def _gla_kernel(q_ref, k_ref, v_ref, g_ref, small_ref, wdh_ref, wdl_ref, bd_ref, gain_ref,
                o_ref, qe_s, ke_s, kd_s, qb_s, dec_s, of_s, ob_s):
    T = q_ref.shape[1]
    nc = T // CHUNK
    R = min(PREP_ROWS, T)
    pos = _chunk_pos(R)

    def prep(t, _):
        r0 = pl.multiple_of(t * R, R)
        sm = small_ref[0, pl.ds(r0, R), :]
        q = q_ref[0, pl.ds(r0, R), :] * HEAD_DIM ** -0.5
        k = k_ref[0, pl.ds(r0, R), :]
        for d in range(2):
            la = _log_sigmoid(_dot3(sm, wdh_ref[d], wdl_ref[d]) + bd_ref[d]) * (1.0 / GLA_TAU)
            b = _chunk_scan(la, pos, jnp.add, 0.0, reverse=bool(d))
            b_mid = _chunk_row(b, d, CHUNK // 2 - 1, CHUNK // 2)
            b_last = _chunk_row(b, d, CHUNK - 1, 0)
            qe_s[d, pl.ds(r0, R), :] = (q * jnp.exp(b - b_mid)).astype(BF16)
            ke_s[d, pl.ds(r0, R), :] = (k * jnp.exp(b_mid - b)).astype(BF16)
            kd_s[d, pl.ds(r0, R), :] = (k * jnp.exp(b_last - b)).astype(BF16)
            qb_s[d, pl.ds(r0, R), :] = (q * jnp.exp(b)).astype(BF16)
            dec = jnp.exp(b_last).reshape(R // CHUNK, CHUNK, LANES)[:, :SUBLANES, :]
            dec_s[d, pl.ds(pl.multiple_of(t * (R // CHUNK) * SUBLANES, SUBLANES), (R // CHUNK) * SUBLANES), :] = (
                dec.reshape((R // CHUNK) * SUBLANES, LANES))
        return 0

    lax.fori_loop(0, T // R, prep, 0)

    outs = (of_s, ob_s)
    masks = _stacked_causal_masks()
    blockdiag = _pair_blockdiag(LANES)
    G = GLA_CHUNKS_PER_STEP

    def step(it, states):
        units = [(d, g) for d in range(2) for g in range(G)]
        ops = {}
        for d, g in units:
            c = it * G + g
            c = (nc - 1 - c) if d else c
            r0 = pl.multiple_of(c * CHUNK, CHUNK)
            ops[d, g] = dict(
                r0=r0,
                qe=qe_s[d, pl.ds(r0, CHUNK), :], ke=ke_s[d, pl.ds(r0, CHUNK), :],
                kd=kd_s[d, pl.ds(r0, CHUNK), :], qb=qb_s[d, pl.ds(r0, CHUNK), :],
                v=v_ref[0, pl.ds(r0, CHUNK), :].astype(BF16),
                dec=dec_s[d, pl.ds(pl.multiple_of(c * SUBLANES, SUBLANES), SUBLANES), :][0:1, :])
        a2, kv = {}, {}
        for u in units:
            o = ops[u]
            a2[u] = _dot_nt(_stack_heads(o["qe"]), o["ke"])
            kv[u] = _dot_tn(o["v"], o["kd"])
        st_in = {}
        new_states = []
        for d in range(2):
            st = states[d]
            for g in range(G):
                st_in[d, g] = st.astype(BF16)
                st = st * ops[d, g]["dec"] + jnp.where(blockdiag, kv[d, g], 0.0)
            new_states.append(st)
        for u in units:
            o = ops[u]
            a = jnp.where(masks[u[0]], a2[u], 0.0).astype(BF16)
            intra = _select_heads(_dot(a, o["v"]))
            outs[u[0]][pl.ds(o["r0"], CHUNK), :] = intra + _dot_nt(o["qb"], st_in[u])
        return tuple(new_states)

    zero = jnp.zeros((LANES, LANES), F32)
    lax.fori_loop(0, nc // G, step, (zero, zero))

    def post(t, _):
        r0 = pl.multiple_of(t * R, R)
        o = _head_rms(of_s[pl.ds(r0, R), :] + ob_s[pl.ds(r0, R), :], gain_ref[...])
        g = g_ref[0, pl.ds(r0, R), :]
        o_ref[0, pl.ds(r0, R), :] = (o * (g * _sigmoid(g))).astype(BF16)
        return 0

    lax.fori_loop(0, T // R, post, 0)


def _gla(gla, small, lp):
    B, T, _ = gla.shape
    pairs = GLA_WIDTH // LANES
    col = lambda off: (lambda b, hp: (b, 0, off * pairs + hp))
    return pl.pallas_call(
        _gla_kernel,
        grid=(B, pairs),
        in_specs=[
            pl.BlockSpec((1, T, LANES), col(0)),
            pl.BlockSpec((1, T, LANES), col(1)),
            pl.BlockSpec((1, T, LANES), col(2)),
            pl.BlockSpec((1, T, LANES), col(3)),
            pl.BlockSpec((1, T, LANES), lambda b, hp: (b, 0, 0)),
            pl.BlockSpec((2, LANES, LANES), lambda b, hp: (0, 0, hp)),
            pl.BlockSpec((2, LANES, LANES), lambda b, hp: (0, 0, hp)),
            pl.BlockSpec((2, 1, LANES), lambda b, hp: (0, 0, hp)),
            pl.BlockSpec((1, LANES), lambda b, hp: (0, 0)),
        ],
        out_specs=pl.BlockSpec((1, T, LANES), lambda b, hp: (b, 0, hp)),
        out_shape=jax.ShapeDtypeStruct((B, T, GLA_WIDTH), BF16),
        scratch_shapes=[
            pltpu.VMEM((2, T, LANES), BF16), pltpu.VMEM((2, T, LANES), BF16),
            pltpu.VMEM((2, T, LANES), BF16), pltpu.VMEM((2, T, LANES), BF16),
            pltpu.VMEM((2, (T // CHUNK) * SUBLANES, LANES), F32),
            pltpu.VMEM((T, LANES), F32), pltpu.VMEM((T, LANES), F32),
        ],
        compiler_params=_params("arbitrary", "arbitrary"),
        name="gla",
    )(gla, gla, gla, gla, small, lp["wd_hi"], lp["wd_lo"], lp["bd"], lp["gla_gain"])


GATE_I = SMALL_GATE_LANE
GATE_F = SMALL_GATE_LANE + MLSTM_HEADS


def _gate_lane(d, hh):
    return GATE_F + 2 * MLSTM_HEADS * d + hh


MLSTM_CHUNKS_PER_STEP = 4


def _chunk_rows8(a, row):
    n = a.shape[0] // CHUNK
    a3 = a.reshape(n, CHUNK, LANES)
    full = jnp.broadcast_to(a3[:, row:row + 1, :], a3.shape)
    return full[:, :SUBLANES, :].reshape(n * SUBLANES, LANES)


def _expand_rows8(a8):
    n, w = a8.shape[0] // SUBLANES, a8.shape[1]
    a3 = a8.reshape(n, SUBLANES, w)[:, 0:1, :]
    return jnp.broadcast_to(a3, (n, CHUNK, w)).reshape(n * CHUNK, w)


def _gate_select():
    src = lax.broadcasted_iota(jnp.int32, (LANES, 2 * LANES), 0)
    dst = lax.broadcasted_iota(jnp.int32, (LANES, 2 * LANES), 1)
    want = GATE_F + 2 * MLSTM_HEADS * (dst // LANES) + (dst % LANES) // HEAD_DIM
    return jnp.where(src == want, 1.0, 0.0).astype(BF16)


def _gate_broadcast(x, sel, gate_mask):
    hi, lo = _split(jnp.where(gate_mask, x, 0.0))
    return _dot(hi, sel) + _dot(lo, sel)


def _mlstm_kernel(q_ref, k_ref, v_ref, og_ref, small_ref, wq_ref, wk_ref, bq_ref, bk_ref,
                  gbias_ref, gain_ref, o_ref, q_s, k_s, b_s, r_s, cm_s, fl1_s, bl_s, rl_s, mf_s, mb_s,
                  wp_s, qi_s, kw_s, rt_s, st_s, of_s, ob_s):
    T = q_ref.shape[1]
    nc = T // CHUNK
    R = min(PREP_ROWS, T)
    hp = pl.program_id(1)
    G = MLSTM_CHUNKS_PER_STEP
    cpt = R // CHUNK

    row = lax.broadcasted_iota(jnp.int32, (T, LANES), 0)
    for src, w_ref, b_ref, dst, scale in ((q_ref, wq_ref, bq_ref, q_s, 1.0),
                                          (k_ref, wk_ref, bk_ref, k_s, HEAD_DIM ** -0.5)):
        xc = src[0]
        prev = jnp.where(row >= 1, pltpu.roll(xc, 1, 0), 0.0)
        nxt = jnp.where(row < T - 1, pltpu.roll(xc, T - 1, 0), 0.0)
        y = prev * w_ref[0:1, :] + xc * w_ref[1:2, :] + nxt * w_ref[2:3, :] + b_ref[...]
        dst[...] = (y * _sigmoid(y) * scale).astype(BF16)

    pos = _chunk_pos(R)
    lane = lax.broadcasted_iota(jnp.int32, (R, LANES), 1)
    is_bwd = lane >= GATE_I + 2 * MLSTM_HEADS
    is_bwd8 = lax.broadcasted_iota(jnp.int32, (cpt * SUBLANES, LANES), 1) >= GATE_I + 2 * MLSTM_HEADS
    heads_per_pair = LANES // HEAD_DIM
    shift = (LANES - heads_per_pair * hp) % LANES

    def prep(t, _):
        r0 = pl.multiple_of(t * R, R)
        gc = pltpu.roll(small_ref[0, pl.ds(r0, R), :] + gbias_ref[...], shift, 1)
        logf = _log_sigmoid(gc)
        b = jnp.where(is_bwd, _chunk_scan(logf, pos, jnp.add, 0.0, True),
                      _chunk_scan(logf, pos, jnp.add, 0.0, False))
        r = pltpu.roll(gc, MLSTM_HEADS, 1) - b
        cm = jnp.where(is_bwd, _chunk_scan(r, pos, jnp.maximum, NEG, True),
                       _chunk_scan(r, pos, jnp.maximum, NEG, False))
        b_s[pl.ds(r0, R), :] = b
        r_s[pl.ds(r0, R), :] = r
        cm_s[pl.ds(r0, R), :] = cm
        rt_s[:, pl.ds(r0, R)] = r.T
        c8 = pl.ds(pl.multiple_of(t * cpt * SUBLANES, SUBLANES), cpt * SUBLANES)
        bl_s[c8, :] = jnp.where(is_bwd8, _chunk_rows8(b, 0), _chunk_rows8(b, CHUNK - 1))
        rl_s[c8, :] = jnp.where(is_bwd8, _chunk_rows8(cm, 0), _chunk_rows8(cm, CHUNK - 1))
        return 0

    lax.fori_loop(0, T // R, prep, 0)

    def m_chain(n, carry):
        mf, mb = carry
        rf = pl.ds(pl.multiple_of(n * SUBLANES, SUBLANES), SUBLANES)
        rb = pl.ds(pl.multiple_of((nc - 1 - n) * SUBLANES, SUBLANES), SUBLANES)
        mf_s[rf, :] = mf
        mb_s[rb, :] = mb
        return (bl_s[rf, :] + jnp.maximum(mf, rl_s[rf, :]), bl_s[rb, :] + jnp.maximum(mb, rl_s[rb, :]))

    m0 = jnp.full((SUBLANES, LANES), NEG, F32)
    lax.fori_loop(0, nc, m_chain, (m0, m0))

    expo_s, floor_s = (b_s, cm_s), (r_s, fl1_s)
    sel = _gate_select()
    gate_lanes = [_gate_lane(d, hh) for d in range(2) for hh in range(2)]
    gate_mask = functools.reduce(jnp.logical_or, [lane == l for l in gate_lanes])
    lane8 = lax.broadcasted_iota(jnp.int32, (cpt * SUBLANES, LANES), 1)
    gate_mask8 = functools.reduce(jnp.logical_or, [lane8 == l for l in gate_lanes])

    def weights(t, _):
        r0 = pl.multiple_of(t * R, R)
        c8 = pl.ds(pl.multiple_of(t * cpt * SUBLANES, SUBLANES), cpt * SUBLANES)
        rows = pl.ds(r0, R)
        m_in8 = jnp.where(is_bwd8, mb_s[c8, :], mf_s[c8, :])
        bl8 = bl_s[c8, :]
        m_out8 = bl8 + jnp.maximum(m_in8, rl_s[c8, :])
        mx = jnp.maximum(_expand_rows8(m_in8), cm_s[rows, :])
        m_in_b = _gate_broadcast(m_in8, sel, gate_mask8)
        wk0_b = _gate_broadcast(bl8 - m_out8, sel, gate_mask8)
        wp_b = jnp.exp(_gate_broadcast(bl8 + m_in8 - m_out8, sel, gate_mask8))
        mx_b = _gate_broadcast(mx, sel, gate_mask)
        b_b = _gate_broadcast(b_s[rows, :], sel, gate_mask)
        r_b = _gate_broadcast(r_s[rows, :], sel, gate_mask)
        w_inter = jnp.exp(_expand_rows8(m_in_b) - mx_b)
        wk = jnp.exp(_expand_rows8(wk0_b) + r_b)
        floor = jnp.exp(-(b_b + mx_b))
        q = q_s[rows, :].astype(F32)
        k = k_s[rows, :].astype(F32)
        for d in range(2):
            half = slice(d * LANES, (d + 1) * LANES)
            qi_s[d, rows, :] = (q * w_inter[:, half]).astype(BF16)
            kw_s[d, rows, :] = (k * wk[:, half]).astype(BF16)
            wp_s[d, c8, :] = wp_b[:, half]
            expo_s[d][rows, :] = -mx_b[:, half]
            floor_s[d][rows, :] = floor[:, half]
        return 0

    lax.fori_loop(0, T // R, weights, 0)

    outs = (of_s, ob_s)
    ci = lax.broadcasted_iota(jnp.int32, (CHUNK, LANES), 0)
    si = lax.broadcasted_iota(jnp.int32, (CHUNK, LANES), 1) % CHUNK
    masks = (ci >= si, ci <= si)
    br = lax.broadcasted_iota(jnp.int32, (LANES, 2 * LANES), 0) // HEAD_DIM
    bc = (lax.broadcasted_iota(jnp.int32, (LANES, 2 * LANES), 1) % LANES) // HEAD_DIM
    blockdiag = br == bc
    ones = jnp.ones((CHUNK, LANES), BF16)
    st_s[...] = jnp.zeros(st_s.shape, F32)
    gate_rows = slice(GATE_I, GATE_I + 4 * MLSTM_HEADS)

    def step(it, _):
        units = [(d, g) for d in range(2) for g in range(G)]
        ops = {}
        for d in range(2):
            first = (nc - (it + 1) * G) if d else it * G
            rt = rt_s[gate_rows, pl.ds(pl.multiple_of(first * CHUNK, G * CHUNK), G * CHUNK)]
            for g in range(G):
                local = (G - 1 - g) if d else g
                c = first + local
                rows = pl.ds(pl.multiple_of(c * CHUNK, CHUNK), CHUNK)
                c8 = pl.ds(pl.multiple_of(c * SUBLANES, SUBLANES), SUBLANES)
                r_row = jnp.concatenate(
                    [rt[_gate_lane(d, hh) - GATE_I:_gate_lane(d, hh) - GATE_I + 1,
                        local * CHUNK:(local + 1) * CHUNK] for hh in range(2)], axis=1)
                wp = wp_s[d, c8, :][0:1, :]
                ops[d, g] = dict(
                    rows=rows, q=q_s[rows, :], k=k_s[rows, :], qi=qi_s[d, rows, :], kw=kw_s[d, rows, :],
                    va=jnp.concatenate([v_ref[0, rows, :].astype(BF16), ones], axis=1),
                    expo=expo_s[d][rows, :] + r_row, floor=floor_s[d][rows, :],
                    wp=jnp.concatenate([wp, wp], axis=1))
        qk, kv = {}, {}
        for u in units:
            o = ops[u]
            qk[u] = _dot_nt(o["q"], _stack_heads(o["k"]))
            kv[u] = _dot_tn(o["kw"], o["va"])
        st_in = {}
        for d in range(2):
            st = st_s[d]
            for g in range(G):
                st_in[d, g] = st.astype(BF16)
                st = st * ops[d, g]["wp"] + jnp.where(blockdiag, kv[d, g], 0.0)
            st_s[d] = st
        for u in units:
            d = u[0]
            o = ops[u]
            smat = (qk[u] * jnp.where(masks[d], jnp.exp(o["expo"]), 0.0)).astype(BF16)
            va_bd = jnp.where(blockdiag, jnp.concatenate([o["va"], o["va"]], axis=0), jnp.zeros((), BF16))
            num = _dot(jnp.concatenate([o["qi"], smat], axis=1), jnp.concatenate([st_in[u], va_bd], axis=0))
            den = jnp.maximum(jnp.abs(num[:, LANES:]), o["floor"])
            outs[d][o["rows"], :] = num[:, :LANES] / den
        return 0

    lax.fori_loop(0, nc // G, step, 0)

    def post(t, _):
        r0 = pl.multiple_of(t * R, R)
        h = _head_rms(of_s[pl.ds(r0, R), :] + ob_s[pl.ds(r0, R), :], gain_ref[...])
        o_ref[0, pl.ds(r0, R), :] = (h * _sigmoid(og_ref[0, pl.ds(r0, R), :])).astype(BF16)
        return 0

    lax.fori_loop(0, T // R, post, 0)


def _mlstm(ml, small, lp):
    B, T, _ = ml.shape
    pairs = MLSTM_WIDTH // LANES
    col = lambda off: (lambda b, hp: (b, 0, off * pairs + hp))
    const = lambda b, hp: (0, 0)
    return pl.pallas_call(
        _mlstm_kernel,
        grid=(B, pairs),
        in_specs=[
            pl.BlockSpec((1, T, LANES), col(0)),
            pl.BlockSpec((1, T, LANES), col(1)),
            pl.BlockSpec((1, T, LANES), col(2)),
            pl.BlockSpec((1, T, LANES), col(3)),
            pl.BlockSpec((1, T, LANES), lambda b, hp: (b, 0, 0)),
            pl.BlockSpec((3, LANES), lambda b, hp: (0, hp)),
            pl.BlockSpec((3, LANES), lambda b, hp: (0, pairs + hp)),
            pl.BlockSpec((1, LANES), lambda b, hp: (0, hp)),
            pl.BlockSpec((1, LANES), lambda b, hp: (0, pairs + hp)),
            pl.BlockSpec((1, LANES), const),
            pl.BlockSpec((1, LANES), const),
        ],
        out_specs=pl.BlockSpec((1, T, LANES), lambda b, hp: (b, 0, hp)),
        out_shape=jax.ShapeDtypeStruct((B, T, MLSTM_WIDTH), BF16),
        scratch_shapes=(
            [pltpu.VMEM((T, LANES), BF16)] * 2
            + [pltpu.VMEM((T, LANES), F32)] * 4
            + [pltpu.VMEM(((T // CHUNK) * SUBLANES, LANES), F32)] * 4
            + [pltpu.VMEM((2, (T // CHUNK) * SUBLANES, LANES), F32)]
            + [pltpu.VMEM((2, T, LANES), BF16)] * 2
            + [pltpu.VMEM((LANES, T), F32),
               pltpu.VMEM((2, LANES, 2 * LANES), F32),
               pltpu.VMEM((T, LANES), F32), pltpu.VMEM((T, LANES), F32)]),
        compiler_params=_params("arbitrary", "arbitrary"),
        name="mlstm",
    )(ml, ml, ml, ml, small, lp["conv_w"], lp["conv_w"], lp["conv_b"], lp["conv_b"],
      lp["gate_bias"], lp["ml_gain"])


def _first_argmax(vals, lane):
    mx = jnp.max(vals, axis=-1, keepdims=True)
    idx = jnp.min(jnp.where(vals == mx, lane, LANES), axis=-1, keepdims=True)
    return mx, idx


def _out_route_kernel(x_ref, a_ref, gl_ref, ml_ref, w_ref, g_ref, wrh_ref, wrl_ref, br_ref,
                      x1_ref, t_ref, route_ref, routet_ref):
    tm = x_ref.shape[0]
    x1 = (x_ref[...]
          + _dot(a_ref[...], w_ref[0:ATTN_WIDTH, :])
          + _dot(gl_ref[...], w_ref[ATTN_WIDTH:ATTN_WIDTH + GLA_WIDTH, :])
          + _dot(ml_ref[...], w_ref[ATTN_WIDTH + GLA_WIDTH:, :]))
    x1_ref[...] = x1
    t = _rms(x1, g_ref[...])
    t_ref[...] = t.astype(BF16)
    logits = _dot3(t, wrh_ref[...], wrl_ref[...]) + br_ref[...]
    lane = lax.broadcasted_iota(jnp.int32, (tm, LANES), 1)
    gl = jnp.where(lane < N_GROUPS, logits, -jnp.inf)
    gmax, gi = _first_argmax(gl, lane)
    g_prob = 1.0 / jnp.sum(jnp.exp(gl - gmax), axis=-1, keepdims=True)
    lo = ROUTE_W_LANE + gi * EXPERTS_PER_GROUP
    el = jnp.where((lane >= lo) & (lane < lo + EXPERTS_PER_GROUP), logits, -jnp.inf)
    v1, i1 = _first_argmax(el, lane)
    v2, i2 = _first_argmax(jnp.where(lane == i1, -jnp.inf, el), lane)
    e2 = jnp.exp(v2 - v1)
    w1 = g_prob / (1.0 + e2)
    w2 = g_prob * e2 / (1.0 + e2)
    comb = jnp.where(lane == i1, w1, jnp.where(lane == i2, w2, 0.0))
    route = jnp.where(lane == 0, gi.astype(F32), comb)
    route_ref[...] = route
    routet_ref[...] = route.T[0:SUBLANES, :]


def _out_route(x2, attn, gla_o, ml_o, lp, tm=256):
    N, D = x2.shape
    const = lambda i: (0, 0)
    tok = lambda i: (i, 0)
    return pl.pallas_call(
        _out_route_kernel,
        grid=(N // tm,),
        in_specs=[
            pl.BlockSpec((tm, D), tok),
            pl.BlockSpec((tm, ATTN_WIDTH), tok),
            pl.BlockSpec((tm, GLA_WIDTH), tok),
            pl.BlockSpec((tm, MLSTM_WIDTH), tok),
            pl.BlockSpec((ATTN_WIDTH + GLA_WIDTH + MLSTM_WIDTH, D), const),
            pl.BlockSpec((1, D), const),
            pl.BlockSpec((D, LANES), const),
            pl.BlockSpec((D, LANES), const),
            pl.BlockSpec((1, LANES), const),
        ],
        out_specs=[
            pl.BlockSpec((tm, D), tok),
            pl.BlockSpec((tm, D), tok),
            pl.BlockSpec((tm, LANES), tok),
            pl.BlockSpec((SUBLANES, tm), lambda i: (0, i)),
        ],
        out_shape=[
            jax.ShapeDtypeStruct((N, D), F32),
            jax.ShapeDtypeStruct((N, D), BF16),
            jax.ShapeDtypeStruct((N, LANES), F32),
            jax.ShapeDtypeStruct((SUBLANES, N), F32),
        ],
        compiler_params=_params("arbitrary"),
        name="out_route",
    )(x2, attn, gla_o, ml_o, lp["w_out"], lp["g_ffn"], lp["wr_hi"], lp["wr_lo"], lp["b_route"])


def _dispatch_kernel(t_ref, route_ref, routet_ref, xs_ref, cws_ref, pos_ref, cnt_ref):
    nb = ROUTE_BLOCK
    gi_row = routet_ref[0:1, :]
    sub = lax.broadcasted_iota(jnp.int32, (SUBLANES, nb), 0).astype(F32)
    onehot = (sub == gi_row)
    ri = lax.broadcasted_iota(jnp.int32, (nb, nb), 0)
    cj = lax.broadcasted_iota(jnp.int32, (nb, nb), 1)
    before = (ri < cj).astype(BF16)
    rank = _dot(onehot.astype(BF16), before)
    counts = jnp.broadcast_to(jnp.sum(onehot.astype(F32), axis=-1, keepdims=True), (SUBLANES, LANES))
    padded = jnp.ceil(counts * (1.0 / ROUTE_TILE)) * ROUTE_TILE
    srow = lax.broadcasted_iota(jnp.int32, (SUBLANES, LANES), 0)
    incl = padded
    s = 1
    while s < SUBLANES:
        incl = incl + jnp.where(srow >= s, pltpu.roll(incl, s, 0), 0.0)
        s *= 2
    start = incl - padded
    pos_row = jnp.sum(jnp.where(onehot, start[:, 0:1] + rank, 0.0), axis=0, keepdims=True)
    cnt_ref[0] = counts.astype(jnp.int32)

    route = route_ref[...]
    lane = lax.broadcasted_iota(jnp.int32, (nb, LANES), 1)
    onehot_c = lane.astype(F32) == route[:, 0:1]
    after = (cj < ri).astype(BF16)
    rank_c = _dot(after, onehot_c.astype(BF16))
    slane = lax.broadcasted_iota(jnp.int32, (SUBLANES, LANES), 1)
    start_c = jnp.sum(jnp.where(srow == slane, start, 0.0), axis=0, keepdims=True)
    pos_col = jnp.sum(jnp.where(onehot_c, start_c + rank_c, 0.0), axis=-1, keepdims=True)
    pos_ref[...] = jnp.broadcast_to(pos_col, (nb, LANES))

    comb = jnp.where(lane >= ROUTE_W_LANE, route, 0.0)
    c_hi, c_lo = _split(comb)
    c_lo2 = (comb - c_hi.astype(F32) - c_lo.astype(F32)).astype(BF16)
    cw = (c_hi.astype(F32) + pltpu.roll(c_lo.astype(F32), ROUTE_LO_SHIFT, 1)
          + pltpu.roll(c_lo2.astype(F32), 2 * ROUTE_LO_SHIFT, 1)).astype(BF16)
    tb = t_ref[...]
    for r in range(TILES_PER_BLOCK):
        rows = (lax.broadcasted_iota(jnp.int32, (ROUTE_TILE, nb), 0) + r * ROUTE_TILE).astype(F32)
        perm = (rows == pos_row).astype(BF16)
        xs_ref[r * ROUTE_TILE:(r + 1) * ROUTE_TILE, :] = _dot(perm, tb).astype(BF16)
        cws_ref[r * ROUTE_TILE:(r + 1) * ROUTE_TILE, :] = _dot(perm, cw)


def _dispatch(t, route, routet):
    N, D = t.shape
    nblk = N // ROUTE_BLOCK
    return pl.pallas_call(
        _dispatch_kernel,
        grid=(nblk,),
        in_specs=[
            pl.BlockSpec((ROUTE_BLOCK, D), lambda i: (i, 0)),
            pl.BlockSpec((ROUTE_BLOCK, LANES), lambda i: (i, 0)),
            pl.BlockSpec((SUBLANES, ROUTE_BLOCK), lambda i: (0, i)),
        ],
        out_specs=[
            pl.BlockSpec((ROUTE_ROWS, D), lambda i: (i, 0)),
            pl.BlockSpec((ROUTE_ROWS, LANES), lambda i: (i, 0)),
            pl.BlockSpec((ROUTE_BLOCK, LANES), lambda i: (i, 0)),
            pl.BlockSpec((1, SUBLANES, LANES), lambda i: (i, 0, 0)),
        ],
        out_shape=[
            jax.ShapeDtypeStruct((nblk * ROUTE_ROWS, D), BF16),
            jax.ShapeDtypeStruct((nblk * ROUTE_ROWS, LANES), F32),
            jax.ShapeDtypeStruct((N, LANES), F32),
            jax.ShapeDtypeStruct((nblk, SUBLANES, LANES), jnp.int32),
        ],
        compiler_params=_params("arbitrary"),
        name="dispatch",
    )(t, route, routet)


def _tile_schedule(cnt):
    nblk = cnt.shape[0]
    ntile = (cnt + ROUTE_TILE - 1) // ROUTE_TILE
    end = jnp.cumsum(ntile, axis=1)
    r = jnp.arange(TILES_PER_BLOCK, dtype=jnp.int32)
    grp = jnp.sum(r[None, :, None] >= end[:, None, :], axis=-1)
    grp = grp.reshape(-1).astype(jnp.int32)
    tile = jnp.arange(nblk * TILES_PER_BLOCK, dtype=jnp.int32)
    order = jnp.argsort(grp * (nblk * TILES_PER_BLOCK) + tile).astype(jnp.int32)
    n_active = jnp.sum(grp < N_GROUPS).astype(jnp.int32)
    g_sorted = grp[order]
    last_group = g_sorted[jnp.maximum(n_active - 1, 0)]
    g_sorted = jnp.where(g_sorted < N_GROUPS, g_sorted, last_group)
    return order, g_sorted, n_active[None]


def _expert_kernel(trow_ref, tgrp_ref, nact_ref, xs_ref, cws_ref, wg_ref, wu_ref, wd_ref, ys_ref):
    i = pl.program_id(0)

    @pl.when(i < nact_ref[0])
    def _():
        x = xs_ref[...]
        cws = cws_ref[...]
        lane = lax.broadcasted_iota(jnp.int32, cws.shape, 1)
        first = ROUTE_W_LANE + tgrp_ref[i] * EXPERTS_PER_GROUP
        y = jnp.zeros(ys_ref.shape, F32)
        for j in range(EXPERTS_PER_GROUP):
            off = lane - (first + j)
            sel = (off == 0) | (off == ROUTE_LO_SHIFT) | (off == 2 * ROUTE_LO_SHIFT)
            wj = jnp.sum(jnp.where(sel, cws, 0.0), axis=-1, keepdims=True)
            h1 = _dot(x, wg_ref[j])
            a = (h1 * _sigmoid(h1) * _dot(x, wu_ref[j])).astype(BF16)
            y = y + wj * _dot(a, wd_ref[j])
        ys_ref[...] = y.astype(BF16)

    @pl.when(i >= nact_ref[0])
    def _():
        ys_ref[...] = jnp.zeros(ys_ref.shape, BF16)


def _experts(xs, cws, order, grp, n_active, lp):
    rows, D = xs.shape
    n_tiles = rows // ROUTE_TILE
    tile = lambda i, trow, tgrp, nact: (trow[i], 0)
    wsel = lambda i, trow, tgrp, nact: (tgrp[i], 0, 0)
    return pl.pallas_call(
        _expert_kernel,
        grid_spec=pltpu.PrefetchScalarGridSpec(
            num_scalar_prefetch=3,
            grid=(n_tiles,),
            in_specs=[
                pl.BlockSpec((ROUTE_TILE, D), tile),
                pl.BlockSpec((ROUTE_TILE, LANES), tile),
                pl.BlockSpec((EXPERTS_PER_GROUP, D, D_EXPERT), wsel),
                pl.BlockSpec((EXPERTS_PER_GROUP, D, D_EXPERT), wsel),
                pl.BlockSpec((EXPERTS_PER_GROUP, D_EXPERT, D), wsel),
            ],
            out_specs=pl.BlockSpec((ROUTE_TILE, D), tile),
        ),
        out_shape=jax.ShapeDtypeStruct((rows, D), BF16),
        compiler_params=_params("arbitrary"),
        name="experts",
    )(order, grp, n_active, xs, cws, lp["w_gate"], lp["w_up"], lp["w_down"])


def _combine_kernel(ys_ref, pos_ref, x1_ref, p_ref, g_ref, wpg_ref, wpp_ref, gfin_ref, o_ref,
                    *, embed, final):
    tm = x1_ref.shape[0]
    pos = pos_ref[...]
    lane = lax.broadcasted_iota(jnp.int32, (tm, LANES), 1).astype(F32)
    perm_t = jnp.concatenate(
        [(lane + r * LANES == pos).astype(BF16) for r in range(ROUTE_ROWS // LANES)], axis=1)
    x = x1_ref[...] + _dot(perm_t, ys_ref[...])
    if embed:
        gate = _sigmoid(_dot(_rms(x, g_ref[...]).astype(BF16), wpg_ref[...]))
        x = x + gate * _dot(p_ref[...].astype(BF16), wpp_ref[...])
    if final:
        x = _rms(x, gfin_ref[...])
    o_ref[...] = x


def _combine(ys, pos, x1, p2, lp, g_final, embed=True, final=False, tm=256):
    N, D = x1.shape
    inner = ROUTE_BLOCK // tm
    tok = lambda b, i: (b * inner + i, 0)
    const = lambda b, i: (0, 0)
    return pl.pallas_call(
        functools.partial(_combine_kernel, embed=embed, final=final),
        grid=(N // ROUTE_BLOCK, inner),
        in_specs=[
            pl.BlockSpec((ROUTE_ROWS, D), lambda b, i: (b, 0)),
            pl.BlockSpec((tm, LANES), tok),
            pl.BlockSpec((tm, D), tok),
            pl.BlockSpec((tm, p2.shape[1]), tok),
            pl.BlockSpec((1, D), const),
            pl.BlockSpec((D, D), const),
            pl.BlockSpec((p2.shape[1], D), const),
            pl.BlockSpec((1, D), const),
        ],
        out_specs=pl.BlockSpec((tm, D), tok),
        out_shape=jax.ShapeDtypeStruct((N, D), F32),
        compiler_params=_params("arbitrary", "arbitrary"),
        name="combine",
    )(ys, pos, x1, p2, lp["g_ple"], lp["w_pg"], lp["w_pp"], g_final)


def _moe(t, route, routet, lp):
    xs, cws, pos, cnt = _dispatch(t, route, routet)
    order, grp, n_active = _tile_schedule(cnt[:, :N_GROUPS, 0])
    return _experts(xs, cws, order, grp, n_active, lp), pos


def _moe_debug(x2, lp):
    N, D = x2.shape
    zeros = lambda w: jnp.zeros((N, w), BF16)
    x1, t, route, routet = _out_route(x2, zeros(ATTN_WIDTH), zeros(GLA_WIDTH), zeros(MLSTM_WIDTH), lp)
    ys, pos = _moe(t, route, routet, lp)
    p2 = jnp.zeros((N, lp["w_pp"].shape[0]), F32)
    return _combine(ys, pos, x1, p2, lp, lp["g_ple"], embed=False) - x1


def _in_perm():
    sizes = (ATTN_WIDTH, KV_WIDTH, KV_WIDTH, GLA_WIDTH, GLA_WIDTH, GLA_WIDTH, GLA_WIDTH,
             2 * GLA_RANK, MLSTM_WIDTH, MLSTM_WIDTH, MLSTM_WIDTH, MLSTM_WIDTH, 4 * MLSTM_HEADS)
    offs = np.concatenate([[0], np.cumsum(sizes)])
    seg = lambda i: np.arange(offs[i], offs[i + 1])
    order = [0, 1, 2, 3, 4, 5, 6, 8, 9, 10, 11, 7, 12]
    return np.concatenate([seg(i) for i in order])


def _rope_tables(T):
    t = jnp.arange(T, dtype=F32)
    row = jnp.floor(t / GRID_W)
    col = t - row * GRID_W
    inv = ROPE_THETA ** (-jnp.arange(0, ROPE_AXIS_DIM, 2, dtype=F32) / ROPE_AXIS_DIM)
    ang_r = row[:, None] * inv[None, :]
    ang_c = col[:, None] * inv[None, :]
    cos_h = jnp.concatenate([jnp.cos(ang_r), jnp.cos(ang_r), jnp.cos(ang_c), jnp.cos(ang_c)], axis=1)
    sin_h = jnp.concatenate([-jnp.sin(ang_r), jnp.sin(ang_r), -jnp.sin(ang_c), jnp.sin(ang_c)], axis=1)
    reps = LANES // HEAD_DIM
    return jnp.tile(cos_h, (1, reps)), jnp.tile(sin_h, (1, reps))


def _head_onehots():
    n_heads = QK_WIDTH // HEAD_DIM
    eh = np.zeros((QK_WIDTH, LANES), np.float32)
    eh[np.arange(QK_WIDTH), np.arange(QK_WIDTH) // HEAD_DIM] = 1.0
    assert n_heads <= LANES
    return jnp.asarray(eh, BF16), jnp.asarray(eh.T, BF16)


def kernel(x, p, norm_mix_g, w_in, attn_q_norm_g, attn_k_norm_g, gla_w_decay, gla_b_decay,
           gla_out_norm_g, mlstm_conv_w, mlstm_conv_b, mlstm_b_input, mlstm_b_forget,
           mlstm_out_norm_g, w_out, norm_ffn_g, w_group, b_group, w_router, b_router,
           w_expert_gate, w_expert_up, w_expert_down, norm_ple_g, w_ple_gate, w_ple_proj,
           final_norm_g):
    params = dict(
        norm_mix_g=norm_mix_g, w_in=w_in, attn_q_norm_g=attn_q_norm_g, attn_k_norm_g=attn_k_norm_g,
        gla_w_decay=gla_w_decay, gla_b_decay=gla_b_decay, gla_out_norm_g=gla_out_norm_g,
        mlstm_conv_w=mlstm_conv_w, mlstm_conv_b=mlstm_conv_b, mlstm_b_input=mlstm_b_input,
        mlstm_b_forget=mlstm_b_forget, mlstm_out_norm_g=mlstm_out_norm_g, w_out=w_out,
        norm_ffn_g=norm_ffn_g, w_group=w_group, b_group=b_group, w_router=w_router,
        b_router=b_router, w_expert_gate=w_expert_gate, w_expert_up=w_expert_up,
        w_expert_down=w_expert_down, norm_ple_g=norm_ple_g, w_ple_gate=w_ple_gate,
        w_ple_proj=w_ple_proj)
    B, T, D = x.shape
    cos, sin = _rope_tables(T)
    eh, eht = _head_onehots()
    N = B * T
    depth = w_in.shape[0]
    x2 = x.reshape(N, D)
    g_final = final_norm_g[None, :]
    for i in range(depth):
        lp = _layer_params(params, i)
        qt, k, vt, gla, ml, small = _in_proj(x2, lp["g_mix"], lp["w_in"], cos, sin, lp["qk_gain"],
                                             eh, eht, B, T)
        attn = _attention(lp["attn_safe"], qt, k, vt).reshape(N, ATTN_WIDTH)
        gla_o = _gla(gla, small, lp).reshape(N, GLA_WIDTH)
        ml_o = _mlstm(ml, small, lp).reshape(N, MLSTM_WIDTH)
        x1, t, route, routet = _out_route(x2, attn, gla_o, ml_o, lp)
        ys, pos = _moe(t, route, routet, lp)
        x2 = _combine(ys, pos, x1, p[i].reshape(N, -1), lp, g_final, final=(i == depth - 1))
    return x2.reshape(B, T, D)


def _split_w(w):
    hi = w.astype(BF16)
    return hi, (w - hi.astype(F32)).astype(BF16)


def _layer_params(p, i):
    D = p["w_in"].shape[1]
    w_in = jnp.pad(p["w_in"][i][:, _in_perm()], ((0, 0), (0, IN_PERM_WIDTH - p["w_in"].shape[2])))
    gq, gk = p["attn_q_norm_g"][i], p["attn_k_norm_g"][i]
    qk_gain = jnp.concatenate([jnp.tile(gq, ATTN_HEADS) * (HEAD_DIM ** -0.5 * LOG2E),
                               jnp.tile(gk, ATTN_KV_HEADS)])
    logit_bound = HEAD_DIM ** 0.5 * jnp.max(jnp.abs(gq)) * jnp.max(jnp.abs(gk))
    attn_safe = (logit_bound <= ATTN_SAFE_LOGIT).astype(jnp.int32)[None]
    wd = jnp.zeros((2, LANES, GLA_WIDTH), F32)
    wd = wd.at[0, :GLA_RANK].set(p["gla_w_decay"][i, 0]).at[1, GLA_RANK:2 * GLA_RANK].set(p["gla_w_decay"][i, 1])
    wd_hi, wd_lo = _split_w(wd)
    gate_bias = jnp.zeros((LANES,), F32).at[SMALL_GATE_LANE:SMALL_GATE_LANE + 4 * MLSTM_HEADS].set(
        jnp.concatenate([p["mlstm_b_input"][i, 0], p["mlstm_b_forget"][i, 0],
                         p["mlstm_b_input"][i, 1], p["mlstm_b_forget"][i, 1]]))
    w_route = jnp.zeros((D, LANES), F32)
    w_route = w_route.at[:, :N_GROUPS].set(p["w_group"][i])
    w_route = w_route.at[:, ROUTE_W_LANE:ROUTE_W_LANE + N_EXPERTS].set(p["w_router"][i])
    wr_hi, wr_lo = _split_w(w_route)
    b_route = jnp.zeros((LANES,), F32).at[:N_GROUPS].set(p["b_group"][i])
    b_route = b_route.at[ROUTE_W_LANE:ROUTE_W_LANE + N_EXPERTS].set(p["b_router"][i])
    return dict(
        g_mix=p["norm_mix_g"][i][None, :],
        w_in=w_in.astype(BF16),
        qk_gain=qk_gain[None, :],
        attn_safe=attn_safe,
        wd_hi=wd_hi, wd_lo=wd_lo,
        bd=p["gla_b_decay"][i][:, None, :],
        gla_gain=jnp.tile(p["gla_out_norm_g"][i], LANES // HEAD_DIM)[None, :],
        conv_w=p["mlstm_conv_w"][i],
        conv_b=p["mlstm_conv_b"][i][None, :],
        gate_bias=gate_bias[None, :],
        ml_gain=jnp.tile(p["mlstm_out_norm_g"][i], LANES // HEAD_DIM)[None, :],
        w_out=p["w_out"][i].astype(BF16),
        g_ffn=p["norm_ffn_g"][i][None, :],
        wr_hi=wr_hi, wr_lo=wr_lo, b_route=b_route[None, :],
        w_gate=p["w_expert_gate"][i].astype(BF16),
        w_up=p["w_expert_up"][i].astype(BF16),
        w_down=p["w_expert_down"][i].astype(BF16),
        g_ple=p["norm_ple_g"][i][None, :],
        w_pg=p["w_ple_gate"][i].astype(BF16),
        w_pp=p["w_ple_proj"][i].astype(BF16),
    )
```

```python
import functools

import jax
import jax.numpy as jnp
import numpy as np
from jax import lax
from jax.experimental import pallas as pl
from jax.experimental.pallas import tpu as pltpu

F32 = jnp.float32
BF16 = jnp.bfloat16

GRID_W = 64
HEAD_DIM = 64
ATTN_HEADS = 8
ATTN_KV_HEADS = 2
GLA_HEADS = 4
MLSTM_HEADS = 4
ATTN_WIDTH = ATTN_HEADS * HEAD_DIM
KV_WIDTH = ATTN_KV_HEADS * HEAD_DIM
GLA_WIDTH = GLA_HEADS * HEAD_DIM
MLSTM_WIDTH = MLSTM_HEADS * HEAD_DIM
GLA_RANK = 16
GLA_TAU = 16.0
CHUNK = 64
ROPE_THETA = 10000.0
ROPE_AXIS_DIM = HEAD_DIM // 2
N_GROUPS = 4
EXPERTS_PER_GROUP = 4
N_EXPERTS = N_GROUPS * EXPERTS_PER_GROUP
D_EXPERT = 512
EPS = 1e-6
NEG = -1e30

LANES = 128
SUBLANES = 8
VMEM_LIMIT_BYTES = 56 * 1024 * 1024

QK_WIDTH = ATTN_WIDTH + KV_WIDTH
OFF_V = QK_WIDTH
OFF_GLA = OFF_V + KV_WIDTH
OFF_ML = OFF_GLA + 4 * GLA_WIDTH
OFF_SMALL = OFF_ML + 4 * MLSTM_WIDTH
IN_PERM_WIDTH = OFF_SMALL + LANES
SMALL_GATE_LANE = 2 * GLA_RANK

ROUTE_BLOCK = 1024
ROUTE_TILE = 128
ROUTE_ROWS = ROUTE_BLOCK + N_GROUPS * ROUTE_TILE
TILES_PER_BLOCK = ROUTE_ROWS // ROUTE_TILE
ROUTE_W_LANE = 4
ROUTE_LO_SHIFT = 32


def _dot(a, b):
    return jnp.dot(a, b, preferred_element_type=F32)


def _dot_nt(a, b):
    return lax.dot_general(a, b, (((1,), (1,)), ((), ())), preferred_element_type=F32)


def _dot_tn(a, b):
    return lax.dot_general(a, b, (((0,), (0,)), ((), ())), preferred_element_type=F32)


def _split(a):
    hi = a.astype(BF16)
    lo = (a - hi.astype(F32)).astype(BF16)
    return hi, lo


def _dot3(a, w_hi, w_lo):
    a_hi, a_lo = _split(a)
    return _dot(a_hi, w_hi) + _dot(a_lo, w_hi) + _dot(a_hi, w_lo)


def _log_sigmoid(x):
    return jnp.minimum(x, 0.0) - jnp.log1p(jnp.exp(-jnp.abs(x)))


def _sigmoid(x):
    return 1.0 / (1.0 + jnp.exp(-x))


def _rms(x, g):
    return x * lax.rsqrt(jnp.mean(x * x, axis=-1, keepdims=True) + EPS) * g


def _params(*semantics):
    return pltpu.CompilerParams(dimension_semantics=semantics, vmem_limit_bytes=VMEM_LIMIT_BYTES)


def _in_proj_kernel(x_ref, g_ref, w_ref, cos_ref, sin_ref, gain_ref, eh_ref, eht_ref,
                    qt_ref, k_ref, vt_ref, gla_ref, ml_ref, small_ref):
    tm = x_ref.shape[0]
    h = _rms(x_ref[...], g_ref[...])
    z = _dot(h.astype(BF16), w_ref[...])
    qk = z[:, :QK_WIDTH]
    sq_hi, sq_lo = _split(qk * qk)
    ssq = _dot(sq_hi, eh_ref[...]) + _dot(sq_lo, eh_ref[...])
    inv_hi, inv_lo = _split(lax.rsqrt(ssq * (1.0 / HEAD_DIM) + EPS))
    scale = _dot(inv_hi, eht_ref[...]) + _dot(inv_lo, eht_ref[...])
    y = qk * scale * gain_ref[...]
    lane = lax.broadcasted_iota(jnp.int32, (tm, LANES), 1)
    first_half = (lane % ROPE_AXIS_DIM) < (ROPE_AXIS_DIM // 2)
    cos = cos_ref[...]
    sin = sin_ref[...]
    for c in range(QK_WIDTH // LANES):
        yc = y[:, c * LANES:(c + 1) * LANES]
        partner = jnp.where(first_half,
                            pltpu.roll(yc, LANES - ROPE_AXIS_DIM // 2, 1),
                            pltpu.roll(yc, ROPE_AXIS_DIM // 2, 1))
        rc = yc * cos + partner * sin
        if c < ATTN_WIDTH // LANES:
            qt_ref[0, c * LANES:(c + 1) * LANES, :] = rc.T.astype(BF16)
        else:
            k_ref[0] = rc.astype(BF16)
    vt_ref[0] = z[:, OFF_V:OFF_V + KV_WIDTH].T.astype(BF16)
    gla_ref[0] = z[:, OFF_GLA:OFF_ML]
    ml_ref[0] = z[:, OFF_ML:OFF_SMALL]
    small_ref[0] = z[:, OFF_SMALL:IN_PERM_WIDTH]


def _in_proj(x2, g, w_perm, cos, sin, gain, eh, eht, B, T, tm=512):
    N, D = x2.shape
    tpb = T // tm
    const = lambda i: (0, 0)
    tok3 = lambda i: (i // tpb, i % tpb, 0)
    tokT = lambda i: (i // tpb, 0, i % tpb)
    return pl.pallas_call(
        _in_proj_kernel,
        grid=(N // tm,),
        in_specs=[
            pl.BlockSpec((tm, D), lambda i: (i, 0)),
            pl.BlockSpec((1, D), const),
            pl.BlockSpec((D, IN_PERM_WIDTH), const),
            pl.BlockSpec((tm, LANES), lambda i: (i % tpb, 0)),
            pl.BlockSpec((tm, LANES), lambda i: (i % tpb, 0)),
            pl.BlockSpec((1, QK_WIDTH), const),
            pl.BlockSpec((QK_WIDTH, LANES), const),
            pl.BlockSpec((LANES, QK_WIDTH), const),
        ],
        out_specs=[
            pl.BlockSpec((1, ATTN_WIDTH, tm), tokT),
            pl.BlockSpec((1, tm, KV_WIDTH), tok3),
            pl.BlockSpec((1, KV_WIDTH, tm), tokT),
            pl.BlockSpec((1, tm, 4 * GLA_WIDTH), tok3),
            pl.BlockSpec((1, tm, 4 * MLSTM_WIDTH), tok3),
            pl.BlockSpec((1, tm, LANES), tok3),
        ],
        out_shape=[
            jax.ShapeDtypeStruct((B, ATTN_WIDTH, T), BF16),
            jax.ShapeDtypeStruct((B, T, KV_WIDTH), BF16),
            jax.ShapeDtypeStruct((B, KV_WIDTH, T), BF16),
            jax.ShapeDtypeStruct((B, T, 4 * GLA_WIDTH), F32),
            jax.ShapeDtypeStruct((B, T, 4 * MLSTM_WIDTH), F32),
            jax.ShapeDtypeStruct((B, T, LANES), F32),
        ],
        compiler_params=_params("arbitrary"),
        name="in_proj",
    )(x2, g, w_perm, cos, sin, gain, eh, eht)


ATTN_SAFE_LOGIT = 40.0
LOG2E = 1.4426950408889634


def _attn_kernel(safe_ref, qt_ref, k_ref, vt_ref, o_ref, *, tk):
    tq = qt_ref.shape[2]
    T = k_ref.shape[1]
    G = ATTN_HEADS // ATTN_KV_HEADS
    n = G * tq
    zeros = jnp.zeros((HEAD_DIM, n), BF16)

    def q_operand(j):
        base = j * G * HEAD_DIM
        qs = jnp.concatenate(
            [qt_ref[0, base + h * HEAD_DIM:base + (h + 1) * HEAD_DIM, :] for h in range(G)], axis=1)
        return jnp.concatenate([qs, zeros] if j == 0 else [zeros, qs], axis=0)

    def finish(j, acc, l):
        base = j * G * HEAD_DIM
        o = acc * (1.0 / l)
        ot = jnp.concatenate([o[:, h * tq:(h + 1) * tq] for h in range(G)], axis=0)
        o_ref[0, :, base:base + G * HEAD_DIM] = ot.T.astype(BF16)

    @pl.when(safe_ref[0] == 1)
    def _():
        qps = [q_operand(j) for j in range(ATTN_KV_HEADS)]
        units = [(c, j) for c in range(T // tk) for j in range(ATTN_KV_HEADS)]

        def scores(u):
            c, j = units[u]
            return _dot(k_ref[0, c * tk:(c + 1) * tk, :], qps[j])

        l8 = [jnp.zeros((SUBLANES, n), F32)] * ATTN_KV_HEADS
        acc = [jnp.zeros((HEAD_DIM, n), F32)] * ATTN_KV_HEADS
        st = scores(0)
        for u, (c, j) in enumerate(units):
            st_next = scores(u + 1) if u + 1 < len(units) else None
            p = jnp.exp2(st)
            l8[j] = l8[j] + jnp.sum(p.reshape(tk // SUBLANES, SUBLANES, n), axis=0)
            vc = vt_ref[0, j * HEAD_DIM:(j + 1) * HEAD_DIM, c * tk:(c + 1) * tk]
            acc[j] = acc[j] + _dot(vc, p.astype(BF16))
            st = st_next
        for j in range(ATTN_KV_HEADS):
            finish(j, acc[j], jnp.sum(l8[j], axis=0, keepdims=True))

    @pl.when(safe_ref[0] == 0)
    def _():
        for j in range(ATTN_KV_HEADS):
            qp = q_operand(j)

            def body(c, carry, qp=qp, j=j):
                m, l, acc = carry
                off = pl.multiple_of(c * tk, tk)
                st = _dot(k_ref[0, pl.ds(off, tk), :], qp)
                m_new = jnp.maximum(m, jnp.max(st, axis=0, keepdims=True))
                alpha = jnp.exp2(m - m_new)
                p = jnp.exp2(st - m_new)
                l = alpha * l + jnp.sum(p, axis=0, keepdims=True)
                vc = vt_ref[0, j * HEAD_DIM:(j + 1) * HEAD_DIM, pl.ds(off, tk)]
                return m_new, l, alpha * acc + _dot(vc, p.astype(BF16))

            init = (jnp.full((1, n), NEG, F32), jnp.zeros((1, n), F32), jnp.zeros((HEAD_DIM, n), F32))
            _, l, acc = lax.fori_loop(0, T // tk, body, init)
            finish(j, acc, l)


def _attention(safe, qt, k, vt, tq=256, tk=128):
    B, _, T = qt.shape
    tk = min(tk, T)
    return pl.pallas_call(
        functools.partial(_attn_kernel, tk=tk),
        grid_spec=pltpu.PrefetchScalarGridSpec(
            num_scalar_prefetch=1,
            grid=(B, T // tq),
            in_specs=[
                pl.BlockSpec((1, ATTN_WIDTH, tq), lambda b, i, s: (b, 0, i)),
                pl.BlockSpec((1, T, KV_WIDTH), lambda b, i, s: (b, 0, 0)),
                pl.BlockSpec((1, KV_WIDTH, T), lambda b, i, s: (b, 0, 0)),
            ],
            out_specs=pl.BlockSpec((1, tq, ATTN_WIDTH), lambda b, i, s: (b, i, 0)),
        ),
        out_shape=jax.ShapeDtypeStruct((B, T, ATTN_WIDTH), BF16),
        compiler_params=_params("arbitrary", "arbitrary"),
        name="attention",
    )(safe, qt, k, vt)


def _chunk_scan(x, pos, op, fill, reverse):
    rows = x.shape[0]
    s = 1
    while s < CHUNK:
        if reverse:
            shifted = jnp.where(pos < CHUNK - s, pltpu.roll(x, rows - s, 0), fill)
        else:
            shifted = jnp.where(pos >= s, pltpu.roll(x, s, 0), fill)
        x = op(x, shifted)
        s *= 2
    return x


def _chunk_pos(rows):
    return lax.broadcasted_iota(jnp.int32, (rows, LANES), 0) % CHUNK


def _chunk_row(a, reverse_dir, idx_fwd, idx_bwd):
    rows = a.shape[0]
    a3 = a.reshape(rows // CHUNK, CHUNK, LANES)
    i = idx_bwd if reverse_dir else idx_fwd
    return jnp.broadcast_to(a3[:, i:i + 1, :], a3.shape).reshape(rows, LANES)


def _stack_heads(x):
    lane = lax.broadcasted_iota(jnp.int32, x.shape, 1)
    zero = jnp.zeros_like(x)
    return jnp.concatenate([jnp.where(lane < HEAD_DIM, x, zero), jnp.where(lane >= HEAD_DIM, x, zero)], axis=0)


def _select_heads(x):
    c = x.shape[0] // 2
    lane = lax.broadcasted_iota(jnp.int32, (c, x.shape[1]), 1)
    return jnp.where(lane < HEAD_DIM, x[:c], x[c:])


def _stacked_causal_masks():
    ci = lax.broadcasted_iota(jnp.int32, (2 * CHUNK, CHUNK), 0) % CHUNK
    si = lax.broadcasted_iota(jnp.int32, (2 * CHUNK, CHUNK), 1)
    return ci >= si, ci <= si


def _pair_blockdiag(width):
    r = lax.broadcasted_iota(jnp.int32, (LANES, width), 0) // HEAD_DIM
    c = lax.broadcasted_iota(jnp.int32, (LANES, width), 1) // (width // 2)
    return r == c


def _head_rms(o, gain):
    lane = lax.broadcasted_iota(jnp.int32, o.shape, 1)
    lo = lane < HEAD_DIM
    sq = o * o
    s_lo = jnp.sum(jnp.where(lo, sq, 0.0), axis=-1, keepdims=True)
    s_hi = jnp.sum(jnp.where(lo, 0.0, sq), axis=-1, keepdims=True)
    ms = jnp.where(lo, s_lo, s_hi) * (1.0 / HEAD_DIM)
    return o * lax.rsqrt(ms + EPS) * gain


PREP_ROWS = 512
GLA_CHUNKS_PER_STEP = 4


def _gla_kernel(q_ref, k_ref, v_ref, g_ref, small_ref, wdh_ref, wdl_ref, bd_ref, gain_ref,
                o_ref, qe_s, ke_s, kd_s, qb_s, dec_s, of_s, ob_s):
    T = q_ref.shape[1]
    nc = T // CHUNK
    R = min(PREP_ROWS, T)
    pos = _chunk_pos(R)

    def prep(t, _):
        r0 = pl.multiple_of(t * R, R)
        sm = small_ref[0, pl.ds(r0, R), :]
        q = q_ref[0, pl.ds(r0, R), :] * HEAD_DIM ** -0.5
        k = k_ref[0, pl.ds(r0, R), :]
        for d in range(2):
            la = _log_sigmoid(_dot3(sm, wdh_ref[d], wdl_ref[d]) + bd_ref[d]) * (1.0 / GLA_TAU)
            b = _chunk_scan(la, pos, jnp.add, 0.0, reverse=bool(d))
            b_mid = _chunk_row(b, d, CHUNK // 2 - 1, CHUNK // 2)
            b_last = _chunk_row(b, d, CHUNK - 1, 0)
            qe_s[d, pl.ds(r0, R), :] = (q * jnp.exp(b - b_mid)).astype(BF16)
            ke_s[d, pl.ds(r0, R), :] = (k * jnp.exp(b_mid - b)).astype(BF16)
            kd_s[d, pl.ds(r0, R), :] = (k * jnp.exp(b_last - b)).astype(BF16)
            qb_s[d, pl.ds(r0, R), :] = (q * jnp.exp(b)).astype(BF16)
            dec = jnp.exp(b_last).reshape(R // CHUNK, CHUNK, LANES)[:, :SUBLANES, :]
            dec_s[d, pl.ds(pl.multiple_of(t * (R // CHUNK) * SUBLANES, SUBLANES), (R // CHUNK) * SUBLANES), :] = (
                dec.reshape((R // CHUNK) * SUBLANES, LANES))
        return 0

    lax.fori_loop(0, T // R, prep, 0)

    outs = (of_s, ob_s)
    masks = _stacked_causal_masks()
    blockdiag = _pair_blockdiag(LANES)
    G = GLA_CHUNKS_PER_STEP

    def step(it, states):
        units = [(d, g) for d in range(2) for g in range(G)]
        ops = {}
        for d, g in units:
            c = it * G + g
            c = (nc - 1 - c) if d else c
            r0 = pl.multiple_of(c * CHUNK, CHUNK)
            ops[d, g] = dict(
                r0=r0,
                qe=qe_s[d, pl.ds(r0, CHUNK), :], ke=ke_s[d, pl.ds(r0, CHUNK), :],
                kd=kd_s[d, pl.ds(r0, CHUNK), :], qb=qb_s[d, pl.ds(r0, CHUNK), :],
                v=v_ref[0, pl.ds(r0, CHUNK), :].astype(BF16),
                dec=dec_s[d, pl.ds(pl.multiple_of(c * SUBLANES, SUBLANES), SUBLANES), :][0:1, :])
        a2, kv = {}, {}
        for u in units:
            o = ops[u]
            a2[u] = _dot_nt(_stack_heads(o["qe"]), o["ke"])
            kv[u] = _dot_tn(o["v"], o["kd"])
        st_in = {}
        new_states = []
        for d in range(2):
            st = states[d]
            for g in range(G):
                st_in[d, g] = st.astype(BF16)
                st = st * ops[d, g]["dec"] + jnp.where(blockdiag, kv[d, g], 0.0)
            new_states.append(st)
        for u in units:
            o = ops[u]
            a = jnp.where(masks[u[0]], a2[u], 0.0).astype(BF16)
            intra = _select_heads(_dot(a, o["v"]))
            outs[u[0]][pl.ds(o["r0"], CHUNK), :] = intra + _dot_nt(o["qb"], st_in[u])
        return tuple(new_states)

    zero = jnp.zeros((LANES, LANES), F32)
    lax.fori_loop(0, nc // G, step, (zero, zero))

    def post(t, _):
        r0 = pl.multiple_of(t * R, R)
        o = _head_rms(of_s[pl.ds(r0, R), :] + ob_s[pl.ds(r0, R), :], gain_ref[...])
        g = g_ref[0, pl.ds(r0, R), :]
        o_ref[0, pl.ds(r0, R), :] = (o * (g * _sigmoid(g))).astype(BF16)
        return 0

    lax.fori_loop(0, T // R, post, 0)


def _gla(gla, small, lp):
    B, T, _ = gla.shape
    pairs = GLA_WIDTH // LANES
    col = lambda off: (lambda b, hp: (b, 0, off * pairs + hp))
    return pl.pallas_call(
        _gla_kernel,
        grid=(B, pairs),
        in_specs=[
            pl.BlockSpec((1, T, LANES), col(0)),
            pl.BlockSpec((1, T, LANES), col(1)),
            pl.BlockSpec((1, T, LANES), col(2)),
            pl.BlockSpec((1, T, LANES), col(3)),
            pl.BlockSpec((1, T, LANES), lambda b, hp: (b, 0, 0)),
            pl.BlockSpec((2, LANES, LANES), lambda b, hp: (0, 0, hp)),
            pl.BlockSpec((2, LANES, LANES), lambda b, hp: (0, 0, hp)),
            pl.BlockSpec((2, 1, LANES), lambda b, hp: (0, 0, hp)),
            pl.BlockSpec((1, LANES), lambda b, hp: (0, 0)),
        ],
        out_specs=pl.BlockSpec((1, T, LANES), lambda b, hp: (b, 0, hp)),
        out_shape=jax.ShapeDtypeStruct((B, T, GLA_WIDTH), BF16),
        scratch_shapes=[
            pltpu.VMEM((2, T, LANES), BF16), pltpu.VMEM((2, T, LANES), BF16),
            pltpu.VMEM((2, T, LANES), BF16), pltpu.VMEM((2, T, LANES), BF16),
            pltpu.VMEM((2, (T // CHUNK) * SUBLANES, LANES), F32),
            pltpu.VMEM((T, LANES), F32), pltpu.VMEM((T, LANES), F32),
        ],
        compiler_params=_params("arbitrary", "arbitrary"),
        name="gla",
    )(gla, gla, gla, gla, small, lp["wd_hi"], lp["wd_lo"], lp["bd"], lp["gla_gain"])


GATE_I = SMALL_GATE_LANE
GATE_F = SMALL_GATE_LANE + MLSTM_HEADS


def _gate_lane(d, hh):
    return GATE_F + 2 * MLSTM_HEADS * d + hh


MLSTM_CHUNKS_PER_STEP = 4


def _chunk_rows8(a, row):
    n = a.shape[0] // CHUNK
    a3 = a.reshape(n, CHUNK, LANES)
    full = jnp.broadcast_to(a3[:, row:row + 1, :], a3.shape)
    return full[:, :SUBLANES, :].reshape(n * SUBLANES, LANES)


def _expand_rows8(a8):
    n, w = a8.shape[0] // SUBLANES, a8.shape[1]
    a3 = a8.reshape(n, SUBLANES, w)[:, 0:1, :]
    return jnp.broadcast_to(a3, (n, CHUNK, w)).reshape(n * CHUNK, w)


def _gate_select():
    src = lax.broadcasted_iota(jnp.int32, (LANES, 2 * LANES), 0)
    dst = lax.broadcasted_iota(jnp.int32, (LANES, 2 * LANES), 1)
    want = GATE_F + 2 * MLSTM_HEADS * (dst // LANES) + (dst % LANES) // HEAD_DIM
    return jnp.where(src == want, 1.0, 0.0).astype(BF16)


def _gate_broadcast(x, sel, gate_mask):
    hi, lo = _split(jnp.where(gate_mask, x, 0.0))
    return _dot(hi, sel) + _dot(lo, sel)


def _mlstm_kernel(q_ref, k_ref, v_ref, og_ref, small_ref, wq_ref, wk_ref, bq_ref, bk_ref,
                  gbias_ref, gain_ref, o_ref, q_s, k_s, b_s, r_s, cm_s, fl1_s, bl_s, rl_s, mf_s, mb_s,
                  wp_s, qi_s, kw_s, rt_s, st_s, of_s, ob_s):
    T = q_ref.shape[1]
    nc = T // CHUNK
    R = min(PREP_ROWS, T)
    hp = pl.program_id(1)
    G = MLSTM_CHUNKS_PER_STEP
    cpt = R // CHUNK

    row = lax.broadcasted_iota(jnp.int32, (T, LANES), 0)
    for src, w_ref, b_ref, dst, scale in ((q_ref, wq_ref, bq_ref, q_s, 1.0),
                                          (k_ref, wk_ref, bk_ref, k_s, HEAD_DIM ** -0.5)):
        xc = src[0]
        prev = jnp.where(row >= 1, pltpu.roll(xc, 1, 0), 0.0)
        nxt = jnp.where(row < T - 1, pltpu.roll(xc, T - 1, 0), 0.0)
        y = prev * w_ref[0:1, :] + xc * w_ref[1:2, :] + nxt * w_ref[2:3, :] + b_ref[...]
        dst[...] = (y * _sigmoid(y) * scale).astype(BF16)

    pos = _chunk_pos(R)
    lane = lax.broadcasted_iota(jnp.int32, (R, LANES), 1)
    is_bwd = lane >= GATE_I + 2 * MLSTM_HEADS
    is_bwd8 = lax.broadcasted_iota(jnp.int32, (cpt * SUBLANES, LANES), 1) >= GATE_I + 2 * MLSTM_HEADS
    heads_per_pair = LANES // HEAD_DIM
    shift = (LANES - heads_per_pair * hp) % LANES

    def prep(t, _):
        r0 = pl.multiple_of(t * R, R)
        gc = pltpu.roll(small_ref[0, pl.ds(r0, R), :] + gbias_ref[...], shift, 1)
        logf = _log_sigmoid(gc)
        b = jnp.where(is_bwd, _chunk_scan(logf, pos, jnp.add, 0.0, True),
                      _chunk_scan(logf, pos, jnp.add, 0.0, False))
        r = pltpu.roll(gc, MLSTM_HEADS, 1) - b
        cm = jnp.where(is_bwd, _chunk_scan(r, pos, jnp.maximum, NEG, True),
                       _chunk_scan(r, pos, jnp.maximum, NEG, False))
        b_s[pl.ds(r0, R), :] = b
        r_s[pl.ds(r0, R), :] = r
        cm_s[pl.ds(r0, R), :] = cm
        rt_s[:, pl.ds(r0, R)] = r.T
        c8 = pl.ds(pl.multiple_of(t * cpt * SUBLANES, SUBLANES), cpt * SUBLANES)
        bl_s[c8, :] = jnp.where(is_bwd8, _chunk_rows8(b, 0), _chunk_rows8(b, CHUNK - 1))
        rl_s[c8, :] = jnp.where(is_bwd8, _chunk_rows8(cm, 0), _chunk_rows8(cm, CHUNK - 1))
        return 0

    lax.fori_loop(0, T // R, prep, 0)

    def m_chain(n, carry):
        mf, mb = carry
        rf = pl.ds(pl.multiple_of(n * SUBLANES, SUBLANES), SUBLANES)
        rb = pl.ds(pl.multiple_of((nc - 1 - n) * SUBLANES, SUBLANES), SUBLANES)
        mf_s[rf, :] = mf
        mb_s[rb, :] = mb
        return (bl_s[rf, :] + jnp.maximum(mf, rl_s[rf, :]), bl_s[rb, :] + jnp.maximum(mb, rl_s[rb, :]))

    m0 = jnp.full((SUBLANES, LANES), NEG, F32)
    lax.fori_loop(0, nc, m_chain, (m0, m0))

    expo_s, floor_s = (b_s, cm_s), (r_s, fl1_s)
    sel = _gate_select()
    gate_lanes = [_gate_lane(d, hh) for d in range(2) for hh in range(2)]
    gate_mask = functools.reduce(jnp.logical_or, [lane == l for l in gate_lanes])
    lane8 = lax.broadcasted_iota(jnp.int32, (cpt * SUBLANES, LANES), 1)
    gate_mask8 = functools.reduce(jnp.logical_or, [lane8 == l for l in gate_lanes])

    def weights(t, _):
        r0 = pl.multiple_of(t * R, R)
        c8 = pl.ds(pl.multiple_of(t * cpt * SUBLANES, SUBLANES), cpt * SUBLANES)
        rows = pl.ds(r0, R)
        m_in8 = jnp.where(is_bwd8, mb_s[c8, :], mf_s[c8, :])
        bl8 = bl_s[c8, :]
        m_out8 = bl8 + jnp.maximum(m_in8, rl_s[c8, :])
        mx = jnp.maximum(_expand_rows8(m_in8), cm_s[rows, :])
        m_in_b = _gate_broadcast(m_in8, sel, gate_mask8)
        wk0_b = _gate_broadcast(bl8 - m_out8, sel, gate_mask8)
        wp_b = jnp.exp(_gate_broadcast(bl8 + m_in8 - m_out8, sel, gate_mask8))
        mx_b = _gate_broadcast(mx, sel, gate_mask)
        b_b = _gate_broadcast(b_s[rows, :], sel, gate_mask)
        r_b = _gate_broadcast(r_s[rows, :], sel, gate_mask)
        w_inter = jnp.exp(_expand_rows8(m_in_b) - mx_b)
        wk = jnp.exp(_expand_rows8(wk0_b) + r_b)
        floor = jnp.exp(-(b_b + mx_b))
        q = q_s[rows, :].astype(F32)
        k = k_s[rows, :].astype(F32)
        for d in range(2):
            half = slice(d * LANES, (d + 1) * LANES)
            qi_s[d, rows, :] = (q * w_inter[:, half]).astype(BF16)
            kw_s[d, rows, :] = (k * wk[:, half]).astype(BF16)
            wp_s[d, c8, :] = wp_b[:, half]
            expo_s[d][rows, :] = -mx_b[:, half]
            floor_s[d][rows, :] = floor[:, half]
        return 0

    lax.fori_loop(0, T // R, weights, 0)

    outs = (of_s, ob_s)
    ci = lax.broadcasted_iota(jnp.int32, (CHUNK, LANES), 0)
    si = lax.broadcasted_iota(jnp.int32, (CHUNK, LANES), 1) % CHUNK
    masks = (ci >= si, ci <= si)
    br = lax.broadcasted_iota(jnp.int32, (LANES, 2 * LANES), 0) // HEAD_DIM
    bc = (lax.broadcasted_iota(jnp.int32, (LANES, 2 * LANES), 1) % LANES) // HEAD_DIM
    blockdiag = br == bc
    ones = jnp.ones((CHUNK, LANES), BF16)
    st_s[...] = jnp.zeros(st_s.shape, F32)
    gate_rows = slice(GATE_I, GATE_I + 4 * MLSTM_HEADS)

    def step(it, _):
        units = [(d, g) for d in range(2) for g in range(G)]
        ops = {}
        for d in range(2):
            first = (nc - (it + 1) * G) if d else it * G
            rt = rt_s[gate_rows, pl.ds(pl.multiple_of(first * CHUNK, G * CHUNK), G * CHUNK)]
            for g in range(G):
                local = (G - 1 - g) if d else g
                c = first + local
                rows = pl.ds(pl.multiple_of(c * CHUNK, CHUNK), CHUNK)
                c8 = pl.ds(pl.multiple_of(c * SUBLANES, SUBLANES), SUBLANES)
                r_row = jnp.concatenate(
                    [rt[_gate_lane(d, hh) - GATE_I:_gate_lane(d, hh) - GATE_I + 1,
                        local * CHUNK:(local + 1) * CHUNK] for hh in range(2)], axis=1)
                wp = wp_s[d, c8, :][0:1, :]
                ops[d, g] = dict(
                    rows=rows, q=q_s[rows, :], k=k_s[rows, :], qi=qi_s[d, rows, :], kw=kw_s[d, rows, :],
                    va=jnp.concatenate([v_ref[0, rows, :].astype(BF16), ones], axis=1),
                    expo=expo_s[d][rows, :] + r_row, floor=floor_s[d][rows, :],
                    wp=jnp.concatenate([wp, wp], axis=1))
        qk, kv = {}, {}
        for u in units:
            o = ops[u]
            qk[u] = _dot_nt(o["q"], _stack_heads(o["k"]))
            kv[u] = _dot_tn(o["kw"], o["va"])
        st_in = {}
        for d in range(2):
            st = st_s[d]
            for g in range(G):
                st_in[d, g] = st.astype(BF16)
                st = st * ops[d, g]["wp"] + jnp.where(blockdiag, kv[d, g], 0.0)
            st_s[d] = st
        for u in units:
            d = u[0]
            o = ops[u]
            smat = (qk[u] * jnp.where(masks[d], jnp.exp(o["expo"]), 0.0)).astype(BF16)
            va_bd = jnp.where(blockdiag, jnp.concatenate([o["va"], o["va"]], axis=0), jnp.zeros((), BF16))
            num = _dot(jnp.concatenate([o["qi"], smat], axis=1), jnp.concatenate([st_in[u], va_bd], axis=0))
            den = jnp.maximum(jnp.abs(num[:, LANES:]), o["floor"])
            outs[d][o["rows"], :] = num[:, :LANES] / den
        return 0

    lax.fori_loop(0, nc // G, step, 0)

    def post(t, _):
        r0 = pl.multiple_of(t * R, R)
        h = _head_rms(of_s[pl.ds(r0, R), :] + ob_s[pl.ds(r0, R), :], gain_ref[...])
        o_ref[0, pl.ds(r0, R), :] = (h * _sigmoid(og_ref[0, pl.ds(r0, R), :])).astype(BF16)
        return 0

    lax.fori_loop(0, T // R, post, 0)


def _mlstm(ml, small, lp):
    B, T, _ = ml.shape
    pairs = MLSTM_WIDTH // LANES
    col = lambda off: (lambda b, hp: (b, 0, off * pairs + hp))
    const = lambda b, hp: (0, 0)
    return pl.pallas_call(
        _mlstm_kernel,
        grid=(B, pairs),
        in_specs=[
            pl.BlockSpec((1, T, LANES), col(0)),
            pl.BlockSpec((1, T, LANES), col(1)),
            pl.BlockSpec((1, T, LANES), col(2)),
            pl.BlockSpec((1, T, LANES), col(3)),
            pl.BlockSpec((1, T, LANES), lambda b, hp: (b, 0, 0)),
            pl.BlockSpec((3, LANES), lambda b, hp: (0, hp)),
            pl.BlockSpec((3, LANES), lambda b, hp: (0, pairs + hp)),
            pl.BlockSpec((1, LANES), lambda b, hp: (0, hp)),
            pl.BlockSpec((1, LANES), lambda b, hp: (0, pairs + hp)),
            pl.BlockSpec((1, LANES), const),
            pl.BlockSpec((1, LANES), const),
        ],
        out_specs=pl.BlockSpec((1, T, LANES), lambda b, hp: (b, 0, hp)),
        out_shape=jax.ShapeDtypeStruct((B, T, MLSTM_WIDTH), BF16),
        scratch_shapes=(
            [pltpu.VMEM((T, LANES), BF16)] * 2
            + [pltpu.VMEM((T, LANES), F32)] * 4
            + [pltpu.VMEM(((T // CHUNK) * SUBLANES, LANES), F32)] * 4
            + [pltpu.VMEM((2, (T // CHUNK) * SUBLANES, LANES), F32)]
            + [pltpu.VMEM((2, T, LANES), BF16)] * 2
            + [pltpu.VMEM((LANES, T), F32),
               pltpu.VMEM((2, LANES, 2 * LANES), F32),
               pltpu.VMEM((T, LANES), F32), pltpu.VMEM((T, LANES), F32)]),
        compiler_params=_params("arbitrary", "arbitrary"),
        name="mlstm",
    )(ml, ml, ml, ml, small, lp["conv_w"], lp["conv_w"], lp["conv_b"], lp["conv_b"],
      lp["gate_bias"], lp["ml_gain"])


def _first_argmax(vals, lane):
    mx = jnp.max(vals, axis=-1, keepdims=True)
    idx = jnp.min(jnp.where(vals == mx, lane, LANES), axis=-1, keepdims=True)
    return mx, idx


def _out_route_kernel(x_ref, a_ref, gl_ref, ml_ref, w_ref, g_ref, wr_ref, br_ref,
                      x1_ref, t_ref, route_ref, routet_ref):
    tm = x_ref.shape[0]
    x1 = (x_ref[...]
          + _dot(a_ref[...], w_ref[0:ATTN_WIDTH, :])
          + _dot(gl_ref[...], w_ref[ATTN_WIDTH:ATTN_WIDTH + GLA_WIDTH, :])
          + _dot(ml_ref[...], w_ref[ATTN_WIDTH + GLA_WIDTH:, :]))
    x1_ref[...] = x1
    t = _rms(x1, g_ref[...])
    t_hi, t_lo = _split(t)
    t_ref[...] = t_hi
    both = _dot(t_hi, wr_ref[...])
    logits = (both[:, :LANES] + both[:, LANES:] + _dot(t_lo, wr_ref[:, :LANES])) + br_ref[...]
    lane = lax.broadcasted_iota(jnp.int32, (tm, LANES), 1)
    gl = jnp.where(lane < N_GROUPS, logits, -jnp.inf)
    gmax, gi = _first_argmax(gl, lane)
    g_prob = 1.0 / jnp.sum(jnp.exp(gl - gmax), axis=-1, keepdims=True)
    lo = ROUTE_W_LANE + gi * EXPERTS_PER_GROUP
    el = jnp.where((lane >= lo) & (lane < lo + EXPERTS_PER_GROUP), logits, -jnp.inf)
    v1, i1 = _first_argmax(el, lane)
    v2, i2 = _first_argmax(jnp.where(lane == i1, -jnp.inf, el), lane)
    e2 = jnp.exp(v2 - v1)
    w1 = g_prob / (1.0 + e2)
    w2 = g_prob * e2 / (1.0 + e2)
    comb = jnp.where(lane == i1, w1, jnp.where(lane == i2, w2, 0.0))
    route = jnp.where(lane == 0, gi.astype(F32), comb)
    route_ref[...] = route
    routet_ref[...] = route.T[0:SUBLANES, :]


def _out_route(x2, attn, gla_o, ml_o, lp, tm=512):
    N, D = x2.shape
    const = lambda i: (0, 0)
    tok = lambda i: (i, 0)
    return pl.pallas_call(
        _out_route_kernel,
        grid=(N // tm,),
        in_specs=[
            pl.BlockSpec((tm, D), tok),
            pl.BlockSpec((tm, ATTN_WIDTH), tok),
            pl.BlockSpec((tm, GLA_WIDTH), tok),
            pl.BlockSpec((tm, MLSTM_WIDTH), tok),
            pl.BlockSpec((ATTN_WIDTH + GLA_WIDTH + MLSTM_WIDTH, D), const),
            pl.BlockSpec((1, D), const),
            pl.BlockSpec((D, 2 * LANES), const),
            pl.BlockSpec((1, LANES), const),
        ],
        out_specs=[
            pl.BlockSpec((tm, D), tok),
            pl.BlockSpec((tm, D), tok),
            pl.BlockSpec((tm, LANES), tok),
            pl.BlockSpec((SUBLANES, tm), lambda i: (0, i)),
        ],
        out_shape=[
            jax.ShapeDtypeStruct((N, D), F32),
            jax.ShapeDtypeStruct((N, D), BF16),
            jax.ShapeDtypeStruct((N, LANES), F32),
            jax.ShapeDtypeStruct((SUBLANES, N), F32),
        ],
        compiler_params=_params("arbitrary"),
        name="out_route",
    )(x2, attn, gla_o, ml_o, lp["w_out"], lp["g_ffn"], lp["w_route"], lp["b_route"])


def _dispatch_kernel(t_ref, route_ref, routet_ref, xs_ref, cws_ref, pos_ref, cnt_ref):
    nb = ROUTE_BLOCK
    gi_row = routet_ref[0:1, :]
    sub = lax.broadcasted_iota(jnp.int32, (SUBLANES, nb), 0).astype(F32)
    onehot = (sub == gi_row)
    ri = lax.broadcasted_iota(jnp.int32, (nb, nb), 0)
    cj = lax.broadcasted_iota(jnp.int32, (nb, nb), 1)
    before = (ri < cj).astype(BF16)
    rank = _dot(onehot.astype(BF16), before)
    counts = jnp.broadcast_to(jnp.sum(onehot.astype(F32), axis=-1, keepdims=True), (SUBLANES, LANES))
    padded = jnp.ceil(counts * (1.0 / ROUTE_TILE)) * ROUTE_TILE
    srow = lax.broadcasted_iota(jnp.int32, (SUBLANES, LANES), 0)
    incl = padded
    s = 1
    while s < SUBLANES:
        incl = incl + jnp.where(srow >= s, pltpu.roll(incl, s, 0), 0.0)
        s *= 2
    start = incl - padded
    pos_row = jnp.sum(jnp.where(onehot, start[:, 0:1] + rank, 0.0), axis=0, keepdims=True)
    cnt_ref[0] = counts.astype(jnp.int32)

    route = route_ref[...]
    lane = lax.broadcasted_iota(jnp.int32, (nb, LANES), 1)
    onehot_c = lane.astype(F32) == route[:, 0:1]
    after = (cj < ri).astype(BF16)
    rank_c = _dot(after, onehot_c.astype(BF16))
    slane = lax.broadcasted_iota(jnp.int32, (SUBLANES, LANES), 1)
    start_c = jnp.sum(jnp.where(srow == slane, start, 0.0), axis=0, keepdims=True)
    pos_col = jnp.sum(jnp.where(onehot_c, start_c + rank_c, 0.0), axis=-1, keepdims=True)
    pos_ref[...] = jnp.broadcast_to(pos_col, (nb, LANES))

    comb = jnp.where(lane >= ROUTE_W_LANE, route, 0.0)
    c_hi, c_lo = _split(comb)
    c_lo2 = (comb - c_hi.astype(F32) - c_lo.astype(F32)).astype(BF16)
    cw = (c_hi.astype(F32) + pltpu.roll(c_lo.astype(F32), ROUTE_LO_SHIFT, 1)
          + pltpu.roll(c_lo2.astype(F32), 2 * ROUTE_LO_SHIFT, 1)).astype(BF16)
    tb = t_ref[...]
    for r in range(TILES_PER_BLOCK):
        rows = (lax.broadcasted_iota(jnp.int32, (ROUTE_TILE, nb), 0) + r * ROUTE_TILE).astype(F32)
        perm = (rows == pos_row).astype(BF16)
        xs_ref[r * ROUTE_TILE:(r + 1) * ROUTE_TILE, :] = _dot(perm, tb).astype(BF16)
        cws_ref[r * ROUTE_TILE:(r + 1) * ROUTE_TILE, :] = _dot(perm, cw)


def _dispatch(t, route, routet):
    N, D = t.shape
    nblk = N // ROUTE_BLOCK
    return pl.pallas_call(
        _dispatch_kernel,
        grid=(nblk,),
        in_specs=[
            pl.BlockSpec((ROUTE_BLOCK, D), lambda i: (i, 0)),
            pl.BlockSpec((ROUTE_BLOCK, LANES), lambda i: (i, 0)),
            pl.BlockSpec((SUBLANES, ROUTE_BLOCK), lambda i: (0, i)),
        ],
        out_specs=[
            pl.BlockSpec((ROUTE_ROWS, D), lambda i: (i, 0)),
            pl.BlockSpec((ROUTE_ROWS, LANES), lambda i: (i, 0)),
            pl.BlockSpec((ROUTE_BLOCK, LANES), lambda i: (i, 0)),
            pl.BlockSpec((1, SUBLANES, LANES), lambda i: (i, 0, 0)),
        ],
        out_shape=[
            jax.ShapeDtypeStruct((nblk * ROUTE_ROWS, D), BF16),
            jax.ShapeDtypeStruct((nblk * ROUTE_ROWS, LANES), F32),
            jax.ShapeDtypeStruct((N, LANES), F32),
            jax.ShapeDtypeStruct((nblk, SUBLANES, LANES), jnp.int32),
        ],
        compiler_params=_params("arbitrary"),
        name="dispatch",
    )(t, route, routet)


def _tile_schedule(cnt):
    nblk = cnt.shape[0]
    ntile = (cnt + ROUTE_TILE - 1) // ROUTE_TILE
    end = jnp.cumsum(ntile, axis=1)
    r = jnp.arange(TILES_PER_BLOCK, dtype=jnp.int32)
    grp = jnp.sum(r[None, :, None] >= end[:, None, :], axis=-1)
    grp = grp.reshape(-1).astype(jnp.int32)
    tile = jnp.arange(nblk * TILES_PER_BLOCK, dtype=jnp.int32)
    order = jnp.argsort(grp * (nblk * TILES_PER_BLOCK) + tile).astype(jnp.int32)
    n_active = jnp.sum(grp < N_GROUPS).astype(jnp.int32)
    g_sorted = grp[order]
    last_group = g_sorted[jnp.maximum(n_active - 1, 0)]
    g_sorted = jnp.where(g_sorted < N_GROUPS, g_sorted, last_group)
    return order, g_sorted, n_active[None]


def _expert_kernel(trow_ref, tgrp_ref, nact_ref, xs_ref, cws_ref, wg_ref, wu_ref, wd_ref, ys_ref):
    i = pl.program_id(0)

    @pl.when(i < nact_ref[0])
    def _():
        x = xs_ref[...]
        cws = cws_ref[...]
        lane = lax.broadcasted_iota(jnp.int32, cws.shape, 1)
        first = ROUTE_W_LANE + tgrp_ref[i] * EXPERTS_PER_GROUP
        y = jnp.zeros(ys_ref.shape, F32)

        def hidden(j):
            return _dot(x, wg_ref[j]), _dot(x, wu_ref[j])

        h = hidden(0)
        for j in range(EXPERTS_PER_GROUP):
            h_next = hidden(j + 1) if j + 1 < EXPERTS_PER_GROUP else None
            off = lane - (first + j)
            sel = (off == 0) | (off == ROUTE_LO_SHIFT) | (off == 2 * ROUTE_LO_SHIFT)
            wj = jnp.sum(jnp.where(sel, cws, 0.0), axis=-1, keepdims=True)
            a = (h[0] * _sigmoid(h[0]) * h[1]).astype(BF16)
            y = y + wj * _dot(a, wd_ref[j])
            h = h_next
        ys_ref[...] = y.astype(BF16)

    @pl.when(i >= nact_ref[0])
    def _():
        ys_ref[...] = jnp.zeros(ys_ref.shape, BF16)


def _experts(xs, cws, order, grp, n_active, lp):
    rows, D = xs.shape
    n_tiles = rows // ROUTE_TILE
    tile = lambda i, trow, tgrp, nact: (trow[i], 0)
    layer_groups = lp["layer"] * N_GROUPS
    wsel = lambda i, trow, tgrp, nact: (layer_groups + tgrp[i], 0, 0)
    return pl.pallas_call(
        _expert_kernel,
        grid_spec=pltpu.PrefetchScalarGridSpec(
            num_scalar_prefetch=3,
            grid=(n_tiles,),
            in_specs=[
                pl.BlockSpec((ROUTE_TILE, D), tile),
                pl.BlockSpec((ROUTE_TILE, LANES), tile),
                pl.BlockSpec((EXPERTS_PER_GROUP, D, D_EXPERT), wsel),
                pl.BlockSpec((EXPERTS_PER_GROUP, D, D_EXPERT), wsel),
                pl.BlockSpec((EXPERTS_PER_GROUP, D_EXPERT, D), wsel),
            ],
            out_specs=pl.BlockSpec((ROUTE_TILE, D), tile),
        ),
        out_shape=jax.ShapeDtypeStruct((rows, D), BF16),
        compiler_params=_params("arbitrary"),
        name="experts",
    )(order, grp, n_active, xs, cws, lp["w_gate"], lp["w_up"], lp["w_down"])


def _combine_kernel(ys_ref, pos_ref, x1_ref, p_ref, g_ref, wpg_ref, wpp_ref, gfin_ref, o_ref,
                    *, embed, final):
    tm = x1_ref.shape[0]
    pos = pos_ref[...]
    lane = lax.broadcasted_iota(jnp.int32, (tm, LANES), 1).astype(F32)
    perm_t = jnp.concatenate(
        [(lane + r * LANES == pos).astype(BF16) for r in range(ROUTE_ROWS // LANES)], axis=1)
    x = x1_ref[...] + _dot(perm_t, ys_ref[...])
    if embed:
        gate = _sigmoid(_dot(_rms(x, g_ref[...]).astype(BF16), wpg_ref[...]))
        x = x + gate * _dot(p_ref[...].astype(BF16), wpp_ref[...])
    if final:
        x = _rms(x, gfin_ref[...])
    o_ref[...] = x


def _combine(ys, pos, x1, p2, lp, g_final, embed=True, final=False, tm=512):
    N, D = x1.shape
    inner = ROUTE_BLOCK // tm
    tok = lambda b, i: (b * inner + i, 0)
    const = lambda b, i: (0, 0)
    return pl.pallas_call(
        functools.partial(_combine_kernel, embed=embed, final=final),
        grid=(N // ROUTE_BLOCK, inner),
        in_specs=[
            pl.BlockSpec((ROUTE_ROWS, D), lambda b, i: (b, 0)),
            pl.BlockSpec((tm, LANES), tok),
            pl.BlockSpec((tm, D), tok),
            pl.BlockSpec((tm, p2.shape[1]), tok),
            pl.BlockSpec((1, D), const),
            pl.BlockSpec((D, D), const),
            pl.BlockSpec((p2.shape[1], D), const),
            pl.BlockSpec((1, D), const),
        ],
        out_specs=pl.BlockSpec((tm, D), tok),
        out_shape=jax.ShapeDtypeStruct((N, D), F32),
        compiler_params=_params("arbitrary", "arbitrary"),
        name="combine",
    )(ys, pos, x1, p2, lp["g_ple"], lp["w_pg"], lp["w_pp"], g_final)


def _moe(t, route, routet, lp):
    xs, cws, pos, cnt = _dispatch(t, route, routet)
    order, grp, n_active = _tile_schedule(cnt[:, :N_GROUPS, 0])
    return _experts(xs, cws, order, grp, n_active, lp), pos


def _moe_debug(x2, lp):
    N, D = x2.shape
    zeros = lambda w: jnp.zeros((N, w), BF16)
    x1, t, route, routet = _out_route(x2, zeros(ATTN_WIDTH), zeros(GLA_WIDTH), zeros(MLSTM_WIDTH), lp)
    ys, pos = _moe(t, route, routet, lp)
    p2 = jnp.zeros((N, lp["w_pp"].shape[0]), F32)
    return _combine(ys, pos, x1, p2, lp, lp["g_ple"], embed=False) - x1


def _permute_in_cols(w):
    glr0 = ATTN_WIDTH + 2 * KV_WIDTH + 4 * GLA_WIDTH
    ml0 = glr0 + 2 * GLA_RANK
    mg0 = ml0 + 4 * MLSTM_WIDTH
    end = mg0 + 4 * MLSTM_HEADS
    assert end == w.shape[1]
    pad = jnp.zeros((w.shape[0], IN_PERM_WIDTH - end), w.dtype)
    return jnp.concatenate([w[:, :glr0], w[:, ml0:mg0], w[:, glr0:ml0], w[:, mg0:end], pad], axis=1)


def _rope_tables(T):
    t = jnp.arange(T, dtype=F32)
    row = jnp.floor(t / GRID_W)
    col = t - row * GRID_W
    inv = ROPE_THETA ** (-jnp.arange(0, ROPE_AXIS_DIM, 2, dtype=F32) / ROPE_AXIS_DIM)
    ang_r = row[:, None] * inv[None, :]
    ang_c = col[:, None] * inv[None, :]
    cos_h = jnp.concatenate([jnp.cos(ang_r), jnp.cos(ang_r), jnp.cos(ang_c), jnp.cos(ang_c)], axis=1)
    sin_h = jnp.concatenate([-jnp.sin(ang_r), jnp.sin(ang_r), -jnp.sin(ang_c), jnp.sin(ang_c)], axis=1)
    reps = LANES // HEAD_DIM
    return jnp.tile(cos_h, (1, reps)), jnp.tile(sin_h, (1, reps))


def _head_onehots():
    n_heads = QK_WIDTH // HEAD_DIM
    eh = np.zeros((QK_WIDTH, LANES), np.float32)
    eh[np.arange(QK_WIDTH), np.arange(QK_WIDTH) // HEAD_DIM] = 1.0
    assert n_heads <= LANES
    return jnp.asarray(eh, BF16), jnp.asarray(eh.T, BF16)


def kernel(x, p, norm_mix_g, w_in, attn_q_norm_g, attn_k_norm_g, gla_w_decay, gla_b_decay,
           gla_out_norm_g, mlstm_conv_w, mlstm_conv_b, mlstm_b_input, mlstm_b_forget,
           mlstm_out_norm_g, w_out, norm_ffn_g, w_group, b_group, w_router, b_router,
           w_expert_gate, w_expert_up, w_expert_down, norm_ple_g, w_ple_gate, w_ple_proj,
           final_norm_g):
    params = dict(
        norm_mix_g=norm_mix_g, w_in=w_in, attn_q_norm_g=attn_q_norm_g, attn_k_norm_g=attn_k_norm_g,
        gla_w_decay=gla_w_decay, gla_b_decay=gla_b_decay, gla_out_norm_g=gla_out_norm_g,
        mlstm_conv_w=mlstm_conv_w, mlstm_conv_b=mlstm_conv_b, mlstm_b_input=mlstm_b_input,
        mlstm_b_forget=mlstm_b_forget, mlstm_out_norm_g=mlstm_out_norm_g, w_out=w_out,
        norm_ffn_g=norm_ffn_g, w_group=w_group, b_group=b_group, w_router=w_router,
        b_router=b_router, w_expert_gate=w_expert_gate, w_expert_up=w_expert_up,
        w_expert_down=w_expert_down, norm_ple_g=norm_ple_g, w_ple_gate=w_ple_gate,
        w_ple_proj=w_ple_proj)
    B, T, D = x.shape
    cos, sin = _rope_tables(T)
    eh, eht = _head_onehots()
    N = B * T
    depth = w_in.shape[0]
    x2 = x.reshape(N, D)
    g_final = final_norm_g[None, :]
    for i in range(depth):
        lp = _layer_params(params, i)
        qt, k, vt, gla, ml, small = _in_proj(x2, lp["g_mix"], lp["w_in"], cos, sin, lp["qk_gain"],
                                             eh, eht, B, T)
        attn = _attention(lp["attn_safe"], qt, k, vt).reshape(N, ATTN_WIDTH)
        gla_o = _gla(gla, small, lp).reshape(N, GLA_WIDTH)
        ml_o = _mlstm(ml, small, lp).reshape(N, MLSTM_WIDTH)
        x1, t, route, routet = _out_route(x2, attn, gla_o, ml_o, lp)
        ys, pos = _moe(t, route, routet, lp)
        x2 = _combine(ys, pos, x1, p[i].reshape(N, -1), lp, g_final, final=(i == depth - 1))
    return x2.reshape(B, T, D)


def _split_w(w):
    hi = w.astype(BF16)
    return hi, (w - hi.astype(F32)).astype(BF16)


def _stacked_experts(w):
    return w.astype(BF16).reshape((w.shape[0] * w.shape[1],) + w.shape[2:])


def _layer_params(p, i):
    D = p["w_in"].shape[1]
    w_in = _permute_in_cols(p["w_in"][i])
    gq, gk = p["attn_q_norm_g"][i], p["attn_k_norm_g"][i]
    qk_gain = jnp.concatenate([jnp.tile(gq, ATTN_HEADS) * (HEAD_DIM ** -0.5 * LOG2E),
                               jnp.tile(gk, ATTN_KV_HEADS)])
    logit_bound = HEAD_DIM ** 0.5 * jnp.max(jnp.abs(gq)) * jnp.max(jnp.abs(gk))
    attn_safe = (logit_bound <= ATTN_SAFE_LOGIT).astype(jnp.int32)[None]
    wd = jnp.zeros((2, LANES, GLA_WIDTH), F32)
    wd = wd.at[0, :GLA_RANK].set(p["gla_w_decay"][i, 0]).at[1, GLA_RANK:2 * GLA_RANK].set(p["gla_w_decay"][i, 1])
    wd_hi, wd_lo = _split_w(wd)
    gate_bias = jnp.zeros((LANES,), F32).at[SMALL_GATE_LANE:SMALL_GATE_LANE + 4 * MLSTM_HEADS].set(
        jnp.concatenate([p["mlstm_b_input"][i, 0], p["mlstm_b_forget"][i, 0],
                         p["mlstm_b_input"][i, 1], p["mlstm_b_forget"][i, 1]]))
    w_route = jnp.zeros((D, LANES), F32)
    w_route = w_route.at[:, :N_GROUPS].set(p["w_group"][i])
    w_route = w_route.at[:, ROUTE_W_LANE:ROUTE_W_LANE + N_EXPERTS].set(p["w_router"][i])
    wr_hi, wr_lo = _split_w(w_route)
    b_route = jnp.zeros((LANES,), F32).at[:N_GROUPS].set(p["b_group"][i])
    b_route = b_route.at[ROUTE_W_LANE:ROUTE_W_LANE + N_EXPERTS].set(p["b_router"][i])
    return dict(
        g_mix=p["norm_mix_g"][i][None, :],
        w_in=w_in.astype(BF16),
        qk_gain=qk_gain[None, :],
        attn_safe=attn_safe,
        wd_hi=wd_hi, wd_lo=wd_lo,
        bd=p["gla_b_decay"][i][:, None, :],
        gla_gain=jnp.tile(p["gla_out_norm_g"][i], LANES // HEAD_DIM)[None, :],
        conv_w=p["mlstm_conv_w"][i],
        conv_b=p["mlstm_conv_b"][i][None, :],
        gate_bias=gate_bias[None, :],
        ml_gain=jnp.tile(p["mlstm_out_norm_g"][i], LANES // HEAD_DIM)[None, :],
        w_out=p["w_out"][i].astype(BF16),
        g_ffn=p["norm_ffn_g"][i][None, :],
        w_route=jnp.concatenate([wr_hi, wr_lo], axis=1), b_route=b_route[None, :],
        layer=i,
        w_gate=_stacked_experts(p["w_expert_gate"]),
        w_up=_stacked_experts(p["w_expert_up"]),
        w_down=_stacked_experts(p["w_expert_down"]),
        g_ple=p["norm_ple_g"][i][None, :],
        w_pg=p["w_ple_gate"][i].astype(BF16),
        w_pp=p["w_ple_proj"][i].astype(BF16),
    )
```

```python
import functools

import jax
import jax.numpy as jnp
import numpy as np
from jax import lax
from jax.experimental import pallas as pl
from jax.experimental.pallas import tpu as pltpu

F32 = jnp.float32
BF16 = jnp.bfloat16

GRID_W = 64
HEAD_DIM = 64
ATTN_HEADS = 8
ATTN_KV_HEADS = 2
GLA_HEADS = 4
MLSTM_HEADS = 4
ATTN_WIDTH = ATTN_HEADS * HEAD_DIM
KV_WIDTH = ATTN_KV_HEADS * HEAD_DIM
GLA_WIDTH = GLA_HEADS * HEAD_DIM
MLSTM_WIDTH = MLSTM_HEADS * HEAD_DIM
GLA_RANK = 16
GLA_TAU = 16.0
CHUNK = 64
ROPE_THETA = 10000.0
ROPE_AXIS_DIM = HEAD_DIM // 2
N_GROUPS = 4
EXPERTS_PER_GROUP = 4
N_EXPERTS = N_GROUPS * EXPERTS_PER_GROUP
D_EXPERT = 512
EPS = 1e-6
NEG = -1e30

LANES = 128
SUBLANES = 8
VMEM_LIMIT_BYTES = 56 * 1024 * 1024

QK_WIDTH = ATTN_WIDTH + KV_WIDTH
OFF_V = QK_WIDTH
OFF_GLA = OFF_V + KV_WIDTH
OFF_ML = OFF_GLA + 4 * GLA_WIDTH
OFF_SMALL = OFF_ML + 4 * MLSTM_WIDTH
IN_PERM_WIDTH = OFF_SMALL + LANES
SMALL_GATE_LANE = 2 * GLA_RANK

ROUTE_BLOCK = 1024
ROUTE_TILE = 128
ROUTE_ROWS = ROUTE_BLOCK + N_GROUPS * ROUTE_TILE
TILES_PER_BLOCK = ROUTE_ROWS // ROUTE_TILE
ROUTE_W_LANE = 8
ROUTE_LO_SHIFT = 32
ROUTE_SUBTILE = 256


def _dot(a, b):
    return jnp.dot(a, b, preferred_element_type=F32)


def _dot_nt(a, b):
    return lax.dot_general(a, b, (((1,), (1,)), ((), ())), preferred_element_type=F32)


def _dot_tn(a, b):
    return lax.dot_general(a, b, (((0,), (0,)), ((), ())), preferred_element_type=F32)


def _split(a):
    hi = a.astype(BF16)
    lo = (a - hi.astype(F32)).astype(BF16)
    return hi, lo


def _dot3(a, w_hi, w_lo):
    a_hi, a_lo = _split(a)
    return _dot(a_hi, w_hi) + _dot(a_lo, w_hi) + _dot(a_hi, w_lo)


def _log_sigmoid(x):
    return jnp.minimum(x, 0.0) - jnp.log1p(jnp.exp(-jnp.abs(x)))


def _sigmoid(x):
    return 1.0 / (1.0 + jnp.exp(-x))


def _rms(x, g):
    return x * lax.rsqrt(jnp.mean(x * x, axis=-1, keepdims=True) + EPS) * g


def _params(*semantics):
    return pltpu.CompilerParams(dimension_semantics=semantics, vmem_limit_bytes=VMEM_LIMIT_BYTES)


def _in_proj_kernel(x_ref, g_ref, w_ref, cos_ref, sin_ref, cost_ref, sint_ref, gq_ref, gk_ref,
                    eh_ref, eht_ref, qt_ref, k_ref, vt_ref, gla_ref, ml_ref, small_ref):
    tm = x_ref.shape[0]
    h = _rms(x_ref[...], g_ref[...])
    z = _dot(h.astype(BF16), w_ref[...])

    heads = LANES // HEAD_DIM
    half = ROPE_AXIS_DIM // 2
    gq = jnp.concatenate([gq_ref[...]] * (tm // LANES), axis=1)
    cost = cost_ref[...]
    sint = sint_ref[...]
    for c in range(ATTN_WIDTH // LANES):
        zt = z[:, c * LANES:(c + 1) * LANES].T
        z3 = zt.reshape(heads, HEAD_DIM, tm)
        inv = lax.rsqrt(jnp.mean(z3 * z3, axis=1, keepdims=True) + EPS)
        y = (z3 * inv).reshape(LANES, tm) * gq
        partner = jnp.concatenate(
            [y[(r ^ 1) * half:((r ^ 1) + 1) * half, :] for r in range(LANES // half)], axis=0)
        qt_ref[0, c * LANES:(c + 1) * LANES, :] = (y * cost + partner * sint).astype(BF16)

    kz = z[:, ATTN_WIDTH:QK_WIDTH]
    sq_hi, sq_lo = _split(kz * kz)
    ssq = _dot(sq_hi, eh_ref[...]) + _dot(sq_lo, eh_ref[...])
    inv_hi, inv_lo = _split(lax.rsqrt(ssq * (1.0 / HEAD_DIM) + EPS))
    yk = kz * (_dot(inv_hi, eht_ref[...]) + _dot(inv_lo, eht_ref[...])) * gk_ref[...]
    lane = lax.broadcasted_iota(jnp.int32, (tm, LANES), 1)
    partner = jnp.where((lane % ROPE_AXIS_DIM) < half,
                        pltpu.roll(yk, LANES - half, 1), pltpu.roll(yk, half, 1))
    k_ref[0] = (yk * cos_ref[...] + partner * sin_ref[...]).astype(BF16)
    vt_ref[0] = z[:, OFF_V:OFF_V + KV_WIDTH].T.astype(BF16)
    gla_ref[0] = z[:, OFF_GLA:OFF_ML]
    ml_ref[0] = z[:, OFF_ML:OFF_SMALL]
    small_ref[0] = z[:, OFF_SMALL:IN_PERM_WIDTH]


def _in_proj(x2, lp, rope, B, T, tm=512):
    N, D = x2.shape
    tpb = T // tm
    const = lambda i: (0, 0)
    tok3 = lambda i: (i // tpb, i % tpb, 0)
    tokT = lambda i: (i // tpb, 0, i % tpb)
    return pl.pallas_call(
        _in_proj_kernel,
        grid=(N // tm,),
        in_specs=[
            pl.BlockSpec((tm, D), lambda i: (i, 0)),
            pl.BlockSpec((1, D), const),
            pl.BlockSpec((D, IN_PERM_WIDTH), const),
            pl.BlockSpec((tm, LANES), lambda i: (i % tpb, 0)),
            pl.BlockSpec((tm, LANES), lambda i: (i % tpb, 0)),
            pl.BlockSpec((LANES, tm), lambda i: (0, i % tpb)),
            pl.BlockSpec((LANES, tm), lambda i: (0, i % tpb)),
            pl.BlockSpec((LANES, LANES), const),
            pl.BlockSpec((1, LANES), const),
            pl.BlockSpec((LANES, LANES), const),
            pl.BlockSpec((LANES, LANES), const),
        ],
        out_specs=[
            pl.BlockSpec((1, ATTN_WIDTH, tm), tokT),
            pl.BlockSpec((1, tm, KV_WIDTH), tok3),
            pl.BlockSpec((1, KV_WIDTH, tm), tokT),
            pl.BlockSpec((1, tm, 4 * GLA_WIDTH), tok3),
            pl.BlockSpec((1, tm, 4 * MLSTM_WIDTH), tok3),
            pl.BlockSpec((1, tm, LANES), tok3),
        ],
        out_shape=[
            jax.ShapeDtypeStruct((B, ATTN_WIDTH, T), BF16),
            jax.ShapeDtypeStruct((B, T, KV_WIDTH), BF16),
            jax.ShapeDtypeStruct((B, KV_WIDTH, T), BF16),
            jax.ShapeDtypeStruct((B, T, 4 * GLA_WIDTH), F32),
            jax.ShapeDtypeStruct((B, T, 4 * MLSTM_WIDTH), F32),
            jax.ShapeDtypeStruct((B, T, LANES), F32),
        ],
        compiler_params=_params("arbitrary"),
        name="in_proj",
    )(x2, lp["g_mix"], lp["w_in"], rope["cos"], rope["sin"], rope["cos_t"], rope["sin_t"],
      lp["q_gain_t"], lp["k_gain"], rope["eh"], rope["eht"])


ATTN_SAFE_LOGIT = 40.0
LOG2E = 1.4426950408889634


def _attn_kernel(safe_ref, qt_ref, k_ref, vt_ref, o_ref, *, tk):
    tq = qt_ref.shape[2]
    T = k_ref.shape[1]
    G = ATTN_HEADS // ATTN_KV_HEADS
    n = G * tq
    zeros = jnp.zeros((HEAD_DIM, n), BF16)

    def q_operand(j):
        base = j * G * HEAD_DIM
        qs = jnp.concatenate(
            [qt_ref[0, base + h * HEAD_DIM:base + (h + 1) * HEAD_DIM, :] for h in range(G)], axis=1)
        return jnp.concatenate([qs, zeros] if j == 0 else [zeros, qs], axis=0)

    def finish(j, acc, l):
        base = j * G * HEAD_DIM
        o = acc * (1.0 / l)
        ot = jnp.concatenate([o[:, h * tq:(h + 1) * tq] for h in range(G)], axis=0)
        o_ref[0, :, base:base + G * HEAD_DIM] = ot.T.astype(BF16)

    @pl.when(safe_ref[0] == 1)
    def _():
        qps = [q_operand(j) for j in range(ATTN_KV_HEADS)]
        units = [(c, j) for c in range(T // tk) for j in range(ATTN_KV_HEADS)]

        def scores(u):
            c, j = units[u]
            return _dot(k_ref[0, c * tk:(c + 1) * tk, :], qps[j])

        l8 = [jnp.zeros((SUBLANES, n), F32)] * ATTN_KV_HEADS
        acc = [jnp.zeros((HEAD_DIM, n), F32)] * ATTN_KV_HEADS
        st = scores(0)
        for u, (c, j) in enumerate(units):
            st_next = scores(u + 1) if u + 1 < len(units) else None
            p = jnp.exp2(st)
            l8[j] = l8[j] + jnp.sum(p.reshape(tk // SUBLANES, SUBLANES, n), axis=0)
            vc = vt_ref[0, j * HEAD_DIM:(j + 1) * HEAD_DIM, c * tk:(c + 1) * tk]
            acc[j] = acc[j] + _dot(vc, p.astype(BF16))
            st = st_next
        for j in range(ATTN_KV_HEADS):
            finish(j, acc[j], jnp.sum(l8[j], axis=0, keepdims=True))

    @pl.when(safe_ref[0] == 0)
    def _():
        for j in range(ATTN_KV_HEADS):
            qp = q_operand(j)

            def body(c, carry, qp=qp, j=j):
                m, l, acc = carry
                off = pl.multiple_of(c * tk, tk)
                st = _dot(k_ref[0, pl.ds(off, tk), :], qp)
                m_new = jnp.maximum(m, jnp.max(st, axis=0, keepdims=True))
                alpha = jnp.exp2(m - m_new)
                p = jnp.exp2(st - m_new)
                l = alpha * l + jnp.sum(p, axis=0, keepdims=True)
                vc = vt_ref[0, j * HEAD_DIM:(j + 1) * HEAD_DIM, pl.ds(off, tk)]
                return m_new, l, alpha * acc + _dot(vc, p.astype(BF16))

            init = (jnp.full((1, n), NEG, F32), jnp.zeros((1, n), F32), jnp.zeros((HEAD_DIM, n), F32))
            _, l, acc = lax.fori_loop(0, T // tk, body, init)
            finish(j, acc, l)


def _attention(safe, qt, k, vt, tq=256, tk=128):
    B, _, T = qt.shape
    tk = min(tk, T)
    return pl.pallas_call(
        functools.partial(_attn_kernel, tk=tk),
        grid_spec=pltpu.PrefetchScalarGridSpec(
            num_scalar_prefetch=1,
            grid=(B, T // tq),
            in_specs=[
                pl.BlockSpec((1, ATTN_WIDTH, tq), lambda b, i, s: (b, 0, i)),
                pl.BlockSpec((1, T, KV_WIDTH), lambda b, i, s: (b, 0, 0)),
                pl.BlockSpec((1, KV_WIDTH, T), lambda b, i, s: (b, 0, 0)),
            ],
            out_specs=pl.BlockSpec((1, tq, ATTN_WIDTH), lambda b, i, s: (b, i, 0)),
        ),
        out_shape=jax.ShapeDtypeStruct((B, T, ATTN_WIDTH), BF16),
        compiler_params=_params("arbitrary", "arbitrary"),
        name="attention",
    )(safe, qt, k, vt)


def _chunk_scan(x, pos, op, fill, reverse):
    rows = x.shape[0]
    s = 1
    while s < CHUNK:
        if reverse:
            shifted = jnp.where(pos < CHUNK - s, pltpu.roll(x, rows - s, 0), fill)
        else:
            shifted = jnp.where(pos >= s, pltpu.roll(x, s, 0), fill)
        x = op(x, shifted)
        s *= 2
    return x


def _chunk_pos(rows):
    return lax.broadcasted_iota(jnp.int32, (rows, LANES), 0) % CHUNK


def _chunk_row(a, reverse_dir, idx_fwd, idx_bwd):
    rows = a.shape[0]
    a3 = a.reshape(rows // CHUNK, CHUNK, LANES)
    i = idx_bwd if reverse_dir else idx_fwd
    return jnp.broadcast_to(a3[:, i:i + 1, :], a3.shape).reshape(rows, LANES)


def _stack_heads(x):
    lane = lax.broadcasted_iota(jnp.int32, x.shape, 1)
    zero = jnp.zeros_like(x)
    return jnp.concatenate([jnp.where(lane < HEAD_DIM, x, zero), jnp.where(lane >= HEAD_DIM, x, zero)], axis=0)


def _select_heads(x):
    c = x.shape[0] // 2
    lane = lax.broadcasted_iota(jnp.int32, (c, x.shape[1]), 1)
    return jnp.where(lane < HEAD_DIM, x[:c], x[c:])


def _stacked_causal_masks():
    ci = lax.broadcasted_iota(jnp.int32, (2 * CHUNK, CHUNK), 0) % CHUNK
    si = lax.broadcasted_iota(jnp.int32, (2 * CHUNK, CHUNK), 1)
    return ci >= si, ci <= si


def _pair_blockdiag(width):
    r = lax.broadcasted_iota(jnp.int32, (LANES, width), 0) // HEAD_DIM
    c = lax.broadcasted_iota(jnp.int32, (LANES, width), 1) // (width // 2)
    return r == c


def _head_rms(o, gain):
    lane = lax.broadcasted_iota(jnp.int32, o.shape, 1)
    lo = lane < HEAD_DIM
    sq = o * o
    s_lo = jnp.sum(jnp.where(lo, sq, 0.0), axis=-1, keepdims=True)
    s_hi = jnp.sum(jnp.where(lo, 0.0, sq), axis=-1, keepdims=True)
    ms = jnp.where(lo, s_lo, s_hi) * (1.0 / HEAD_DIM)
    return o * lax.rsqrt(ms + EPS) * gain


PREP_ROWS = 512
GLA_CHUNKS_PER_STEP = 4


def _gla_kernel(q_ref, k_ref, v_ref, g_ref, small_ref, wdh_ref, wdl_ref, bd_ref, gain_ref,
                o_ref, qe_s, ke_s, kd_s, qb_s, dec_s, of_s, ob_s):
    T = q_ref.shape[1]
    nc = T // CHUNK
    R = min(PREP_ROWS, T)
    pos = _chunk_pos(R)
    w_hi = jnp.concatenate([wdh_ref[0], wdh_ref[1]], axis=1)
    w_lo = jnp.concatenate([wdl_ref[0], wdl_ref[1]], axis=1)
    bias = jnp.concatenate([bd_ref[0], bd_ref[1]], axis=1)

    def prep(t, _):
        r0 = pl.multiple_of(t * R, R)
        q = q_ref[0, pl.ds(r0, R), :] * HEAD_DIM ** -0.5
        k = k_ref[0, pl.ds(r0, R), :]
        la2 = _log_sigmoid(_dot3(small_ref[0, pl.ds(r0, R), :], w_hi, w_lo) + bias) * (1.0 / GLA_TAU)
        for d in range(2):
            b = _chunk_scan(la2[:, d * LANES:(d + 1) * LANES], pos, jnp.add, 0.0, reverse=bool(d))
            b_mid = _chunk_row(b, d, CHUNK // 2 - 1, CHUNK // 2)
            b_last = _chunk_row(b, d, CHUNK - 1, 0)
            qe_s[d, pl.ds(r0, R), :] = (q * jnp.exp(b - b_mid)).astype(BF16)
            ke_s[d, pl.ds(r0, R), :] = (k * jnp.exp(b_mid - b)).astype(BF16)
            kd_s[d, pl.ds(r0, R), :] = (k * jnp.exp(b_last - b)).astype(BF16)
            qb_s[d, pl.ds(r0, R), :] = (q * jnp.exp(b)).astype(BF16)
            dec = jnp.exp(b_last).reshape(R // CHUNK, CHUNK, LANES)[:, :SUBLANES, :]
            dec_s[d, pl.ds(pl.multiple_of(t * (R // CHUNK) * SUBLANES, SUBLANES), (R // CHUNK) * SUBLANES), :] = (
                dec.reshape((R // CHUNK) * SUBLANES, LANES))
        return 0

    lax.fori_loop(0, T // R, prep, 0)

    outs = (of_s, ob_s)
    masks = _stacked_causal_masks()
    blockdiag = _pair_blockdiag(LANES)
    G = GLA_CHUNKS_PER_STEP

    def step(it, states):
        units = [(d, g) for d in range(2) for g in range(G)]
        ops = {}
        for d, g in units:
            c = it * G + g
            c = (nc - 1 - c) if d else c
            r0 = pl.multiple_of(c * CHUNK, CHUNK)
            ops[d, g] = dict(
                r0=r0,
                qe=qe_s[d, pl.ds(r0, CHUNK), :], ke=ke_s[d, pl.ds(r0, CHUNK), :],
                kd=kd_s[d, pl.ds(r0, CHUNK), :], qb=qb_s[d, pl.ds(r0, CHUNK), :],
                v=v_ref[0, pl.ds(r0, CHUNK), :].astype(BF16),
                dec=dec_s[d, pl.ds(pl.multiple_of(c * SUBLANES, SUBLANES), SUBLANES), :][0:1, :])
        a2, kv = {}, {}
        for u in units:
            o = ops[u]
            a2[u] = _dot_nt(_stack_heads(o["qe"]), o["ke"])
            kv[u] = _dot_tn(o["v"], o["kd"])
        st_in = {}
        new_states = []
        for d in range(2):
            st = states[d]
            for g in range(G):
                st_in[d, g] = st.astype(BF16)
                st = st * ops[d, g]["dec"] + jnp.where(blockdiag, kv[d, g], 0.0)
            new_states.append(st)
        for u in units:
            o = ops[u]
            a = jnp.where(masks[u[0]], a2[u], 0.0).astype(BF16)
            intra = _select_heads(_dot(a, o["v"]))
            outs[u[0]][pl.ds(o["r0"], CHUNK), :] = intra + _dot_nt(o["qb"], st_in[u])
        return tuple(new_states)

    zero = jnp.zeros((LANES, LANES), F32)
    lax.fori_loop(0, nc // G, step, (zero, zero))

    def post(t, _):
        r0 = pl.multiple_of(t * R, R)
        o = _head_rms(of_s[pl.ds(r0, R), :] + ob_s[pl.ds(r0, R), :], gain_ref[...])
        g = g_ref[0, pl.ds(r0, R), :]
        o_ref[0, pl.ds(r0, R), :] = (o * (g * _sigmoid(g))).astype(BF16)
        return 0

    lax.fori_loop(0, T // R, post, 0)


def _gla(gla, small, lp):
    B, T, _ = gla.shape
    pairs = GLA_WIDTH // LANES
    col = lambda off: (lambda b, hp: (b, 0, off * pairs + hp))
    return pl.pallas_call(
        _gla_kernel,
        grid=(B, pairs),
        in_specs=[
            pl.BlockSpec((1, T, LANES), col(0)),
            pl.BlockSpec((1, T, LANES), col(1)),
            pl.BlockSpec((1, T, LANES), col(2)),
            pl.BlockSpec((1, T, LANES), col(3)),
            pl.BlockSpec((1, T, LANES), lambda b, hp: (b, 0, 0)),
            pl.BlockSpec((2, LANES, LANES), lambda b, hp: (0, 0, hp)),
            pl.BlockSpec((2, LANES, LANES), lambda b, hp: (0, 0, hp)),
            pl.BlockSpec((2, 1, LANES), lambda b, hp: (0, 0, hp)),
            pl.BlockSpec((1, LANES), lambda b, hp: (0, 0)),
        ],
        out_specs=pl.BlockSpec((1, T, LANES), lambda b, hp: (b, 0, hp)),
        out_shape=jax.ShapeDtypeStruct((B, T, GLA_WIDTH), BF16),
        scratch_shapes=[
            pltpu.VMEM((2, T, LANES), BF16), pltpu.VMEM((2, T, LANES), BF16),
            pltpu.VMEM((2, T, LANES), BF16), pltpu.VMEM((2, T, LANES), BF16),
            pltpu.VMEM((2, (T // CHUNK) * SUBLANES, LANES), F32),
            pltpu.VMEM((T, LANES), F32), pltpu.VMEM((T, LANES), F32),
        ],
        compiler_params=_params("arbitrary", "arbitrary"),
        name="gla",
    )(gla, gla, gla, gla, small, lp["wd_hi"], lp["wd_lo"], lp["bd"], lp["gla_gain"])


GATE_I = SMALL_GATE_LANE
GATE_F = SMALL_GATE_LANE + MLSTM_HEADS


def _gate_lane(d, hh):
    return GATE_F + 2 * MLSTM_HEADS * d + hh


MLSTM_CHUNKS_PER_STEP = 4


def _chunk_rows8(a, row):
    n = a.shape[0] // CHUNK
    a3 = a.reshape(n, CHUNK, LANES)
    full = jnp.broadcast_to(a3[:, row:row + 1, :], a3.shape)
    return full[:, :SUBLANES, :].reshape(n * SUBLANES, LANES)


def _expand_rows8(a8):
    n, w = a8.shape[0] // SUBLANES, a8.shape[1]
    a3 = a8.reshape(n, SUBLANES, w)[:, 0:1, :]
    return jnp.broadcast_to(a3, (n, CHUNK, w)).reshape(n * CHUNK, w)


def _gate_select():
    src = lax.broadcasted_iota(jnp.int32, (LANES, 2 * LANES), 0)
    dst = lax.broadcasted_iota(jnp.int32, (LANES, 2 * LANES), 1)
    want = GATE_F + 2 * MLSTM_HEADS * (dst // LANES) + (dst % LANES) // HEAD_DIM
    return jnp.where(src == want, 1.0, 0.0).astype(BF16)


def _gate_broadcast(x, sel, gate_mask):
    hi, lo = _split(jnp.where(gate_mask, x, 0.0))
    return _dot(hi, sel) + _dot(lo, sel)


def _mlstm_kernel(q_ref, k_ref, v_ref, og_ref, small_ref, wq_ref, wk_ref, bq_ref, bk_ref,
                  gbias_ref, gain_ref, o_ref, q_s, k_s, b_s, r_s, cm_s, fl1_s, bl_s, rl_s, mf_s, mb_s,
                  wp_s, qi_s, kw_s, rt_s, st_s, of_s, ob_s):
    T = q_ref.shape[1]
    nc = T // CHUNK
    R = min(PREP_ROWS, T)
    hp = pl.program_id(1)
    G = MLSTM_CHUNKS_PER_STEP
    cpt = R // CHUNK

    row = lax.broadcasted_iota(jnp.int32, (T, LANES), 0)
    for src, w_ref, b_ref, dst, scale in ((q_ref, wq_ref, bq_ref, q_s, 1.0),
                                          (k_ref, wk_ref, bk_ref, k_s, HEAD_DIM ** -0.5)):
        xc = src[0]
        prev = jnp.where(row >= 1, pltpu.roll(xc, 1, 0), 0.0)
        nxt = jnp.where(row < T - 1, pltpu.roll(xc, T - 1, 0), 0.0)
        y = prev * w_ref[0:1, :] + xc * w_ref[1:2, :] + nxt * w_ref[2:3, :] + b_ref[...]
        dst[...] = (y * _sigmoid(y) * scale).astype(BF16)

    pos = _chunk_pos(R)
    lane = lax.broadcasted_iota(jnp.int32, (R, LANES), 1)
    is_bwd = lane >= GATE_I + 2 * MLSTM_HEADS
    is_bwd8 = lax.broadcasted_iota(jnp.int32, (cpt * SUBLANES, LANES), 1) >= GATE_I + 2 * MLSTM_HEADS
    heads_per_pair = LANES // HEAD_DIM
    shift = (LANES - heads_per_pair * hp) % LANES
    gate_lanes = [_gate_lane(d, hh) for d in range(2) for hh in range(2)]
    gate_mask = functools.reduce(jnp.logical_or, [lane == l for l in gate_lanes])
    lane8 = lax.broadcasted_iota(jnp.int32, (cpt * SUBLANES, LANES), 1)
    gate_mask8 = functools.reduce(jnp.logical_or, [lane8 == l for l in gate_lanes])

    def prep(t, _):
        r0 = pl.multiple_of(t * R, R)
        gc = pltpu.roll(small_ref[0, pl.ds(r0, R), :] + gbias_ref[...], shift, 1)
        logf = _log_sigmoid(gc)
        b = jnp.where(is_bwd, _chunk_scan(logf, pos, jnp.add, 0.0, True),
                      _chunk_scan(logf, pos, jnp.add, 0.0, False))
        r = pltpu.roll(gc, MLSTM_HEADS, 1) - b
        cm = jnp.where(is_bwd, _chunk_scan(r, pos, jnp.maximum, NEG, True),
                       _chunk_scan(r, pos, jnp.maximum, NEG, False))
        b_s[pl.ds(r0, R), :] = b
        r_s[pl.ds(r0, R), :] = r
        cm_s[pl.ds(r0, R), :] = cm
        rt_s[:, pl.ds(r0, R)] = r.T
        c8 = pl.ds(pl.multiple_of(t * cpt * SUBLANES, SUBLANES), cpt * SUBLANES)
        bl_s[c8, :] = jnp.where(is_bwd8, _chunk_rows8(b, 0), _chunk_rows8(b, CHUNK - 1))
        rl_s[c8, :] = jnp.where(is_bwd8, _chunk_rows8(cm, 0), _chunk_rows8(cm, CHUNK - 1))
        return 0

    lax.fori_loop(0, T // R, prep, 0)

    def m_chain(n, carry):
        mf, mb = carry
        rf = pl.ds(pl.multiple_of(n * SUBLANES, SUBLANES), SUBLANES)
        rb = pl.ds(pl.multiple_of((nc - 1 - n) * SUBLANES, SUBLANES), SUBLANES)
        mf_s[rf, :] = mf
        mb_s[rb, :] = mb
        return (bl_s[rf, :] + jnp.maximum(mf, rl_s[rf, :]), bl_s[rb, :] + jnp.maximum(mb, rl_s[rb, :]))

    m0 = jnp.full((SUBLANES, LANES), NEG, F32)
    lax.fori_loop(0, nc, m_chain, (m0, m0))

    expo_s, floor_s = (b_s, cm_s), (r_s, fl1_s)
    sel = _gate_select()

    def weights(t, _):
        r0 = pl.multiple_of(t * R, R)
        c8 = pl.ds(pl.multiple_of(t * cpt * SUBLANES, SUBLANES), cpt * SUBLANES)
        rows = pl.ds(r0, R)
        m_in8 = jnp.where(is_bwd8, mb_s[c8, :], mf_s[c8, :])
        bl8 = bl_s[c8, :]
        m_out8 = bl8 + jnp.maximum(m_in8, rl_s[c8, :])
        mx = jnp.maximum(_expand_rows8(m_in8), cm_s[rows, :])
        m_in_b = _gate_broadcast(m_in8, sel, gate_mask8)
        wk0_b = _gate_broadcast(bl8 - m_out8, sel, gate_mask8)
        wp_b = jnp.exp(_gate_broadcast(bl8 + m_in8 - m_out8, sel, gate_mask8))
        mx_b = _gate_broadcast(mx, sel, gate_mask)
        b_b = _gate_broadcast(b_s[rows, :], sel, gate_mask)
        r_b = _gate_broadcast(r_s[rows, :], sel, gate_mask)
        w_inter = jnp.exp(_expand_rows8(m_in_b) - mx_b)
        wk = jnp.exp(_expand_rows8(wk0_b) + r_b)
        floor = jnp.exp(-(b_b + mx_b))
        q = q_s[rows, :].astype(F32)
        k = k_s[rows, :].astype(F32)
        for d in range(2):
            half = slice(d * LANES, (d + 1) * LANES)
            qi_s[d, rows, :] = (q * w_inter[:, half]).astype(BF16)
            kw_s[d, rows, :] = (k * wk[:, half]).astype(BF16)
            wp_s[d, c8, :] = wp_b[:, half]
            expo_s[d][rows, :] = -mx_b[:, half]
            floor_s[d][rows, :] = floor[:, half]
        return 0

    lax.fori_loop(0, T // R, weights, 0)

    outs = (of_s, ob_s)
    ci = lax.broadcasted_iota(jnp.int32, (CHUNK, LANES), 0)
    si = lax.broadcasted_iota(jnp.int32, (CHUNK, LANES), 1) % CHUNK
    masks = (ci >= si, ci <= si)
    br = lax.broadcasted_iota(jnp.int32, (LANES, 2 * LANES), 0) // HEAD_DIM
    bc = (lax.broadcasted_iota(jnp.int32, (LANES, 2 * LANES), 1) % LANES) // HEAD_DIM
    blockdiag = br == bc
    ones = jnp.ones((CHUNK, LANES), BF16)
    st_s[...] = jnp.zeros(st_s.shape, F32)
    gate_rows = slice(GATE_I, GATE_I + 4 * MLSTM_HEADS)

    def step(it, _):
        units = [(d, g) for d in range(2) for g in range(G)]
        ops = {}
        for d in range(2):
            first = (nc - (it + 1) * G) if d else it * G
            rt = rt_s[gate_rows, pl.ds(pl.multiple_of(first * CHUNK, G * CHUNK), G * CHUNK)]
            for g in range(G):
                local = (G - 1 - g) if d else g
                c = first + local
                rows = pl.ds(pl.multiple_of(c * CHUNK, CHUNK), CHUNK)
                c8 = pl.ds(pl.multiple_of(c * SUBLANES, SUBLANES), SUBLANES)
                r_row = jnp.concatenate(
                    [rt[_gate_lane(d, hh) - GATE_I:_gate_lane(d, hh) - GATE_I + 1,
                        local * CHUNK:(local + 1) * CHUNK] for hh in range(2)], axis=1)
                wp = wp_s[d, c8, :][0:1, :]
                ops[d, g] = dict(
                    rows=rows, q=q_s[rows, :], k=k_s[rows, :], qi=qi_s[d, rows, :], kw=kw_s[d, rows, :],
                    va=jnp.concatenate([v_ref[0, rows, :].astype(BF16), ones], axis=1),
                    expo=expo_s[d][rows, :] + r_row, floor=floor_s[d][rows, :],
                    wp=jnp.concatenate([wp, wp], axis=1))
        qk, kv = {}, {}
        for u in units:
            o = ops[u]
            qk[u] = _dot_nt(o["q"], _stack_heads(o["k"]))
            kv[u] = _dot_tn(o["kw"], o["va"])
        st_in = {}
        for d in range(2):
            st = st_s[d]
            for g in range(G):
                st_in[d, g] = st.astype(BF16)
                st = st * ops[d, g]["wp"] + jnp.where(blockdiag, kv[d, g], 0.0)
            st_s[d] = st
        for u in units:
            d = u[0]
            o = ops[u]
            smat = (qk[u] * jnp.where(masks[d], jnp.exp(o["expo"]), 0.0)).astype(BF16)
            va_bd = jnp.where(blockdiag, jnp.concatenate([o["va"], o["va"]], axis=0), jnp.zeros((), BF16))
            num = _dot(jnp.concatenate([o["qi"], smat], axis=1), jnp.concatenate([st_in[u], va_bd], axis=0))
            den = jnp.maximum(jnp.abs(num[:, LANES:]), o["floor"])
            outs[d][o["rows"], :] = num[:, :LANES] / den
        return 0

    lax.fori_loop(0, nc // G, step, 0)

    def post(t, _):
        r0 = pl.multiple_of(t * R, R)
        h = _head_rms(of_s[pl.ds(r0, R), :] + ob_s[pl.ds(r0, R), :], gain_ref[...])
        o_ref[0, pl.ds(r0, R), :] = (h * _sigmoid(og_ref[0, pl.ds(r0, R), :])).astype(BF16)
        return 0

    lax.fori_loop(0, T // R, post, 0)


def _mlstm(ml, small, lp):
    B, T, _ = ml.shape
    pairs = MLSTM_WIDTH // LANES
    col = lambda off: (lambda b, hp: (b, 0, off * pairs + hp))
    const = lambda b, hp: (0, 0)
    return pl.pallas_call(
        _mlstm_kernel,
        grid=(B, pairs),
        in_specs=[
            pl.BlockSpec((1, T, LANES), col(0)),
            pl.BlockSpec((1, T, LANES), col(1)),
            pl.BlockSpec((1, T, LANES), col(2)),
            pl.BlockSpec((1, T, LANES), col(3)),
            pl.BlockSpec((1, T, LANES), lambda b, hp: (b, 0, 0)),
            pl.BlockSpec((3, LANES), lambda b, hp: (0, hp)),
            pl.BlockSpec((3, LANES), lambda b, hp: (0, pairs + hp)),
            pl.BlockSpec((1, LANES), lambda b, hp: (0, hp)),
            pl.BlockSpec((1, LANES), lambda b, hp: (0, pairs + hp)),
            pl.BlockSpec((1, LANES), const),
            pl.BlockSpec((1, LANES), const),
        ],
        out_specs=pl.BlockSpec((1, T, LANES), lambda b, hp: (b, 0, hp)),
        out_shape=jax.ShapeDtypeStruct((B, T, MLSTM_WIDTH), BF16),
        scratch_shapes=(
            [pltpu.VMEM((T, LANES), BF16)] * 2
            + [pltpu.VMEM((T, LANES), F32)] * 4
            + [pltpu.VMEM(((T // CHUNK) * SUBLANES, LANES), F32)] * 4
            + [pltpu.VMEM((2, (T // CHUNK) * SUBLANES, LANES), F32)]
            + [pltpu.VMEM((2, T, LANES), BF16)] * 2
            + [pltpu.VMEM((LANES, T), F32),
               pltpu.VMEM((2, LANES, 2 * LANES), F32),
               pltpu.VMEM((T, LANES), F32), pltpu.VMEM((T, LANES), F32)]),
        compiler_params=_params("arbitrary", "arbitrary"),
        name="mlstm",
    )(ml, ml, ml, ml, small, lp["conv_w"], lp["conv_w"], lp["conv_b"], lp["conv_b"],
      lp["gate_bias"], lp["ml_gain"])


def _first_argmax(vals, row):
    mx = jnp.max(vals, axis=0, keepdims=True)
    idx = jnp.min(jnp.where(vals == mx, row, vals.shape[0]), axis=0, keepdims=True)
    return mx, idx


def _out_route_kernel(x_ref, a_ref, gl_ref, ml_ref, w_ref, g_ref, wr_ref, br_ref,
                      x1_ref, t_ref, route_ref, routet_ref):
    tm = x_ref.shape[0]
    sub = min(tm, ROUTE_SUBTILE)
    spans = [slice(r, r + sub) for r in range(0, tm, sub)]
    x1s, logits = [], []
    for rs in spans:
        x1 = (x_ref[rs, :]
              + _dot(a_ref[rs, :], w_ref[0:ATTN_WIDTH, :])
              + _dot(gl_ref[rs, :], w_ref[ATTN_WIDTH:ATTN_WIDTH + GLA_WIDTH, :])
              + _dot(ml_ref[rs, :], w_ref[ATTN_WIDTH + GLA_WIDTH:, :]))
        x1_ref[rs, :] = x1
        x1s.append(x1)
    for rs, x1 in zip(spans, x1s):
        t_hi, t_lo = _split(_rms(x1, g_ref[...]))
        t_ref[rs, :] = t_hi
        both = _dot(t_hi, wr_ref[...])
        logits.append((both[:, :LANES] + both[:, LANES:] + _dot(t_lo, wr_ref[:, :LANES])) + br_ref[...])
    grow = lax.broadcasted_iota(jnp.int32, (SUBLANES, sub), 0)
    erow = lax.broadcasted_iota(jnp.int32, (N_EXPERTS, sub), 0)
    for rs, lg in zip(spans, logits):
        lt = lg.T
        gl = jnp.where(grow < N_GROUPS, lt[0:SUBLANES, :], -jnp.inf)
        gmax, gi = _first_argmax(gl, grow)
        g_prob = 1.0 / jnp.sum(jnp.exp(gl - gmax), axis=0, keepdims=True)
        el = jnp.where(erow // EXPERTS_PER_GROUP == gi, lt[ROUTE_W_LANE:ROUTE_W_LANE + N_EXPERTS, :], -jnp.inf)
        v1, i1 = _first_argmax(el, erow)
        v2, i2 = _first_argmax(jnp.where(erow == i1, -jnp.inf, el), erow)
        e2 = jnp.exp(v2 - v1)
        w1 = g_prob / (1.0 + e2)
        w2 = g_prob * e2 / (1.0 + e2)
        comb = jnp.where(erow == i1, w1, jnp.where(erow == i2, w2, 0.0))
        head = jnp.where(grow == 0, gi.astype(F32), 0.0)
        route_t = jnp.concatenate(
            [head, comb, jnp.zeros((LANES - SUBLANES - N_EXPERTS, sub), F32)], axis=0)
        route_ref[rs, :] = route_t.T
        routet_ref[:, rs] = head


def _out_route(x2, attn, gla_o, ml_o, lp, tm=512):
    N, D = x2.shape
    const = lambda i: (0, 0)
    tok = lambda i: (i, 0)
    return pl.pallas_call(
        _out_route_kernel,
        grid=(N // tm,),
        in_specs=[
            pl.BlockSpec((tm, D), tok),
            pl.BlockSpec((tm, ATTN_WIDTH), tok),
            pl.BlockSpec((tm, GLA_WIDTH), tok),
            pl.BlockSpec((tm, MLSTM_WIDTH), tok),
            pl.BlockSpec((ATTN_WIDTH + GLA_WIDTH + MLSTM_WIDTH, D), const),
            pl.BlockSpec((1, D), const),
            pl.BlockSpec((D, 2 * LANES), const),
            pl.BlockSpec((1, LANES), const),
        ],
        out_specs=[
            pl.BlockSpec((tm, D), tok),
            pl.BlockSpec((tm, D), tok),
            pl.BlockSpec((tm, LANES), tok),
            pl.BlockSpec((SUBLANES, tm), lambda i: (0, i)),
        ],
        out_shape=[
            jax.ShapeDtypeStruct((N, D), F32),
            jax.ShapeDtypeStruct((N, D), BF16),
            jax.ShapeDtypeStruct((N, LANES), F32),
            jax.ShapeDtypeStruct((SUBLANES, N), F32),
        ],
        compiler_params=_params("arbitrary"),
        name="out_route",
    )(x2, attn, gla_o, ml_o, lp["w_out"], lp["g_ffn"], lp["w_route"], lp["b_route"])


def _dispatch_kernel(t_ref, route_ref, routet_ref, xs_ref, cws_ref, pos_ref, cnt_ref):
    nb = ROUTE_BLOCK
    gi_row = routet_ref[0:1, :]
    sub = lax.broadcasted_iota(jnp.int32, (SUBLANES, nb), 0).astype(F32)
    onehot = (sub == gi_row)
    ri = lax.broadcasted_iota(jnp.int32, (nb, nb), 0)
    cj = lax.broadcasted_iota(jnp.int32, (nb, nb), 1)
    before = (ri < cj).astype(BF16)
    rank = _dot(onehot.astype(BF16), before)
    counts = jnp.broadcast_to(jnp.sum(onehot.astype(F32), axis=-1, keepdims=True), (SUBLANES, LANES))
    padded = jnp.ceil(counts * (1.0 / ROUTE_TILE)) * ROUTE_TILE
    srow = lax.broadcasted_iota(jnp.int32, (SUBLANES, LANES), 0)
    incl = padded
    s = 1
    while s < SUBLANES:
        incl = incl + jnp.where(srow >= s, pltpu.roll(incl, s, 0), 0.0)
        s *= 2
    start = incl - padded
    pos_row = jnp.sum(jnp.where(onehot, start[:, 0:1] + rank, 0.0), axis=0, keepdims=True)
    cnt_ref[0] = counts.astype(jnp.int32)

    route = route_ref[...]
    lane = lax.broadcasted_iota(jnp.int32, (nb, LANES), 1)
    onehot_c = lane.astype(F32) == route[:, 0:1]
    after = (cj < ri).astype(BF16)
    rank_c = _dot(after, onehot_c.astype(BF16))
    slane = lax.broadcasted_iota(jnp.int32, (SUBLANES, LANES), 1)
    start_c = jnp.sum(jnp.where(srow == slane, start, 0.0), axis=0, keepdims=True)
    pos_col = jnp.sum(jnp.where(onehot_c, start_c + rank_c, 0.0), axis=-1, keepdims=True)
    pos_ref[...] = jnp.broadcast_to(pos_col, (nb, LANES))

    comb = jnp.where(lane >= ROUTE_W_LANE, route, 0.0)
    c_hi, c_lo = _split(comb)
    c_lo2 = (comb - c_hi.astype(F32) - c_lo.astype(F32)).astype(BF16)
    cw = (c_hi.astype(F32) + pltpu.roll(c_lo.astype(F32), ROUTE_LO_SHIFT, 1)
          + pltpu.roll(c_lo2.astype(F32), 2 * ROUTE_LO_SHIFT, 1)).astype(BF16)
    tb = t_ref[...]
    for r in range(TILES_PER_BLOCK):
        rows = (lax.broadcasted_iota(jnp.int32, (ROUTE_TILE, nb), 0) + r * ROUTE_TILE).astype(F32)
        perm = (rows == pos_row).astype(BF16)
        xs_ref[r * ROUTE_TILE:(r + 1) * ROUTE_TILE, :] = _dot(perm, tb).astype(BF16)
        cws_ref[r * ROUTE_TILE:(r + 1) * ROUTE_TILE, :] = _dot(perm, cw)


def _dispatch(t, route, routet):
    N, D = t.shape
    nblk = N // ROUTE_BLOCK
    return pl.pallas_call(
        _dispatch_kernel,
        grid=(nblk,),
        in_specs=[
            pl.BlockSpec((ROUTE_BLOCK, D), lambda i: (i, 0)),
            pl.BlockSpec((ROUTE_BLOCK, LANES), lambda i: (i, 0)),
            pl.BlockSpec((SUBLANES, ROUTE_BLOCK), lambda i: (0, i)),
        ],
        out_specs=[
            pl.BlockSpec((ROUTE_ROWS, D), lambda i: (i, 0)),
            pl.BlockSpec((ROUTE_ROWS, LANES), lambda i: (i, 0)),
            pl.BlockSpec((ROUTE_BLOCK, LANES), lambda i: (i, 0)),
            pl.BlockSpec((1, SUBLANES, LANES), lambda i: (i, 0, 0)),
        ],
        out_shape=[
            jax.ShapeDtypeStruct((nblk * ROUTE_ROWS, D), BF16),
            jax.ShapeDtypeStruct((nblk * ROUTE_ROWS, LANES), F32),
            jax.ShapeDtypeStruct((N, LANES), F32),
            jax.ShapeDtypeStruct((nblk, SUBLANES, LANES), jnp.int32),
        ],
        compiler_params=_params("arbitrary"),
        name="dispatch",
    )(t, route, routet)


def _tile_schedule(cnt):
    nblk = cnt.shape[0]
    ntile = (cnt + ROUTE_TILE - 1) // ROUTE_TILE
    end = jnp.cumsum(ntile, axis=1)
    r = jnp.arange(TILES_PER_BLOCK, dtype=jnp.int32)
    grp = jnp.sum(r[None, :, None] >= end[:, None, :], axis=-1)
    grp = grp.reshape(-1).astype(jnp.int32)
    tile = jnp.arange(nblk * TILES_PER_BLOCK, dtype=jnp.int32)
    order = jnp.argsort(grp * (nblk * TILES_PER_BLOCK) + tile).astype(jnp.int32)
    n_active = jnp.sum(grp < N_GROUPS).astype(jnp.int32)
    g_sorted = grp[order]
    last_group = g_sorted[jnp.maximum(n_active - 1, 0)]
    g_sorted = jnp.where(g_sorted < N_GROUPS, g_sorted, last_group)
    return order, g_sorted, n_active[None]


def _expert_kernel(trow_ref, tgrp_ref, nact_ref, xs_ref, cws_ref, wg_ref, wu_ref, wd_ref, ys_ref):
    i = pl.program_id(0)

    @pl.when(i < nact_ref[0])
    def _():
        x = xs_ref[...]
        cws = cws_ref[...]
        lane = lax.broadcasted_iota(jnp.int32, cws.shape, 1)
        first = ROUTE_W_LANE + tgrp_ref[i] * EXPERTS_PER_GROUP
        y = jnp.zeros(ys_ref.shape, F32)

        def hidden(j):
            return _dot(x, wg_ref[j]), _dot(x, wu_ref[j])

        h = hidden(0)
        for j in range(EXPERTS_PER_GROUP):
            h_next = hidden(j + 1) if j + 1 < EXPERTS_PER_GROUP else None
            off = lane - (first + j)
            sel = (off == 0) | (off == ROUTE_LO_SHIFT) | (off == 2 * ROUTE_LO_SHIFT)
            wj = jnp.sum(jnp.where(sel, cws, 0.0), axis=-1, keepdims=True)
            a = (h[0] * _sigmoid(h[0]) * h[1]).astype(BF16)
            y = y + wj * _dot(a, wd_ref[j])
            h = h_next
        ys_ref[...] = y.astype(BF16)

    @pl.when(i >= nact_ref[0])
    def _():
        ys_ref[...] = jnp.zeros(ys_ref.shape, BF16)


def _experts(xs, cws, order, grp, n_active, lp):
    rows, D = xs.shape
    n_tiles = rows // ROUTE_TILE
    tile = lambda i, trow, tgrp, nact: (trow[i], 0)
    layer_groups = lp["layer"] * N_GROUPS
    wsel = lambda i, trow, tgrp, nact: (layer_groups + tgrp[i], 0, 0)
    return pl.pallas_call(
        _expert_kernel,
        grid_spec=pltpu.PrefetchScalarGridSpec(
            num_scalar_prefetch=3,
            grid=(n_tiles,),
            in_specs=[
                pl.BlockSpec((ROUTE_TILE, D), tile),
                pl.BlockSpec((ROUTE_TILE, LANES), tile),
                pl.BlockSpec((EXPERTS_PER_GROUP, D, D_EXPERT), wsel),
                pl.BlockSpec((EXPERTS_PER_GROUP, D, D_EXPERT), wsel),
                pl.BlockSpec((EXPERTS_PER_GROUP, D_EXPERT, D), wsel),
            ],
            out_specs=pl.BlockSpec((ROUTE_TILE, D), tile),
        ),
        out_shape=jax.ShapeDtypeStruct((rows, D), BF16),
        compiler_params=_params("arbitrary"),
        name="experts",
    )(order, grp, n_active, xs, cws, lp["w_gate"], lp["w_up"], lp["w_down"])


def _combine_kernel(ys_ref, pos_ref, x1_ref, p_ref, g_ref, wpg_ref, wpp_ref, gfin_ref, o_ref,
                    *, embed, final):
    tm = x1_ref.shape[0]
    pos = pos_ref[...]
    lane = lax.broadcasted_iota(jnp.int32, (tm, LANES), 1).astype(F32)
    perm_t = jnp.concatenate(
        [(lane + r * LANES == pos).astype(BF16) for r in range(ROUTE_ROWS // LANES)], axis=1)
    x = x1_ref[...] + _dot(perm_t, ys_ref[...])
    if embed:
        gate = _sigmoid(_dot(_rms(x, g_ref[...]).astype(BF16), wpg_ref[...]))
        x = x + gate * _dot(p_ref[...].astype(BF16), wpp_ref[...])
    if final:
        x = _rms(x, gfin_ref[...])
    o_ref[...] = x


def _combine(ys, pos, x1, p2, lp, g_final, embed=True, final=False, tm=512):
    N, D = x1.shape
    inner = ROUTE_BLOCK // tm
    tok = lambda b, i: (b * inner + i, 0)
    const = lambda b, i: (0, 0)
    return pl.pallas_call(
        functools.partial(_combine_kernel, embed=embed, final=final),
        grid=(N // ROUTE_BLOCK, inner),
        in_specs=[
            pl.BlockSpec((ROUTE_ROWS, D), lambda b, i: (b, 0)),
            pl.BlockSpec((tm, LANES), tok),
            pl.BlockSpec((tm, D), tok),
            pl.BlockSpec((tm, p2.shape[1]), tok),
            pl.BlockSpec((1, D), const),
            pl.BlockSpec((D, D), const),
            pl.BlockSpec((p2.shape[1], D), const),
            pl.BlockSpec((1, D), const),
        ],
        out_specs=pl.BlockSpec((tm, D), tok),
        out_shape=jax.ShapeDtypeStruct((N, D), F32),
        compiler_params=_params("arbitrary", "arbitrary"),
        name="combine",
    )(ys, pos, x1, p2, lp["g_ple"], lp["w_pg"], lp["w_pp"], g_final)


def _moe(t, route, routet, lp):
    xs, cws, pos, cnt = _dispatch(t, route, routet)
    order, grp, n_active = _tile_schedule(cnt[:, :N_GROUPS, 0])
    return _experts(xs, cws, order, grp, n_active, lp), pos


def _moe_debug(x2, lp):
    N, D = x2.shape
    zeros = lambda w: jnp.zeros((N, w), BF16)
    x1, t, route, routet = _out_route(x2, zeros(ATTN_WIDTH), zeros(GLA_WIDTH), zeros(MLSTM_WIDTH), lp)
    ys, pos = _moe(t, route, routet, lp)
    p2 = jnp.zeros((N, lp["w_pp"].shape[0]), F32)
    return _combine(ys, pos, x1, p2, lp, lp["g_ple"], embed=False) - x1


def _permute_in_cols(w):
    glr0 = ATTN_WIDTH + 2 * KV_WIDTH + 4 * GLA_WIDTH
    ml0 = glr0 + 2 * GLA_RANK
    mg0 = ml0 + 4 * MLSTM_WIDTH
    end = mg0 + 4 * MLSTM_HEADS
    assert end == w.shape[1]
    pad = jnp.zeros((w.shape[0], IN_PERM_WIDTH - end), w.dtype)
    return jnp.concatenate([w[:, :glr0], w[:, ml0:mg0], w[:, glr0:ml0], w[:, mg0:end], pad], axis=1)


def _rope_tables(T):
    t = jnp.arange(T, dtype=F32)
    row = jnp.floor(t / GRID_W)
    col = t - row * GRID_W
    inv = ROPE_THETA ** (-jnp.arange(0, ROPE_AXIS_DIM, 2, dtype=F32) / ROPE_AXIS_DIM)
    ang_r = row[:, None] * inv[None, :]
    ang_c = col[:, None] * inv[None, :]
    cos_h = jnp.concatenate([jnp.cos(ang_r), jnp.cos(ang_r), jnp.cos(ang_c), jnp.cos(ang_c)], axis=1)
    sin_h = jnp.concatenate([-jnp.sin(ang_r), jnp.sin(ang_r), -jnp.sin(ang_c), jnp.sin(ang_c)], axis=1)
    reps = LANES // HEAD_DIM
    cos, sin = jnp.tile(cos_h, (1, reps)), jnp.tile(sin_h, (1, reps))
    eh = np.zeros((LANES, LANES), np.float32)
    eh[np.arange(LANES), np.arange(LANES) // HEAD_DIM] = 1.0
    return dict(cos=cos, sin=sin, cos_t=cos.T, sin_t=sin.T,
                eh=jnp.asarray(eh, BF16), eht=jnp.asarray(eh.T, BF16))


def kernel(x, p, norm_mix_g, w_in, attn_q_norm_g, attn_k_norm_g, gla_w_decay, gla_b_decay,
           gla_out_norm_g, mlstm_conv_w, mlstm_conv_b, mlstm_b_input, mlstm_b_forget,
           mlstm_out_norm_g, w_out, norm_ffn_g, w_group, b_group, w_router, b_router,
           w_expert_gate, w_expert_up, w_expert_down, norm_ple_g, w_ple_gate, w_ple_proj,
           final_norm_g):
    params = dict(
        norm_mix_g=norm_mix_g, w_in=w_in, attn_q_norm_g=attn_q_norm_g, attn_k_norm_g=attn_k_norm_g,
        gla_w_decay=gla_w_decay, gla_b_decay=gla_b_decay, gla_out_norm_g=gla_out_norm_g,
        mlstm_conv_w=mlstm_conv_w, mlstm_conv_b=mlstm_conv_b, mlstm_b_input=mlstm_b_input,
        mlstm_b_forget=mlstm_b_forget, mlstm_out_norm_g=mlstm_out_norm_g, w_out=w_out,
        norm_ffn_g=norm_ffn_g, w_group=w_group, b_group=b_group, w_router=w_router,
        b_router=b_router, w_expert_gate=w_expert_gate, w_expert_up=w_expert_up,
        w_expert_down=w_expert_down, norm_ple_g=norm_ple_g, w_ple_gate=w_ple_gate,
        w_ple_proj=w_ple_proj)
    B, T, D = x.shape
    rope = _rope_tables(T)
    N = B * T
    depth = w_in.shape[0]
    x2 = x.reshape(N, D)
    g_final = final_norm_g[None, :]
    for i in range(depth):
        lp = _layer_params(params, i)
        qt, k, vt, gla, ml, small = _in_proj(x2, lp, rope, B, T)
        attn = _attention(lp["attn_safe"], qt, k, vt).reshape(N, ATTN_WIDTH)
        gla_o = _gla(gla, small, lp).reshape(N, GLA_WIDTH)
        ml_o = _mlstm(ml, small, lp).reshape(N, MLSTM_WIDTH)
        x1, t, route, routet = _out_route(x2, attn, gla_o, ml_o, lp)
        ys, pos = _moe(t, route, routet, lp)
        x2 = _combine(ys, pos, x1, p[i].reshape(N, -1), lp, g_final, final=(i == depth - 1))
    return x2.reshape(B, T, D)


def _split_w(w):
    hi = w.astype(BF16)
    return hi, (w - hi.astype(F32)).astype(BF16)


def _stacked_experts(w):
    return w.astype(BF16).reshape((w.shape[0] * w.shape[1],) + w.shape[2:])


def _layer_params(p, i):
    D = p["w_in"].shape[1]
    w_in = _permute_in_cols(p["w_in"][i])
    gq, gk = p["attn_q_norm_g"][i], p["attn_k_norm_g"][i]
    q_gain = jnp.tile(gq, LANES // HEAD_DIM) * (HEAD_DIM ** -0.5 * LOG2E)
    logit_bound = HEAD_DIM ** 0.5 * jnp.max(jnp.abs(gq)) * jnp.max(jnp.abs(gk))
    attn_safe = (logit_bound <= ATTN_SAFE_LOGIT).astype(jnp.int32)[None]
    wd = jnp.zeros((2, LANES, GLA_WIDTH), F32)
    wd = wd.at[0, :GLA_RANK].set(p["gla_w_decay"][i, 0]).at[1, GLA_RANK:2 * GLA_RANK].set(p["gla_w_decay"][i, 1])
    wd_hi, wd_lo = _split_w(wd)
    gate_bias = jnp.zeros((LANES,), F32).at[SMALL_GATE_LANE:SMALL_GATE_LANE + 4 * MLSTM_HEADS].set(
        jnp.concatenate([p["mlstm_b_input"][i, 0], p["mlstm_b_forget"][i, 0],
                         p["mlstm_b_input"][i, 1], p["mlstm_b_forget"][i, 1]]))
    w_route = jnp.zeros((D, LANES), F32)
    w_route = w_route.at[:, :N_GROUPS].set(p["w_group"][i])
    w_route = w_route.at[:, ROUTE_W_LANE:ROUTE_W_LANE + N_EXPERTS].set(p["w_router"][i])
    wr_hi, wr_lo = _split_w(w_route)
    b_route = jnp.zeros((LANES,), F32).at[:N_GROUPS].set(p["b_group"][i])
    b_route = b_route.at[ROUTE_W_LANE:ROUTE_W_LANE + N_EXPERTS].set(p["b_router"][i])
    return dict(
        g_mix=p["norm_mix_g"][i][None, :],
        w_in=w_in.astype(BF16),
        q_gain_t=jnp.broadcast_to(q_gain[:, None], (LANES, LANES)),
        k_gain=jnp.tile(gk, LANES // HEAD_DIM)[None, :],
        attn_safe=attn_safe,
        wd_hi=wd_hi, wd_lo=wd_lo,
        bd=p["gla_b_decay"][i][:, None, :],
        gla_gain=jnp.tile(p["gla_out_norm_g"][i], LANES // HEAD_DIM)[None, :],
        conv_w=p["mlstm_conv_w"][i],
        conv_b=p["mlstm_conv_b"][i][None, :],
        gate_bias=gate_bias[None, :],
        ml_gain=jnp.tile(p["mlstm_out_norm_g"][i], LANES // HEAD_DIM)[None, :],
        w_out=p["w_out"][i].astype(BF16),
        g_ffn=p["norm_ffn_g"][i][None, :],
        w_route=jnp.concatenate([wr_hi, wr_lo], axis=1), b_route=b_route[None, :],
        layer=i,
        w_gate=_stacked_experts(p["w_expert_gate"]),
        w_up=_stacked_experts(p["w_expert_up"]),
        w_down=_stacked_experts(p["w_expert_down"]),
        g_ple=p["norm_ple_g"][i][None, :],
        w_pg=p["w_ple_gate"][i].astype(BF16),
        w_pp=p["w_ple_proj"][i].astype(BF16),
    )
```

```python
import functools

import jax
import jax.numpy as jnp
import numpy as np
from jax import lax
from jax.experimental import pallas as pl
from jax.experimental.pallas import tpu as pltpu

F32 = jnp.float32
BF16 = jnp.bfloat16

GRID_W = 64
HEAD_DIM = 64
ATTN_HEADS = 8
ATTN_KV_HEADS = 2
GLA_HEADS = 4
MLSTM_HEADS = 4
ATTN_WIDTH = ATTN_HEADS * HEAD_DIM
KV_WIDTH = ATTN_KV_HEADS * HEAD_DIM
GLA_WIDTH = GLA_HEADS * HEAD_DIM
MLSTM_WIDTH = MLSTM_HEADS * HEAD_DIM
GLA_RANK = 16
GLA_TAU = 16.0
CHUNK = 64
ROPE_THETA = 10000.0
ROPE_AXIS_DIM = HEAD_DIM // 2
N_GROUPS = 4
EXPERTS_PER_GROUP = 4
N_EXPERTS = N_GROUPS * EXPERTS_PER_GROUP
D_EXPERT = 512
EPS = 1e-6
NEG = -1e30

LANES = 128
SUBLANES = 8
VMEM_LIMIT_BYTES = 56 * 1024 * 1024

QK_WIDTH = ATTN_WIDTH + KV_WIDTH
OFF_V = QK_WIDTH
OFF_GLA = OFF_V + KV_WIDTH
OFF_ML = OFF_GLA + 4 * GLA_WIDTH
OFF_SMALL = OFF_ML + 4 * MLSTM_WIDTH
IN_PERM_WIDTH = OFF_SMALL + LANES
SMALL_GATE_LANE = 2 * GLA_RANK

ROUTE_BLOCK = 1024
ROUTE_TILE = 128
ROUTE_ROWS = ROUTE_BLOCK + N_GROUPS * ROUTE_TILE
TILES_PER_BLOCK = ROUTE_ROWS // ROUTE_TILE
ROUTE_W_LANE = 8
ROUTE_LO_SHIFT = 32
ROUTE_SUBTILE = 256


def _dot(a, b):
    return jnp.dot(a, b, preferred_element_type=F32)


def _dot_nt(a, b):
    return lax.dot_general(a, b, (((1,), (1,)), ((), ())), preferred_element_type=F32)


def _dot_tn(a, b):
    return lax.dot_general(a, b, (((0,), (0,)), ((), ())), preferred_element_type=F32)


def _split(a):
    hi = a.astype(BF16)
    lo = (a - hi.astype(F32)).astype(BF16)
    return hi, lo


def _dot3(a, w_hi, w_lo):
    a_hi, a_lo = _split(a)
    return _dot(a_hi, w_hi) + _dot(a_lo, w_hi) + _dot(a_hi, w_lo)


def _log_sigmoid(x):
    return jnp.minimum(x, 0.0) - jnp.log(1.0 + jnp.exp(-jnp.abs(x)))


def _sigmoid(x):
    return 1.0 / (1.0 + jnp.exp(-x))


def _rms(x, g):
    return x * lax.rsqrt(jnp.mean(x * x, axis=-1, keepdims=True) + EPS) * g


def _params(*semantics):
    return pltpu.CompilerParams(dimension_semantics=semantics, vmem_limit_bytes=VMEM_LIMIT_BYTES)


def _in_proj_kernel(x_ref, g_ref, w_ref, cos_ref, sin_ref, cost_ref, sint_ref, gq_ref, gk_ref,
                    eh_ref, eht_ref, qt_ref, k_ref, vt_ref, gla_ref, ml_ref, small_ref):
    tm = x_ref.shape[0]
    h = _rms(x_ref[...], g_ref[...])
    z = _dot(h.astype(BF16), w_ref[...])

    heads = LANES // HEAD_DIM
    half = ROPE_AXIS_DIM // 2
    gq = jnp.concatenate([gq_ref[...]] * (tm // LANES), axis=1)
    cost = cost_ref[...]
    sint = sint_ref[...]
    for c in range(ATTN_WIDTH // LANES):
        zt = z[:, c * LANES:(c + 1) * LANES].T
        z3 = zt.reshape(heads, HEAD_DIM, tm)
        inv = lax.rsqrt(jnp.mean(z3 * z3, axis=1, keepdims=True) + EPS)
        y = (z3 * inv).reshape(LANES, tm) * gq
        partner = jnp.concatenate(
            [y[(r ^ 1) * half:((r ^ 1) + 1) * half, :] for r in range(LANES // half)], axis=0)
        qt_ref[0, c * LANES:(c + 1) * LANES, :] = (y * cost + partner * sint).astype(BF16)

    kz = z[:, ATTN_WIDTH:QK_WIDTH]
    sq_hi, sq_lo = _split(kz * kz)
    ssq = _dot(sq_hi, eh_ref[...]) + _dot(sq_lo, eh_ref[...])
    inv_hi, inv_lo = _split(lax.rsqrt(ssq * (1.0 / HEAD_DIM) + EPS))
    yk = kz * (_dot(inv_hi, eht_ref[...]) + _dot(inv_lo, eht_ref[...])) * gk_ref[...]
    lane = lax.broadcasted_iota(jnp.int32, (tm, LANES), 1)
    partner = jnp.where((lane % ROPE_AXIS_DIM) < half,
                        pltpu.roll(yk, LANES - half, 1), pltpu.roll(yk, half, 1))
    k_ref[0] = (yk * cos_ref[...] + partner * sin_ref[...]).astype(BF16)
    vt_ref[0] = z[:, OFF_V:OFF_V + KV_WIDTH].T.astype(BF16)
    gla_ref[0] = z[:, OFF_GLA:OFF_ML]
    ml_ref[0] = z[:, OFF_ML:OFF_SMALL]
    small_ref[0] = z[:, OFF_SMALL:IN_PERM_WIDTH]


def _in_proj(x2, lp, rope, B, T, tm=512):
    N, D = x2.shape
    tpb = T // tm
    const = lambda i: (0, 0)
    tok3 = lambda i: (i // tpb, i % tpb, 0)
    tokT = lambda i: (i // tpb, 0, i % tpb)
    return pl.pallas_call(
        _in_proj_kernel,
        grid=(N // tm,),
        in_specs=[
            pl.BlockSpec((tm, D), lambda i: (i, 0)),
            pl.BlockSpec((1, D), const),
            pl.BlockSpec((D, IN_PERM_WIDTH), const),
            pl.BlockSpec((tm, LANES), lambda i: (i % tpb, 0)),
            pl.BlockSpec((tm, LANES), lambda i: (i % tpb, 0)),
            pl.BlockSpec((LANES, tm), lambda i: (0, i % tpb)),
            pl.BlockSpec((LANES, tm), lambda i: (0, i % tpb)),
            pl.BlockSpec((LANES, LANES), const),
            pl.BlockSpec((1, LANES), const),
            pl.BlockSpec((LANES, LANES), const),
            pl.BlockSpec((LANES, LANES), const),
        ],
        out_specs=[
            pl.BlockSpec((1, ATTN_WIDTH, tm), tokT),
            pl.BlockSpec((1, tm, KV_WIDTH), tok3),
            pl.BlockSpec((1, KV_WIDTH, tm), tokT),
            pl.BlockSpec((1, tm, 4 * GLA_WIDTH), tok3),
            pl.BlockSpec((1, tm, 4 * MLSTM_WIDTH), tok3),
            pl.BlockSpec((1, tm, LANES), tok3),
        ],
        out_shape=[
            jax.ShapeDtypeStruct((B, ATTN_WIDTH, T), BF16),
            jax.ShapeDtypeStruct((B, T, KV_WIDTH), BF16),
            jax.ShapeDtypeStruct((B, KV_WIDTH, T), BF16),
            jax.ShapeDtypeStruct((B, T, 4 * GLA_WIDTH), F32),
            jax.ShapeDtypeStruct((B, T, 4 * MLSTM_WIDTH), F32),
            jax.ShapeDtypeStruct((B, T, LANES), F32),
        ],
        compiler_params=_params("arbitrary"),
        name="in_proj",
    )(x2, lp["g_mix"], lp["w_in"], rope["cos"], rope["sin"], rope["cos_t"], rope["sin_t"],
      lp["q_gain_t"], lp["k_gain"], rope["eh"], rope["eht"])


ATTN_SAFE_LOGIT = 40.0
LOG2E = 1.4426950408889634


def _attn_kernel(safe_ref, qt_ref, k_ref, vt_ref, o_ref, *, tk):
    tq = qt_ref.shape[2]
    T = k_ref.shape[1]
    G = ATTN_HEADS // ATTN_KV_HEADS
    n = G * tq
    zeros = jnp.zeros((HEAD_DIM, n), BF16)

    def q_operand(j):
        base = j * G * HEAD_DIM
        qs = jnp.concatenate(
            [qt_ref[0, base + h * HEAD_DIM:base + (h + 1) * HEAD_DIM, :] for h in range(G)], axis=1)
        return jnp.concatenate([qs, zeros] if j == 0 else [zeros, qs], axis=0)

    def finish(j, acc, l):
        base = j * G * HEAD_DIM
        o = acc * (1.0 / l)
        ot = jnp.concatenate([o[:, h * tq:(h + 1) * tq] for h in range(G)], axis=0)
        o_ref[0, :, base:base + G * HEAD_DIM] = ot.T.astype(BF16)

    @pl.when(safe_ref[0] == 1)
    def _():
        qps = [q_operand(j) for j in range(ATTN_KV_HEADS)]
        units = [(c, j) for c in range(T // tk) for j in range(ATTN_KV_HEADS)]

        def scores(u):
            c, j = units[u]
            return _dot(k_ref[0, c * tk:(c + 1) * tk, :], qps[j])

        l8 = [jnp.zeros((SUBLANES, n), F32)] * ATTN_KV_HEADS
        acc = [jnp.zeros((HEAD_DIM, n), F32)] * ATTN_KV_HEADS
        st = scores(0)
        for u, (c, j) in enumerate(units):
            st_next = scores(u + 1) if u + 1 < len(units) else None
            p = jnp.exp2(st)
            l8[j] = l8[j] + jnp.sum(p.reshape(tk // SUBLANES, SUBLANES, n), axis=0)
            vc = vt_ref[0, j * HEAD_DIM:(j + 1) * HEAD_DIM, c * tk:(c + 1) * tk]
            acc[j] = acc[j] + _dot(vc, p.astype(BF16))
            st = st_next
        for j in range(ATTN_KV_HEADS):
            finish(j, acc[j], jnp.sum(l8[j], axis=0, keepdims=True))

    @pl.when(safe_ref[0] == 0)
    def _():
        for j in range(ATTN_KV_HEADS):
            qp = q_operand(j)

            def body(c, carry, qp=qp, j=j):
                m, l, acc = carry
                off = pl.multiple_of(c * tk, tk)
                st = _dot(k_ref[0, pl.ds(off, tk), :], qp)
                m_new = jnp.maximum(m, jnp.max(st, axis=0, keepdims=True))
                alpha = jnp.exp2(m - m_new)
                p = jnp.exp2(st - m_new)
                l = alpha * l + jnp.sum(p, axis=0, keepdims=True)
                vc = vt_ref[0, j * HEAD_DIM:(j + 1) * HEAD_DIM, pl.ds(off, tk)]
                return m_new, l, alpha * acc + _dot(vc, p.astype(BF16))

            init = (jnp.full((1, n), NEG, F32), jnp.zeros((1, n), F32), jnp.zeros((HEAD_DIM, n), F32))
            _, l, acc = lax.fori_loop(0, T // tk, body, init)
            finish(j, acc, l)


def _attention(safe, qt, k, vt, tq=256, tk=256):
    B, _, T = qt.shape
    tk = min(tk, T)
    return pl.pallas_call(
        functools.partial(_attn_kernel, tk=tk),
        grid_spec=pltpu.PrefetchScalarGridSpec(
            num_scalar_prefetch=1,
            grid=(B, T // tq),
            in_specs=[
                pl.BlockSpec((1, ATTN_WIDTH, tq), lambda b, i, s: (b, 0, i)),
                pl.BlockSpec((1, T, KV_WIDTH), lambda b, i, s: (b, 0, 0)),
                pl.BlockSpec((1, KV_WIDTH, T), lambda b, i, s: (b, 0, 0)),
            ],
            out_specs=pl.BlockSpec((1, tq, ATTN_WIDTH), lambda b, i, s: (b, i, 0)),
        ),
        out_shape=jax.ShapeDtypeStruct((B, T, ATTN_WIDTH), BF16),
        compiler_params=_params("arbitrary", "arbitrary"),
        name="attention",
    )(safe, qt, k, vt)


def _chunk_scan(x, pos, op, fill, reverse):
    rows = x.shape[0]
    s = 1
    while s < CHUNK:
        if reverse:
            shifted = jnp.where(pos < CHUNK - s, pltpu.roll(x, rows - s, 0), fill)
        else:
            shifted = jnp.where(pos >= s, pltpu.roll(x, s, 0), fill)
        x = op(x, shifted)
        s *= 2
    return x


def _chunk_pos(rows):
    return lax.broadcasted_iota(jnp.int32, (rows, LANES), 0) % CHUNK


def _chunk_row(a, reverse_dir, idx_fwd, idx_bwd):
    rows = a.shape[0]
    a3 = a.reshape(rows // CHUNK, CHUNK, LANES)
    i = idx_bwd if reverse_dir else idx_fwd
    return jnp.broadcast_to(a3[:, i:i + 1, :], a3.shape).reshape(rows, LANES)


def _stack_heads(x):
    lane = lax.broadcasted_iota(jnp.int32, x.shape, 1)
    zero = jnp.zeros_like(x)
    return jnp.concatenate([jnp.where(lane < HEAD_DIM, x, zero), jnp.where(lane >= HEAD_DIM, x, zero)], axis=0)


def _select_heads(x):
    c = x.shape[0] // 2
    lane = lax.broadcasted_iota(jnp.int32, (c, x.shape[1]), 1)
    return jnp.where(lane < HEAD_DIM, x[:c], x[c:])


def _stacked_causal_masks():
    ci = lax.broadcasted_iota(jnp.int32, (2 * CHUNK, CHUNK), 0) % CHUNK
    si = lax.broadcasted_iota(jnp.int32, (2 * CHUNK, CHUNK), 1)
    return ci >= si, ci <= si


def _pair_blockdiag(width):
    r = lax.broadcasted_iota(jnp.int32, (LANES, width), 0) // HEAD_DIM
    c = lax.broadcasted_iota(jnp.int32, (LANES, width), 1) // (width // 2)
    return r == c


def _head_rms(o, gain):
    lane = lax.broadcasted_iota(jnp.int32, o.shape, 1)
    lo = lane < HEAD_DIM
    sq = o * o
    s_lo = jnp.sum(jnp.where(lo, sq, 0.0), axis=-1, keepdims=True)
    s_hi = jnp.sum(jnp.where(lo, 0.0, sq), axis=-1, keepdims=True)
    ms = jnp.where(lo, s_lo, s_hi) * (1.0 / HEAD_DIM)
    return o * lax.rsqrt(ms + EPS) * gain


PREP_ROWS = 512
GLA_CHUNKS_PER_STEP = 4


def _gla_kernel(q_ref, k_ref, v_ref, g_ref, small_ref, wdh_ref, wdl_ref, bd_ref, gain_ref,
                o_ref, qe_s, ke_s, kd_s, qb_s, dec_s, of_s, ob_s):
    T = q_ref.shape[1]
    nc = T // CHUNK
    R = min(PREP_ROWS, T)
    pos = _chunk_pos(R)
    w_hi = jnp.concatenate([wdh_ref[0], wdh_ref[1]], axis=1)
    w_lo = jnp.concatenate([wdl_ref[0], wdl_ref[1]], axis=1)
    bias = jnp.concatenate([bd_ref[0], bd_ref[1]], axis=1)

    def prep(t, _):
        r0 = pl.multiple_of(t * R, R)
        q = q_ref[0, pl.ds(r0, R), :] * HEAD_DIM ** -0.5
        k = k_ref[0, pl.ds(r0, R), :]
        la2 = _log_sigmoid(_dot3(small_ref[0, pl.ds(r0, R), :], w_hi, w_lo) + bias) * (1.0 / GLA_TAU)
        for d in range(2):
            b = _chunk_scan(la2[:, d * LANES:(d + 1) * LANES], pos, jnp.add, 0.0, reverse=bool(d))
            b_mid = _chunk_row(b, d, CHUNK // 2 - 1, CHUNK // 2)
            b_last = _chunk_row(b, d, CHUNK - 1, 0)
            qe_s[d, pl.ds(r0, R), :] = (q * jnp.exp(b - b_mid)).astype(BF16)
            ke_s[d, pl.ds(r0, R), :] = (k * jnp.exp(b_mid - b)).astype(BF16)
            kd_s[d, pl.ds(r0, R), :] = (k * jnp.exp(b_last - b)).astype(BF16)
            qb_s[d, pl.ds(r0, R), :] = (q * jnp.exp(b)).astype(BF16)
            dec = jnp.exp(b_last).reshape(R // CHUNK, CHUNK, LANES)[:, :SUBLANES, :]
            dec_s[d, pl.ds(pl.multiple_of(t * (R // CHUNK) * SUBLANES, SUBLANES), (R // CHUNK) * SUBLANES), :] = (
                dec.reshape((R // CHUNK) * SUBLANES, LANES))
        return 0

    lax.fori_loop(0, T // R, prep, 0)

    outs = (of_s, ob_s)
    masks = _stacked_causal_masks()
    blockdiag = _pair_blockdiag(LANES)
    G = GLA_CHUNKS_PER_STEP

    def step(it, states):
        units = [(d, g) for d in range(2) for g in range(G)]
        ops = {}
        for d, g in units:
            c = it * G + g
            c = (nc - 1 - c) if d else c
            r0 = pl.multiple_of(c * CHUNK, CHUNK)
            ops[d, g] = dict(
                r0=r0,
                qe=qe_s[d, pl.ds(r0, CHUNK), :], ke=ke_s[d, pl.ds(r0, CHUNK), :],
                kd=kd_s[d, pl.ds(r0, CHUNK), :], qb=qb_s[d, pl.ds(r0, CHUNK), :],
                v=v_ref[0, pl.ds(r0, CHUNK), :].astype(BF16),
                dec=dec_s[d, pl.ds(pl.multiple_of(c * SUBLANES, SUBLANES), SUBLANES), :][0:1, :])
        a2, kv = {}, {}
        for u in units:
            o = ops[u]
            a2[u] = _dot_nt(_stack_heads(o["qe"]), o["ke"])
            kv[u] = _dot_tn(o["v"], o["kd"])
        st_in = {}
        new_states = []
        for d in range(2):
            st = states[d]
            for g in range(G):
                st_in[d, g] = st.astype(BF16)
                st = st * ops[d, g]["dec"] + jnp.where(blockdiag, kv[d, g], 0.0)
            new_states.append(st)
        for u in units:
            o = ops[u]
            a = jnp.where(masks[u[0]], a2[u], 0.0).astype(BF16)
            intra = _select_heads(_dot(a, o["v"]))
            outs[u[0]][pl.ds(o["r0"], CHUNK), :] = intra + _dot_nt(o["qb"], st_in[u])
        return tuple(new_states)

    zero = jnp.zeros((LANES, LANES), F32)
    lax.fori_loop(0, nc // G, step, (zero, zero))

    def post(t, _):
        r0 = pl.multiple_of(t * R, R)
        o = _head_rms(of_s[pl.ds(r0, R), :] + ob_s[pl.ds(r0, R), :], gain_ref[...])
        g = g_ref[0, pl.ds(r0, R), :]
        o_ref[0, pl.ds(r0, R), :] = (o * (g * _sigmoid(g))).astype(BF16)
        return 0

    lax.fori_loop(0, T // R, post, 0)


def _gla(gla, small, lp):
    B, T, _ = gla.shape
    pairs = GLA_WIDTH // LANES
    col = lambda off: (lambda b, hp: (b, 0, off * pairs + hp))
    return pl.pallas_call(
        _gla_kernel,
        grid=(B, pairs),
        in_specs=[
            pl.BlockSpec((1, T, LANES), col(0)),
            pl.BlockSpec((1, T, LANES), col(1)),
            pl.BlockSpec((1, T, LANES), col(2)),
            pl.BlockSpec((1, T, LANES), col(3)),
            pl.BlockSpec((1, T, LANES), lambda b, hp: (b, 0, 0)),
            pl.BlockSpec((2, LANES, LANES), lambda b, hp: (0, 0, hp)),
            pl.BlockSpec((2, LANES, LANES), lambda b, hp: (0, 0, hp)),
            pl.BlockSpec((2, 1, LANES), lambda b, hp: (0, 0, hp)),
            pl.BlockSpec((1, LANES), lambda b, hp: (0, 0)),
        ],
        out_specs=pl.BlockSpec((1, T, LANES), lambda b, hp: (b, 0, hp)),
        out_shape=jax.ShapeDtypeStruct((B, T, GLA_WIDTH), BF16),
        scratch_shapes=[
            pltpu.VMEM((2, T, LANES), BF16), pltpu.VMEM((2, T, LANES), BF16),
            pltpu.VMEM((2, T, LANES), BF16), pltpu.VMEM((2, T, LANES), BF16),
            pltpu.VMEM((2, (T // CHUNK) * SUBLANES, LANES), F32),
            pltpu.VMEM((T, LANES), F32), pltpu.VMEM((T, LANES), F32),
        ],
        compiler_params=_params("arbitrary", "arbitrary"),
        name="gla",
    )(gla, gla, gla, gla, small, lp["wd_hi"], lp["wd_lo"], lp["bd"], lp["gla_gain"])


GATE_I = SMALL_GATE_LANE
GATE_F = SMALL_GATE_LANE + MLSTM_HEADS


def _gate_lane(d, hh):
    return GATE_F + 2 * MLSTM_HEADS * d + hh


MLSTM_CHUNKS_PER_STEP = 4


def _chunk_rows8(a, row):
    n = a.shape[0] // CHUNK
    a3 = a.reshape(n, CHUNK, LANES)
    full = jnp.broadcast_to(a3[:, row:row + 1, :], a3.shape)
    return full[:, :SUBLANES, :].reshape(n * SUBLANES, LANES)


def _expand_rows8(a8):
    n, w = a8.shape[0] // SUBLANES, a8.shape[1]
    a3 = a8.reshape(n, SUBLANES, w)[:, 0:1, :]
    return jnp.broadcast_to(a3, (n, CHUNK, w)).reshape(n * CHUNK, w)


def _gate_select():
    src = lax.broadcasted_iota(jnp.int32, (LANES, 2 * LANES), 0)
    dst = lax.broadcasted_iota(jnp.int32, (LANES, 2 * LANES), 1)
    want = GATE_F + 2 * MLSTM_HEADS * (dst // LANES) + (dst % LANES) // HEAD_DIM
    return jnp.where(src == want, 1.0, 0.0).astype(BF16)


def _gate_broadcast(x, sel, gate_mask):
    hi, lo = _split(jnp.where(gate_mask, x, 0.0))
    return _dot(hi, sel) + _dot(lo, sel)


def _mlstm_kernel(q_ref, k_ref, v_ref, og_ref, small_ref, wq_ref, wk_ref, bq_ref, bk_ref,
                  gbias_ref, gain_ref, o_ref, q_s, k_s, b_s, r_s, cm_s, fl1_s, bl_s, rl_s, mf_s, mb_s,
                  wp_s, qi_s, kw_s, rt_s, st_s, of_s, ob_s):
    T = q_ref.shape[1]
    nc = T // CHUNK
    R = min(PREP_ROWS, T)
    hp = pl.program_id(1)
    G = MLSTM_CHUNKS_PER_STEP
    cpt = R // CHUNK

    row = lax.broadcasted_iota(jnp.int32, (T, LANES), 0)
    for src, w_ref, b_ref, dst, scale in ((q_ref, wq_ref, bq_ref, q_s, 1.0),
                                          (k_ref, wk_ref, bk_ref, k_s, HEAD_DIM ** -0.5)):
        xc = src[0]
        prev = jnp.where(row >= 1, pltpu.roll(xc, 1, 0), 0.0)
        nxt = jnp.where(row < T - 1, pltpu.roll(xc, T - 1, 0), 0.0)
        y = prev * w_ref[0:1, :] + xc * w_ref[1:2, :] + nxt * w_ref[2:3, :] + b_ref[...]
        dst[...] = (y * _sigmoid(y) * scale).astype(BF16)

    pos = _chunk_pos(R)
    lane = lax.broadcasted_iota(jnp.int32, (R, LANES), 1)
    is_bwd = lane >= GATE_I + 2 * MLSTM_HEADS
    is_bwd8 = lax.broadcasted_iota(jnp.int32, (cpt * SUBLANES, LANES), 1) >= GATE_I + 2 * MLSTM_HEADS
    heads_per_pair = LANES // HEAD_DIM
    shift = (LANES - heads_per_pair * hp) % LANES
    gate_lanes = [_gate_lane(d, hh) for d in range(2) for hh in range(2)]
    gate_mask = functools.reduce(jnp.logical_or, [lane == l for l in gate_lanes])
    lane8 = lax.broadcasted_iota(jnp.int32, (cpt * SUBLANES, LANES), 1)
    gate_mask8 = functools.reduce(jnp.logical_or, [lane8 == l for l in gate_lanes])

    def prep(t, _):
        r0 = pl.multiple_of(t * R, R)
        gc = pltpu.roll(small_ref[0, pl.ds(r0, R), :] + gbias_ref[...], shift, 1)
        logf = _log_sigmoid(gc)
        pre = _chunk_scan(logf, pos, jnp.add, 0.0, False)
        b = jnp.where(is_bwd, _chunk_row(pre, 0, CHUNK - 1, CHUNK - 1) - pre + logf, pre)
        r = pltpu.roll(gc, MLSTM_HEADS, 1) - b
        cm = jnp.where(is_bwd, _chunk_scan(r, pos, jnp.maximum, NEG, True),
                       _chunk_scan(r, pos, jnp.maximum, NEG, False))
        b_s[pl.ds(r0, R), :] = b
        r_s[pl.ds(r0, R), :] = r
        cm_s[pl.ds(r0, R), :] = cm
        rt_s[:, pl.ds(r0, R)] = r.T
        c8 = pl.ds(pl.multiple_of(t * cpt * SUBLANES, SUBLANES), cpt * SUBLANES)
        bl_s[c8, :] = jnp.where(is_bwd8, _chunk_rows8(b, 0), _chunk_rows8(b, CHUNK - 1))
        rl_s[c8, :] = jnp.where(is_bwd8, _chunk_rows8(cm, 0), _chunk_rows8(cm, CHUNK - 1))
        return 0

    lax.fori_loop(0, T // R, prep, 0)

    def m_chain(n, carry):
        mf, mb = carry
        rf = pl.ds(pl.multiple_of(n * SUBLANES, SUBLANES), SUBLANES)
        rb = pl.ds(pl.multiple_of((nc - 1 - n) * SUBLANES, SUBLANES), SUBLANES)
        mf_s[rf, :] = mf
        mb_s[rb, :] = mb
        return (bl_s[rf, :] + jnp.maximum(mf, rl_s[rf, :]), bl_s[rb, :] + jnp.maximum(mb, rl_s[rb, :]))

    m0 = jnp.full((SUBLANES, LANES), NEG, F32)
    lax.fori_loop(0, nc, m_chain, (m0, m0))

    expo_s, floor_s = (b_s, cm_s), (r_s, fl1_s)
    sel = _gate_select()

    def weights(t, _):
        r0 = pl.multiple_of(t * R, R)
        c8 = pl.ds(pl.multiple_of(t * cpt * SUBLANES, SUBLANES), cpt * SUBLANES)
        rows = pl.ds(r0, R)
        m_in8 = jnp.where(is_bwd8, mb_s[c8, :], mf_s[c8, :])
        bl8 = bl_s[c8, :]
        m_out8 = bl8 + jnp.maximum(m_in8, rl_s[c8, :])
        mx = jnp.maximum(_expand_rows8(m_in8), cm_s[rows, :])
        m_in_b = _gate_broadcast(m_in8, sel, gate_mask8)
        wk0_b = _gate_broadcast(bl8 - m_out8, sel, gate_mask8)
        wp_b = jnp.exp(_gate_broadcast(bl8 + m_in8 - m_out8, sel, gate_mask8))
        mx_b = _gate_broadcast(mx, sel, gate_mask)
        b_b = _gate_broadcast(b_s[rows, :], sel, gate_mask)
        r_b = _gate_broadcast(r_s[rows, :], sel, gate_mask)
        w_inter = jnp.exp(_expand_rows8(m_in_b) - mx_b)
        wk = jnp.exp(_expand_rows8(wk0_b) + r_b)
        floor = jnp.exp(-(b_b + mx_b))
        q = q_s[rows, :].astype(F32)
        k = k_s[rows, :].astype(F32)
        for d in range(2):
            half = slice(d * LANES, (d + 1) * LANES)
            qi_s[d, rows, :] = (q * w_inter[:, half]).astype(BF16)
            kw_s[d, rows, :] = (k * wk[:, half]).astype(BF16)
            wp_s[d, c8, :] = wp_b[:, half]
            expo_s[d][rows, :] = -mx_b[:, half]
            floor_s[d][rows, :] = floor[:, half]
        return 0

    lax.fori_loop(0, T // R, weights, 0)

    outs = (of_s, ob_s)
    ci = lax.broadcasted_iota(jnp.int32, (CHUNK, LANES), 0)
    si = lax.broadcasted_iota(jnp.int32, (CHUNK, LANES), 1) % CHUNK
    masks = (ci >= si, ci <= si)
    br = lax.broadcasted_iota(jnp.int32, (LANES, 2 * LANES), 0) // HEAD_DIM
    bc = (lax.broadcasted_iota(jnp.int32, (LANES, 2 * LANES), 1) % LANES) // HEAD_DIM
    blockdiag = br == bc
    ones = jnp.ones((CHUNK, LANES), BF16)
    st_s[...] = jnp.zeros(st_s.shape, F32)
    gate_rows = slice(GATE_I, GATE_I + 4 * MLSTM_HEADS)

    def step(it, _):
        units = [(d, g) for d in range(2) for g in range(G)]
        ops = {}
        for d in range(2):
            first = (nc - (it + 1) * G) if d else it * G
            rt = rt_s[gate_rows, pl.ds(pl.multiple_of(first * CHUNK, G * CHUNK), G * CHUNK)]
            for g in range(G):
                local = (G - 1 - g) if d else g
                c = first + local
                rows = pl.ds(pl.multiple_of(c * CHUNK, CHUNK), CHUNK)
                c8 = pl.ds(pl.multiple_of(c * SUBLANES, SUBLANES), SUBLANES)
                r_row = jnp.concatenate(
                    [rt[_gate_lane(d, hh) - GATE_I:_gate_lane(d, hh) - GATE_I + 1,
                        local * CHUNK:(local + 1) * CHUNK] for hh in range(2)], axis=1)
                wp = wp_s[d, c8, :][0:1, :]
                ops[d, g] = dict(
                    rows=rows, q=q_s[rows, :], k=k_s[rows, :], qi=qi_s[d, rows, :], kw=kw_s[d, rows, :],
                    va=jnp.concatenate([v_ref[0, rows, :].astype(BF16), ones], axis=1),
                    expo=expo_s[d][rows, :] + r_row, floor=floor_s[d][rows, :],
                    wp=jnp.concatenate([wp, wp], axis=1))
        qk, kv = {}, {}
        for u in units:
            o = ops[u]
            qk[u] = _dot_nt(o["q"], _stack_heads(o["k"]))
            kv[u] = _dot_tn(o["kw"], o["va"])
        st_in = {}
        for d in range(2):
            st = st_s[d]
            for g in range(G):
                st_in[d, g] = st.astype(BF16)
                st = st * ops[d, g]["wp"] + jnp.where(blockdiag, kv[d, g], 0.0)
            st_s[d] = st
        for u in units:
            d = u[0]
            o = ops[u]
            smat = (qk[u] * jnp.where(masks[d], jnp.exp(o["expo"]), 0.0)).astype(BF16)
            va_bd = jnp.where(blockdiag, jnp.concatenate([o["va"], o["va"]], axis=0), jnp.zeros((), BF16))
            num = _dot(jnp.concatenate([o["qi"], smat], axis=1), jnp.concatenate([st_in[u], va_bd], axis=0))
            den = jnp.maximum(jnp.abs(num[:, LANES:]), o["floor"])
            outs[d][o["rows"], :] = num[:, :LANES] / den
        return 0

    lax.fori_loop(0, nc // G, step, 0)

    def post(t, _):
        r0 = pl.multiple_of(t * R, R)
        h = _head_rms(of_s[pl.ds(r0, R), :] + ob_s[pl.ds(r0, R), :], gain_ref[...])
        o_ref[0, pl.ds(r0, R), :] = (h * _sigmoid(og_ref[0, pl.ds(r0, R), :])).astype(BF16)
        return 0

    lax.fori_loop(0, T // R, post, 0)


def _mlstm(ml, small, lp):
    B, T, _ = ml.shape
    pairs = MLSTM_WIDTH // LANES
    col = lambda off: (lambda b, hp: (b, 0, off * pairs + hp))
    const = lambda b, hp: (0, 0)
    return pl.pallas_call(
        _mlstm_kernel,
        grid=(B, pairs),
        in_specs=[
            pl.BlockSpec((1, T, LANES), col(0)),
            pl.BlockSpec((1, T, LANES), col(1)),
            pl.BlockSpec((1, T, LANES), col(2)),
            pl.BlockSpec((1, T, LANES), col(3)),
            pl.BlockSpec((1, T, LANES), lambda b, hp: (b, 0, 0)),
            pl.BlockSpec((3, LANES), lambda b, hp: (0, hp)),
            pl.BlockSpec((3, LANES), lambda b, hp: (0, pairs + hp)),
            pl.BlockSpec((1, LANES), lambda b, hp: (0, hp)),
            pl.BlockSpec((1, LANES), lambda b, hp: (0, pairs + hp)),
            pl.BlockSpec((1, LANES), const),
            pl.BlockSpec((1, LANES), const),
        ],
        out_specs=pl.BlockSpec((1, T, LANES), lambda b, hp: (b, 0, hp)),
        out_shape=jax.ShapeDtypeStruct((B, T, MLSTM_WIDTH), BF16),
        scratch_shapes=(
            [pltpu.VMEM((T, LANES), BF16)] * 2
            + [pltpu.VMEM((T, LANES), F32)] * 4
            + [pltpu.VMEM(((T // CHUNK) * SUBLANES, LANES), F32)] * 4
            + [pltpu.VMEM((2, (T // CHUNK) * SUBLANES, LANES), F32)]
            + [pltpu.VMEM((2, T, LANES), BF16)] * 2
            + [pltpu.VMEM((LANES, T), F32),
               pltpu.VMEM((2, LANES, 2 * LANES), F32),
               pltpu.VMEM((T, LANES), F32), pltpu.VMEM((T, LANES), F32)]),
        compiler_params=_params("arbitrary", "arbitrary"),
        name="mlstm",
    )(ml, ml, ml, ml, small, lp["conv_w"], lp["conv_w"], lp["conv_b"], lp["conv_b"],
      lp["gate_bias"], lp["ml_gain"])


def _first_argmax(vals, row):
    mx = jnp.max(vals, axis=0, keepdims=True)
    idx = jnp.min(jnp.where(vals == mx, row, vals.shape[0]), axis=0, keepdims=True)
    return mx, idx


def _out_route_kernel(x_ref, a_ref, gl_ref, ml_ref, w_ref, g_ref, wr_ref, br_ref,
                      x1_ref, t_ref, route_ref, routet_ref):
    tm = x_ref.shape[0]
    sub = min(tm, ROUTE_SUBTILE)
    spans = [slice(r, r + sub) for r in range(0, tm, sub)]
    x1s, logits = [], []
    for rs in spans:
        x1 = (x_ref[rs, :]
              + _dot(a_ref[rs, :], w_ref[0:ATTN_WIDTH, :])
              + _dot(gl_ref[rs, :], w_ref[ATTN_WIDTH:ATTN_WIDTH + GLA_WIDTH, :])
              + _dot(ml_ref[rs, :], w_ref[ATTN_WIDTH + GLA_WIDTH:, :]))
        x1_ref[rs, :] = x1
        x1s.append(x1)
    for rs, x1 in zip(spans, x1s):
        t_hi, t_lo = _split(_rms(x1, g_ref[...]))
        t_ref[rs, :] = t_hi
        both = _dot(t_hi, wr_ref[...])
        logits.append((both[:, :LANES] + both[:, LANES:] + _dot(t_lo, wr_ref[:, :LANES])) + br_ref[...])
    grow = lax.broadcasted_iota(jnp.int32, (SUBLANES, sub), 0)
    erow = lax.broadcasted_iota(jnp.int32, (N_EXPERTS, sub), 0)
    for rs, lg in zip(spans, logits):
        lt = lg.T
        gl = jnp.where(grow < N_GROUPS, lt[0:SUBLANES, :], -jnp.inf)
        gmax, gi = _first_argmax(gl, grow)
        g_prob = 1.0 / jnp.sum(jnp.exp(gl - gmax), axis=0, keepdims=True)
        el = jnp.where(erow // EXPERTS_PER_GROUP == gi, lt[ROUTE_W_LANE:ROUTE_W_LANE + N_EXPERTS, :], -jnp.inf)
        v1, i1 = _first_argmax(el, erow)
        v2, i2 = _first_argmax(jnp.where(erow == i1, -jnp.inf, el), erow)
        e2 = jnp.exp(v2 - v1)
        w1 = g_prob / (1.0 + e2)
        w2 = g_prob * e2 / (1.0 + e2)
        comb = jnp.where(erow == i1, w1, jnp.where(erow == i2, w2, 0.0))
        head = jnp.where(grow == 0, gi.astype(F32), 0.0)
        route_t = jnp.concatenate(
            [head, comb, jnp.zeros((LANES - SUBLANES - N_EXPERTS, sub), F32)], axis=0)
        route_ref[rs, :] = route_t.T
        routet_ref[:, rs] = head


def _dispatch_kernel(t_ref, route_ref, routet_ref, xs_ref, cws_ref, pos_ref, cnt_ref):
    nb = ROUTE_BLOCK
    gi_row = routet_ref[0:1, :]
    sub = lax.broadcasted_iota(jnp.int32, (SUBLANES, nb), 0).astype(F32)
    onehot = (sub == gi_row)
    ri = lax.broadcasted_iota(jnp.int32, (nb, nb), 0)
    cj = lax.broadcasted_iota(jnp.int32, (nb, nb), 1)
    before = (ri < cj).astype(BF16)
    rank = _dot(onehot.astype(BF16), before)
    counts = jnp.broadcast_to(jnp.sum(onehot.astype(F32), axis=-1, keepdims=True), (SUBLANES, LANES))
    padded = jnp.ceil(counts * (1.0 / ROUTE_TILE)) * ROUTE_TILE
    srow = lax.broadcasted_iota(jnp.int32, (SUBLANES, LANES), 0)
    incl = padded
    s = 1
    while s < SUBLANES:
        incl = incl + jnp.where(srow >= s, pltpu.roll(incl, s, 0), 0.0)
        s *= 2
    start = incl - padded
    pos_row = jnp.sum(jnp.where(onehot, start[:, 0:1] + rank, 0.0), axis=0, keepdims=True)
    cnt_ref[0] = counts.astype(jnp.int32)

    route = route_ref[...]
    lane = lax.broadcasted_iota(jnp.int32, (nb, LANES), 1)
    onehot_c = lane.astype(F32) == route[:, 0:1]
    after = (cj < ri).astype(BF16)
    rank_c = _dot(after, onehot_c.astype(BF16))
    slane = lax.broadcasted_iota(jnp.int32, (SUBLANES, LANES), 1)
    start_c = jnp.sum(jnp.where(srow == slane, start, 0.0), axis=0, keepdims=True)
    pos_col = jnp.sum(jnp.where(onehot_c, start_c + rank_c, 0.0), axis=-1, keepdims=True)
    pos_ref[...] = jnp.broadcast_to(pos_col, (nb, LANES))

    comb = jnp.where(lane >= ROUTE_W_LANE, route, 0.0)
    c_hi, c_lo = _split(comb)
    c_lo2 = (comb - c_hi.astype(F32) - c_lo.astype(F32)).astype(BF16)
    cw = (c_hi.astype(F32) + pltpu.roll(c_lo.astype(F32), ROUTE_LO_SHIFT, 1)
          + pltpu.roll(c_lo2.astype(F32), 2 * ROUTE_LO_SHIFT, 1)).astype(BF16)
    tb = t_ref[...]
    for r in range(TILES_PER_BLOCK):
        rows = (lax.broadcasted_iota(jnp.int32, (ROUTE_TILE, nb), 0) + r * ROUTE_TILE).astype(F32)
        perm = (rows == pos_row).astype(BF16)
        xs_ref[r * ROUTE_TILE:(r + 1) * ROUTE_TILE, :] = _dot(perm, tb).astype(BF16)
        cws_ref[r * ROUTE_TILE:(r + 1) * ROUTE_TILE, :] = _dot(perm, cw)


def _route_dispatch_kernel(x_ref, a_ref, gl_ref, ml_ref, w_ref, g_ref, wr_ref, br_ref,
                           x1_ref, xs_ref, cws_ref, pos_ref, cnt_ref, t_s, route_s, routet_s):
    _out_route_kernel(x_ref, a_ref, gl_ref, ml_ref, w_ref, g_ref, wr_ref, br_ref,
                      x1_ref, t_s, route_s, routet_s)
    _dispatch_kernel(t_s, route_s, routet_s, xs_ref, cws_ref, pos_ref, cnt_ref)


def _route_dispatch(x2, attn, gla_o, ml_o, lp):
    N, D = x2.shape
    nblk = N // ROUTE_BLOCK
    const = lambda i: (0, 0)
    tok = lambda i: (i, 0)
    return pl.pallas_call(
        _route_dispatch_kernel,
        grid=(nblk,),
        in_specs=[
            pl.BlockSpec((ROUTE_BLOCK, D), tok),
            pl.BlockSpec((ROUTE_BLOCK, ATTN_WIDTH), tok),
            pl.BlockSpec((ROUTE_BLOCK, GLA_WIDTH), tok),
            pl.BlockSpec((ROUTE_BLOCK, MLSTM_WIDTH), tok),
            pl.BlockSpec((ATTN_WIDTH + GLA_WIDTH + MLSTM_WIDTH, D), const),
            pl.BlockSpec((1, D), const),
            pl.BlockSpec((D, 2 * LANES), const),
            pl.BlockSpec((1, LANES), const),
        ],
        out_specs=[
            pl.BlockSpec((ROUTE_BLOCK, D), tok),
            pl.BlockSpec((ROUTE_ROWS, D), tok),
            pl.BlockSpec((ROUTE_ROWS, LANES), tok),
            pl.BlockSpec((ROUTE_BLOCK, LANES), tok),
            pl.BlockSpec((1, SUBLANES, LANES), lambda i: (i, 0, 0)),
        ],
        out_shape=[
            jax.ShapeDtypeStruct((N, D), F32),
            jax.ShapeDtypeStruct((nblk * ROUTE_ROWS, D), BF16),
            jax.ShapeDtypeStruct((nblk * ROUTE_ROWS, LANES), F32),
            jax.ShapeDtypeStruct((N, LANES), F32),
            jax.ShapeDtypeStruct((nblk, SUBLANES, LANES), jnp.int32),
        ],
        scratch_shapes=[
            pltpu.VMEM((ROUTE_BLOCK, D), BF16),
            pltpu.VMEM((ROUTE_BLOCK, LANES), F32),
            pltpu.VMEM((SUBLANES, ROUTE_BLOCK), F32),
        ],
        compiler_params=_params("arbitrary"),
        name="route_dispatch",
    )(x2, attn, gla_o, ml_o, lp["w_out"], lp["g_ffn"], lp["w_route"], lp["b_route"])


def _tile_schedule(cnt):
    nblk = cnt.shape[0]
    ntile = (cnt + ROUTE_TILE - 1) // ROUTE_TILE
    end = jnp.cumsum(ntile, axis=1)
    r = jnp.arange(TILES_PER_BLOCK, dtype=jnp.int32)
    grp = jnp.sum(r[None, :, None] >= end[:, None, :], axis=-1)
    grp = grp.reshape(-1).astype(jnp.int32)
    tile = jnp.arange(nblk * TILES_PER_BLOCK, dtype=jnp.int32)
    order = jnp.argsort(grp * (nblk * TILES_PER_BLOCK) + tile).astype(jnp.int32)
    n_active = jnp.sum(grp < N_GROUPS).astype(jnp.int32)
    g_sorted = grp[order]
    last_group = g_sorted[jnp.maximum(n_active - 1, 0)]
    g_sorted = jnp.where(g_sorted < N_GROUPS, g_sorted, last_group)
    return order, g_sorted, n_active[None]


def _expert_kernel(trow_ref, tgrp_ref, nact_ref, xs_ref, cws_ref, wg_ref, wu_ref, wd_ref, ys_ref):
    i = pl.program_id(0)

    @pl.when(i < nact_ref[0])
    def _():
        x = xs_ref[...]
        cws = cws_ref[...]
        lane = lax.broadcasted_iota(jnp.int32, cws.shape, 1)
        first = ROUTE_W_LANE + tgrp_ref[i] * EXPERTS_PER_GROUP
        y = jnp.zeros(ys_ref.shape, F32)

        def hidden(j):
            return _dot(x, wg_ref[j]), _dot(x, wu_ref[j])

        h = hidden(0)
        for j in range(EXPERTS_PER_GROUP):
            h_next = hidden(j + 1) if j + 1 < EXPERTS_PER_GROUP else None
            off = lane - (first + j)
            sel = (off == 0) | (off == ROUTE_LO_SHIFT) | (off == 2 * ROUTE_LO_SHIFT)
            wj = jnp.sum(jnp.where(sel, cws, 0.0), axis=-1, keepdims=True)
            a = (h[0] * _sigmoid(h[0]) * h[1]).astype(BF16)
            y = y + wj * _dot(a, wd_ref[j])
            h = h_next
        ys_ref[...] = y.astype(BF16)

    @pl.when(i >= nact_ref[0])
    def _():
        ys_ref[...] = jnp.zeros(ys_ref.shape, BF16)


def _experts(xs, cws, order, grp, n_active, lp):
    rows, D = xs.shape
    n_tiles = rows // ROUTE_TILE
    tile = lambda i, trow, tgrp, nact: (trow[i], 0)
    layer_groups = lp["layer"] * N_GROUPS
    wsel = lambda i, trow, tgrp, nact: (layer_groups + tgrp[i], 0, 0)
    return pl.pallas_call(
        _expert_kernel,
        grid_spec=pltpu.PrefetchScalarGridSpec(
            num_scalar_prefetch=3,
            grid=(n_tiles,),
            in_specs=[
                pl.BlockSpec((ROUTE_TILE, D), tile),
                pl.BlockSpec((ROUTE_TILE, LANES), tile),
                pl.BlockSpec((EXPERTS_PER_GROUP, D, D_EXPERT), wsel),
                pl.BlockSpec((EXPERTS_PER_GROUP, D, D_EXPERT), wsel),
                pl.BlockSpec((EXPERTS_PER_GROUP, D_EXPERT, D), wsel),
            ],
            out_specs=pl.BlockSpec((ROUTE_TILE, D), tile),
        ),
        out_shape=jax.ShapeDtypeStruct((rows, D), BF16),
        compiler_params=_params("arbitrary"),
        name="experts",
    )(order, grp, n_active, xs, cws, lp["w_gate"], lp["w_up"], lp["w_down"])


def _combine_kernel(ys_ref, pos_ref, x1_ref, p_ref, g_ref, wpg_ref, wpp_ref, gfin_ref, o_ref,
                    *, embed, final):
    tm = x1_ref.shape[0]
    pos = pos_ref[...]
    lane = lax.broadcasted_iota(jnp.int32, (tm, LANES), 1).astype(F32)
    perm_t = jnp.concatenate(
        [(lane + r * LANES == pos).astype(BF16) for r in range(ROUTE_ROWS // LANES)], axis=1)
    x = x1_ref[...] + _dot(perm_t, ys_ref[...])
    if embed:
        gate = _sigmoid(_dot(_rms(x, g_ref[...]).astype(BF16), wpg_ref[...]))
        x = x + gate * _dot(p_ref[...].astype(BF16), wpp_ref[...])
    if final:
        x = _rms(x, gfin_ref[...])
    o_ref[...] = x


def _combine(ys, pos, x1, p2, lp, g_final, embed=True, final=False, tm=512):
    N, D = x1.shape
    inner = ROUTE_BLOCK // tm
    tok = lambda b, i: (b * inner + i, 0)
    const = lambda b, i: (0, 0)
    return pl.pallas_call(
        functools.partial(_combine_kernel, embed=embed, final=final),
        grid=(N // ROUTE_BLOCK, inner),
        in_specs=[
            pl.BlockSpec((ROUTE_ROWS, D), lambda b, i: (b, 0)),
            pl.BlockSpec((tm, LANES), tok),
            pl.BlockSpec((tm, D), tok),
            pl.BlockSpec((tm, p2.shape[1]), tok),
            pl.BlockSpec((1, D), const),
            pl.BlockSpec((D, D), const),
            pl.BlockSpec((p2.shape[1], D), const),
            pl.BlockSpec((1, D), const),
        ],
        out_specs=pl.BlockSpec((tm, D), tok),
        out_shape=jax.ShapeDtypeStruct((N, D), F32),
        compiler_params=_params("arbitrary", "arbitrary"),
        name="combine",
    )(ys, pos, x1, p2, lp["g_ple"], lp["w_pg"], lp["w_pp"], g_final)


def _mix_out_moe(x2, attn, gla_o, ml_o, lp):
    x1, xs, cws, pos, cnt = _route_dispatch(x2, attn, gla_o, ml_o, lp)
    order, grp, n_active = _tile_schedule(cnt[:, :N_GROUPS, 0])
    return x1, _experts(xs, cws, order, grp, n_active, lp), pos


def _moe_debug(x2, lp):
    N, D = x2.shape
    zeros = lambda w: jnp.zeros((N, w), BF16)
    x1, ys, pos = _mix_out_moe(x2, zeros(ATTN_WIDTH), zeros(GLA_WIDTH), zeros(MLSTM_WIDTH), lp)
    p2 = jnp.zeros((N, lp["w_pp"].shape[0]), F32)
    return _combine(ys, pos, x1, p2, lp, lp["g_ple"], embed=False) - x1


def _permute_in_cols(w):
    glr0 = ATTN_WIDTH + 2 * KV_WIDTH + 4 * GLA_WIDTH
    ml0 = glr0 + 2 * GLA_RANK
    mg0 = ml0 + 4 * MLSTM_WIDTH
    end = mg0 + 4 * MLSTM_HEADS
    assert end == w.shape[1]
    pad = jnp.zeros((w.shape[0], IN_PERM_WIDTH - end), w.dtype)
    return jnp.concatenate([w[:, :glr0], w[:, ml0:mg0], w[:, glr0:ml0], w[:, mg0:end], pad], axis=1)


def _rope_tables(T):
    t = jnp.arange(T, dtype=F32)
    row = jnp.floor(t / GRID_W)
    col = t - row * GRID_W
    inv = ROPE_THETA ** (-jnp.arange(0, ROPE_AXIS_DIM, 2, dtype=F32) / ROPE_AXIS_DIM)
    ang_r = row[:, None] * inv[None, :]
    ang_c = col[:, None] * inv[None, :]
    cos_h = jnp.concatenate([jnp.cos(ang_r), jnp.cos(ang_r), jnp.cos(ang_c), jnp.cos(ang_c)], axis=1)
    sin_h = jnp.concatenate([-jnp.sin(ang_r), jnp.sin(ang_r), -jnp.sin(ang_c), jnp.sin(ang_c)], axis=1)
    reps = LANES // HEAD_DIM
    cos, sin = jnp.tile(cos_h, (1, reps)), jnp.tile(sin_h, (1, reps))
    eh = np.zeros((LANES, LANES), np.float32)
    eh[np.arange(LANES), np.arange(LANES) // HEAD_DIM] = 1.0
    return dict(cos=cos, sin=sin, cos_t=cos.T, sin_t=sin.T,
                eh=jnp.asarray(eh, BF16), eht=jnp.asarray(eh.T, BF16))


def kernel(x, p, norm_mix_g, w_in, attn_q_norm_g, attn_k_norm_g, gla_w_decay, gla_b_decay,
           gla_out_norm_g, mlstm_conv_w, mlstm_conv_b, mlstm_b_input, mlstm_b_forget,
           mlstm_out_norm_g, w_out, norm_ffn_g, w_group, b_group, w_router, b_router,
           w_expert_gate, w_expert_up, w_expert_down, norm_ple_g, w_ple_gate, w_ple_proj,
           final_norm_g):
    params = dict(
        norm_mix_g=norm_mix_g, w_in=w_in, attn_q_norm_g=attn_q_norm_g, attn_k_norm_g=attn_k_norm_g,
        gla_w_decay=gla_w_decay, gla_b_decay=gla_b_decay, gla_out_norm_g=gla_out_norm_g,
        mlstm_conv_w=mlstm_conv_w, mlstm_conv_b=mlstm_conv_b, mlstm_b_input=mlstm_b_input,
        mlstm_b_forget=mlstm_b_forget, mlstm_out_norm_g=mlstm_out_norm_g, w_out=w_out,
        norm_ffn_g=norm_ffn_g, w_group=w_group, b_group=b_group, w_router=w_router,
        b_router=b_router, w_expert_gate=w_expert_gate, w_expert_up=w_expert_up,
        w_expert_down=w_expert_down, norm_ple_g=norm_ple_g, w_ple_gate=w_ple_gate,
        w_ple_proj=w_ple_proj)
    B, T, D = x.shape
    rope = _rope_tables(T)
    N = B * T
    depth = w_in.shape[0]
    x2 = x.reshape(N, D)
    g_final = final_norm_g[None, :]
    for i in range(depth):
        lp = _layer_params(params, i)
        qt, k, vt, gla, ml, small = _in_proj(x2, lp, rope, B, T)
        attn = _attention(lp["attn_safe"], qt, k, vt).reshape(N, ATTN_WIDTH)
        gla_o = _gla(gla, small, lp).reshape(N, GLA_WIDTH)
        ml_o = _mlstm(ml, small, lp).reshape(N, MLSTM_WIDTH)
        x1, ys, pos = _mix_out_moe(x2, attn, gla_o, ml_o, lp)
        x2 = _combine(ys, pos, x1, p[i].reshape(N, -1), lp, g_final, final=(i == depth - 1))
    return x2.reshape(B, T, D)


def _split_w(w):
    hi = w.astype(BF16)
    return hi, (w - hi.astype(F32)).astype(BF16)


def _stacked_experts(w):
    return w.astype(BF16).reshape((w.shape[0] * w.shape[1],) + w.shape[2:])


def _layer_params(p, i):
    D = p["w_in"].shape[1]
    w_in = _permute_in_cols(p["w_in"][i])
    gq, gk = p["attn_q_norm_g"][i], p["attn_k_norm_g"][i]
    q_gain = jnp.tile(gq, LANES // HEAD_DIM) * (HEAD_DIM ** -0.5 * LOG2E)
    logit_bound = HEAD_DIM ** 0.5 * jnp.max(jnp.abs(gq)) * jnp.max(jnp.abs(gk))
    attn_safe = (logit_bound <= ATTN_SAFE_LOGIT).astype(jnp.int32)[None]
    wd = jnp.zeros((2, LANES, GLA_WIDTH), F32)
    wd = wd.at[0, :GLA_RANK].set(p["gla_w_decay"][i, 0]).at[1, GLA_RANK:2 * GLA_RANK].set(p["gla_w_decay"][i, 1])
    wd_hi, wd_lo = _split_w(wd)
    gate_bias = jnp.zeros((LANES,), F32).at[SMALL_GATE_LANE:SMALL_GATE_LANE + 4 * MLSTM_HEADS].set(
        jnp.concatenate([p["mlstm_b_input"][i, 0], p["mlstm_b_forget"][i, 0],
                         p["mlstm_b_input"][i, 1], p["mlstm_b_forget"][i, 1]]))
    w_route = jnp.zeros((D, LANES), F32)
    w_route = w_route.at[:, :N_GROUPS].set(p["w_group"][i])
    w_route = w_route.at[:, ROUTE_W_LANE:ROUTE_W_LANE + N_EXPERTS].set(p["w_router"][i])
    wr_hi, wr_lo = _split_w(w_route)
    b_route = jnp.zeros((LANES,), F32).at[:N_GROUPS].set(p["b_group"][i])
    b_route = b_route.at[ROUTE_W_LANE:ROUTE_W_LANE + N_EXPERTS].set(p["b_router"][i])
    return dict(
        g_mix=p["norm_mix_g"][i][None, :],
        w_in=w_in.astype(BF16),
        q_gain_t=jnp.broadcast_to(q_gain[:, None], (LANES, LANES)),
        k_gain=jnp.tile(gk, LANES // HEAD_DIM)[None, :],
        attn_safe=attn_safe,
        wd_hi=wd_hi, wd_lo=wd_lo,
        bd=p["gla_b_decay"][i][:, None, :],
        gla_gain=jnp.tile(p["gla_out_norm_g"][i], LANES // HEAD_DIM)[None, :],
        conv_w=p["mlstm_conv_w"][i],
        conv_b=p["mlstm_conv_b"][i][None, :],
        gate_bias=gate_bias[None, :],
        ml_gain=jnp.tile(p["mlstm_out_norm_g"][i], LANES // HEAD_DIM)[None, :],
        w_out=p["w_out"][i].astype(BF16),
        g_ffn=p["norm_ffn_g"][i][None, :],
        w_route=jnp.concatenate([wr_hi, wr_lo], axis=1), b_route=b_route[None, :],
        layer=i,
        w_gate=_stacked_experts(p["w_expert_gate"]),
        w_up=_stacked_experts(p["w_expert_up"]),
        w_down=_stacked_experts(p["w_expert_down"]),
        g_ple=p["norm_ple_g"][i][None, :],
        w_pg=p["w_ple_gate"][i].astype(BF16),
        w_pp=p["w_ple_proj"][i].astype(BF16),
    )
```

```python
import functools

import jax
import jax.numpy as jnp
import numpy as np
from jax import lax
from jax.experimental import pallas as pl
from jax.experimental.pallas import tpu as pltpu

F32 = jnp.float32
BF16 = jnp.bfloat16

GRID_W = 64
HEAD_DIM = 64
ATTN_HEADS = 8
ATTN_KV_HEADS = 2
GLA_HEADS = 4
MLSTM_HEADS = 4
ATTN_WIDTH = ATTN_HEADS * HEAD_DIM
KV_WIDTH = ATTN_KV_HEADS * HEAD_DIM
GLA_WIDTH = GLA_HEADS * HEAD_DIM
MLSTM_WIDTH = MLSTM_HEADS * HEAD_DIM
GLA_RANK = 16
GLA_TAU = 16.0
CHUNK = 64
ROPE_THETA = 10000.0
ROPE_AXIS_DIM = HEAD_DIM // 2
N_GROUPS = 4
EXPERTS_PER_GROUP = 4
N_EXPERTS = N_GROUPS * EXPERTS_PER_GROUP
D_EXPERT = 512
EPS = 1e-6
NEG = -1e30

LANES = 128
SUBLANES = 8
VMEM_LIMIT_BYTES = 56 * 1024 * 1024

QK_WIDTH = ATTN_WIDTH + KV_WIDTH
OFF_V = QK_WIDTH
OFF_GLA = OFF_V + KV_WIDTH
OFF_ML = OFF_GLA + 4 * GLA_WIDTH
OFF_SMALL = OFF_ML + 4 * MLSTM_WIDTH
IN_PERM_WIDTH = OFF_SMALL + LANES
SMALL_GATE_LANE = 2 * GLA_RANK

ROUTE_BLOCK = 1024
ROUTE_TILE = 128
ROUTE_ROWS = ROUTE_BLOCK + N_GROUPS * ROUTE_TILE
TILES_PER_BLOCK = ROUTE_ROWS // ROUTE_TILE
ROUTE_W_LANE = 8
ROUTE_LO_SHIFT = 32
ROUTE_SUBTILE = 256


def _dot(a, b):
    return jnp.dot(a, b, preferred_element_type=F32)


def _dot_nt(a, b):
    return lax.dot_general(a, b, (((1,), (1,)), ((), ())), preferred_element_type=F32)


def _dot_tn(a, b):
    return lax.dot_general(a, b, (((0,), (0,)), ((), ())), preferred_element_type=F32)


def _split(a):
    hi = a.astype(BF16)
    lo = (a - hi.astype(F32)).astype(BF16)
    return hi, lo


def _dot3(a, w_hi, w_lo):
    a_hi, a_lo = _split(a)
    return _dot(a_hi, w_hi) + _dot(a_lo, w_hi) + _dot(a_hi, w_lo)


def _log_sigmoid(x):
    return jnp.minimum(x, 0.0) - jnp.log(1.0 + jnp.exp(-jnp.abs(x)))


def _sigmoid(x):
    return 1.0 / (1.0 + jnp.exp(-x))


def _rms(x, g):
    return x * lax.rsqrt(jnp.mean(x * x, axis=-1, keepdims=True) + EPS) * g


def _params(*semantics):
    return pltpu.CompilerParams(dimension_semantics=semantics, vmem_limit_bytes=VMEM_LIMIT_BYTES)


def _in_proj_kernel(x_ref, g_ref, w_ref, cos_ref, sin_ref, cost_ref, sint_ref, gq_ref, gk_ref,
                    eh_ref, eht_ref, qt_ref, k_ref, vt_ref, gla_ref, ml_ref, small_ref):
    tm = x_ref.shape[0]
    h = _rms(x_ref[...], g_ref[...])
    z = _dot(h.astype(BF16), w_ref[...])

    heads = LANES // HEAD_DIM
    half = ROPE_AXIS_DIM // 2
    gq = jnp.concatenate([gq_ref[...]] * (tm // LANES), axis=1)
    cost = cost_ref[...]
    sint = sint_ref[...]
    for c in range(ATTN_WIDTH // LANES):
        zt = z[:, c * LANES:(c + 1) * LANES].T
        z3 = zt.reshape(heads, HEAD_DIM, tm)
        inv = lax.rsqrt(jnp.mean(z3 * z3, axis=1, keepdims=True) + EPS)
        y = (z3 * inv).reshape(LANES, tm) * gq
        partner = jnp.concatenate(
            [y[(r ^ 1) * half:((r ^ 1) + 1) * half, :] for r in range(LANES // half)], axis=0)
        qt_ref[0, c * LANES:(c + 1) * LANES, :] = (y * cost + partner * sint).astype(BF16)

    kz = z[:, ATTN_WIDTH:QK_WIDTH]
    sq_hi, sq_lo = _split(kz * kz)
    ssq = _dot(sq_hi, eh_ref[...]) + _dot(sq_lo, eh_ref[...])
    inv_hi, inv_lo = _split(lax.rsqrt(ssq * (1.0 / HEAD_DIM) + EPS))
    yk = kz * (_dot(inv_hi, eht_ref[...]) + _dot(inv_lo, eht_ref[...])) * gk_ref[...]
    lane = lax.broadcasted_iota(jnp.int32, (tm, LANES), 1)
    partner = jnp.where((lane % ROPE_AXIS_DIM) < half,
                        pltpu.roll(yk, LANES - half, 1), pltpu.roll(yk, half, 1))
    k_ref[0] = (yk * cos_ref[...] + partner * sin_ref[...]).astype(BF16)
    vt_ref[0] = z[:, OFF_V:OFF_V + KV_WIDTH].T.astype(BF16)
    gla_ref[0] = z[:, OFF_GLA:OFF_ML]
    ml_ref[0] = z[:, OFF_ML:OFF_SMALL]
    small_ref[0] = z[:, OFF_SMALL:IN_PERM_WIDTH]


def _in_proj(x2, lp, rope, B, T, tm=512):
    N, D = x2.shape
    tpb = T // tm
    const = lambda i: (0, 0)
    tok3 = lambda i: (i // tpb, i % tpb, 0)
    tokT = lambda i: (i // tpb, 0, i % tpb)
    return pl.pallas_call(
        _in_proj_kernel,
        grid=(N // tm,),
        in_specs=[
            pl.BlockSpec((tm, D), lambda i: (i, 0)),
            pl.BlockSpec((1, D), const),
            pl.BlockSpec((D, IN_PERM_WIDTH), const),
            pl.BlockSpec((tm, LANES), lambda i: (i % tpb, 0)),
            pl.BlockSpec((tm, LANES), lambda i: (i % tpb, 0)),
            pl.BlockSpec((LANES, tm), lambda i: (0, i % tpb)),
            pl.BlockSpec((LANES, tm), lambda i: (0, i % tpb)),
            pl.BlockSpec((LANES, LANES), const),
            pl.BlockSpec((1, LANES), const),
            pl.BlockSpec((LANES, LANES), const),
            pl.BlockSpec((LANES, LANES), const),
        ],
        out_specs=[
            pl.BlockSpec((1, ATTN_WIDTH, tm), tokT),
            pl.BlockSpec((1, tm, KV_WIDTH), tok3),
            pl.BlockSpec((1, KV_WIDTH, tm), tokT),
            pl.BlockSpec((1, tm, 4 * GLA_WIDTH), tok3),
            pl.BlockSpec((1, tm, 4 * MLSTM_WIDTH), tok3),
            pl.BlockSpec((1, tm, LANES), tok3),
        ],
        out_shape=[
            jax.ShapeDtypeStruct((B, ATTN_WIDTH, T), BF16),
            jax.ShapeDtypeStruct((B, T, KV_WIDTH), BF16),
            jax.ShapeDtypeStruct((B, KV_WIDTH, T), BF16),
            jax.ShapeDtypeStruct((B, T, 4 * GLA_WIDTH), F32),
            jax.ShapeDtypeStruct((B, T, 4 * MLSTM_WIDTH), F32),
            jax.ShapeDtypeStruct((B, T, LANES), F32),
        ],
        compiler_params=_params("arbitrary"),
        name="in_proj",
    )(x2, lp["g_mix"], lp["w_in"], rope["cos"], rope["sin"], rope["cos_t"], rope["sin_t"],
      lp["q_gain_t"], lp["k_gain"], rope["eh"], rope["eht"])


ATTN_SAFE_LOGIT = 40.0
LOG2E = 1.4426950408889634


def _attn_kernel(safe_ref, qt_ref, k_ref, vt_ref, o_ref, *, tk):
    tq = qt_ref.shape[2]
    T = k_ref.shape[1]
    G = ATTN_HEADS // ATTN_KV_HEADS
    n = G * tq
    zeros = jnp.zeros((HEAD_DIM, n), BF16)

    def q_operand(j):
        base = j * G * HEAD_DIM
        qs = jnp.concatenate(
            [qt_ref[0, base + h * HEAD_DIM:base + (h + 1) * HEAD_DIM, :] for h in range(G)], axis=1)
        return jnp.concatenate([qs, zeros] if j == 0 else [zeros, qs], axis=0)

    def finish(j, acc, l):
        base = j * G * HEAD_DIM
        o = acc * (1.0 / l)
        ot = jnp.concatenate([o[:, h * tq:(h + 1) * tq] for h in range(G)], axis=0)
        o_ref[0, :, base:base + G * HEAD_DIM] = ot.T.astype(BF16)

    @pl.when(safe_ref[0] == 1)
    def _():
        qps = [q_operand(j) for j in range(ATTN_KV_HEADS)]
        units = [(c, j) for c in range(T // tk) for j in range(ATTN_KV_HEADS)]

        def scores(u):
            c, j = units[u]
            return _dot(k_ref[0, c * tk:(c + 1) * tk, :], qps[j])

        l8 = [jnp.zeros((SUBLANES, n), F32)] * ATTN_KV_HEADS
        acc = [jnp.zeros((HEAD_DIM, n), F32)] * ATTN_KV_HEADS
        st = scores(0)
        for u, (c, j) in enumerate(units):
            st_next = scores(u + 1) if u + 1 < len(units) else None
            p = jnp.exp2(st)
            l8[j] = l8[j] + jnp.sum(p.reshape(tk // SUBLANES, SUBLANES, n), axis=0)
            vc = vt_ref[0, j * HEAD_DIM:(j + 1) * HEAD_DIM, c * tk:(c + 1) * tk]
            acc[j] = acc[j] + _dot(vc, p.astype(BF16))
            st = st_next
        for j in range(ATTN_KV_HEADS):
            finish(j, acc[j], jnp.sum(l8[j], axis=0, keepdims=True))

    @pl.when(safe_ref[0] == 0)
    def _():
        for j in range(ATTN_KV_HEADS):
            qp = q_operand(j)

            def body(c, carry, qp=qp, j=j):
                m, l, acc = carry
                off = pl.multiple_of(c * tk, tk)
                st = _dot(k_ref[0, pl.ds(off, tk), :], qp)
                m_new = jnp.maximum(m, jnp.max(st, axis=0, keepdims=True))
                alpha = jnp.exp2(m - m_new)
                p = jnp.exp2(st - m_new)
                l = alpha * l + jnp.sum(p, axis=0, keepdims=True)
                vc = vt_ref[0, j * HEAD_DIM:(j + 1) * HEAD_DIM, pl.ds(off, tk)]
                return m_new, l, alpha * acc + _dot(vc, p.astype(BF16))

            init = (jnp.full((1, n), NEG, F32), jnp.zeros((1, n), F32), jnp.zeros((HEAD_DIM, n), F32))
            _, l, acc = lax.fori_loop(0, T // tk, body, init)
            finish(j, acc, l)


def _attention(safe, qt, k, vt, tq=256, tk=128):
    B, _, T = qt.shape
    tk = min(tk, T)
    return pl.pallas_call(
        functools.partial(_attn_kernel, tk=tk),
        grid_spec=pltpu.PrefetchScalarGridSpec(
            num_scalar_prefetch=1,
            grid=(B, T // tq),
            in_specs=[
                pl.BlockSpec((1, ATTN_WIDTH, tq), lambda b, i, s: (b, 0, i)),
                pl.BlockSpec((1, T, KV_WIDTH), lambda b, i, s: (b, 0, 0)),
                pl.BlockSpec((1, KV_WIDTH, T), lambda b, i, s: (b, 0, 0)),
            ],
            out_specs=pl.BlockSpec((1, tq, ATTN_WIDTH), lambda b, i, s: (b, i, 0)),
        ),
        out_shape=jax.ShapeDtypeStruct((B, T, ATTN_WIDTH), BF16),
        compiler_params=_params("arbitrary", "arbitrary"),
        name="attention",
    )(safe, qt, k, vt)


def _chunk_scan(x, pos, op, fill, reverse):
    rows = x.shape[0]
    s = 1
    while s < CHUNK:
        if reverse:
            shifted = jnp.where(pos < CHUNK - s, pltpu.roll(x, rows - s, 0), fill)
        else:
            shifted = jnp.where(pos >= s, pltpu.roll(x, s, 0), fill)
        x = op(x, shifted)
        s *= 2
    return x


def _chunk_pos(rows):
    return lax.broadcasted_iota(jnp.int32, (rows, LANES), 0) % CHUNK


def _chunk_row(a, reverse_dir, idx_fwd, idx_bwd):
    rows = a.shape[0]
    a3 = a.reshape(rows // CHUNK, CHUNK, LANES)
    i = idx_bwd if reverse_dir else idx_fwd
    return jnp.broadcast_to(a3[:, i:i + 1, :], a3.shape).reshape(rows, LANES)


def _stack_heads(x):
    lane = lax.broadcasted_iota(jnp.int32, x.shape, 1)
    zero = jnp.zeros_like(x)
    return jnp.concatenate([jnp.where(lane < HEAD_DIM, x, zero), jnp.where(lane >= HEAD_DIM, x, zero)], axis=0)


def _select_heads(x):
    c = x.shape[0] // 2
    lane = lax.broadcasted_iota(jnp.int32, (c, x.shape[1]), 1)
    return jnp.where(lane < HEAD_DIM, x[:c], x[c:])


def _stacked_causal_masks():
    ci = lax.broadcasted_iota(jnp.int32, (2 * CHUNK, CHUNK), 0) % CHUNK
    si = lax.broadcasted_iota(jnp.int32, (2 * CHUNK, CHUNK), 1)
    return ci >= si, ci <= si


def _pair_blockdiag(width):
    r = lax.broadcasted_iota(jnp.int32, (LANES, width), 0) // HEAD_DIM
    c = lax.broadcasted_iota(jnp.int32, (LANES, width), 1) // (width // 2)
    return r == c


def _head_rms(o, gain):
    lane = lax.broadcasted_iota(jnp.int32, o.shape, 1)
    lo = lane < HEAD_DIM
    sq = o * o
    s_lo = jnp.sum(jnp.where(lo, sq, 0.0), axis=-1, keepdims=True)
    s_hi = jnp.sum(jnp.where(lo, 0.0, sq), axis=-1, keepdims=True)
    ms = jnp.where(lo, s_lo, s_hi) * (1.0 / HEAD_DIM)
    return o * lax.rsqrt(ms + EPS) * gain


PREP_ROWS = 512
GLA_CHUNKS_PER_STEP = 4


def _gla_kernel(q_ref, k_ref, v_ref, g_ref, small_ref, wdh_ref, wdl_ref, bd_ref, gain_ref,
                o_ref, qe_s, ke_s, dec_s, of_s, ob_s):
    T = q_ref.shape[1]
    nc = T // CHUNK
    R = min(PREP_ROWS, T)
    cpt = R // CHUNK
    pos = _chunk_pos(R)
    w_hi = jnp.concatenate([wdh_ref[0], wdh_ref[1]], axis=1)
    w_lo = jnp.concatenate([wdl_ref[0], wdl_ref[1]], axis=1)
    bias = jnp.concatenate([bd_ref[0], bd_ref[1]], axis=1)

    def prep(t, _):
        r0 = pl.multiple_of(t * R, R)
        q = q_ref[0, pl.ds(r0, R), :] * HEAD_DIM ** -0.5
        k = k_ref[0, pl.ds(r0, R), :]
        la2 = _log_sigmoid(_dot3(small_ref[0, pl.ds(r0, R), :], w_hi, w_lo) + bias) * (1.0 / GLA_TAU)
        for d in range(2):
            b = _chunk_scan(la2[:, d * LANES:(d + 1) * LANES], pos, jnp.add, 0.0, reverse=bool(d))
            b_mid = _chunk_row(b, d, CHUNK // 2 - 1, CHUNK // 2)
            b_last = _chunk_row(b, d, CHUNK - 1, 0)
            qe_s[d, pl.ds(r0, R), :] = (q * jnp.exp(b - b_mid)).astype(BF16)
            ke_s[d, pl.ds(r0, R), :] = (k * jnp.exp(b_mid - b)).astype(BF16)
            c8 = pl.ds(pl.multiple_of(t * cpt * SUBLANES, SUBLANES), cpt * SUBLANES)
            for kind, val in enumerate((b_last, b_last - b_mid, b_mid)):
                rows8 = val.reshape(cpt, CHUNK, LANES)[:, :SUBLANES, :].reshape(cpt * SUBLANES, LANES)
                dec_s[d, kind, c8, :] = jnp.exp(rows8)
        return 0

    lax.fori_loop(0, T // R, prep, 0)

    outs = (of_s, ob_s)
    masks = _stacked_causal_masks()
    blockdiag = _pair_blockdiag(LANES)
    G = GLA_CHUNKS_PER_STEP

    def step(it, states):
        units = [(d, g) for d in range(2) for g in range(G)]
        ops = {}
        for d, g in units:
            c = it * G + g
            c = (nc - 1 - c) if d else c
            r0 = pl.multiple_of(c * CHUNK, CHUNK)
            c8 = pl.ds(pl.multiple_of(c * SUBLANES, SUBLANES), SUBLANES)
            ops[d, g] = dict(
                r0=r0,
                qe=qe_s[d, pl.ds(r0, CHUNK), :], ke=ke_s[d, pl.ds(r0, CHUNK), :],
                v=v_ref[0, pl.ds(r0, CHUNK), :].astype(BF16),
                dec=dec_s[d, 0, c8, :][0:1, :], to_end=dec_s[d, 1, c8, :][0:1, :],
                from_start=dec_s[d, 2, c8, :][0:1, :])
        a2, kv = {}, {}
        for u in units:
            o = ops[u]
            a2[u] = _dot_nt(_stack_heads(o["qe"]), o["ke"])
            kv[u] = _dot_tn(o["v"], o["ke"]) * o["to_end"]
        st_in = {}
        new_states = []
        for d in range(2):
            st = states[d]
            for g in range(G):
                st_in[d, g] = (st * ops[d, g]["from_start"]).astype(BF16)
                st = st * ops[d, g]["dec"] + jnp.where(blockdiag, kv[d, g], 0.0)
            new_states.append(st)
        for u in units:
            o = ops[u]
            a = jnp.where(masks[u[0]], a2[u], 0.0).astype(BF16)
            intra = _select_heads(_dot(a, o["v"]))
            outs[u[0]][pl.ds(o["r0"], CHUNK), :] = intra + _dot_nt(o["qe"], st_in[u])
        return tuple(new_states)

    zero = jnp.zeros((LANES, LANES), F32)
    lax.fori_loop(0, nc // G, step, (zero, zero))

    def post(t, _):
        r0 = pl.multiple_of(t * R, R)
        o = _head_rms(of_s[pl.ds(r0, R), :] + ob_s[pl.ds(r0, R), :], gain_ref[...])
        g = g_ref[0, pl.ds(r0, R), :]
        o_ref[0, pl.ds(r0, R), :] = (o * (g * _sigmoid(g))).astype(BF16)
        return 0

    lax.fori_loop(0, T // R, post, 0)


def _gla(gla, small, lp):
    B, T, _ = gla.shape
    pairs = GLA_WIDTH // LANES
    col = lambda off: (lambda b, hp: (b, 0, off * pairs + hp))
    return pl.pallas_call(
        _gla_kernel,
        grid=(B, pairs),
        in_specs=[
            pl.BlockSpec((1, T, LANES), col(0)),
            pl.BlockSpec((1, T, LANES), col(1)),
            pl.BlockSpec((1, T, LANES), col(2)),
            pl.BlockSpec((1, T, LANES), col(3)),
            pl.BlockSpec((1, T, LANES), lambda b, hp: (b, 0, 0)),
            pl.BlockSpec((2, LANES, LANES), lambda b, hp: (0, 0, hp)),
            pl.BlockSpec((2, LANES, LANES), lambda b, hp: (0, 0, hp)),
            pl.BlockSpec((2, 1, LANES), lambda b, hp: (0, 0, hp)),
            pl.BlockSpec((1, LANES), lambda b, hp: (0, 0)),
        ],
        out_specs=pl.BlockSpec((1, T, LANES), lambda b, hp: (b, 0, hp)),
        out_shape=jax.ShapeDtypeStruct((B, T, GLA_WIDTH), BF16),
        scratch_shapes=[
            pltpu.VMEM((2, T, LANES), BF16), pltpu.VMEM((2, T, LANES), BF16),
            pltpu.VMEM((2, 3, (T // CHUNK) * SUBLANES, LANES), F32),
            pltpu.VMEM((T, LANES), F32), pltpu.VMEM((T, LANES), F32),
        ],
        compiler_params=_params("arbitrary", "arbitrary"),
        name="gla",
    )(gla, gla, gla, gla, small, lp["wd_hi"], lp["wd_lo"], lp["bd"], lp["gla_gain"])


GATE_I = SMALL_GATE_LANE
GATE_F = SMALL_GATE_LANE + MLSTM_HEADS


def _gate_lane(d, hh):
    return GATE_F + 2 * MLSTM_HEADS * d + hh


MLSTM_CHUNKS_PER_STEP = 8


def _chunk_rows8(a, row):
    n = a.shape[0] // CHUNK
    a3 = a.reshape(n, CHUNK, LANES)
    full = jnp.broadcast_to(a3[:, row:row + 1, :], a3.shape)
    return full[:, :SUBLANES, :].reshape(n * SUBLANES, LANES)


def _expand_rows8(a8):
    n, w = a8.shape[0] // SUBLANES, a8.shape[1]
    a3 = a8.reshape(n, SUBLANES, w)[:, 0:1, :]
    return jnp.broadcast_to(a3, (n, CHUNK, w)).reshape(n * CHUNK, w)


def _gate_select():
    src = lax.broadcasted_iota(jnp.int32, (LANES, 2 * LANES), 0)
    dst = lax.broadcasted_iota(jnp.int32, (LANES, 2 * LANES), 1)
    want = GATE_F + 2 * MLSTM_HEADS * (dst // LANES) + (dst % LANES) // HEAD_DIM
    return jnp.where(src == want, 1.0, 0.0).astype(BF16)


def _gate_broadcast(x, sel, gate_mask, pieces=2):
    x = jnp.where(gate_mask, x, 0.0)
    if pieces == 1:
        return _dot(x.astype(BF16), sel)
    hi, lo = _split(x)
    return _dot(hi, sel) + _dot(lo, sel)


def _mlstm_kernel(q_ref, k_ref, v_ref, og_ref, small_ref, wq_ref, wk_ref, bq_ref, bk_ref,
                  gbias_ref, gain_ref, o_ref, q_s, k_s, b_s, r_s, cm_s, fl1_s, bl_s, rl_s, mf_s, mb_s,
                  wp_s, qi_s, kw_s, rt_s, st_s, of_s, ob_s):
    T = q_ref.shape[1]
    nc = T // CHUNK
    R = min(PREP_ROWS, T)
    hp = pl.program_id(1)
    G = MLSTM_CHUNKS_PER_STEP
    cpt = R // CHUNK

    row = lax.broadcasted_iota(jnp.int32, (T, LANES), 0)
    for src, w_ref, b_ref, dst, scale in ((q_ref, wq_ref, bq_ref, q_s, 1.0),
                                          (k_ref, wk_ref, bk_ref, k_s, HEAD_DIM ** -0.5)):
        xc = src[0]
        prev = jnp.where(row >= 1, pltpu.roll(xc, 1, 0), 0.0)
        nxt = jnp.where(row < T - 1, pltpu.roll(xc, T - 1, 0), 0.0)
        y = prev * w_ref[0:1, :] + xc * w_ref[1:2, :] + nxt * w_ref[2:3, :] + b_ref[...]
        dst[...] = (y * _sigmoid(y) * scale).astype(BF16)

    pos = _chunk_pos(R)
    lane = lax.broadcasted_iota(jnp.int32, (R, LANES), 1)
    is_bwd = lane >= GATE_I + 2 * MLSTM_HEADS
    is_bwd8 = lax.broadcasted_iota(jnp.int32, (cpt * SUBLANES, LANES), 1) >= GATE_I + 2 * MLSTM_HEADS
    heads_per_pair = LANES // HEAD_DIM
    shift = (LANES - heads_per_pair * hp) % LANES
    gate_lanes = [_gate_lane(d, hh) for d in range(2) for hh in range(2)]
    gate_mask = functools.reduce(jnp.logical_or, [lane == l for l in gate_lanes])
    lane8 = lax.broadcasted_iota(jnp.int32, (cpt * SUBLANES, LANES), 1)
    gate_mask8 = functools.reduce(jnp.logical_or, [lane8 == l for l in gate_lanes])

    def prep(t, _):
        r0 = pl.multiple_of(t * R, R)
        gc = pltpu.roll(small_ref[0, pl.ds(r0, R), :] + gbias_ref[...], shift, 1)
        logf = _log_sigmoid(gc)
        pre = _chunk_scan(logf, pos, jnp.add, 0.0, False)
        b = jnp.where(is_bwd, _chunk_row(pre, 0, CHUNK - 1, CHUNK - 1) - pre + logf, pre)
        r = pltpu.roll(gc, MLSTM_HEADS, 1) - b
        cm = jnp.where(is_bwd, _chunk_scan(r, pos, jnp.maximum, NEG, True),
                       _chunk_scan(r, pos, jnp.maximum, NEG, False))
        b_s[pl.ds(r0, R), :] = b
        r_s[pl.ds(r0, R), :] = r
        cm_s[pl.ds(r0, R), :] = cm
        rt_s[:, pl.ds(r0, R)] = r.T
        c8 = pl.ds(pl.multiple_of(t * cpt * SUBLANES, SUBLANES), cpt * SUBLANES)
        bl_s[c8, :] = jnp.where(is_bwd8, _chunk_rows8(b, 0), _chunk_rows8(b, CHUNK - 1))
        rl_s[c8, :] = jnp.where(is_bwd8, _chunk_rows8(cm, 0), _chunk_rows8(cm, CHUNK - 1))
        return 0

    lax.fori_loop(0, T // R, prep, 0)

    def m_chain(n, carry):
        mf, mb = carry
        rf = pl.ds(pl.multiple_of(n * SUBLANES, SUBLANES), SUBLANES)
        rb = pl.ds(pl.multiple_of((nc - 1 - n) * SUBLANES, SUBLANES), SUBLANES)
        mf_s[rf, :] = mf
        mb_s[rb, :] = mb
        return (bl_s[rf, :] + jnp.maximum(mf, rl_s[rf, :]), bl_s[rb, :] + jnp.maximum(mb, rl_s[rb, :]))

    m0 = jnp.full((SUBLANES, LANES), NEG, F32)
    lax.fori_loop(0, nc, m_chain, (m0, m0))

    expo_s, floor_s = (b_s, cm_s), (r_s, fl1_s)
    sel = _gate_select()

    def weights(t, _):
        r0 = pl.multiple_of(t * R, R)
        c8 = pl.ds(pl.multiple_of(t * cpt * SUBLANES, SUBLANES), cpt * SUBLANES)
        rows = pl.ds(r0, R)
        m_in8 = jnp.where(is_bwd8, mb_s[c8, :], mf_s[c8, :])
        bl8 = bl_s[c8, :]
        m_out8 = bl8 + jnp.maximum(m_in8, rl_s[c8, :])
        m_in = _expand_rows8(m_in8)
        mx = jnp.maximum(m_in, cm_s[rows, :])
        wp_b = jnp.exp(_gate_broadcast(bl8 + m_in8 - m_out8, sel, gate_mask8))
        mx_b = _gate_broadcast(mx, sel, gate_mask)
        floor = jnp.exp(-_gate_broadcast(b_s[rows, :] + mx, sel, gate_mask))
        w_inter = _gate_broadcast(jnp.exp(m_in - mx), sel, gate_mask, pieces=1)
        wk = _gate_broadcast(jnp.exp(_expand_rows8(bl8 - m_out8) + r_s[rows, :]), sel, gate_mask, pieces=1)
        q = q_s[rows, :].astype(F32)
        k = k_s[rows, :].astype(F32)
        for d in range(2):
            half = slice(d * LANES, (d + 1) * LANES)
            qi_s[d, rows, :] = (q * w_inter[:, half]).astype(BF16)
            kw_s[d, rows, :] = (k * wk[:, half]).astype(BF16)
            wp_s[d, c8, :] = wp_b[:, half]
            expo_s[d][rows, :] = -mx_b[:, half]
            floor_s[d][rows, :] = floor[:, half]
        return 0

    lax.fori_loop(0, T // R, weights, 0)

    outs = (of_s, ob_s)
    ci = lax.broadcasted_iota(jnp.int32, (CHUNK, LANES), 0)
    si = lax.broadcasted_iota(jnp.int32, (CHUNK, LANES), 1) % CHUNK
    masks = (ci >= si, ci <= si)
    br = lax.broadcasted_iota(jnp.int32, (LANES, 2 * LANES), 0) // HEAD_DIM
    bc = (lax.broadcasted_iota(jnp.int32, (LANES, 2 * LANES), 1) % LANES) // HEAD_DIM
    blockdiag = br == bc
    ones = jnp.ones((CHUNK, LANES), BF16)
    st_s[...] = jnp.zeros(st_s.shape, F32)
    gate_rows = slice(GATE_I, GATE_I + 4 * MLSTM_HEADS)

    def step(it, _):
        units = [(d, g) for d in range(2) for g in range(G)]
        ops = {}
        for d in range(2):
            first = (nc - (it + 1) * G) if d else it * G
            rt = rt_s[gate_rows, pl.ds(pl.multiple_of(first * CHUNK, G * CHUNK), G * CHUNK)]
            for g in range(G):
                local = (G - 1 - g) if d else g
                c = first + local
                rows = pl.ds(pl.multiple_of(c * CHUNK, CHUNK), CHUNK)
                c8 = pl.ds(pl.multiple_of(c * SUBLANES, SUBLANES), SUBLANES)
                r_row = jnp.concatenate(
                    [rt[_gate_lane(d, hh) - GATE_I:_gate_lane(d, hh) - GATE_I + 1,
                        local * CHUNK:(local + 1) * CHUNK] for hh in range(2)], axis=1)
                wp = wp_s[d, c8, :][0:1, :]
                ops[d, g] = dict(
                    rows=rows, q=q_s[rows, :], k=k_s[rows, :], qi=qi_s[d, rows, :], kw=kw_s[d, rows, :],
                    va=jnp.concatenate([v_ref[0, rows, :].astype(BF16), ones], axis=1),
                    expo=expo_s[d][rows, :] + r_row, floor=floor_s[d][rows, :],
                    wp=jnp.concatenate([wp, wp], axis=1))
        qk, kv = {}, {}
        for u in units:
            o = ops[u]
            qk[u] = _dot_nt(o["q"], _stack_heads(o["k"]))
            kv[u] = _dot_tn(o["kw"], o["va"])
        st_in = {}
        for d in range(2):
            st = st_s[d]
            for g in range(G):
                st_in[d, g] = st.astype(BF16)
                st = st * ops[d, g]["wp"] + jnp.where(blockdiag, kv[d, g], 0.0)
            st_s[d] = st
        for u in units:
            d = u[0]
            o = ops[u]
            smat = (qk[u] * jnp.where(masks[d], jnp.exp(o["expo"]), 0.0)).astype(BF16)
            va_bd = jnp.where(blockdiag, jnp.concatenate([o["va"], o["va"]], axis=0), jnp.zeros((), BF16))
            num = _dot(jnp.concatenate([o["qi"], smat], axis=1), jnp.concatenate([st_in[u], va_bd], axis=0))
            den = jnp.maximum(jnp.abs(num[:, LANES:]), o["floor"])
            outs[d][o["rows"], :] = num[:, :LANES] / den
        return 0

    lax.fori_loop(0, nc // G, step, 0)

    def post(t, _):
        r0 = pl.multiple_of(t * R, R)
        h = _head_rms(of_s[pl.ds(r0, R), :] + ob_s[pl.ds(r0, R), :], gain_ref[...])
        o_ref[0, pl.ds(r0, R), :] = (h * _sigmoid(og_ref[0, pl.ds(r0, R), :])).astype(BF16)
        return 0

    lax.fori_loop(0, T // R, post, 0)


def _mlstm(ml, small, lp):
    B, T, _ = ml.shape
    pairs = MLSTM_WIDTH // LANES
    col = lambda off: (lambda b, hp: (b, 0, off * pairs + hp))
    const = lambda b, hp: (0, 0)
    return pl.pallas_call(
        _mlstm_kernel,
        grid=(B, pairs),
        in_specs=[
            pl.BlockSpec((1, T, LANES), col(0)),
            pl.BlockSpec((1, T, LANES), col(1)),
            pl.BlockSpec((1, T, LANES), col(2)),
            pl.BlockSpec((1, T, LANES), col(3)),
            pl.BlockSpec((1, T, LANES), lambda b, hp: (b, 0, 0)),
            pl.BlockSpec((3, LANES), lambda b, hp: (0, hp)),
            pl.BlockSpec((3, LANES), lambda b, hp: (0, pairs + hp)),
            pl.BlockSpec((1, LANES), lambda b, hp: (0, hp)),
            pl.BlockSpec((1, LANES), lambda b, hp: (0, pairs + hp)),
            pl.BlockSpec((1, LANES), const),
            pl.BlockSpec((1, LANES), const),
        ],
        out_specs=pl.BlockSpec((1, T, LANES), lambda b, hp: (b, 0, hp)),
        out_shape=jax.ShapeDtypeStruct((B, T, MLSTM_WIDTH), BF16),
        scratch_shapes=(
            [pltpu.VMEM((T, LANES), BF16)] * 2
            + [pltpu.VMEM((T, LANES), F32)] * 4
            + [pltpu.VMEM(((T // CHUNK) * SUBLANES, LANES), F32)] * 4
            + [pltpu.VMEM((2, (T // CHUNK) * SUBLANES, LANES), F32)]
            + [pltpu.VMEM((2, T, LANES), BF16)] * 2
            + [pltpu.VMEM((LANES, T), F32),
               pltpu.VMEM((2, LANES, 2 * LANES), F32),
               pltpu.VMEM((T, LANES), F32), pltpu.VMEM((T, LANES), F32)]),
        compiler_params=_params("arbitrary", "arbitrary"),
        name="mlstm",
    )(ml, ml, ml, ml, small, lp["conv_w"], lp["conv_w"], lp["conv_b"], lp["conv_b"],
      lp["gate_bias"], lp["ml_gain"])


def _first_argmax(vals, row):
    mx = jnp.max(vals, axis=0, keepdims=True)
    idx = jnp.min(jnp.where(vals == mx, row, vals.shape[0]), axis=0, keepdims=True)
    return mx, idx


def _out_route_kernel(x_ref, a_ref, gl_ref, ml_ref, w_ref, g_ref, wr_ref, br_ref,
                      x1_ref, t_ref, route_ref, routet_ref):
    tm = x_ref.shape[0]
    sub = min(tm, ROUTE_SUBTILE)
    spans = [slice(r, r + sub) for r in range(0, tm, sub)]
    x1s, logits = [], []
    for rs in spans:
        x1 = (x_ref[rs, :]
              + _dot(a_ref[rs, :], w_ref[0:ATTN_WIDTH, :])
              + _dot(gl_ref[rs, :], w_ref[ATTN_WIDTH:ATTN_WIDTH + GLA_WIDTH, :])
              + _dot(ml_ref[rs, :], w_ref[ATTN_WIDTH + GLA_WIDTH:, :]))
        x1_ref[rs, :] = x1
        x1s.append(x1)
    for rs, x1 in zip(spans, x1s):
        t_hi, t_lo = _split(_rms(x1, g_ref[...]))
        t_ref[rs, :] = t_hi
        both = _dot(t_hi, wr_ref[...])
        logits.append((both[:, :LANES] + both[:, LANES:] + _dot(t_lo, wr_ref[:, :LANES])) + br_ref[...])
    grow = lax.broadcasted_iota(jnp.int32, (SUBLANES, sub), 0)
    erow = lax.broadcasted_iota(jnp.int32, (N_EXPERTS, sub), 0)
    for rs, lg in zip(spans, logits):
        lt = lg.T
        gl = jnp.where(grow < N_GROUPS, lt[0:SUBLANES, :], -jnp.inf)
        gmax, gi = _first_argmax(gl, grow)
        g_prob = 1.0 / jnp.sum(jnp.exp(gl - gmax), axis=0, keepdims=True)
        el = jnp.where(erow // EXPERTS_PER_GROUP == gi, lt[ROUTE_W_LANE:ROUTE_W_LANE + N_EXPERTS, :], -jnp.inf)
        v1, i1 = _first_argmax(el, erow)
        v2, i2 = _first_argmax(jnp.where(erow == i1, -jnp.inf, el), erow)
        e2 = jnp.exp(v2 - v1)
        w1 = g_prob / (1.0 + e2)
        w2 = g_prob * e2 / (1.0 + e2)
        comb = jnp.where(erow == i1, w1, jnp.where(erow == i2, w2, 0.0))
        head = jnp.where(grow == 0, gi.astype(F32), 0.0)
        route_t = jnp.concatenate(
            [head, comb, jnp.zeros((LANES - SUBLANES - N_EXPERTS, sub), F32)], axis=0)
        route_ref[rs, :] = route_t.T
        routet_ref[:, rs] = head


def _dispatch_kernel(t_ref, route_ref, routet_ref, xs_ref, cws_ref, pos_ref, cnt_ref):
    nb = ROUTE_BLOCK
    gi_row = routet_ref[0:1, :]
    sub = lax.broadcasted_iota(jnp.int32, (SUBLANES, nb), 0).astype(F32)
    onehot = (sub == gi_row)
    ri = lax.broadcasted_iota(jnp.int32, (nb, nb), 0)
    cj = lax.broadcasted_iota(jnp.int32, (nb, nb), 1)
    before = (ri < cj).astype(BF16)
    rank = _dot(onehot.astype(BF16), before)
    counts = jnp.broadcast_to(jnp.sum(onehot.astype(F32), axis=-1, keepdims=True), (SUBLANES, LANES))
    padded = jnp.ceil(counts * (1.0 / ROUTE_TILE)) * ROUTE_TILE
    srow = lax.broadcasted_iota(jnp.int32, (SUBLANES, LANES), 0)
    incl = padded
    s = 1
    while s < SUBLANES:
        incl = incl + jnp.where(srow >= s, pltpu.roll(incl, s, 0), 0.0)
        s *= 2
    start = incl - padded
    pos_row = jnp.sum(jnp.where(onehot, start[:, 0:1] + rank, 0.0), axis=0, keepdims=True)
    cnt_ref[0] = counts.astype(jnp.int32)

    route = route_ref[...]
    lane = lax.broadcasted_iota(jnp.int32, (nb, LANES), 1)
    onehot_c = lane.astype(F32) == route[:, 0:1]
    after = (cj < ri).astype(BF16)
    rank_c = _dot(after, onehot_c.astype(BF16))
    slane = lax.broadcasted_iota(jnp.int32, (SUBLANES, LANES), 1)
    start_c = jnp.sum(jnp.where(srow == slane, start, 0.0), axis=0, keepdims=True)
    pos_col = jnp.sum(jnp.where(onehot_c, start_c + rank_c, 0.0), axis=-1, keepdims=True)
    pos_ref[...] = jnp.broadcast_to(pos_col, (nb, LANES))

    comb = jnp.where(lane >= ROUTE_W_LANE, route, 0.0)
    c_hi, c_lo = _split(comb)
    c_lo2 = (comb - c_hi.astype(F32) - c_lo.astype(F32)).astype(BF16)
    cw = (c_hi.astype(F32) + pltpu.roll(c_lo.astype(F32), ROUTE_LO_SHIFT, 1)
          + pltpu.roll(c_lo2.astype(F32), 2 * ROUTE_LO_SHIFT, 1)).astype(BF16)
    tb = t_ref[...]
    for r in range(TILES_PER_BLOCK):
        rows = (lax.broadcasted_iota(jnp.int32, (ROUTE_TILE, nb), 0) + r * ROUTE_TILE).astype(F32)
        perm = (rows == pos_row).astype(BF16)
        xs_ref[r * ROUTE_TILE:(r + 1) * ROUTE_TILE, :] = _dot(perm, tb).astype(BF16)
        cws_ref[r * ROUTE_TILE:(r + 1) * ROUTE_TILE, :] = _dot(perm, cw)


def _route_dispatch_kernel(x_ref, a_ref, gl_ref, ml_ref, w_ref, g_ref, wr_ref, br_ref,
                           x1_ref, xs_ref, cws_ref, pos_ref, cnt_ref, t_s, route_s, routet_s):
    _out_route_kernel(x_ref, a_ref, gl_ref, ml_ref, w_ref, g_ref, wr_ref, br_ref,
                      x1_ref, t_s, route_s, routet_s)
    _dispatch_kernel(t_s, route_s, routet_s, xs_ref, cws_ref, pos_ref, cnt_ref)


def _route_dispatch(x2, attn, gla_o, ml_o, lp):
    N, D = x2.shape
    nblk = N // ROUTE_BLOCK
    const = lambda i: (0, 0)
    tok = lambda i: (i, 0)
    return pl.pallas_call(
        _route_dispatch_kernel,
        grid=(nblk,),
        in_specs=[
            pl.BlockSpec((ROUTE_BLOCK, D), tok),
            pl.BlockSpec((ROUTE_BLOCK, ATTN_WIDTH), tok),
            pl.BlockSpec((ROUTE_BLOCK, GLA_WIDTH), tok),
            pl.BlockSpec((ROUTE_BLOCK, MLSTM_WIDTH), tok),
            pl.BlockSpec((ATTN_WIDTH + GLA_WIDTH + MLSTM_WIDTH, D), const),
            pl.BlockSpec((1, D), const),
            pl.BlockSpec((D, 2 * LANES), const),
            pl.BlockSpec((1, LANES), const),
        ],
        out_specs=[
            pl.BlockSpec((ROUTE_BLOCK, D), tok),
            pl.BlockSpec((ROUTE_ROWS, D), tok),
            pl.BlockSpec((ROUTE_ROWS, LANES), tok),
            pl.BlockSpec((ROUTE_BLOCK, LANES), tok),
            pl.BlockSpec((1, SUBLANES, LANES), lambda i: (i, 0, 0)),
        ],
        out_shape=[
            jax.ShapeDtypeStruct((N, D), F32),
            jax.ShapeDtypeStruct((nblk * ROUTE_ROWS, D), BF16),
            jax.ShapeDtypeStruct((nblk * ROUTE_ROWS, LANES), F32),
            jax.ShapeDtypeStruct((N, LANES), F32),
            jax.ShapeDtypeStruct((nblk, SUBLANES, LANES), jnp.int32),
        ],
        scratch_shapes=[
            pltpu.VMEM((ROUTE_BLOCK, D), BF16),
            pltpu.VMEM((ROUTE_BLOCK, LANES), F32),
            pltpu.VMEM((SUBLANES, ROUTE_BLOCK), F32),
        ],
        compiler_params=_params("arbitrary"),
        name="route_dispatch",
    )(x2, attn, gla_o, ml_o, lp["w_out"], lp["g_ffn"], lp["w_route"], lp["b_route"])


def _tile_schedule(cnt):
    nblk = cnt.shape[0]
    ntile = (cnt + ROUTE_TILE - 1) // ROUTE_TILE
    end = jnp.cumsum(ntile, axis=1)
    r = jnp.arange(TILES_PER_BLOCK, dtype=jnp.int32)
    grp = jnp.sum(r[None, :, None] >= end[:, None, :], axis=-1)
    grp = grp.reshape(-1).astype(jnp.int32)
    tile = jnp.arange(nblk * TILES_PER_BLOCK, dtype=jnp.int32)
    order = jnp.argsort(grp * (nblk * TILES_PER_BLOCK) + tile).astype(jnp.int32)
    n_active = jnp.sum(grp < N_GROUPS).astype(jnp.int32)
    g_sorted = grp[order]
    last_group = g_sorted[jnp.maximum(n_active - 1, 0)]
    g_sorted = jnp.where(g_sorted < N_GROUPS, g_sorted, last_group)
    return order, g_sorted, n_active[None]


def _expert_kernel(trow_ref, tgrp_ref, nact_ref, xs_ref, cws_ref, wg_ref, wu_ref, wd_ref, ys_ref,
                   wg_s, wu_s, wd_s):
    i = pl.program_id(0)

    @pl.when((i == 0) | (tgrp_ref[i] != tgrp_ref[jnp.maximum(i - 1, 0)]))
    def _():
        for j in range(EXPERTS_PER_GROUP):
            wg_s[j] = wg_ref[j].astype(BF16)
            wu_s[j] = wu_ref[j].astype(BF16)
            wd_s[j] = wd_ref[j].astype(BF16)

    @pl.when(i < nact_ref[0])
    def _():
        x = xs_ref[...]
        cws = cws_ref[...]
        lane = lax.broadcasted_iota(jnp.int32, cws.shape, 1)
        first = ROUTE_W_LANE + tgrp_ref[i] * EXPERTS_PER_GROUP
        y = jnp.zeros(ys_ref.shape, F32)

        def hidden(j):
            return _dot(x, wg_s[j]), _dot(x, wu_s[j])

        h = hidden(0)
        for j in range(EXPERTS_PER_GROUP):
            h_next = hidden(j + 1) if j + 1 < EXPERTS_PER_GROUP else None
            off = lane - (first + j)
            sel = (off == 0) | (off == ROUTE_LO_SHIFT) | (off == 2 * ROUTE_LO_SHIFT)
            wj = jnp.sum(jnp.where(sel, cws, 0.0), axis=-1, keepdims=True)
            a = (h[0] * _sigmoid(h[0]) * h[1]).astype(BF16)
            y = y + wj * _dot(a, wd_s[j])
            h = h_next
        ys_ref[...] = y.astype(BF16)

    @pl.when(i >= nact_ref[0])
    def _():
        ys_ref[...] = jnp.zeros(ys_ref.shape, BF16)


def _experts(xs, cws, order, grp, n_active, lp):
    rows, D = xs.shape
    n_tiles = rows // ROUTE_TILE
    tile = lambda i, trow, tgrp, nact: (trow[i], 0)
    layer_groups = lp["layer"] * N_GROUPS
    wsel = lambda i, trow, tgrp, nact: (layer_groups + tgrp[i], 0, 0)
    once = pl.Buffered(1)
    return pl.pallas_call(
        _expert_kernel,
        grid_spec=pltpu.PrefetchScalarGridSpec(
            num_scalar_prefetch=3,
            grid=(n_tiles,),
            in_specs=[
                pl.BlockSpec((ROUTE_TILE, D), tile),
                pl.BlockSpec((ROUTE_TILE, LANES), tile),
                pl.BlockSpec((EXPERTS_PER_GROUP, D, D_EXPERT), wsel, pipeline_mode=once),
                pl.BlockSpec((EXPERTS_PER_GROUP, D, D_EXPERT), wsel, pipeline_mode=once),
                pl.BlockSpec((EXPERTS_PER_GROUP, D_EXPERT, D), wsel, pipeline_mode=once),
            ],
            out_specs=pl.BlockSpec((ROUTE_TILE, D), tile),
            scratch_shapes=[
                pltpu.VMEM((EXPERTS_PER_GROUP, D, D_EXPERT), BF16),
                pltpu.VMEM((EXPERTS_PER_GROUP, D, D_EXPERT), BF16),
                pltpu.VMEM((EXPERTS_PER_GROUP, D_EXPERT, D), BF16),
            ],
        ),
        out_shape=jax.ShapeDtypeStruct((rows, D), BF16),
        compiler_params=_params("arbitrary"),
        name="experts",
    )(order, grp, n_active, xs, cws, lp["w_gate"], lp["w_up"], lp["w_down"])


def _combine_kernel(ys_ref, pos_ref, x1_ref, p_ref, g_ref, wpg_ref, wpp_ref, gfin_ref, o_ref,
                    *, embed, final):
    tm = x1_ref.shape[0]
    pos = pos_ref[...]
    lane = lax.broadcasted_iota(jnp.int32, (tm, LANES), 1).astype(F32)
    perm_t = jnp.concatenate(
        [(lane + r * LANES == pos).astype(BF16) for r in range(ROUTE_ROWS // LANES)], axis=1)
    x = x1_ref[...] + _dot(perm_t, ys_ref[...])
    if embed:
        gate = _sigmoid(_dot(_rms(x, g_ref[...]).astype(BF16), wpg_ref[...]))
        x = x + gate * _dot(p_ref[...].astype(BF16), wpp_ref[...])
    if final:
        x = _rms(x, gfin_ref[...])
    o_ref[...] = x


def _combine(ys, pos, x1, p2, lp, g_final, embed=True, final=False, tm=512):
    N, D = x1.shape
    inner = ROUTE_BLOCK // tm
    tok = lambda b, i: (b * inner + i, 0)
    const = lambda b, i: (0, 0)
    return pl.pallas_call(
        functools.partial(_combine_kernel, embed=embed, final=final),
        grid=(N // ROUTE_BLOCK, inner),
        in_specs=[
            pl.BlockSpec((ROUTE_ROWS, D), lambda b, i: (b, 0)),
            pl.BlockSpec((tm, LANES), tok),
            pl.BlockSpec((tm, D), tok),
            pl.BlockSpec((tm, p2.shape[1]), tok),
            pl.BlockSpec((1, D), const),
            pl.BlockSpec((D, D), const),
            pl.BlockSpec((p2.shape[1], D), const),
            pl.BlockSpec((1, D), const),
        ],
        out_specs=pl.BlockSpec((tm, D), tok),
        out_shape=jax.ShapeDtypeStruct((N, D), F32),
        compiler_params=_params("arbitrary", "arbitrary"),
        name="combine",
    )(ys, pos, x1, p2, lp["g_ple"], lp["w_pg"], lp["w_pp"], g_final)


def _mix_out_moe(x2, attn, gla_o, ml_o, lp):
    x1, xs, cws, pos, cnt = _route_dispatch(x2, attn, gla_o, ml_o, lp)
    order, grp, n_active = _tile_schedule(cnt[:, :N_GROUPS, 0])
    return x1, _experts(xs, cws, order, grp, n_active, lp), pos


def _moe_debug(x2, lp):
    N, D = x2.shape
    zeros = lambda w: jnp.zeros((N, w), BF16)
    x1, ys, pos = _mix_out_moe(x2, zeros(ATTN_WIDTH), zeros(GLA_WIDTH), zeros(MLSTM_WIDTH), lp)
    p2 = jnp.zeros((N, lp["w_pp"].shape[0]), F32)
    return _combine(ys, pos, x1, p2, lp, lp["g_ple"], embed=False) - x1


def _permute_in_cols(w):
    glr0 = ATTN_WIDTH + 2 * KV_WIDTH + 4 * GLA_WIDTH
    ml0 = glr0 + 2 * GLA_RANK
    mg0 = ml0 + 4 * MLSTM_WIDTH
    end = mg0 + 4 * MLSTM_HEADS
    assert end == w.shape[1]
    pad = jnp.zeros((w.shape[0], IN_PERM_WIDTH - end), w.dtype)
    return jnp.concatenate([w[:, :glr0], w[:, ml0:mg0], w[:, glr0:ml0], w[:, mg0:end], pad], axis=1)


def _rope_tables(T):
    t = jnp.arange(T, dtype=F32)
    row = jnp.floor(t / GRID_W)
    col = t - row * GRID_W
    inv = ROPE_THETA ** (-jnp.arange(0, ROPE_AXIS_DIM, 2, dtype=F32) / ROPE_AXIS_DIM)
    ang_r = row[:, None] * inv[None, :]
    ang_c = col[:, None] * inv[None, :]
    cos_h = jnp.concatenate([jnp.cos(ang_r), jnp.cos(ang_r), jnp.cos(ang_c), jnp.cos(ang_c)], axis=1)
    sin_h = jnp.concatenate([-jnp.sin(ang_r), jnp.sin(ang_r), -jnp.sin(ang_c), jnp.sin(ang_c)], axis=1)
    reps = LANES // HEAD_DIM
    cos, sin = jnp.tile(cos_h, (1, reps)), jnp.tile(sin_h, (1, reps))
    eh = np.zeros((LANES, LANES), np.float32)
    eh[np.arange(LANES), np.arange(LANES) // HEAD_DIM] = 1.0
    return dict(cos=cos, sin=sin, cos_t=cos.T, sin_t=sin.T,
                eh=jnp.asarray(eh, BF16), eht=jnp.asarray(eh.T, BF16))


def kernel(x, p, norm_mix_g, w_in, attn_q_norm_g, attn_k_norm_g, gla_w_decay, gla_b_decay,
           gla_out_norm_g, mlstm_conv_w, mlstm_conv_b, mlstm_b_input, mlstm_b_forget,
           mlstm_out_norm_g, w_out, norm_ffn_g, w_group, b_group, w_router, b_router,
           w_expert_gate, w_expert_up, w_expert_down, norm_ple_g, w_ple_gate, w_ple_proj,
           final_norm_g):
    params = dict(
        norm_mix_g=norm_mix_g, w_in=w_in, attn_q_norm_g=attn_q_norm_g, attn_k_norm_g=attn_k_norm_g,
        gla_w_decay=gla_w_decay, gla_b_decay=gla_b_decay, gla_out_norm_g=gla_out_norm_g,
        mlstm_conv_w=mlstm_conv_w, mlstm_conv_b=mlstm_conv_b, mlstm_b_input=mlstm_b_input,
        mlstm_b_forget=mlstm_b_forget, mlstm_out_norm_g=mlstm_out_norm_g, w_out=w_out,
        norm_ffn_g=norm_ffn_g, w_group=w_group, b_group=b_group, w_router=w_router,
        b_router=b_router, w_expert_gate=w_expert_gate, w_expert_up=w_expert_up,
        w_expert_down=w_expert_down, norm_ple_g=norm_ple_g, w_ple_gate=w_ple_gate,
        w_ple_proj=w_ple_proj)
    B, T, D = x.shape
    rope = _rope_tables(T)
    N = B * T
    depth = w_in.shape[0]
    x2 = x.reshape(N, D)
    g_final = final_norm_g[None, :]
    for i in range(depth):
        lp = _layer_params(params, i)
        qt, k, vt, gla, ml, small = _in_proj(x2, lp, rope, B, T)
        attn = _attention(lp["attn_safe"], qt, k, vt).reshape(N, ATTN_WIDTH)
        gla_o = _gla(gla, small, lp).reshape(N, GLA_WIDTH)
        ml_o = _mlstm(ml, small, lp).reshape(N, MLSTM_WIDTH)
        x1, ys, pos = _mix_out_moe(x2, attn, gla_o, ml_o, lp)
        x2 = _combine(ys, pos, x1, p[i].reshape(N, -1), lp, g_final, final=(i == depth - 1))
    return x2.reshape(B, T, D)


def _split_w(w):
    hi = w.astype(BF16)
    return hi, (w - hi.astype(F32)).astype(BF16)


def _stacked_experts(w):
    return w.reshape((w.shape[0] * w.shape[1],) + w.shape[2:])


def _layer_params(p, i):
    D = p["w_in"].shape[1]
    w_in = _permute_in_cols(p["w_in"][i])
    gq, gk = p["attn_q_norm_g"][i], p["attn_k_norm_g"][i]
    q_gain = jnp.tile(gq, LANES // HEAD_DIM) * (HEAD_DIM ** -0.5 * LOG2E)
    logit_bound = HEAD_DIM ** 0.5 * jnp.max(jnp.abs(gq)) * jnp.max(jnp.abs(gk))
    attn_safe = (logit_bound <= ATTN_SAFE_LOGIT).astype(jnp.int32)[None]
    wd = jnp.zeros((2, LANES, GLA_WIDTH), F32)
    wd = wd.at[0, :GLA_RANK].set(p["gla_w_decay"][i, 0]).at[1, GLA_RANK:2 * GLA_RANK].set(p["gla_w_decay"][i, 1])
    wd_hi, wd_lo = _split_w(wd)
    gate_bias = jnp.zeros((LANES,), F32).at[SMALL_GATE_LANE:SMALL_GATE_LANE + 4 * MLSTM_HEADS].set(
        jnp.concatenate([p["mlstm_b_input"][i, 0], p["mlstm_b_forget"][i, 0],
                         p["mlstm_b_input"][i, 1], p["mlstm_b_forget"][i, 1]]))
    w_route = jnp.zeros((D, LANES), F32)
    w_route = w_route.at[:, :N_GROUPS].set(p["w_group"][i])
    w_route = w_route.at[:, ROUTE_W_LANE:ROUTE_W_LANE + N_EXPERTS].set(p["w_router"][i])
    wr_hi, wr_lo = _split_w(w_route)
    b_route = jnp.zeros((LANES,), F32).at[:N_GROUPS].set(p["b_group"][i])
    b_route = b_route.at[ROUTE_W_LANE:ROUTE_W_LANE + N_EXPERTS].set(p["b_router"][i])
    return dict(
        g_mix=p["norm_mix_g"][i][None, :],
        w_in=w_in.astype(BF16),
        q_gain_t=jnp.broadcast_to(q_gain[:, None], (LANES, LANES)),
        k_gain=jnp.tile(gk, LANES // HEAD_DIM)[None, :],
        attn_safe=attn_safe,
        wd_hi=wd_hi, wd_lo=wd_lo,
        bd=p["gla_b_decay"][i][:, None, :],
        gla_gain=jnp.tile(p["gla_out_norm_g"][i], LANES // HEAD_DIM)[None, :],
        conv_w=p["mlstm_conv_w"][i],
        conv_b=p["mlstm_conv_b"][i][None, :],
        gate_bias=gate_bias[None, :],
        ml_gain=jnp.tile(p["mlstm_out_norm_g"][i], LANES // HEAD_DIM)[None, :],
        w_out=p["w_out"][i].astype(BF16),
        g_ffn=p["norm_ffn_g"][i][None, :],
        w_route=jnp.concatenate([wr_hi, wr_lo], axis=1), b_route=b_route[None, :],
        layer=i,
        w_gate=_stacked_experts(p["w_expert_gate"]),
        w_up=_stacked_experts(p["w_expert_up"]),
        w_down=_stacked_experts(p["w_expert_down"]),
        g_ple=p["norm_ple_g"][i][None, :],
        w_pg=p["w_ple_gate"][i].astype(BF16),
        w_pp=p["w_ple_proj"][i].astype(BF16),
    )
```

```python
import functools

import jax
import jax.numpy as jnp
import numpy as np
from jax import lax
from jax.experimental import pallas as pl
from jax.experimental.pallas import tpu as pltpu

F32 = jnp.float32
BF16 = jnp.bfloat16

GRID_W = 64
HEAD_DIM = 64
ATTN_HEADS = 8
ATTN_KV_HEADS = 2
GLA_HEADS = 4
MLSTM_HEADS = 4
ATTN_WIDTH = ATTN_HEADS * HEAD_DIM
KV_WIDTH = ATTN_KV_HEADS * HEAD_DIM
GLA_WIDTH = GLA_HEADS * HEAD_DIM
MLSTM_WIDTH = MLSTM_HEADS * HEAD_DIM
GLA_RANK = 16
GLA_TAU = 16.0
CHUNK = 64
ROPE_THETA = 10000.0
ROPE_AXIS_DIM = HEAD_DIM // 2
N_GROUPS = 4
EXPERTS_PER_GROUP = 4
N_EXPERTS = N_GROUPS * EXPERTS_PER_GROUP
D_EXPERT = 512
EPS = 1e-6
NEG = -1e30

LANES = 128
SUBLANES = 8
VMEM_LIMIT_BYTES = 56 * 1024 * 1024

QK_WIDTH = ATTN_WIDTH + KV_WIDTH
OFF_V = QK_WIDTH
OFF_GLA = OFF_V + KV_WIDTH
OFF_ML = OFF_GLA + 4 * GLA_WIDTH
OFF_SMALL = OFF_ML + 4 * MLSTM_WIDTH
IN_PERM_WIDTH = OFF_SMALL + LANES
SMALL_GATE_LANE = 2 * GLA_RANK

ROUTE_BLOCK = 1024
ROUTE_TILE = 128
TILES_PER_BLOCK = (ROUTE_BLOCK + N_GROUPS * (ROUTE_TILE - 1)) // ROUTE_TILE
ROUTE_ROWS = TILES_PER_BLOCK * ROUTE_TILE
ROUTE_W_LANE = 8
ROUTE_LO_SHIFT = 32
ROUTE_SUBTILE = 256


def _dot(a, b):
    return jnp.dot(a, b, preferred_element_type=F32)


def _dot_nt(a, b):
    return lax.dot_general(a, b, (((1,), (1,)), ((), ())), preferred_element_type=F32)


def _dot_tn(a, b):
    return lax.dot_general(a, b, (((0,), (0,)), ((), ())), preferred_element_type=F32)


def _split(a):
    hi = a.astype(BF16)
    lo = (a - hi.astype(F32)).astype(BF16)
    return hi, lo


def _dot3(a, w_hi, w_lo):
    a_hi, a_lo = _split(a)
    return _dot(a_hi, w_hi) + _dot(a_lo, w_hi) + _dot(a_hi, w_lo)


def _log_sigmoid(x):
    return jnp.minimum(x, 0.0) - jnp.log(1.0 + jnp.exp(-jnp.abs(x)))


def _sigmoid(x):
    return 1.0 / (1.0 + jnp.exp(-x))


def _rms(x, g):
    return x * lax.rsqrt(jnp.mean(x * x, axis=-1, keepdims=True) + EPS) * g


def _params(*semantics):
    return pltpu.CompilerParams(dimension_semantics=semantics, vmem_limit_bytes=VMEM_LIMIT_BYTES)


def _in_proj_kernel(x_ref, g_ref, w_ref, cos_ref, sin_ref, cost_ref, sint_ref, gq_ref, gk_ref,
                    eh_ref, eht_ref, qt_ref, k_ref, vt_ref, gla_ref, ml_ref, small_ref):
    tm = x_ref.shape[0]
    h = _rms(x_ref[...], g_ref[...])
    z = _dot(h.astype(BF16), w_ref[...])

    heads = LANES // HEAD_DIM
    half = ROPE_AXIS_DIM // 2
    gq = jnp.concatenate([gq_ref[...]] * (tm // LANES), axis=1)
    cost = cost_ref[...]
    sint = sint_ref[...]
    for c in range(ATTN_WIDTH // LANES):
        zt = z[:, c * LANES:(c + 1) * LANES].T
        z3 = zt.reshape(heads, HEAD_DIM, tm)
        inv = lax.rsqrt(jnp.mean(z3 * z3, axis=1, keepdims=True) + EPS)
        y = (z3 * inv).reshape(LANES, tm) * gq
        partner = jnp.concatenate(
            [y[(r ^ 1) * half:((r ^ 1) + 1) * half, :] for r in range(LANES // half)], axis=0)
        qt_ref[0, c * LANES:(c + 1) * LANES, :] = (y * cost + partner * sint).astype(BF16)

    kz = z[:, ATTN_WIDTH:QK_WIDTH]
    sq_hi, sq_lo = _split(kz * kz)
    ssq = _dot(sq_hi, eh_ref[...]) + _dot(sq_lo, eh_ref[...])
    inv_hi, inv_lo = _split(lax.rsqrt(ssq * (1.0 / HEAD_DIM) + EPS))
    yk = kz * (_dot(inv_hi, eht_ref[...]) + _dot(inv_lo, eht_ref[...])) * gk_ref[...]
    lane = lax.broadcasted_iota(jnp.int32, (tm, LANES), 1)
    partner = jnp.where((lane % ROPE_AXIS_DIM) < half,
                        pltpu.roll(yk, LANES - half, 1), pltpu.roll(yk, half, 1))
    k_ref[0] = (yk * cos_ref[...] + partner * sin_ref[...]).astype(BF16)
    vt_ref[0] = z[:, OFF_V:OFF_V + KV_WIDTH].T.astype(BF16)
    gla_ref[0] = z[:, OFF_GLA:OFF_ML]
    ml_ref[0] = z[:, OFF_ML:OFF_SMALL]
    small_ref[0] = z[:, OFF_SMALL:IN_PERM_WIDTH]


def _in_proj(x2, lp, rope, B, T, tm=512):
    N, D = x2.shape
    tpb = T // tm
    const = lambda i: (0, 0)
    tok3 = lambda i: (i // tpb, i % tpb, 0)
    tokT = lambda i: (i // tpb, 0, i % tpb)
    return pl.pallas_call(
        _in_proj_kernel,
        grid=(N // tm,),
        in_specs=[
            pl.BlockSpec((tm, D), lambda i: (i, 0)),
            pl.BlockSpec((1, D), const),
            pl.BlockSpec((D, IN_PERM_WIDTH), const),
            pl.BlockSpec((tm, LANES), lambda i: (i % tpb, 0)),
            pl.BlockSpec((tm, LANES), lambda i: (i % tpb, 0)),
            pl.BlockSpec((LANES, tm), lambda i: (0, i % tpb)),
            pl.BlockSpec((LANES, tm), lambda i: (0, i % tpb)),
            pl.BlockSpec((LANES, LANES), const),
            pl.BlockSpec((1, LANES), const),
            pl.BlockSpec((LANES, LANES), const),
            pl.BlockSpec((LANES, LANES), const),
        ],
        out_specs=[
            pl.BlockSpec((1, ATTN_WIDTH, tm), tokT),
            pl.BlockSpec((1, tm, KV_WIDTH), tok3),
            pl.BlockSpec((1, KV_WIDTH, tm), tokT),
            pl.BlockSpec((1, tm, 4 * GLA_WIDTH), tok3),
            pl.BlockSpec((1, tm, 4 * MLSTM_WIDTH), tok3),
            pl.BlockSpec((1, tm, LANES), tok3),
        ],
        out_shape=[
            jax.ShapeDtypeStruct((B, ATTN_WIDTH, T), BF16),
            jax.ShapeDtypeStruct((B, T, KV_WIDTH), BF16),
            jax.ShapeDtypeStruct((B, KV_WIDTH, T), BF16),
            jax.ShapeDtypeStruct((B, T, 4 * GLA_WIDTH), F32),
            jax.ShapeDtypeStruct((B, T, 4 * MLSTM_WIDTH), F32),
            jax.ShapeDtypeStruct((B, T, LANES), F32),
        ],
        compiler_params=_params("arbitrary"),
        name="in_proj",
    )(x2, lp["g_mix"], lp["w_in"], rope["cos"], rope["sin"], rope["cos_t"], rope["sin_t"],
      lp["q_gain_t"], lp["k_gain"], rope["eh"], rope["eht"])


ATTN_SAFE_LOGIT = 40.0
LOG2E = 1.4426950408889634


def _attn_kernel(safe_ref, qt_ref, k_ref, vt_ref, o_ref, *, tk):
    tq = qt_ref.shape[2]
    T = k_ref.shape[1]
    G = ATTN_HEADS // ATTN_KV_HEADS
    n = G * tq
    zeros = jnp.zeros((HEAD_DIM, n), BF16)

    def q_operand(j):
        base = j * G * HEAD_DIM
        qs = jnp.concatenate(
            [qt_ref[0, base + h * HEAD_DIM:base + (h + 1) * HEAD_DIM, :] for h in range(G)], axis=1)
        return jnp.concatenate([qs, zeros] if j == 0 else [zeros, qs], axis=0)

    def finish(j, acc, l):
        base = j * G * HEAD_DIM
        o = acc * (1.0 / l)
        ot = jnp.concatenate([o[:, h * tq:(h + 1) * tq] for h in range(G)], axis=0)
        o_ref[0, :, base:base + G * HEAD_DIM] = ot.T.astype(BF16)

    @pl.when(safe_ref[0] == 1)
    def _():
        qps = [q_operand(j) for j in range(ATTN_KV_HEADS)]
        units = [(c, j) for c in range(T // tk) for j in range(ATTN_KV_HEADS)]

        def scores(u):
            c, j = units[u]
            return _dot(k_ref[0, c * tk:(c + 1) * tk, :], qps[j])

        l8 = [jnp.zeros((SUBLANES, n), F32)] * ATTN_KV_HEADS
        acc = [jnp.zeros((HEAD_DIM, n), F32)] * ATTN_KV_HEADS
        st = scores(0)
        for u, (c, j) in enumerate(units):
            st_next = scores(u + 1) if u + 1 < len(units) else None
            p = jnp.exp2(st)
            l8[j] = l8[j] + jnp.sum(p.reshape(tk // SUBLANES, SUBLANES, n), axis=0)
            vc = vt_ref[0, j * HEAD_DIM:(j + 1) * HEAD_DIM, c * tk:(c + 1) * tk]
            acc[j] = acc[j] + _dot(vc, p.astype(BF16))
            st = st_next
        for j in range(ATTN_KV_HEADS):
            finish(j, acc[j], jnp.sum(l8[j], axis=0, keepdims=True))

    @pl.when(safe_ref[0] == 0)
    def _():
        for j in range(ATTN_KV_HEADS):
            qp = q_operand(j)

            def body(c, carry, qp=qp, j=j):
                m, l, acc = carry
                off = pl.multiple_of(c * tk, tk)
                st = _dot(k_ref[0, pl.ds(off, tk), :], qp)
                m_new = jnp.maximum(m, jnp.max(st, axis=0, keepdims=True))
                alpha = jnp.exp2(m - m_new)
                p = jnp.exp2(st - m_new)
                l = alpha * l + jnp.sum(p, axis=0, keepdims=True)
                vc = vt_ref[0, j * HEAD_DIM:(j + 1) * HEAD_DIM, pl.ds(off, tk)]
                return m_new, l, alpha * acc + _dot(vc, p.astype(BF16))

            init = (jnp.full((1, n), NEG, F32), jnp.zeros((1, n), F32), jnp.zeros((HEAD_DIM, n), F32))
            _, l, acc = lax.fori_loop(0, T // tk, body, init)
            finish(j, acc, l)


def _attention(safe, qt, k, vt, tq=256, tk=128):
    B, _, T = qt.shape
    tk = min(tk, T)
    return pl.pallas_call(
        functools.partial(_attn_kernel, tk=tk),
        grid_spec=pltpu.PrefetchScalarGridSpec(
            num_scalar_prefetch=1,
            grid=(B, T // tq),
            in_specs=[
                pl.BlockSpec((1, ATTN_WIDTH, tq), lambda b, i, s: (b, 0, i)),
                pl.BlockSpec((1, T, KV_WIDTH), lambda b, i, s: (b, 0, 0)),
                pl.BlockSpec((1, KV_WIDTH, T), lambda b, i, s: (b, 0, 0)),
            ],
            out_specs=pl.BlockSpec((1, tq, ATTN_WIDTH), lambda b, i, s: (b, i, 0)),
        ),
        out_shape=jax.ShapeDtypeStruct((B, T, ATTN_WIDTH), BF16),
        compiler_params=_params("arbitrary", "arbitrary"),
        name="attention",
    )(safe, qt, k, vt)


def _chunk_scan(x, pos, op, fill, reverse):
    rows = x.shape[0]
    s = 1
    while s < CHUNK:
        if reverse:
            shifted = jnp.where(pos < CHUNK - s, pltpu.roll(x, rows - s, 0), fill)
        else:
            shifted = jnp.where(pos >= s, pltpu.roll(x, s, 0), fill)
        x = op(x, shifted)
        s *= 2
    return x


def _chunk_pos(rows):
    return lax.broadcasted_iota(jnp.int32, (rows, LANES), 0) % CHUNK


def _chunk_row(a, reverse_dir, idx_fwd, idx_bwd):
    rows = a.shape[0]
    a3 = a.reshape(rows // CHUNK, CHUNK, LANES)
    i = idx_bwd if reverse_dir else idx_fwd
    return jnp.broadcast_to(a3[:, i:i + 1, :], a3.shape).reshape(rows, LANES)


def _stack_heads(x):
    lane = lax.broadcasted_iota(jnp.int32, x.shape, 1)
    zero = jnp.zeros_like(x)
    return jnp.concatenate([jnp.where(lane < HEAD_DIM, x, zero), jnp.where(lane >= HEAD_DIM, x, zero)], axis=0)


def _select_heads(x):
    c = x.shape[0] // 2
    lane = lax.broadcasted_iota(jnp.int32, (c, x.shape[1]), 1)
    return jnp.where(lane < HEAD_DIM, x[:c], x[c:])


def _stacked_causal_masks():
    ci = lax.broadcasted_iota(jnp.int32, (2 * CHUNK, CHUNK), 0) % CHUNK
    si = lax.broadcasted_iota(jnp.int32, (2 * CHUNK, CHUNK), 1)
    return ci >= si, ci <= si


def _pair_blockdiag(width):
    r = lax.broadcasted_iota(jnp.int32, (LANES, width), 0) // HEAD_DIM
    c = lax.broadcasted_iota(jnp.int32, (LANES, width), 1) // (width // 2)
    return r == c


def _head_rms(o, gain):
    lane = lax.broadcasted_iota(jnp.int32, o.shape, 1)
    lo = lane < HEAD_DIM
    sq = o * o
    s_lo = jnp.sum(jnp.where(lo, sq, 0.0), axis=-1, keepdims=True)
    s_hi = jnp.sum(jnp.where(lo, 0.0, sq), axis=-1, keepdims=True)
    ms = jnp.where(lo, s_lo, s_hi) * (1.0 / HEAD_DIM)
    return o * lax.rsqrt(ms + EPS) * gain


PREP_ROWS = 512
GLA_CHUNKS_PER_STEP = 4


def _gla_kernel(q_ref, k_ref, v_ref, g_ref, small_ref, wdh_ref, wdl_ref, bd_ref, gain_ref,
                o_ref, qe_s, ke_s, dec_s, of_s, ob_s):
    T = q_ref.shape[1]
    nc = T // CHUNK
    R = min(PREP_ROWS, T)
    cpt = R // CHUNK
    pos = _chunk_pos(R)
    w_hi = jnp.concatenate([wdh_ref[0], wdh_ref[1]], axis=1)
    w_lo = jnp.concatenate([wdl_ref[0], wdl_ref[1]], axis=1)
    bias = jnp.concatenate([bd_ref[0], bd_ref[1]], axis=1)

    def prep(t, _):
        r0 = pl.multiple_of(t * R, R)
        q = q_ref[0, pl.ds(r0, R), :] * HEAD_DIM ** -0.5
        k = k_ref[0, pl.ds(r0, R), :]
        la2 = _log_sigmoid(_dot3(small_ref[0, pl.ds(r0, R), :], w_hi, w_lo) + bias) * (1.0 / GLA_TAU)
        for d in range(2):
            b = _chunk_scan(la2[:, d * LANES:(d + 1) * LANES], pos, jnp.add, 0.0, reverse=bool(d))
            b_mid = _chunk_row(b, d, CHUNK // 2 - 1, CHUNK // 2)
            b_last = _chunk_row(b, d, CHUNK - 1, 0)
            qe_s[d, pl.ds(r0, R), :] = (q * jnp.exp(b - b_mid)).astype(BF16)
            ke_s[d, pl.ds(r0, R), :] = (k * jnp.exp(b_mid - b)).astype(BF16)
            c8 = pl.ds(pl.multiple_of(t * cpt * SUBLANES, SUBLANES), cpt * SUBLANES)
            for kind, val in enumerate((b_last, b_last - b_mid, b_mid)):
                rows8 = val.reshape(cpt, CHUNK, LANES)[:, :SUBLANES, :].reshape(cpt * SUBLANES, LANES)
                dec_s[d, kind, c8, :] = jnp.exp(rows8)
        return 0

    lax.fori_loop(0, T // R, prep, 0)

    outs = (of_s, ob_s)
    masks = _stacked_causal_masks()
    blockdiag = _pair_blockdiag(LANES)
    G = GLA_CHUNKS_PER_STEP

    def step(it, states):
        units = [(d, g) for d in range(2) for g in range(G)]
        ops = {}
        for d, g in units:
            c = it * G + g
            c = (nc - 1 - c) if d else c
            r0 = pl.multiple_of(c * CHUNK, CHUNK)
            c8 = pl.ds(pl.multiple_of(c * SUBLANES, SUBLANES), SUBLANES)
            ops[d, g] = dict(
                r0=r0,
                qe=qe_s[d, pl.ds(r0, CHUNK), :], ke=ke_s[d, pl.ds(r0, CHUNK), :],
                v=v_ref[0, pl.ds(r0, CHUNK), :].astype(BF16),
                dec=dec_s[d, 0, c8, :][0:1, :], to_end=dec_s[d, 1, c8, :][0:1, :],
                from_start=dec_s[d, 2, c8, :][0:1, :])
        a2, kv = {}, {}
        for u in units:
            o = ops[u]
            a2[u] = _dot_nt(_stack_heads(o["qe"]), o["ke"])
            kv[u] = _dot_tn(o["v"], o["ke"]) * o["to_end"]
        st_in = {}
        new_states = []
        for d in range(2):
            st = states[d]
            for g in range(G):
                st_in[d, g] = (st * ops[d, g]["from_start"]).astype(BF16)
                st = st * ops[d, g]["dec"] + jnp.where(blockdiag, kv[d, g], 0.0)
            new_states.append(st)
        for u in units:
            o = ops[u]
            a = jnp.where(masks[u[0]], a2[u], 0.0).astype(BF16)
            intra = _select_heads(_dot(a, o["v"]))
            outs[u[0]][pl.ds(o["r0"], CHUNK), :] = intra + _dot_nt(o["qe"], st_in[u])
        return tuple(new_states)

    zero = jnp.zeros((LANES, LANES), F32)
    lax.fori_loop(0, nc // G, step, (zero, zero))

    def post(t, _):
        r0 = pl.multiple_of(t * R, R)
        o = _head_rms(of_s[pl.ds(r0, R), :] + ob_s[pl.ds(r0, R), :], gain_ref[...])
        g = g_ref[0, pl.ds(r0, R), :]
        o_ref[0, pl.ds(r0, R), :] = (o * (g * _sigmoid(g))).astype(BF16)
        return 0

    lax.fori_loop(0, T // R, post, 0)


def _gla(gla, small, lp):
    B, T, _ = gla.shape
    pairs = GLA_WIDTH // LANES
    col = lambda off: (lambda b, hp: (b, 0, off * pairs + hp))
    return pl.pallas_call(
        _gla_kernel,
        grid=(B, pairs),
        in_specs=[
            pl.BlockSpec((1, T, LANES), col(0)),
            pl.BlockSpec((1, T, LANES), col(1)),
            pl.BlockSpec((1, T, LANES), col(2)),
            pl.BlockSpec((1, T, LANES), col(3)),
            pl.BlockSpec((1, T, LANES), lambda b, hp: (b, 0, 0)),
            pl.BlockSpec((2, LANES, LANES), lambda b, hp: (0, 0, hp)),
            pl.BlockSpec((2, LANES, LANES), lambda b, hp: (0, 0, hp)),
            pl.BlockSpec((2, 1, LANES), lambda b, hp: (0, 0, hp)),
            pl.BlockSpec((1, LANES), lambda b, hp: (0, 0)),
        ],
        out_specs=pl.BlockSpec((1, T, LANES), lambda b, hp: (b, 0, hp)),
        out_shape=jax.ShapeDtypeStruct((B, T, GLA_WIDTH), BF16),
        scratch_shapes=[
            pltpu.VMEM((2, T, LANES), BF16), pltpu.VMEM((2, T, LANES), BF16),
            pltpu.VMEM((2, 3, (T // CHUNK) * SUBLANES, LANES), F32),
            pltpu.VMEM((T, LANES), F32), pltpu.VMEM((T, LANES), F32),
        ],
        compiler_params=_params("arbitrary", "arbitrary"),
        name="gla",
    )(gla, gla, gla, gla, small, lp["wd_hi"], lp["wd_lo"], lp["bd"], lp["gla_gain"])


GATE_I = SMALL_GATE_LANE
GATE_F = SMALL_GATE_LANE + MLSTM_HEADS


def _gate_lane(d, hh):
    return GATE_F + 2 * MLSTM_HEADS * d + hh


MLSTM_CHUNKS_PER_STEP = 8


def _chunk_rows8(a, row):
    n = a.shape[0] // CHUNK
    a3 = a.reshape(n, CHUNK, LANES)
    full = jnp.broadcast_to(a3[:, row:row + 1, :], a3.shape)
    return full[:, :SUBLANES, :].reshape(n * SUBLANES, LANES)


def _expand_rows8(a8):
    n, w = a8.shape[0] // SUBLANES, a8.shape[1]
    a3 = a8.reshape(n, SUBLANES, w)[:, 0:1, :]
    return jnp.broadcast_to(a3, (n, CHUNK, w)).reshape(n * CHUNK, w)


def _gate_select():
    src = lax.broadcasted_iota(jnp.int32, (LANES, 2 * LANES), 0)
    dst = lax.broadcasted_iota(jnp.int32, (LANES, 2 * LANES), 1)
    want = GATE_F + 2 * MLSTM_HEADS * (dst // LANES) + (dst % LANES) // HEAD_DIM
    return jnp.where(src == want, 1.0, 0.0).astype(BF16)


def _gate_broadcast(x, sel, gate_mask, pieces=2):
    x = jnp.where(gate_mask, x, 0.0)
    if pieces == 1:
        return _dot(x.astype(BF16), sel)
    hi, lo = _split(x)
    return _dot(hi, sel) + _dot(lo, sel)


def _mlstm_kernel(q_ref, k_ref, v_ref, og_ref, small_ref, wq_ref, wk_ref, bq_ref, bk_ref,
                  gbias_ref, gain_ref, o_ref, q_s, k_s, b_s, r_s, cm_s, fl1_s, bl_s, rl_s, mf_s, mb_s,
                  wp_s, qi_s, kw_s, rt_s, st_s, of_s, ob_s):
    T = q_ref.shape[1]
    nc = T // CHUNK
    R = min(PREP_ROWS, T)
    hp = pl.program_id(1)
    G = MLSTM_CHUNKS_PER_STEP
    cpt = R // CHUNK

    row = lax.broadcasted_iota(jnp.int32, (T, LANES), 0)
    for src, w_ref, b_ref, dst, scale in ((q_ref, wq_ref, bq_ref, q_s, 1.0),
                                          (k_ref, wk_ref, bk_ref, k_s, HEAD_DIM ** -0.5)):
        xc = src[0]
        prev = jnp.where(row >= 1, pltpu.roll(xc, 1, 0), 0.0)
        nxt = jnp.where(row < T - 1, pltpu.roll(xc, T - 1, 0), 0.0)
        y = prev * w_ref[0:1, :] + xc * w_ref[1:2, :] + nxt * w_ref[2:3, :] + b_ref[...]
        dst[...] = (y * _sigmoid(y) * scale).astype(BF16)

    pos = _chunk_pos(R)
    lane = lax.broadcasted_iota(jnp.int32, (R, LANES), 1)
    is_bwd = lane >= GATE_I + 2 * MLSTM_HEADS
    is_bwd8 = lax.broadcasted_iota(jnp.int32, (cpt * SUBLANES, LANES), 1) >= GATE_I + 2 * MLSTM_HEADS
    heads_per_pair = LANES // HEAD_DIM
    shift = (LANES - heads_per_pair * hp) % LANES
    gate_lanes = [_gate_lane(d, hh) for d in range(2) for hh in range(2)]
    gate_mask = functools.reduce(jnp.logical_or, [lane == l for l in gate_lanes])
    lane8 = lax.broadcasted_iota(jnp.int32, (cpt * SUBLANES, LANES), 1)
    gate_mask8 = functools.reduce(jnp.logical_or, [lane8 == l for l in gate_lanes])

    def prep(t, _):
        r0 = pl.multiple_of(t * R, R)
        gc = pltpu.roll(small_ref[0, pl.ds(r0, R), :] + gbias_ref[...], shift, 1)
        logf = _log_sigmoid(gc)
        pre = _chunk_scan(logf, pos, jnp.add, 0.0, False)
        b = jnp.where(is_bwd, _chunk_row(pre, 0, CHUNK - 1, CHUNK - 1) - pre + logf, pre)
        r = pltpu.roll(gc, MLSTM_HEADS, 1) - b
        cm = jnp.where(is_bwd, _chunk_scan(r, pos, jnp.maximum, NEG, True),
                       _chunk_scan(r, pos, jnp.maximum, NEG, False))
        b_s[pl.ds(r0, R), :] = b
        r_s[pl.ds(r0, R), :] = r
        cm_s[pl.ds(r0, R), :] = cm
        rt_s[:, pl.ds(r0, R)] = r.T
        c8 = pl.ds(pl.multiple_of(t * cpt * SUBLANES, SUBLANES), cpt * SUBLANES)
        bl_s[c8, :] = jnp.where(is_bwd8, _chunk_rows8(b, 0), _chunk_rows8(b, CHUNK - 1))
        rl_s[c8, :] = jnp.where(is_bwd8, _chunk_rows8(cm, 0), _chunk_rows8(cm, CHUNK - 1))
        return 0

    lax.fori_loop(0, T // R, prep, 0)

    def m_chain(n, carry):
        mf, mb = carry
        rf = pl.ds(pl.multiple_of(n * SUBLANES, SUBLANES), SUBLANES)
        rb = pl.ds(pl.multiple_of((nc - 1 - n) * SUBLANES, SUBLANES), SUBLANES)
        mf_s[rf, :] = mf
        mb_s[rb, :] = mb
        return (bl_s[rf, :] + jnp.maximum(mf, rl_s[rf, :]), bl_s[rb, :] + jnp.maximum(mb, rl_s[rb, :]))

    m0 = jnp.full((SUBLANES, LANES), NEG, F32)
    lax.fori_loop(0, nc, m_chain, (m0, m0))

    expo_s, floor_s = (b_s, cm_s), (r_s, fl1_s)
    sel = _gate_select()

    def weights(t, _):
        r0 = pl.multiple_of(t * R, R)
        c8 = pl.ds(pl.multiple_of(t * cpt * SUBLANES, SUBLANES), cpt * SUBLANES)
        rows = pl.ds(r0, R)
        m_in8 = jnp.where(is_bwd8, mb_s[c8, :], mf_s[c8, :])
        bl8 = bl_s[c8, :]
        m_out8 = bl8 + jnp.maximum(m_in8, rl_s[c8, :])
        m_in = _expand_rows8(m_in8)
        mx = jnp.maximum(m_in, cm_s[rows, :])
        wp_b = jnp.exp(_gate_broadcast(bl8 + m_in8 - m_out8, sel, gate_mask8))
        mx_b = _gate_broadcast(mx, sel, gate_mask)
        floor = jnp.exp(-_gate_broadcast(b_s[rows, :] + mx, sel, gate_mask))
        w_inter = _gate_broadcast(jnp.exp(m_in - mx), sel, gate_mask, pieces=1)
        wk = _gate_broadcast(jnp.exp(_expand_rows8(bl8 - m_out8) + r_s[rows, :]), sel, gate_mask, pieces=1)
        q = q_s[rows, :].astype(F32)
        k = k_s[rows, :].astype(F32)
        for d in range(2):
            half = slice(d * LANES, (d + 1) * LANES)
            qi_s[d, rows, :] = (q * w_inter[:, half]).astype(BF16)
            kw_s[d, rows, :] = (k * wk[:, half]).astype(BF16)
            wp_s[d, c8, :] = wp_b[:, half]
            expo_s[d][rows, :] = -mx_b[:, half]
            floor_s[d][rows, :] = floor[:, half]
        return 0

    lax.fori_loop(0, T // R, weights, 0)

    outs = (of_s, ob_s)
    ci = lax.broadcasted_iota(jnp.int32, (CHUNK, LANES), 0)
    si = lax.broadcasted_iota(jnp.int32, (CHUNK, LANES), 1) % CHUNK
    masks = (ci >= si, ci <= si)
    br = lax.broadcasted_iota(jnp.int32, (LANES, 2 * LANES), 0) // HEAD_DIM
    bc = (lax.broadcasted_iota(jnp.int32, (LANES, 2 * LANES), 1) % LANES) // HEAD_DIM
    blockdiag = br == bc
    ones = jnp.ones((CHUNK, LANES), BF16)
    st_s[...] = jnp.zeros(st_s.shape, F32)
    gate_rows = slice(GATE_I, GATE_I + 4 * MLSTM_HEADS)

    def step(it, _):
        units = [(d, g) for d in range(2) for g in range(G)]
        ops = {}
        for d in range(2):
            first = (nc - (it + 1) * G) if d else it * G
            rt = rt_s[gate_rows, pl.ds(pl.multiple_of(first * CHUNK, G * CHUNK), G * CHUNK)]
            for g in range(G):
                local = (G - 1 - g) if d else g
                c = first + local
                rows = pl.ds(pl.multiple_of(c * CHUNK, CHUNK), CHUNK)
                c8 = pl.ds(pl.multiple_of(c * SUBLANES, SUBLANES), SUBLANES)
                r_row = jnp.concatenate(
                    [rt[_gate_lane(d, hh) - GATE_I:_gate_lane(d, hh) - GATE_I + 1,
                        local * CHUNK:(local + 1) * CHUNK] for hh in range(2)], axis=1)
                wp = wp_s[d, c8, :][0:1, :]
                ops[d, g] = dict(
                    rows=rows, q=q_s[rows, :], k=k_s[rows, :], qi=qi_s[d, rows, :], kw=kw_s[d, rows, :],
                    va=jnp.concatenate([v_ref[0, rows, :].astype(BF16), ones], axis=1),
                    expo=expo_s[d][rows, :] + r_row, floor=floor_s[d][rows, :],
                    wp=jnp.concatenate([wp, wp], axis=1))
        qk, kv = {}, {}
        for u in units:
            o = ops[u]
            qk[u] = _dot_nt(o["q"], _stack_heads(o["k"]))
            kv[u] = _dot_tn(o["kw"], o["va"])
        st_in = {}
        for d in range(2):
            st = st_s[d]
            for g in range(G):
                st_in[d, g] = st.astype(BF16)
                st = st * ops[d, g]["wp"] + jnp.where(blockdiag, kv[d, g], 0.0)
            st_s[d] = st
        for u in units:
            d = u[0]
            o = ops[u]
            smat = (qk[u] * jnp.where(masks[d], jnp.exp(o["expo"]), 0.0)).astype(BF16)
            va_bd = jnp.where(blockdiag, jnp.concatenate([o["va"], o["va"]], axis=0), jnp.zeros((), BF16))
            num = _dot(jnp.concatenate([o["qi"], smat], axis=1), jnp.concatenate([st_in[u], va_bd], axis=0))
            den = jnp.maximum(jnp.abs(num[:, LANES:]), o["floor"])
            outs[d][o["rows"], :] = num[:, :LANES] / den
        return 0

    lax.fori_loop(0, nc // G, step, 0)

    def post(t, _):
        r0 = pl.multiple_of(t * R, R)
        h = _head_rms(of_s[pl.ds(r0, R), :] + ob_s[pl.ds(r0, R), :], gain_ref[...])
        o_ref[0, pl.ds(r0, R), :] = (h * _sigmoid(og_ref[0, pl.ds(r0, R), :])).astype(BF16)
        return 0

    lax.fori_loop(0, T // R, post, 0)


def _mlstm(ml, small, lp):
    B, T, _ = ml.shape
    pairs = MLSTM_WIDTH // LANES
    col = lambda off: (lambda b, hp: (b, 0, off * pairs + hp))
    const = lambda b, hp: (0, 0)
    return pl.pallas_call(
        _mlstm_kernel,
        grid=(B, pairs),
        in_specs=[
            pl.BlockSpec((1, T, LANES), col(0)),
            pl.BlockSpec((1, T, LANES), col(1)),
            pl.BlockSpec((1, T, LANES), col(2)),
            pl.BlockSpec((1, T, LANES), col(3)),
            pl.BlockSpec((1, T, LANES), lambda b, hp: (b, 0, 0)),
            pl.BlockSpec((3, LANES), lambda b, hp: (0, hp)),
            pl.BlockSpec((3, LANES), lambda b, hp: (0, pairs + hp)),
            pl.BlockSpec((1, LANES), lambda b, hp: (0, hp)),
            pl.BlockSpec((1, LANES), lambda b, hp: (0, pairs + hp)),
            pl.BlockSpec((1, LANES), const),
            pl.BlockSpec((1, LANES), const),
        ],
        out_specs=pl.BlockSpec((1, T, LANES), lambda b, hp: (b, 0, hp)),
        out_shape=jax.ShapeDtypeStruct((B, T, MLSTM_WIDTH), BF16),
        scratch_shapes=(
            [pltpu.VMEM((T, LANES), BF16)] * 2
            + [pltpu.VMEM((T, LANES), F32)] * 4
            + [pltpu.VMEM(((T // CHUNK) * SUBLANES, LANES), F32)] * 4
            + [pltpu.VMEM((2, (T // CHUNK) * SUBLANES, LANES), F32)]
            + [pltpu.VMEM((2, T, LANES), BF16)] * 2
            + [pltpu.VMEM((LANES, T), F32),
               pltpu.VMEM((2, LANES, 2 * LANES), F32),
               pltpu.VMEM((T, LANES), F32), pltpu.VMEM((T, LANES), F32)]),
        compiler_params=_params("arbitrary", "arbitrary"),
        name="mlstm",
    )(ml, ml, ml, ml, small, lp["conv_w"], lp["conv_w"], lp["conv_b"], lp["conv_b"],
      lp["gate_bias"], lp["ml_gain"])


def _first_argmax(vals, row):
    mx = jnp.max(vals, axis=0, keepdims=True)
    idx = jnp.min(jnp.where(vals == mx, row, vals.shape[0]), axis=0, keepdims=True)
    return mx, idx


def _out_route_kernel(x_ref, a_ref, gl_ref, ml_ref, w_ref, g_ref, wr_ref, br_ref,
                      x1_ref, t_ref, route_ref, routet_ref):
    tm = x_ref.shape[0]
    sub = min(tm, ROUTE_SUBTILE)
    spans = [slice(r, r + sub) for r in range(0, tm, sub)]
    x1s, logits = [], []
    for rs in spans:
        x1 = (x_ref[rs, :]
              + _dot(a_ref[rs, :], w_ref[0:ATTN_WIDTH, :])
              + _dot(gl_ref[rs, :], w_ref[ATTN_WIDTH:ATTN_WIDTH + GLA_WIDTH, :])
              + _dot(ml_ref[rs, :], w_ref[ATTN_WIDTH + GLA_WIDTH:, :]))
        x1_ref[rs, :] = x1
        x1s.append(x1)
    for rs, x1 in zip(spans, x1s):
        t_hi, t_lo = _split(_rms(x1, g_ref[...]))
        t_ref[rs, 0:t_hi.shape[1]] = t_hi
        both = _dot(t_hi, wr_ref[...])
        logits.append((both[:, :LANES] + both[:, LANES:] + _dot(t_lo, wr_ref[:, :LANES])) + br_ref[...])
    grow = lax.broadcasted_iota(jnp.int32, (SUBLANES, sub), 0)
    erow = lax.broadcasted_iota(jnp.int32, (N_EXPERTS, sub), 0)
    for rs, lg in zip(spans, logits):
        lt = lg.T
        gl = jnp.where(grow < N_GROUPS, lt[0:SUBLANES, :], -jnp.inf)
        gmax, gi = _first_argmax(gl, grow)
        g_prob = 1.0 / jnp.sum(jnp.exp(gl - gmax), axis=0, keepdims=True)
        el = jnp.where(erow // EXPERTS_PER_GROUP == gi, lt[ROUTE_W_LANE:ROUTE_W_LANE + N_EXPERTS, :], -jnp.inf)
        v1, i1 = _first_argmax(el, erow)
        v2, i2 = _first_argmax(jnp.where(erow == i1, -jnp.inf, el), erow)
        e2 = jnp.exp(v2 - v1)
        w1 = g_prob / (1.0 + e2)
        w2 = g_prob * e2 / (1.0 + e2)
        comb = jnp.where(erow == i1, w1, jnp.where(erow == i2, w2, 0.0))
        head = jnp.where(grow == 0, gi.astype(F32), 0.0)
        route_t = jnp.concatenate(
            [head, comb, jnp.zeros((LANES - SUBLANES - N_EXPERTS, sub), F32)], axis=0)
        route_ref[rs, :] = route_t.T
        routet_ref[:, rs] = head


def _dispatch_kernel(t_ref, route_ref, routet_ref, xs_ref, cws_ref, pos_ref, cnt_ref):
    nb = ROUTE_BLOCK
    gi_row = routet_ref[0:1, :]
    sub = lax.broadcasted_iota(jnp.int32, (SUBLANES, nb), 0).astype(F32)
    onehot = (sub == gi_row)
    ri = lax.broadcasted_iota(jnp.int32, (nb, nb), 0)
    cj = lax.broadcasted_iota(jnp.int32, (nb, nb), 1)
    before = (ri < cj).astype(BF16)
    rank = _dot(onehot.astype(BF16), before)
    counts = jnp.broadcast_to(jnp.sum(onehot.astype(F32), axis=-1, keepdims=True), (SUBLANES, LANES))
    padded = jnp.ceil(counts * (1.0 / ROUTE_TILE)) * ROUTE_TILE
    srow = lax.broadcasted_iota(jnp.int32, (SUBLANES, LANES), 0)
    incl = padded
    s = 1
    while s < SUBLANES:
        incl = incl + jnp.where(srow >= s, pltpu.roll(incl, s, 0), 0.0)
        s *= 2
    start = incl - padded
    pos_row = jnp.sum(jnp.where(onehot, start[:, 0:1] + rank, 0.0), axis=0, keepdims=True)
    cnt_ref[0] = counts.astype(jnp.int32)

    pos_hi = jnp.floor(pos_row * (1.0 / ROUTE_TILE))
    prow = lax.broadcasted_iota(jnp.int32, (2 * SUBLANES, nb), 0)
    pieces = jnp.where(prow == 0, pos_hi, jnp.where(prow == 1, pos_row - ROUTE_TILE * pos_hi, 0.0))
    wrow = lax.broadcasted_iota(jnp.int32, (2 * SUBLANES, LANES), 0)
    weights = jnp.where(wrow == 0, float(ROUTE_TILE), jnp.where(wrow == 1, 1.0, 0.0))
    pos_ref[...] = _dot_tn(pieces.astype(BF16), weights.astype(BF16))

    route = route_ref[...]
    lane = lax.broadcasted_iota(jnp.int32, (nb, LANES), 1)
    comb = jnp.where(lane >= ROUTE_W_LANE, route, 0.0)
    c_hi, c_lo = _split(comb)
    c_lo2 = (comb - c_hi.astype(F32) - c_lo.astype(F32)).astype(BF16)
    d_model = xs_ref.shape[1]
    t_ref[:, d_model:] = (c_hi.astype(F32) + pltpu.roll(c_lo.astype(F32), ROUTE_LO_SHIFT, 1)
                          + pltpu.roll(c_lo2.astype(F32), 2 * ROUTE_LO_SHIFT, 1)).astype(BF16)
    tb = t_ref[...]
    for r in range(TILES_PER_BLOCK):
        rows = (lax.broadcasted_iota(jnp.int32, (ROUTE_TILE, nb), 0) + r * ROUTE_TILE).astype(F32)
        perm = (rows == pos_row).astype(BF16)
        moved = _dot(perm, tb)
        xs_ref[r * ROUTE_TILE:(r + 1) * ROUTE_TILE, :] = moved[:, :d_model].astype(BF16)
        cws_ref[r * ROUTE_TILE:(r + 1) * ROUTE_TILE, :] = moved[:, d_model:]


def _route_dispatch_kernel(x_ref, a_ref, gl_ref, ml_ref, w_ref, g_ref, wr_ref, br_ref,
                           x1_ref, xs_ref, cws_ref, pos_ref, cnt_ref, t_s, route_s, routet_s):
    _out_route_kernel(x_ref, a_ref, gl_ref, ml_ref, w_ref, g_ref, wr_ref, br_ref,
                      x1_ref, t_s, route_s, routet_s)
    _dispatch_kernel(t_s, route_s, routet_s, xs_ref, cws_ref, pos_ref, cnt_ref)


def _route_dispatch(x2, attn, gla_o, ml_o, lp):
    N, D = x2.shape
    nblk = N // ROUTE_BLOCK
    const = lambda i: (0, 0)
    tok = lambda i: (i, 0)
    return pl.pallas_call(
        _route_dispatch_kernel,
        grid=(nblk,),
        in_specs=[
            pl.BlockSpec((ROUTE_BLOCK, D), tok),
            pl.BlockSpec((ROUTE_BLOCK, ATTN_WIDTH), tok),
            pl.BlockSpec((ROUTE_BLOCK, GLA_WIDTH), tok),
            pl.BlockSpec((ROUTE_BLOCK, MLSTM_WIDTH), tok),
            pl.BlockSpec((ATTN_WIDTH + GLA_WIDTH + MLSTM_WIDTH, D), const),
            pl.BlockSpec((1, D), const),
            pl.BlockSpec((D, 2 * LANES), const),
            pl.BlockSpec((1, LANES), const),
        ],
        out_specs=[
            pl.BlockSpec((ROUTE_BLOCK, D), tok),
            pl.BlockSpec((ROUTE_ROWS, D), tok),
            pl.BlockSpec((ROUTE_ROWS, LANES), tok),
            pl.BlockSpec((ROUTE_BLOCK, LANES), tok),
            pl.BlockSpec((1, SUBLANES, LANES), lambda i: (i, 0, 0)),
        ],
        out_shape=[
            jax.ShapeDtypeStruct((N, D), F32),
            jax.ShapeDtypeStruct((nblk * ROUTE_ROWS, D), BF16),
            jax.ShapeDtypeStruct((nblk * ROUTE_ROWS, LANES), F32),
            jax.ShapeDtypeStruct((N, LANES), F32),
            jax.ShapeDtypeStruct((nblk, SUBLANES, LANES), jnp.int32),
        ],
        scratch_shapes=[
            pltpu.VMEM((ROUTE_BLOCK, D + LANES), BF16),
            pltpu.VMEM((ROUTE_BLOCK, LANES), F32),
            pltpu.VMEM((SUBLANES, ROUTE_BLOCK), F32),
        ],
        compiler_params=_params("arbitrary"),
        name="route_dispatch",
    )(x2, attn, gla_o, ml_o, lp["w_out"], lp["g_ffn"], lp["w_route"], lp["b_route"])


def _tile_schedule(cnt):
    nblk = cnt.shape[0]
    ntile = (cnt + ROUTE_TILE - 1) // ROUTE_TILE
    end = jnp.cumsum(ntile, axis=1)
    r = jnp.arange(TILES_PER_BLOCK, dtype=jnp.int32)
    grp = jnp.sum(r[None, :, None] >= end[:, None, :], axis=-1)
    grp = grp.reshape(-1).astype(jnp.int32)
    tile = jnp.arange(nblk * TILES_PER_BLOCK, dtype=jnp.int32)
    order = jnp.argsort(grp * (nblk * TILES_PER_BLOCK) + tile).astype(jnp.int32)
    n_active = jnp.sum(grp < N_GROUPS).astype(jnp.int32)
    g_sorted = grp[order]
    last_group = g_sorted[jnp.maximum(n_active - 1, 0)]
    g_sorted = jnp.where(g_sorted < N_GROUPS, g_sorted, last_group)
    return order, g_sorted, n_active[None]


def _expert_kernel(trow_ref, tgrp_ref, nact_ref, xs_ref, cws_ref, wg_ref, wu_ref, wd_ref, ys_ref,
                   wg_s, wu_s, wd_s):
    i = pl.program_id(0)

    @pl.when((i == 0) | (tgrp_ref[i] != tgrp_ref[jnp.maximum(i - 1, 0)]))
    def _():
        for j in range(EXPERTS_PER_GROUP):
            wg_s[j] = wg_ref[j].astype(BF16)
            wu_s[j] = wu_ref[j].astype(BF16)
            wd_s[j] = wd_ref[j].astype(BF16)

    @pl.when(i < nact_ref[0])
    def _():
        x = xs_ref[...]
        cws = cws_ref[...]
        lane = lax.broadcasted_iota(jnp.int32, cws.shape, 1)
        first = ROUTE_W_LANE + tgrp_ref[i] * EXPERTS_PER_GROUP
        y = jnp.zeros(ys_ref.shape, F32)

        def hidden(j):
            return _dot(x, wg_s[j]), _dot(x, wu_s[j])

        h = hidden(0)
        for j in range(EXPERTS_PER_GROUP):
            h_next = hidden(j + 1) if j + 1 < EXPERTS_PER_GROUP else None
            off = lane - (first + j)
            sel = (off == 0) | (off == ROUTE_LO_SHIFT) | (off == 2 * ROUTE_LO_SHIFT)
            wj = jnp.sum(jnp.where(sel, cws, 0.0), axis=-1, keepdims=True)
            a = (h[0] * _sigmoid(h[0]) * h[1]).astype(BF16)
            y = y + wj * _dot(a, wd_s[j])
            h = h_next
        ys_ref[...] = y.astype(BF16)

    @pl.when(i >= nact_ref[0])
    def _():
        ys_ref[...] = jnp.zeros(ys_ref.shape, BF16)


def _experts(xs, cws, order, grp, n_active, lp):
    rows, D = xs.shape
    n_tiles = rows // ROUTE_TILE
    tile = lambda i, trow, tgrp, nact: (trow[i], 0)
    layer_groups = lp["layer"] * N_GROUPS
    wsel = lambda i, trow, tgrp, nact: (layer_groups + tgrp[i], 0, 0)
    once = pl.Buffered(1)
    return pl.pallas_call(
        _expert_kernel,
        grid_spec=pltpu.PrefetchScalarGridSpec(
            num_scalar_prefetch=3,
            grid=(n_tiles,),
            in_specs=[
                pl.BlockSpec((ROUTE_TILE, D), tile),
                pl.BlockSpec((ROUTE_TILE, LANES), tile),
                pl.BlockSpec((EXPERTS_PER_GROUP, D, D_EXPERT), wsel, pipeline_mode=once),
                pl.BlockSpec((EXPERTS_PER_GROUP, D, D_EXPERT), wsel, pipeline_mode=once),
                pl.BlockSpec((EXPERTS_PER_GROUP, D_EXPERT, D), wsel, pipeline_mode=once),
            ],
            out_specs=pl.BlockSpec((ROUTE_TILE, D), tile),
            scratch_shapes=[
                pltpu.VMEM((EXPERTS_PER_GROUP, D, D_EXPERT), BF16),
                pltpu.VMEM((EXPERTS_PER_GROUP, D, D_EXPERT), BF16),
                pltpu.VMEM((EXPERTS_PER_GROUP, D_EXPERT, D), BF16),
            ],
        ),
        out_shape=jax.ShapeDtypeStruct((rows, D), BF16),
        compiler_params=_params("arbitrary"),
        name="experts",
    )(order, grp, n_active, xs, cws, lp["w_gate"], lp["w_up"], lp["w_down"])


def _combine_kernel(ys_ref, pos_ref, x1_ref, p_ref, g_ref, wpg_ref, wpp_ref, gfin_ref, o_ref,
                    *, embed, final):
    tm = x1_ref.shape[0]
    pos = pos_ref[...]
    lane = lax.broadcasted_iota(jnp.int32, (tm, LANES), 1).astype(F32)
    perm_t = jnp.concatenate(
        [(lane + r * LANES == pos).astype(BF16) for r in range(ROUTE_ROWS // LANES)], axis=1)
    x = x1_ref[...] + _dot(perm_t, ys_ref[...])
    if embed:
        gate = _sigmoid(_dot(_rms(x, g_ref[...]).astype(BF16), wpg_ref[...]))
        x = x + gate * _dot(p_ref[...].astype(BF16), wpp_ref[...])
    if final:
        x = _rms(x, gfin_ref[...])
    o_ref[...] = x


def _combine(ys, pos, x1, p2, lp, g_final, embed=True, final=False, tm=512):
    N, D = x1.shape
    inner = ROUTE_BLOCK // tm
    tok = lambda b, i: (b * inner + i, 0)
    const = lambda b, i: (0, 0)
    return pl.pallas_call(
        functools.partial(_combine_kernel, embed=embed, final=final),
        grid=(N // ROUTE_BLOCK, inner),
        in_specs=[
            pl.BlockSpec((ROUTE_ROWS, D), lambda b, i: (b, 0)),
            pl.BlockSpec((tm, LANES), tok),
            pl.BlockSpec((tm, D), tok),
            pl.BlockSpec((tm, p2.shape[1]), tok),
            pl.BlockSpec((1, D), const),
            pl.BlockSpec((D, D), const),
            pl.BlockSpec((p2.shape[1], D), const),
            pl.BlockSpec((1, D), const),
        ],
        out_specs=pl.BlockSpec((tm, D), tok),
        out_shape=jax.ShapeDtypeStruct((N, D), F32),
        compiler_params=_params("arbitrary", "arbitrary"),
        name="combine",
    )(ys, pos, x1, p2, lp["g_ple"], lp["w_pg"], lp["w_pp"], g_final)


def _mix_out_moe(x2, attn, gla_o, ml_o, lp):
    x1, xs, cws, pos, cnt = _route_dispatch(x2, attn, gla_o, ml_o, lp)
    order, grp, n_active = _tile_schedule(cnt[:, :N_GROUPS, 0])
    return x1, _experts(xs, cws, order, grp, n_active, lp), pos


def _moe_debug(x2, lp):
    N, D = x2.shape
    zeros = lambda w: jnp.zeros((N, w), BF16)
    x1, ys, pos = _mix_out_moe(x2, zeros(ATTN_WIDTH), zeros(GLA_WIDTH), zeros(MLSTM_WIDTH), lp)
    p2 = jnp.zeros((N, lp["w_pp"].shape[0]), F32)
    return _combine(ys, pos, x1, p2, lp, lp["g_ple"], embed=False) - x1


def _permute_in_cols(w):
    glr0 = ATTN_WIDTH + 2 * KV_WIDTH + 4 * GLA_WIDTH
    ml0 = glr0 + 2 * GLA_RANK
    mg0 = ml0 + 4 * MLSTM_WIDTH
    end = mg0 + 4 * MLSTM_HEADS
    assert end == w.shape[1]
    pad = jnp.zeros((w.shape[0], IN_PERM_WIDTH - end), BF16)
    parts = [w[:, :glr0], w[:, ml0:mg0], w[:, glr0:ml0], w[:, mg0:end]]
    return jnp.concatenate([part.astype(BF16) for part in parts] + [pad], axis=1)


def _rope_tables(T):
    t = jnp.arange(T, dtype=F32)
    row = jnp.floor(t / GRID_W)
    col = t - row * GRID_W
    inv = ROPE_THETA ** (-jnp.arange(0, ROPE_AXIS_DIM, 2, dtype=F32) / ROPE_AXIS_DIM)
    ang_r = row[:, None] * inv[None, :]
    ang_c = col[:, None] * inv[None, :]
    cos_h = jnp.concatenate([jnp.cos(ang_r), jnp.cos(ang_r), jnp.cos(ang_c), jnp.cos(ang_c)], axis=1)
    sin_h = jnp.concatenate([-jnp.sin(ang_r), jnp.sin(ang_r), -jnp.sin(ang_c), jnp.sin(ang_c)], axis=1)
    reps = LANES // HEAD_DIM
    cos, sin = jnp.tile(cos_h, (1, reps)), jnp.tile(sin_h, (1, reps))
    eh = np.zeros((LANES, LANES), np.float32)
    eh[np.arange(LANES), np.arange(LANES) // HEAD_DIM] = 1.0
    return dict(cos=cos, sin=sin, cos_t=cos.T, sin_t=sin.T,
                eh=jnp.asarray(eh, BF16), eht=jnp.asarray(eh.T, BF16))


def kernel(x, p, norm_mix_g, w_in, attn_q_norm_g, attn_k_norm_g, gla_w_decay, gla_b_decay,
           gla_out_norm_g, mlstm_conv_w, mlstm_conv_b, mlstm_b_input, mlstm_b_forget,
           mlstm_out_norm_g, w_out, norm_ffn_g, w_group, b_group, w_router, b_router,
           w_expert_gate, w_expert_up, w_expert_down, norm_ple_g, w_ple_gate, w_ple_proj,
           final_norm_g):
    params = dict(
        norm_mix_g=norm_mix_g, w_in=w_in, attn_q_norm_g=attn_q_norm_g, attn_k_norm_g=attn_k_norm_g,
        gla_w_decay=gla_w_decay, gla_b_decay=gla_b_decay, gla_out_norm_g=gla_out_norm_g,
        mlstm_conv_w=mlstm_conv_w, mlstm_conv_b=mlstm_conv_b, mlstm_b_input=mlstm_b_input,
        mlstm_b_forget=mlstm_b_forget, mlstm_out_norm_g=mlstm_out_norm_g, w_out=w_out,
        norm_ffn_g=norm_ffn_g, w_group=w_group, b_group=b_group, w_router=w_router,
        b_router=b_router, w_expert_gate=w_expert_gate, w_expert_up=w_expert_up,
        w_expert_down=w_expert_down, norm_ple_g=norm_ple_g, w_ple_gate=w_ple_gate,
        w_ple_proj=w_ple_proj)
    B, T, D = x.shape
    rope = _rope_tables(T)
    N = B * T
    depth = w_in.shape[0]
    x2 = x.reshape(N, D)
    g_final = final_norm_g[None, :]
    for i in range(depth):
        lp = _layer_params(params, i)
        qt, k, vt, gla, ml, small = _in_proj(x2, lp, rope, B, T)
        attn = _attention(lp["attn_safe"], qt, k, vt).reshape(N, ATTN_WIDTH)
        gla_o = _gla(gla, small, lp).reshape(N, GLA_WIDTH)
        ml_o = _mlstm(ml, small, lp).reshape(N, MLSTM_WIDTH)
        x1, ys, pos = _mix_out_moe(x2, attn, gla_o, ml_o, lp)
        x2 = _combine(ys, pos, x1, p[i].reshape(N, -1), lp, g_final, final=(i == depth - 1))
    return x2.reshape(B, T, D)


def _split_w(w):
    hi = w.astype(BF16)
    return hi, (w - hi.astype(F32)).astype(BF16)


def _stacked_experts(w):
    return w.reshape((w.shape[0] * w.shape[1],) + w.shape[2:])


def _layer_params(p, i):
    D = p["w_in"].shape[1]
    w_in = _permute_in_cols(p["w_in"][i])
    gq, gk = p["attn_q_norm_g"][i], p["attn_k_norm_g"][i]
    q_gain = jnp.tile(gq, LANES // HEAD_DIM) * (HEAD_DIM ** -0.5 * LOG2E)
    logit_bound = HEAD_DIM ** 0.5 * jnp.max(jnp.abs(gq)) * jnp.max(jnp.abs(gk))
    attn_safe = (logit_bound <= ATTN_SAFE_LOGIT).astype(jnp.int32)[None]
    wd = jnp.zeros((2, LANES, GLA_WIDTH), F32)
    wd = wd.at[0, :GLA_RANK].set(p["gla_w_decay"][i, 0]).at[1, GLA_RANK:2 * GLA_RANK].set(p["gla_w_decay"][i, 1])
    wd_hi, wd_lo = _split_w(wd)
    gate_bias = jnp.zeros((LANES,), F32).at[SMALL_GATE_LANE:SMALL_GATE_LANE + 4 * MLSTM_HEADS].set(
        jnp.concatenate([p["mlstm_b_input"][i, 0], p["mlstm_b_forget"][i, 0],
                         p["mlstm_b_input"][i, 1], p["mlstm_b_forget"][i, 1]]))
    w_route = jnp.zeros((D, LANES), F32)
    w_route = w_route.at[:, :N_GROUPS].set(p["w_group"][i])
    w_route = w_route.at[:, ROUTE_W_LANE:ROUTE_W_LANE + N_EXPERTS].set(p["w_router"][i])
    wr_hi, wr_lo = _split_w(w_route)
    b_route = jnp.zeros((LANES,), F32).at[:N_GROUPS].set(p["b_group"][i])
    b_route = b_route.at[ROUTE_W_LANE:ROUTE_W_LANE + N_EXPERTS].set(p["b_router"][i])
    return dict(
        g_mix=p["norm_mix_g"][i][None, :],
        w_in=w_in,
        q_gain_t=jnp.broadcast_to(q_gain[:, None], (LANES, LANES)),
        k_gain=jnp.tile(gk, LANES // HEAD_DIM)[None, :],
        attn_safe=attn_safe,
        wd_hi=wd_hi, wd_lo=wd_lo,
        bd=p["gla_b_decay"][i][:, None, :],
        gla_gain=jnp.tile(p["gla_out_norm_g"][i], LANES // HEAD_DIM)[None, :],
        conv_w=p["mlstm_conv_w"][i],
        conv_b=p["mlstm_conv_b"][i][None, :],
        gate_bias=gate_bias[None, :],
        ml_gain=jnp.tile(p["mlstm_out_norm_g"][i], LANES // HEAD_DIM)[None, :],
        w_out=p["w_out"][i].astype(BF16),
        g_ffn=p["norm_ffn_g"][i][None, :],
        w_route=jnp.concatenate([wr_hi, wr_lo], axis=1), b_route=b_route[None, :],
        layer=i,
        w_gate=_stacked_experts(p["w_expert_gate"]),
        w_up=_stacked_experts(p["w_expert_up"]),
        w_down=_stacked_experts(p["w_expert_down"]),
        g_ple=p["norm_ple_g"][i][None, :],
        w_pg=p["w_ple_gate"][i].astype(BF16),
        w_pp=p["w_ple_proj"][i].astype(BF16),
    )
```

```python
import functools

import jax
import jax.numpy as jnp
import numpy as np
from jax import lax
from jax.experimental import pallas as pl
from jax.experimental.pallas import tpu as pltpu

F32 = jnp.float32
BF16 = jnp.bfloat16

GRID_W = 64
HEAD_DIM = 64
ATTN_HEADS = 8
ATTN_KV_HEADS = 2
GLA_HEADS = 4
MLSTM_HEADS = 4
ATTN_WIDTH = ATTN_HEADS * HEAD_DIM
KV_WIDTH = ATTN_KV_HEADS * HEAD_DIM
GLA_WIDTH = GLA_HEADS * HEAD_DIM
MLSTM_WIDTH = MLSTM_HEADS * HEAD_DIM
GLA_RANK = 16
GLA_TAU = 16.0
CHUNK = 64
ROPE_THETA = 10000.0
ROPE_AXIS_DIM = HEAD_DIM // 2
N_GROUPS = 4
EXPERTS_PER_GROUP = 4
N_EXPERTS = N_GROUPS * EXPERTS_PER_GROUP
D_EXPERT = 512
EPS = 1e-6
NEG = -1e30

LANES = 128
SUBLANES = 8
VMEM_LIMIT_BYTES = 56 * 1024 * 1024

QK_WIDTH = ATTN_WIDTH + KV_WIDTH
OFF_V = QK_WIDTH
OFF_GLA = OFF_V + KV_WIDTH
OFF_ML = OFF_GLA + 4 * GLA_WIDTH
OFF_SMALL = OFF_ML + 4 * MLSTM_WIDTH
IN_PERM_WIDTH = OFF_SMALL + LANES
SMALL_GATE_LANE = 2 * GLA_RANK

ROUTE_BLOCK = 1024
ROUTE_TILE = 128
TILES_PER_BLOCK = (ROUTE_BLOCK + N_GROUPS * (ROUTE_TILE - 1)) // ROUTE_TILE
ROUTE_ROWS = TILES_PER_BLOCK * ROUTE_TILE
ROUTE_W_LANE = 8
ROUTE_LO_SHIFT = 32
ROUTE_SUBTILE = 256


def _dot(a, b):
    return jnp.dot(a, b, preferred_element_type=F32)


def _dot_nt(a, b):
    return lax.dot_general(a, b, (((1,), (1,)), ((), ())), preferred_element_type=F32)


def _dot_tn(a, b):
    return lax.dot_general(a, b, (((0,), (0,)), ((), ())), preferred_element_type=F32)


def _split(a):
    hi = a.astype(BF16)
    lo = (a - hi.astype(F32)).astype(BF16)
    return hi, lo


def _dot3(a, w_hi, w_lo):
    a_hi, a_lo = _split(a)
    return _dot(a_hi, w_hi) + _dot(a_lo, w_hi) + _dot(a_hi, w_lo)


def _log_sigmoid(x):
    return jnp.minimum(x, 0.0) - jnp.log(1.0 + jnp.exp(-jnp.abs(x)))


def _sigmoid(x):
    return 1.0 / (1.0 + jnp.exp(-x))


def _rms(x, g):
    return x * lax.rsqrt(jnp.mean(x * x, axis=-1, keepdims=True) + EPS) * g


def _params(*semantics):
    return pltpu.CompilerParams(dimension_semantics=semantics, vmem_limit_bytes=VMEM_LIMIT_BYTES)


def _in_proj_kernel(x_ref, g_ref, w_ref, cost_ref, sint_ref, gq_ref, gk_ref,
                    qt_ref, k_ref, vt_ref, gla_ref, ml_ref, small_ref):
    tm = x_ref.shape[0]
    h = _rms(x_ref[...], g_ref[...])
    z = _dot(h.astype(BF16), w_ref[...])

    heads = LANES // HEAD_DIM
    half = ROPE_AXIS_DIM // 2
    gq = jnp.concatenate([gq_ref[...]] * (tm // LANES), axis=1)
    cost = cost_ref[...]
    sint = sint_ref[...]
    gk = jnp.concatenate([gk_ref[...]] * (tm // LANES), axis=1)

    def norm_rope_t(c, gain):
        zt = z[:, c * LANES:(c + 1) * LANES].T
        z3 = zt.reshape(heads, HEAD_DIM, tm)
        inv = lax.rsqrt(jnp.mean(z3 * z3, axis=1, keepdims=True) + EPS)
        y = (z3 * inv).reshape(LANES, tm) * gain
        partner = jnp.concatenate(
            [y[(r ^ 1) * half:((r ^ 1) + 1) * half, :] for r in range(LANES // half)], axis=0)
        return y * cost + partner * sint

    for c in range(ATTN_WIDTH // LANES):
        qt_ref[0, c * LANES:(c + 1) * LANES, :] = norm_rope_t(c, gq).astype(BF16)
    k_ref[0] = norm_rope_t(ATTN_WIDTH // LANES, gk).T.astype(BF16)
    vt_ref[0] = z[:, OFF_V:OFF_V + KV_WIDTH].T.astype(BF16)
    gla_ref[0] = z[:, OFF_GLA:OFF_ML]
    ml_ref[0] = z[:, OFF_ML:OFF_SMALL]
    small_ref[0] = z[:, OFF_SMALL:IN_PERM_WIDTH]


def _in_proj(x2, lp, rope, B, T, tm=512):
    N, D = x2.shape
    tpb = T // tm
    const = lambda i: (0, 0)
    tok3 = lambda i: (i // tpb, i % tpb, 0)
    tokT = lambda i: (i // tpb, 0, i % tpb)
    return pl.pallas_call(
        _in_proj_kernel,
        grid=(N // tm,),
        in_specs=[
            pl.BlockSpec((tm, D), lambda i: (i, 0)),
            pl.BlockSpec((1, D), const),
            pl.BlockSpec((D, IN_PERM_WIDTH), const),
            pl.BlockSpec((LANES, tm), lambda i: (0, i % tpb)),
            pl.BlockSpec((LANES, tm), lambda i: (0, i % tpb)),
            pl.BlockSpec((LANES, LANES), const),
            pl.BlockSpec((LANES, LANES), const),
        ],
        out_specs=[
            pl.BlockSpec((1, ATTN_WIDTH, tm), tokT),
            pl.BlockSpec((1, tm, KV_WIDTH), tok3),
            pl.BlockSpec((1, KV_WIDTH, tm), tokT),
            pl.BlockSpec((1, tm, 4 * GLA_WIDTH), tok3),
            pl.BlockSpec((1, tm, 4 * MLSTM_WIDTH), tok3),
            pl.BlockSpec((1, tm, LANES), tok3),
        ],
        out_shape=[
            jax.ShapeDtypeStruct((B, ATTN_WIDTH, T), BF16),
            jax.ShapeDtypeStruct((B, T, KV_WIDTH), BF16),
            jax.ShapeDtypeStruct((B, KV_WIDTH, T), BF16),
            jax.ShapeDtypeStruct((B, T, 4 * GLA_WIDTH), F32),
            jax.ShapeDtypeStruct((B, T, 4 * MLSTM_WIDTH), F32),
            jax.ShapeDtypeStruct((B, T, LANES), F32),
        ],
        compiler_params=_params("arbitrary"),
        name="in_proj",
    )(x2, lp["g_mix"], lp["w_in"], rope["cos_t"], rope["sin_t"], lp["q_gain_t"], lp["k_gain_t"])


ATTN_SAFE_LOGIT = 40.0
LOG2E = 1.4426950408889634


def _attn_kernel(safe_ref, qt_ref, k_ref, vt_ref, o_ref, *, tk):
    tq = qt_ref.shape[2]
    T = k_ref.shape[1]
    G = ATTN_HEADS // ATTN_KV_HEADS
    n = G * tq
    zeros = jnp.zeros((HEAD_DIM, n), BF16)

    def q_operand(j):
        base = j * G * HEAD_DIM
        qs = jnp.concatenate(
            [qt_ref[0, base + h * HEAD_DIM:base + (h + 1) * HEAD_DIM, :] for h in range(G)], axis=1)
        return jnp.concatenate([qs, zeros] if j == 0 else [zeros, qs], axis=0)

    def finish(j, acc, l):
        base = j * G * HEAD_DIM
        o = acc * (1.0 / l)
        ot = jnp.concatenate([o[:, h * tq:(h + 1) * tq] for h in range(G)], axis=0)
        o_ref[0, :, base:base + G * HEAD_DIM] = ot.T.astype(BF16)

    @pl.when(safe_ref[0] == 1)
    def _():
        qps = [q_operand(j) for j in range(ATTN_KV_HEADS)]
        units = [(c, j) for c in range(T // tk) for j in range(ATTN_KV_HEADS)]

        def scores(u):
            c, j = units[u]
            return _dot(k_ref[0, c * tk:(c + 1) * tk, :], qps[j])

        l8 = [jnp.zeros((SUBLANES, n), F32)] * ATTN_KV_HEADS
        acc = [jnp.zeros((HEAD_DIM, n), F32)] * ATTN_KV_HEADS
        st = scores(0)
        for u, (c, j) in enumerate(units):
            st_next = scores(u + 1) if u + 1 < len(units) else None
            p = jnp.exp2(st)
            l8[j] = l8[j] + jnp.sum(p.reshape(tk // SUBLANES, SUBLANES, n), axis=0)
            vc = vt_ref[0, j * HEAD_DIM:(j + 1) * HEAD_DIM, c * tk:(c + 1) * tk]
            acc[j] = acc[j] + _dot(vc, p.astype(BF16))
            st = st_next
        for j in range(ATTN_KV_HEADS):
            finish(j, acc[j], jnp.sum(l8[j], axis=0, keepdims=True))

    @pl.when(safe_ref[0] == 0)
    def _():
        for j in range(ATTN_KV_HEADS):
            qp = q_operand(j)

            def body(c, carry, qp=qp, j=j):
                m, l, acc = carry
                off = pl.multiple_of(c * tk, tk)
                st = _dot(k_ref[0, pl.ds(off, tk), :], qp)
                m_new = jnp.maximum(m, jnp.max(st, axis=0, keepdims=True))
                alpha = jnp.exp2(m - m_new)
                p = jnp.exp2(st - m_new)
                l = alpha * l + jnp.sum(p, axis=0, keepdims=True)
                vc = vt_ref[0, j * HEAD_DIM:(j + 1) * HEAD_DIM, pl.ds(off, tk)]
                return m_new, l, alpha * acc + _dot(vc, p.astype(BF16))

            init = (jnp.full((1, n), NEG, F32), jnp.zeros((1, n), F32), jnp.zeros((HEAD_DIM, n), F32))
            _, l, acc = lax.fori_loop(0, T // tk, body, init)
            finish(j, acc, l)


def _attention(safe, qt, k, vt, tq=256, tk=128):
    B, _, T = qt.shape
    tk = min(tk, T)
    return pl.pallas_call(
        functools.partial(_attn_kernel, tk=tk),
        grid_spec=pltpu.PrefetchScalarGridSpec(
            num_scalar_prefetch=1,
            grid=(B, T // tq),
            in_specs=[
                pl.BlockSpec((1, ATTN_WIDTH, tq), lambda b, i, s: (b, 0, i)),
                pl.BlockSpec((1, T, KV_WIDTH), lambda b, i, s: (b, 0, 0)),
                pl.BlockSpec((1, KV_WIDTH, T), lambda b, i, s: (b, 0, 0)),
            ],
            out_specs=pl.BlockSpec((1, tq, ATTN_WIDTH), lambda b, i, s: (b, i, 0)),
        ),
        out_shape=jax.ShapeDtypeStruct((B, T, ATTN_WIDTH), BF16),
        compiler_params=_params("arbitrary", "arbitrary"),
        name="attention",
    )(safe, qt, k, vt)


def _chunk_scan(x, pos, op, fill, reverse):
    rows = x.shape[0]
    s = 1
    while s < CHUNK:
        if reverse:
            shifted = jnp.where(pos < CHUNK - s, pltpu.roll(x, rows - s, 0), fill)
        else:
            shifted = jnp.where(pos >= s, pltpu.roll(x, s, 0), fill)
        x = op(x, shifted)
        s *= 2
    return x


def _chunk_pos(rows):
    return lax.broadcasted_iota(jnp.int32, (rows, LANES), 0) % CHUNK


def _chunk_row(a, reverse_dir, idx_fwd, idx_bwd):
    rows = a.shape[0]
    a3 = a.reshape(rows // CHUNK, CHUNK, LANES)
    i = idx_bwd if reverse_dir else idx_fwd
    return jnp.broadcast_to(a3[:, i:i + 1, :], a3.shape).reshape(rows, LANES)


def _stack_heads(x):
    lane = lax.broadcasted_iota(jnp.int32, x.shape, 1)
    zero = jnp.zeros_like(x)
    return jnp.concatenate([jnp.where(lane < HEAD_DIM, x, zero), jnp.where(lane >= HEAD_DIM, x, zero)], axis=0)


def _select_heads(x):
    c = x.shape[0] // 2
    lane = lax.broadcasted_iota(jnp.int32, (c, x.shape[1]), 1)
    return jnp.where(lane < HEAD_DIM, x[:c], x[c:])


def _stacked_causal_masks():
    ci = lax.broadcasted_iota(jnp.int32, (2 * CHUNK, CHUNK), 0) % CHUNK
    si = lax.broadcasted_iota(jnp.int32, (2 * CHUNK, CHUNK), 1)
    return ci >= si, ci <= si


def _pair_blockdiag(width):
    r = lax.broadcasted_iota(jnp.int32, (LANES, width), 0) // HEAD_DIM
    c = lax.broadcasted_iota(jnp.int32, (LANES, width), 1) // (width // 2)
    return r == c


def _head_rms(o, gain):
    lane = lax.broadcasted_iota(jnp.int32, o.shape, 1)
    lo = lane < HEAD_DIM
    sq = o * o
    s_lo = jnp.sum(jnp.where(lo, sq, 0.0), axis=-1, keepdims=True)
    s_hi = jnp.sum(jnp.where(lo, 0.0, sq), axis=-1, keepdims=True)
    ms = jnp.where(lo, s_lo, s_hi) * (1.0 / HEAD_DIM)
    return o * lax.rsqrt(ms + EPS) * gain


PREP_ROWS = 512
GLA_CHUNKS_PER_STEP = 4


def _gla_kernel(q_ref, k_ref, v_ref, g_ref, small_ref, wdh_ref, wdl_ref, bd_ref, gain_ref,
                o_ref, qe_s, ke_s, dec_s, of_s, ob_s):
    T = q_ref.shape[1]
    nc = T // CHUNK
    R = min(PREP_ROWS, T)
    cpt = R // CHUNK
    pos = _chunk_pos(R)
    w_hi = jnp.concatenate([wdh_ref[0], wdh_ref[1]], axis=1)
    w_lo = jnp.concatenate([wdl_ref[0], wdl_ref[1]], axis=1)
    bias = jnp.concatenate([bd_ref[0], bd_ref[1]], axis=1)

    def prep(t, _):
        r0 = pl.multiple_of(t * R, R)
        q = q_ref[0, pl.ds(r0, R), :] * HEAD_DIM ** -0.5
        k = k_ref[0, pl.ds(r0, R), :]
        la2 = _log_sigmoid(_dot3(small_ref[0, pl.ds(r0, R), :], w_hi, w_lo) + bias) * (1.0 / GLA_TAU)
        for d in range(2):
            b = _chunk_scan(la2[:, d * LANES:(d + 1) * LANES], pos, jnp.add, 0.0, reverse=bool(d))
            b_mid = _chunk_row(b, d, CHUNK // 2 - 1, CHUNK // 2)
            b_last = _chunk_row(b, d, CHUNK - 1, 0)
            qe_s[d, pl.ds(r0, R), :] = (q * jnp.exp(b - b_mid)).astype(BF16)
            ke_s[d, pl.ds(r0, R), :] = (k * jnp.exp(b_mid - b)).astype(BF16)
            c8 = pl.ds(pl.multiple_of(t * cpt * SUBLANES, SUBLANES), cpt * SUBLANES)
            for kind, val in enumerate((b_last, b_last - b_mid, b_mid)):
                rows8 = val.reshape(cpt, CHUNK, LANES)[:, :SUBLANES, :].reshape(cpt * SUBLANES, LANES)
                dec_s[d, kind, c8, :] = jnp.exp(rows8)
        return 0

    lax.fori_loop(0, T // R, prep, 0)

    outs = (of_s, ob_s)
    masks = _stacked_causal_masks()
    blockdiag = _pair_blockdiag(LANES)
    G = GLA_CHUNKS_PER_STEP

    def step(it, states):
        units = [(d, g) for d in range(2) for g in range(G)]
        ops = {}
        for d, g in units:
            c = it * G + g
            c = (nc - 1 - c) if d else c
            r0 = pl.multiple_of(c * CHUNK, CHUNK)
            c8 = pl.ds(pl.multiple_of(c * SUBLANES, SUBLANES), SUBLANES)
            ops[d, g] = dict(
                r0=r0,
                qe=qe_s[d, pl.ds(r0, CHUNK), :], ke=ke_s[d, pl.ds(r0, CHUNK), :],
                v=v_ref[0, pl.ds(r0, CHUNK), :].astype(BF16),
                dec=dec_s[d, 0, c8, :][0:1, :], to_end=dec_s[d, 1, c8, :][0:1, :],
                from_start=dec_s[d, 2, c8, :][0:1, :])
        a2, kv = {}, {}
        for u in units:
            o = ops[u]
            a2[u] = _dot_nt(_stack_heads(o["qe"]), o["ke"])
            kv[u] = _dot_tn(o["v"], o["ke"]) * o["to_end"]
        st_in = {}
        new_states = []
        for d in range(2):
            st = states[d]
            for g in range(G):
                st_in[d, g] = (st * ops[d, g]["from_start"]).astype(BF16)
                st = st * ops[d, g]["dec"] + jnp.where(blockdiag, kv[d, g], 0.0)
            new_states.append(st)
        for u in units:
            o = ops[u]
            a = jnp.where(masks[u[0]], a2[u], 0.0).astype(BF16)
            intra = _select_heads(_dot(a, o["v"]))
            outs[u[0]][pl.ds(o["r0"], CHUNK), :] = intra + _dot_nt(o["qe"], st_in[u])
        return tuple(new_states)

    zero = jnp.zeros((LANES, LANES), F32)
    lax.fori_loop(0, nc // G, step, (zero, zero))

    def post(t, _):
        r0 = pl.multiple_of(t * R, R)
        o = _head_rms(of_s[pl.ds(r0, R), :] + ob_s[pl.ds(r0, R), :], gain_ref[...])
        g = g_ref[0, pl.ds(r0, R), :]
        o_ref[0, pl.ds(r0, R), :] = (o * (g * _sigmoid(g))).astype(BF16)
        return 0

    lax.fori_loop(0, T // R, post, 0)


def _gla(gla, small, lp):
    B, T, _ = gla.shape
    pairs = GLA_WIDTH // LANES
    col = lambda off: (lambda b, hp: (b, 0, off * pairs + hp))
    return pl.pallas_call(
        _gla_kernel,
        grid=(B, pairs),
        in_specs=[
            pl.BlockSpec((1, T, LANES), col(0)),
            pl.BlockSpec((1, T, LANES), col(1)),
            pl.BlockSpec((1, T, LANES), col(2)),
            pl.BlockSpec((1, T, LANES), col(3)),
            pl.BlockSpec((1, T, LANES), lambda b, hp: (b, 0, 0)),
            pl.BlockSpec((2, LANES, LANES), lambda b, hp: (0, 0, hp)),
            pl.BlockSpec((2, LANES, LANES), lambda b, hp: (0, 0, hp)),
            pl.BlockSpec((2, 1, LANES), lambda b, hp: (0, 0, hp)),
            pl.BlockSpec((1, LANES), lambda b, hp: (0, 0)),
        ],
        out_specs=pl.BlockSpec((1, T, LANES), lambda b, hp: (b, 0, hp)),
        out_shape=jax.ShapeDtypeStruct((B, T, GLA_WIDTH), BF16),
        scratch_shapes=[
            pltpu.VMEM((2, T, LANES), BF16), pltpu.VMEM((2, T, LANES), BF16),
            pltpu.VMEM((2, 3, (T // CHUNK) * SUBLANES, LANES), F32),
            pltpu.VMEM((T, LANES), F32), pltpu.VMEM((T, LANES), F32),
        ],
        compiler_params=_params("arbitrary", "arbitrary"),
        name="gla",
    )(gla, gla, gla, gla, small, lp["wd_hi"], lp["wd_lo"], lp["bd"], lp["gla_gain"])


GATE_I = SMALL_GATE_LANE
GATE_F = SMALL_GATE_LANE + MLSTM_HEADS


def _gate_lane(d, hh):
    return GATE_F + 2 * MLSTM_HEADS * d + hh


MLSTM_CHUNKS_PER_STEP = 8


def _chunk_rows8(a, row):
    n = a.shape[0] // CHUNK
    a3 = a.reshape(n, CHUNK, LANES)
    full = jnp.broadcast_to(a3[:, row:row + 1, :], a3.shape)
    return full[:, :SUBLANES, :].reshape(n * SUBLANES, LANES)


def _expand_rows8(a8):
    n, w = a8.shape[0] // SUBLANES, a8.shape[1]
    a3 = a8.reshape(n, SUBLANES, w)[:, 0:1, :]
    return jnp.broadcast_to(a3, (n, CHUNK, w)).reshape(n * CHUNK, w)


def _gate_select():
    src = lax.broadcasted_iota(jnp.int32, (LANES, 2 * LANES), 0)
    dst = lax.broadcasted_iota(jnp.int32, (LANES, 2 * LANES), 1)
    want = GATE_F + 2 * MLSTM_HEADS * (dst // LANES) + (dst % LANES) // HEAD_DIM
    return jnp.where(src == want, 1.0, 0.0).astype(BF16)


def _gate_broadcast(x, sel, gate_mask, pieces=2):
    x = jnp.where(gate_mask, x, 0.0)
    if pieces == 1:
        return _dot(x.astype(BF16), sel)
    hi, lo = _split(x)
    return _dot(hi, sel) + _dot(lo, sel)


def _mlstm_kernel(q_ref, k_ref, v_ref, og_ref, small_ref, wq_ref, wk_ref, bq_ref, bk_ref,
                  gbias_ref, gain_ref, o_ref, q_s, k_s, b_s, r_s, cm_s, fl1_s, bl_s, rl_s, mf_s, mb_s,
                  wp_s, qi_s, kw_s, rt_s, st_s, of_s, ob_s):
    T = q_ref.shape[1]
    nc = T // CHUNK
    R = min(PREP_ROWS, T)
    hp = pl.program_id(1)
    G = MLSTM_CHUNKS_PER_STEP
    cpt = R // CHUNK

    row = lax.broadcasted_iota(jnp.int32, (T, LANES), 0)
    for src, w_ref, b_ref, dst, scale in ((q_ref, wq_ref, bq_ref, q_s, 1.0),
                                          (k_ref, wk_ref, bk_ref, k_s, HEAD_DIM ** -0.5)):
        xc = src[0]
        prev = jnp.where(row >= 1, pltpu.roll(xc, 1, 0), 0.0)
        nxt = jnp.where(row < T - 1, pltpu.roll(xc, T - 1, 0), 0.0)
        y = prev * w_ref[0:1, :] + xc * w_ref[1:2, :] + nxt * w_ref[2:3, :] + b_ref[...]
        dst[...] = (y * _sigmoid(y) * scale).astype(BF16)

    pos = _chunk_pos(R)
    lane = lax.broadcasted_iota(jnp.int32, (R, LANES), 1)
    is_bwd = lane >= GATE_I + 2 * MLSTM_HEADS
    is_bwd8 = lax.broadcasted_iota(jnp.int32, (cpt * SUBLANES, LANES), 1) >= GATE_I + 2 * MLSTM_HEADS
    heads_per_pair = LANES // HEAD_DIM
    shift = (LANES - heads_per_pair * hp) % LANES
    gate_lanes = [_gate_lane(d, hh) for d in range(2) for hh in range(2)]
    gate_mask = functools.reduce(jnp.logical_or, [lane == l for l in gate_lanes])
    lane8 = lax.broadcasted_iota(jnp.int32, (cpt * SUBLANES, LANES), 1)
    gate_mask8 = functools.reduce(jnp.logical_or, [lane8 == l for l in gate_lanes])

    def prep(t, _):
        r0 = pl.multiple_of(t * R, R)
        gc = pltpu.roll(small_ref[0, pl.ds(r0, R), :] + gbias_ref[...], shift, 1)
        logf = _log_sigmoid(gc)
        pre = _chunk_scan(logf, pos, jnp.add, 0.0, False)
        b = jnp.where(is_bwd, _chunk_row(pre, 0, CHUNK - 1, CHUNK - 1) - pre + logf, pre)
        r = pltpu.roll(gc, MLSTM_HEADS, 1) - b
        cm = jnp.where(is_bwd, _chunk_scan(r, pos, jnp.maximum, NEG, True),
                       _chunk_scan(r, pos, jnp.maximum, NEG, False))
        b_s[pl.ds(r0, R), :] = b
        r_s[pl.ds(r0, R), :] = r
        cm_s[pl.ds(r0, R), :] = cm
        rt_s[:, pl.ds(r0, R)] = r.T
        c8 = pl.ds(pl.multiple_of(t * cpt * SUBLANES, SUBLANES), cpt * SUBLANES)
        bl_s[c8, :] = jnp.where(is_bwd8, _chunk_rows8(b, 0), _chunk_rows8(b, CHUNK - 1))
        rl_s[c8, :] = jnp.where(is_bwd8, _chunk_rows8(cm, 0), _chunk_rows8(cm, CHUNK - 1))
        return 0

    lax.fori_loop(0, T // R, prep, 0)

    def m_chain(n, carry):
        mf, mb = carry
        rf = pl.ds(pl.multiple_of(n * SUBLANES, SUBLANES), SUBLANES)
        rb = pl.ds(pl.multiple_of((nc - 1 - n) * SUBLANES, SUBLANES), SUBLANES)
        mf_s[rf, :] = mf
        mb_s[rb, :] = mb
        return (bl_s[rf, :] + jnp.maximum(mf, rl_s[rf, :]), bl_s[rb, :] + jnp.maximum(mb, rl_s[rb, :]))

    m0 = jnp.full((SUBLANES, LANES), NEG, F32)
    lax.fori_loop(0, nc, m_chain, (m0, m0))

    expo_s, floor_s = (b_s, cm_s), (r_s, fl1_s)
    sel = _gate_select()

    def weights(t, _):
        r0 = pl.multiple_of(t * R, R)
        c8 = pl.ds(pl.multiple_of(t * cpt * SUBLANES, SUBLANES), cpt * SUBLANES)
        rows = pl.ds(r0, R)
        m_in8 = jnp.where(is_bwd8, mb_s[c8, :], mf_s[c8, :])
        bl8 = bl_s[c8, :]
        m_out8 = bl8 + jnp.maximum(m_in8, rl_s[c8, :])
        m_in = _expand_rows8(m_in8)
        mx = jnp.maximum(m_in, cm_s[rows, :])
        wp_b = jnp.exp(_gate_broadcast(bl8 + m_in8 - m_out8, sel, gate_mask8))
        mx_b = _gate_broadcast(mx, sel, gate_mask)
        floor = jnp.exp(-_gate_broadcast(b_s[rows, :] + mx, sel, gate_mask))
        w_inter = _gate_broadcast(jnp.exp(m_in - mx), sel, gate_mask, pieces=1)
        wk = _gate_broadcast(jnp.exp(_expand_rows8(bl8 - m_out8) + r_s[rows, :]), sel, gate_mask, pieces=1)
        q = q_s[rows, :].astype(F32)
        k = k_s[rows, :].astype(F32)
        for d in range(2):
            half = slice(d * LANES, (d + 1) * LANES)
            qi_s[d, rows, :] = (q * w_inter[:, half]).astype(BF16)
            kw_s[d, rows, :] = (k * wk[:, half]).astype(BF16)
            wp_s[d, c8, :] = wp_b[:, half]
            expo_s[d][rows, :] = -mx_b[:, half]
            floor_s[d][rows, :] = floor[:, half]
        return 0

    lax.fori_loop(0, T // R, weights, 0)

    outs = (of_s, ob_s)
    ci = lax.broadcasted_iota(jnp.int32, (CHUNK, LANES), 0)
    si = lax.broadcasted_iota(jnp.int32, (CHUNK, LANES), 1) % CHUNK
    masks = (ci >= si, ci <= si)
    br = lax.broadcasted_iota(jnp.int32, (LANES, 2 * LANES), 0) // HEAD_DIM
    bc = (lax.broadcasted_iota(jnp.int32, (LANES, 2 * LANES), 1) % LANES) // HEAD_DIM
    blockdiag = br == bc
    ones = jnp.ones((CHUNK, LANES), BF16)
    st_s[...] = jnp.zeros(st_s.shape, F32)
    gate_rows = slice(GATE_I, GATE_I + 4 * MLSTM_HEADS)

    def step(it, _):
        units = [(d, g) for d in range(2) for g in range(G)]
        ops = {}
        for d in range(2):
            first = (nc - (it + 1) * G) if d else it * G
            rt = rt_s[gate_rows, pl.ds(pl.multiple_of(first * CHUNK, G * CHUNK), G * CHUNK)]
            for g in range(G):
                local = (G - 1 - g) if d else g
                c = first + local
                rows = pl.ds(pl.multiple_of(c * CHUNK, CHUNK), CHUNK)
                c8 = pl.ds(pl.multiple_of(c * SUBLANES, SUBLANES), SUBLANES)
                r_row = jnp.concatenate(
                    [rt[_gate_lane(d, hh) - GATE_I:_gate_lane(d, hh) - GATE_I + 1,
                        local * CHUNK:(local + 1) * CHUNK] for hh in range(2)], axis=1)
                wp = wp_s[d, c8, :][0:1, :]
                ops[d, g] = dict(
                    rows=rows, q=q_s[rows, :], k=k_s[rows, :], qi=qi_s[d, rows, :], kw=kw_s[d, rows, :],
                    va=jnp.concatenate([v_ref[0, rows, :].astype(BF16), ones], axis=1),
                    expo=expo_s[d][rows, :] + r_row, floor=floor_s[d][rows, :],
                    wp=jnp.concatenate([wp, wp], axis=1))
        qk, kv = {}, {}
        for u in units:
            o = ops[u]
            qk[u] = _dot_nt(o["q"], _stack_heads(o["k"]))
            kv[u] = _dot_tn(o["kw"], o["va"])
        st_in = {}
        for d in range(2):
            st = st_s[d]
            for g in range(G):
                st_in[d, g] = st.astype(BF16)
                st = st * ops[d, g]["wp"] + jnp.where(blockdiag, kv[d, g], 0.0)
            st_s[d] = st
        for u in units:
            d = u[0]
            o = ops[u]
            smat = (qk[u] * jnp.where(masks[d], jnp.exp(o["expo"]), 0.0)).astype(BF16)
            va_bd = jnp.where(blockdiag, jnp.concatenate([o["va"], o["va"]], axis=0), jnp.zeros((), BF16))
            num = _dot(jnp.concatenate([o["qi"], smat], axis=1), jnp.concatenate([st_in[u], va_bd], axis=0))
            den = jnp.maximum(jnp.abs(num[:, LANES:]), o["floor"])
            outs[d][o["rows"], :] = num[:, :LANES] / den
        return 0

    lax.fori_loop(0, nc // G, step, 0)

    def post(t, _):
        r0 = pl.multiple_of(t * R, R)
        h = _head_rms(of_s[pl.ds(r0, R), :] + ob_s[pl.ds(r0, R), :], gain_ref[...])
        o_ref[0, pl.ds(r0, R), :] = (h * _sigmoid(og_ref[0, pl.ds(r0, R), :])).astype(BF16)
        return 0

    lax.fori_loop(0, T // R, post, 0)


def _mlstm(ml, small, lp):
    B, T, _ = ml.shape
    pairs = MLSTM_WIDTH // LANES
    col = lambda off: (lambda b, hp: (b, 0, off * pairs + hp))
    const = lambda b, hp: (0, 0)
    return pl.pallas_call(
        _mlstm_kernel,
        grid=(B, pairs),
        in_specs=[
            pl.BlockSpec((1, T, LANES), col(0)),
            pl.BlockSpec((1, T, LANES), col(1)),
            pl.BlockSpec((1, T, LANES), col(2)),
            pl.BlockSpec((1, T, LANES), col(3)),
            pl.BlockSpec((1, T, LANES), lambda b, hp: (b, 0, 0)),
            pl.BlockSpec((3, LANES), lambda b, hp: (0, hp)),
            pl.BlockSpec((3, LANES), lambda b, hp: (0, pairs + hp)),
            pl.BlockSpec((1, LANES), lambda b, hp: (0, hp)),
            pl.BlockSpec((1, LANES), lambda b, hp: (0, pairs + hp)),
            pl.BlockSpec((1, LANES), const),
            pl.BlockSpec((1, LANES), const),
        ],
        out_specs=pl.BlockSpec((1, T, LANES), lambda b, hp: (b, 0, hp)),
        out_shape=jax.ShapeDtypeStruct((B, T, MLSTM_WIDTH), BF16),
        scratch_shapes=(
            [pltpu.VMEM((T, LANES), BF16)] * 2
            + [pltpu.VMEM((T, LANES), F32)] * 4
            + [pltpu.VMEM(((T // CHUNK) * SUBLANES, LANES), F32)] * 4
            + [pltpu.VMEM((2, (T // CHUNK) * SUBLANES, LANES), F32)]
            + [pltpu.VMEM((2, T, LANES), BF16)] * 2
            + [pltpu.VMEM((LANES, T), F32),
               pltpu.VMEM((2, LANES, 2 * LANES), F32),
               pltpu.VMEM((T, LANES), F32), pltpu.VMEM((T, LANES), F32)]),
        compiler_params=_params("arbitrary", "arbitrary"),
        name="mlstm",
    )(ml, ml, ml, ml, small, lp["conv_w"], lp["conv_w"], lp["conv_b"], lp["conv_b"],
      lp["gate_bias"], lp["ml_gain"])


def _first_argmax(vals, row):
    mx = jnp.max(vals, axis=0, keepdims=True)
    idx = jnp.min(jnp.where(vals == mx, row, vals.shape[0]), axis=0, keepdims=True)
    return mx, idx


def _out_route_kernel(x_ref, a_ref, gl_ref, ml_ref, w_ref, g_ref, wr_ref, br_ref,
                      x1_ref, t_ref, route_ref, routet_ref):
    tm = x_ref.shape[0]
    sub = min(tm, ROUTE_SUBTILE)
    spans = [slice(r, r + sub) for r in range(0, tm, sub)]
    x1s, logits = [], []
    for rs in spans:
        x1 = (x_ref[rs, :]
              + _dot(a_ref[rs, :], w_ref[0:ATTN_WIDTH, :])
              + _dot(gl_ref[rs, :], w_ref[ATTN_WIDTH:ATTN_WIDTH + GLA_WIDTH, :])
              + _dot(ml_ref[rs, :], w_ref[ATTN_WIDTH + GLA_WIDTH:, :]))
        x1_ref[rs, :] = x1
        x1s.append(x1)
    for rs, x1 in zip(spans, x1s):
        t_hi, t_lo = _split(_rms(x1, g_ref[...]))
        t_ref[rs, 0:t_hi.shape[1]] = t_hi
        both = _dot(t_hi, wr_ref[...])
        logits.append((both[:, :LANES] + both[:, LANES:] + _dot(t_lo, wr_ref[:, :LANES])) + br_ref[...])
    grow = lax.broadcasted_iota(jnp.int32, (SUBLANES, sub), 0)
    erow = lax.broadcasted_iota(jnp.int32, (N_EXPERTS, sub), 0)
    for rs, lg in zip(spans, logits):
        lt = lg.T
        gl = jnp.where(grow < N_GROUPS, lt[0:SUBLANES, :], -jnp.inf)
        gmax, gi = _first_argmax(gl, grow)
        g_prob = 1.0 / jnp.sum(jnp.exp(gl - gmax), axis=0, keepdims=True)
        el = jnp.where(erow // EXPERTS_PER_GROUP == gi, lt[ROUTE_W_LANE:ROUTE_W_LANE + N_EXPERTS, :], -jnp.inf)
        v1, i1 = _first_argmax(el, erow)
        v2, i2 = _first_argmax(jnp.where(erow == i1, -jnp.inf, el), erow)
        e2 = jnp.exp(v2 - v1)
        w1 = g_prob / (1.0 + e2)
        w2 = g_prob * e2 / (1.0 + e2)
        comb = jnp.where(erow == i1, w1, jnp.where(erow == i2, w2, 0.0))
        head = jnp.where(grow == 0, gi.astype(F32), 0.0)
        route_t = jnp.concatenate(
            [head, comb, jnp.zeros((LANES - SUBLANES - N_EXPERTS, sub), F32)], axis=0)
        route_ref[rs, :] = route_t.T
        routet_ref[:, rs] = head


def _dispatch_kernel(t_ref, route_ref, routet_ref, xs_ref, cws_ref, pos_ref, cnt_ref):
    nb = ROUTE_BLOCK
    gi_row = routet_ref[0:1, :]
    sub = lax.broadcasted_iota(jnp.int32, (SUBLANES, nb), 0).astype(F32)
    onehot = (sub == gi_row)
    ri = lax.broadcasted_iota(jnp.int32, (nb, nb), 0)
    cj = lax.broadcasted_iota(jnp.int32, (nb, nb), 1)
    before = (ri < cj).astype(BF16)
    rank = _dot(onehot.astype(BF16), before)
    counts = jnp.broadcast_to(jnp.sum(onehot.astype(F32), axis=-1, keepdims=True), (SUBLANES, LANES))
    padded = jnp.ceil(counts * (1.0 / ROUTE_TILE)) * ROUTE_TILE
    srow = lax.broadcasted_iota(jnp.int32, (SUBLANES, LANES), 0)
    incl = padded
    s = 1
    while s < SUBLANES:
        incl = incl + jnp.where(srow >= s, pltpu.roll(incl, s, 0), 0.0)
        s *= 2
    start = incl - padded
    pos_row = jnp.sum(jnp.where(onehot, start[:, 0:1] + rank, 0.0), axis=0, keepdims=True)
    cnt_ref[0] = counts.astype(jnp.int32)

    pos_hi = jnp.floor(pos_row * (1.0 / ROUTE_TILE))
    prow = lax.broadcasted_iota(jnp.int32, (2 * SUBLANES, nb), 0)
    pieces = jnp.where(prow == 0, pos_hi, jnp.where(prow == 1, pos_row - ROUTE_TILE * pos_hi, 0.0))
    wrow = lax.broadcasted_iota(jnp.int32, (2 * SUBLANES, LANES), 0)
    weights = jnp.where(wrow == 0, float(ROUTE_TILE), jnp.where(wrow == 1, 1.0, 0.0))
    pos_ref[...] = _dot_tn(pieces.astype(BF16), weights.astype(BF16))

    route = route_ref[...]
    lane = lax.broadcasted_iota(jnp.int32, (nb, LANES), 1)
    comb = jnp.where(lane >= ROUTE_W_LANE, route, 0.0)
    c_hi, c_lo = _split(comb)
    c_lo2 = (comb - c_hi.astype(F32) - c_lo.astype(F32)).astype(BF16)
    d_model = xs_ref.shape[1]
    t_ref[:, d_model:] = (c_hi.astype(F32) + pltpu.roll(c_lo.astype(F32), ROUTE_LO_SHIFT, 1)
                          + pltpu.roll(c_lo2.astype(F32), 2 * ROUTE_LO_SHIFT, 1)).astype(BF16)
    tb = t_ref[...]
    for r in range(TILES_PER_BLOCK):
        rows = (lax.broadcasted_iota(jnp.int32, (ROUTE_TILE, nb), 0) + r * ROUTE_TILE).astype(F32)
        perm = (rows == pos_row).astype(BF16)
        moved = _dot(perm, tb)
        xs_ref[r * ROUTE_TILE:(r + 1) * ROUTE_TILE, :] = moved[:, :d_model].astype(BF16)
        cws_ref[r * ROUTE_TILE:(r + 1) * ROUTE_TILE, :] = moved[:, d_model:]


def _route_dispatch_kernel(x_ref, a_ref, gl_ref, ml_ref, w_ref, g_ref, wr_ref, br_ref,
                           x1_ref, xs_ref, cws_ref, pos_ref, cnt_ref, t_s, route_s, routet_s):
    _out_route_kernel(x_ref, a_ref, gl_ref, ml_ref, w_ref, g_ref, wr_ref, br_ref,
                      x1_ref, t_s, route_s, routet_s)
    _dispatch_kernel(t_s, route_s, routet_s, xs_ref, cws_ref, pos_ref, cnt_ref)


def _route_dispatch(x2, attn, gla_o, ml_o, lp):
    N, D = x2.shape
    nblk = N // ROUTE_BLOCK
    const = lambda i: (0, 0)
    tok = lambda i: (i, 0)
    return pl.pallas_call(
        _route_dispatch_kernel,
        grid=(nblk,),
        in_specs=[
            pl.BlockSpec((ROUTE_BLOCK, D), tok),
            pl.BlockSpec((ROUTE_BLOCK, ATTN_WIDTH), tok),
            pl.BlockSpec((ROUTE_BLOCK, GLA_WIDTH), tok),
            pl.BlockSpec((ROUTE_BLOCK, MLSTM_WIDTH), tok),
            pl.BlockSpec((ATTN_WIDTH + GLA_WIDTH + MLSTM_WIDTH, D), const),
            pl.BlockSpec((1, D), const),
            pl.BlockSpec((D, 2 * LANES), const),
            pl.BlockSpec((1, LANES), const),
        ],
        out_specs=[
            pl.BlockSpec((ROUTE_BLOCK, D), tok),
            pl.BlockSpec((ROUTE_ROWS, D), tok),
            pl.BlockSpec((ROUTE_ROWS, LANES), tok),
            pl.BlockSpec((ROUTE_BLOCK, LANES), tok),
            pl.BlockSpec((1, SUBLANES, LANES), lambda i: (i, 0, 0)),
        ],
        out_shape=[
            jax.ShapeDtypeStruct((N, D), F32),
            jax.ShapeDtypeStruct((nblk * ROUTE_ROWS, D), BF16),
            jax.ShapeDtypeStruct((nblk * ROUTE_ROWS, LANES), F32),
            jax.ShapeDtypeStruct((N, LANES), F32),
            jax.ShapeDtypeStruct((nblk, SUBLANES, LANES), jnp.int32),
        ],
        scratch_shapes=[
            pltpu.VMEM((ROUTE_BLOCK, D + LANES), BF16),
            pltpu.VMEM((ROUTE_BLOCK, LANES), F32),
            pltpu.VMEM((SUBLANES, ROUTE_BLOCK), F32),
        ],
        compiler_params=_params("arbitrary"),
        name="route_dispatch",
    )(x2, attn, gla_o, ml_o, lp["w_out"], lp["g_ffn"], lp["w_route"], lp["b_route"])


def _tile_schedule(cnt):
    nblk = cnt.shape[0]
    ntile = (cnt + ROUTE_TILE - 1) // ROUTE_TILE
    end = jnp.cumsum(ntile, axis=1)
    r = jnp.arange(TILES_PER_BLOCK, dtype=jnp.int32)
    grp = jnp.sum(r[None, :, None] >= end[:, None, :], axis=-1)
    grp = grp.reshape(-1).astype(jnp.int32)
    tile = jnp.arange(nblk * TILES_PER_BLOCK, dtype=jnp.int32)
    order = jnp.argsort(grp * (nblk * TILES_PER_BLOCK) + tile).astype(jnp.int32)
    n_active = jnp.sum(grp < N_GROUPS).astype(jnp.int32)
    g_sorted = grp[order]
    last_group = g_sorted[jnp.maximum(n_active - 1, 0)]
    g_sorted = jnp.where(g_sorted < N_GROUPS, g_sorted, last_group)
    return order, g_sorted, n_active[None]


def _expert_kernel(trow_ref, tgrp_ref, nact_ref, xs_ref, cws_ref, wg_ref, wu_ref, wd_ref, ys_ref,
                   wg_s, wu_s, wd_s):
    i = pl.program_id(0)

    @pl.when((i == 0) | (tgrp_ref[i] != tgrp_ref[jnp.maximum(i - 1, 0)]))
    def _():
        for j in range(EXPERTS_PER_GROUP):
            wg_s[j] = wg_ref[j].astype(BF16)
            wu_s[j] = wu_ref[j].astype(BF16)
            wd_s[j] = wd_ref[j].astype(BF16)

    @pl.when(i < nact_ref[0])
    def _():
        x = xs_ref[...]
        cws = cws_ref[...]
        lane = lax.broadcasted_iota(jnp.int32, cws.shape, 1)
        first = ROUTE_W_LANE + tgrp_ref[i] * EXPERTS_PER_GROUP
        y = jnp.zeros(ys_ref.shape, F32)

        def hidden(j):
            return _dot(x, wg_s[j]), _dot(x, wu_s[j])

        h = hidden(0)
        for j in range(EXPERTS_PER_GROUP):
            h_next = hidden(j + 1) if j + 1 < EXPERTS_PER_GROUP else None
            off = lane - (first + j)
            sel = (off == 0) | (off == ROUTE_LO_SHIFT) | (off == 2 * ROUTE_LO_SHIFT)
            wj = jnp.sum(jnp.where(sel, cws, 0.0), axis=-1, keepdims=True)
            a = (h[0] * _sigmoid(h[0]) * h[1]).astype(BF16)
            y = y + wj * _dot(a, wd_s[j])
            h = h_next
        ys_ref[...] = y.astype(BF16)

    @pl.when(i >= nact_ref[0])
    def _():
        ys_ref[...] = jnp.zeros(ys_ref.shape, BF16)


def _experts(xs, cws, order, grp, n_active, lp):
    rows, D = xs.shape
    n_tiles = rows // ROUTE_TILE
    tile = lambda i, trow, tgrp, nact: (trow[i], 0)
    layer_groups = lp["layer"] * N_GROUPS
    wsel = lambda i, trow, tgrp, nact: (layer_groups + tgrp[i], 0, 0)
    once = pl.Buffered(1)
    return pl.pallas_call(
        _expert_kernel,
        grid_spec=pltpu.PrefetchScalarGridSpec(
            num_scalar_prefetch=3,
            grid=(n_tiles,),
            in_specs=[
                pl.BlockSpec((ROUTE_TILE, D), tile),
                pl.BlockSpec((ROUTE_TILE, LANES), tile),
                pl.BlockSpec((EXPERTS_PER_GROUP, D, D_EXPERT), wsel, pipeline_mode=once),
                pl.BlockSpec((EXPERTS_PER_GROUP, D, D_EXPERT), wsel, pipeline_mode=once),
                pl.BlockSpec((EXPERTS_PER_GROUP, D_EXPERT, D), wsel, pipeline_mode=once),
            ],
            out_specs=pl.BlockSpec((ROUTE_TILE, D), tile),
            scratch_shapes=[
                pltpu.VMEM((EXPERTS_PER_GROUP, D, D_EXPERT), BF16),
                pltpu.VMEM((EXPERTS_PER_GROUP, D, D_EXPERT), BF16),
                pltpu.VMEM((EXPERTS_PER_GROUP, D_EXPERT, D), BF16),
            ],
        ),
        out_shape=jax.ShapeDtypeStruct((rows, D), BF16),
        compiler_params=_params("arbitrary"),
        name="experts",
    )(order, grp, n_active, xs, cws, lp["w_gate"], lp["w_up"], lp["w_down"])


def _combine_kernel(ys_ref, pos_ref, x1_ref, p_ref, g_ref, wpg_ref, wpp_ref, gfin_ref, o_ref,
                    *, embed, final):
    tm = x1_ref.shape[0]
    pos = pos_ref[...]
    lane = lax.broadcasted_iota(jnp.int32, (tm, LANES), 1).astype(F32)
    perm_t = jnp.concatenate(
        [(lane + r * LANES == pos).astype(BF16) for r in range(ROUTE_ROWS // LANES)], axis=1)
    x = x1_ref[...] + _dot(perm_t, ys_ref[...])
    if embed:
        gate = _sigmoid(_dot(_rms(x, g_ref[...]).astype(BF16), wpg_ref[...]))
        x = x + gate * _dot(p_ref[...].astype(BF16), wpp_ref[...])
    if final:
        x = _rms(x, gfin_ref[...])
    o_ref[...] = x


def _combine(ys, pos, x1, p2, lp, g_final, embed=True, final=False, tm=512):
    N, D = x1.shape
    inner = ROUTE_BLOCK // tm
    tok = lambda b, i: (b * inner + i, 0)
    const = lambda b, i: (0, 0)
    return pl.pallas_call(
        functools.partial(_combine_kernel, embed=embed, final=final),
        grid=(N // ROUTE_BLOCK, inner),
        in_specs=[
            pl.BlockSpec((ROUTE_ROWS, D), lambda b, i: (b, 0)),
            pl.BlockSpec((tm, LANES), tok),
            pl.BlockSpec((tm, D), tok),
            pl.BlockSpec((tm, p2.shape[1]), tok),
            pl.BlockSpec((1, D), const),
            pl.BlockSpec((D, D), const),
            pl.BlockSpec((p2.shape[1], D), const),
            pl.BlockSpec((1, D), const),
        ],
        out_specs=pl.BlockSpec((tm, D), tok),
        out_shape=jax.ShapeDtypeStruct((N, D), F32),
        compiler_params=_params("arbitrary", "arbitrary"),
        name="combine",
    )(ys, pos, x1, p2, lp["g_ple"], lp["w_pg"], lp["w_pp"], g_final)


def _mix_out_moe(x2, attn, gla_o, ml_o, lp):
    x1, xs, cws, pos, cnt = _route_dispatch(x2, attn, gla_o, ml_o, lp)
    order, grp, n_active = _tile_schedule(cnt[:, :N_GROUPS, 0])
    return x1, _experts(xs, cws, order, grp, n_active, lp), pos


def _moe_debug(x2, lp):
    N, D = x2.shape
    zeros = lambda w: jnp.zeros((N, w), BF16)
    x1, ys, pos = _mix_out_moe(x2, zeros(ATTN_WIDTH), zeros(GLA_WIDTH), zeros(MLSTM_WIDTH), lp)
    p2 = jnp.zeros((N, lp["w_pp"].shape[0]), F32)
    return _combine(ys, pos, x1, p2, lp, lp["g_ple"], embed=False) - x1


def _permute_in_cols(w):
    glr0 = ATTN_WIDTH + 2 * KV_WIDTH + 4 * GLA_WIDTH
    ml0 = glr0 + 2 * GLA_RANK
    mg0 = ml0 + 4 * MLSTM_WIDTH
    end = mg0 + 4 * MLSTM_HEADS
    assert end == w.shape[1]
    pad = jnp.zeros((w.shape[0], IN_PERM_WIDTH - end), BF16)
    parts = [w[:, :glr0], w[:, ml0:mg0], w[:, glr0:ml0], w[:, mg0:end]]
    return jnp.concatenate([part.astype(BF16) for part in parts] + [pad], axis=1)


def _rope_tables(T):
    t = np.arange(T)
    inv = ROPE_THETA ** (-np.arange(0, ROPE_AXIS_DIM, 2, dtype=np.float64) / ROPE_AXIS_DIM)
    ang_r = (t // GRID_W)[None, :] * inv[:, None]
    ang_c = (t % GRID_W)[None, :] * inv[:, None]
    cos_h = np.concatenate([np.cos(ang_r), np.cos(ang_r), np.cos(ang_c), np.cos(ang_c)], axis=0)
    sin_h = np.concatenate([-np.sin(ang_r), np.sin(ang_r), -np.sin(ang_c), np.sin(ang_c)], axis=0)
    reps = LANES // HEAD_DIM
    return dict(cos_t=jnp.asarray(np.tile(cos_h, (reps, 1)), F32),
                sin_t=jnp.asarray(np.tile(sin_h, (reps, 1)), F32))


def kernel(x, p, norm_mix_g, w_in, attn_q_norm_g, attn_k_norm_g, gla_w_decay, gla_b_decay,
           gla_out_norm_g, mlstm_conv_w, mlstm_conv_b, mlstm_b_input, mlstm_b_forget,
           mlstm_out_norm_g, w_out, norm_ffn_g, w_group, b_group, w_router, b_router,
           w_expert_gate, w_expert_up, w_expert_down, norm_ple_g, w_ple_gate, w_ple_proj,
           final_norm_g):
    params = dict(
        norm_mix_g=norm_mix_g, w_in=w_in, attn_q_norm_g=attn_q_norm_g, attn_k_norm_g=attn_k_norm_g,
        gla_w_decay=gla_w_decay, gla_b_decay=gla_b_decay, gla_out_norm_g=gla_out_norm_g,
        mlstm_conv_w=mlstm_conv_w, mlstm_conv_b=mlstm_conv_b, mlstm_b_input=mlstm_b_input,
        mlstm_b_forget=mlstm_b_forget, mlstm_out_norm_g=mlstm_out_norm_g, w_out=w_out,
        norm_ffn_g=norm_ffn_g, w_group=w_group, b_group=b_group, w_router=w_router,
        b_router=b_router, w_expert_gate=w_expert_gate, w_expert_up=w_expert_up,
        w_expert_down=w_expert_down, norm_ple_g=norm_ple_g, w_ple_gate=w_ple_gate,
        w_ple_proj=w_ple_proj)
    B, T, D = x.shape
    rope = _rope_tables(T)
    N = B * T
    depth = w_in.shape[0]
    x2 = x.reshape(N, D)
    g_final = final_norm_g[None, :]
    for i in range(depth):
        lp = _layer_params(params, i)
        qt, k, vt, gla, ml, small = _in_proj(x2, lp, rope, B, T)
        attn = _attention(lp["attn_safe"], qt, k, vt).reshape(N, ATTN_WIDTH)
        gla_o = _gla(gla, small, lp).reshape(N, GLA_WIDTH)
        ml_o = _mlstm(ml, small, lp).reshape(N, MLSTM_WIDTH)
        x1, ys, pos = _mix_out_moe(x2, attn, gla_o, ml_o, lp)
        x2 = _combine(ys, pos, x1, p[i].reshape(N, -1), lp, g_final, final=(i == depth - 1))
    return x2.reshape(B, T, D)


def _split_w(w):
    hi = w.astype(BF16)
    return hi, (w - hi.astype(F32)).astype(BF16)


def _stacked_experts(w):
    return w.reshape((w.shape[0] * w.shape[1],) + w.shape[2:])


def _layer_params(p, i):
    D = p["w_in"].shape[1]
    w_in = _permute_in_cols(p["w_in"][i])
    gq, gk = p["attn_q_norm_g"][i], p["attn_k_norm_g"][i]
    q_gain = jnp.tile(gq, LANES // HEAD_DIM) * (HEAD_DIM ** -0.5 * LOG2E)
    logit_bound = HEAD_DIM ** 0.5 * jnp.max(jnp.abs(gq)) * jnp.max(jnp.abs(gk))
    attn_safe = (logit_bound <= ATTN_SAFE_LOGIT).astype(jnp.int32)[None]
    wd = jnp.zeros((2, LANES, GLA_WIDTH), F32)
    wd = wd.at[0, :GLA_RANK].set(p["gla_w_decay"][i, 0]).at[1, GLA_RANK:2 * GLA_RANK].set(p["gla_w_decay"][i, 1])
    wd_hi, wd_lo = _split_w(wd)
    gate_bias = jnp.zeros((LANES,), F32).at[SMALL_GATE_LANE:SMALL_GATE_LANE + 4 * MLSTM_HEADS].set(
        jnp.concatenate([p["mlstm_b_input"][i, 0], p["mlstm_b_forget"][i, 0],
                         p["mlstm_b_input"][i, 1], p["mlstm_b_forget"][i, 1]]))
    w_route = jnp.zeros((D, LANES), F32)
    w_route = w_route.at[:, :N_GROUPS].set(p["w_group"][i])
    w_route = w_route.at[:, ROUTE_W_LANE:ROUTE_W_LANE + N_EXPERTS].set(p["w_router"][i])
    wr_hi, wr_lo = _split_w(w_route)
    b_route = jnp.zeros((LANES,), F32).at[:N_GROUPS].set(p["b_group"][i])
    b_route = b_route.at[ROUTE_W_LANE:ROUTE_W_LANE + N_EXPERTS].set(p["b_router"][i])
    return dict(
        g_mix=p["norm_mix_g"][i][None, :],
        w_in=w_in,
        q_gain_t=jnp.broadcast_to(q_gain[:, None], (LANES, LANES)),
        k_gain_t=jnp.broadcast_to(jnp.tile(gk, LANES // HEAD_DIM)[:, None], (LANES, LANES)),
        attn_safe=attn_safe,
        wd_hi=wd_hi, wd_lo=wd_lo,
        bd=p["gla_b_decay"][i][:, None, :],
        gla_gain=jnp.tile(p["gla_out_norm_g"][i], LANES // HEAD_DIM)[None, :],
        conv_w=p["mlstm_conv_w"][i],
        conv_b=p["mlstm_conv_b"][i][None, :],
        gate_bias=gate_bias[None, :],
        ml_gain=jnp.tile(p["mlstm_out_norm_g"][i], LANES // HEAD_DIM)[None, :],
        w_out=p["w_out"][i].astype(BF16),
        g_ffn=p["norm_ffn_g"][i][None, :],
        w_route=jnp.concatenate([wr_hi, wr_lo], axis=1), b_route=b_route[None, :],
        layer=i,
        w_gate=_stacked_experts(p["w_expert_gate"]),
        w_up=_stacked_experts(p["w_expert_up"]),
        w_down=_stacked_experts(p["w_expert_down"]),
        g_ple=p["norm_ple_g"][i][None, :],
        w_pg=p["w_ple_gate"][i].astype(BF16),
        w_pp=p["w_ple_proj"][i].astype(BF16),
    )
```

```python
import functools

import jax
import jax.numpy as jnp
import numpy as np
from jax import lax
from jax.experimental import pallas as pl
from jax.experimental.pallas import tpu as pltpu

F32 = jnp.float32
BF16 = jnp.bfloat16

GRID_W = 64
HEAD_DIM = 64
ATTN_HEADS = 8
ATTN_KV_HEADS = 2
GLA_HEADS = 4
MLSTM_HEADS = 4
ATTN_WIDTH = ATTN_HEADS * HEAD_DIM
KV_WIDTH = ATTN_KV_HEADS * HEAD_DIM
GLA_WIDTH = GLA_HEADS * HEAD_DIM
MLSTM_WIDTH = MLSTM_HEADS * HEAD_DIM
GLA_RANK = 16
GLA_TAU = 16.0
CHUNK = 64
ROPE_THETA = 10000.0
ROPE_AXIS_DIM = HEAD_DIM // 2
N_GROUPS = 4
EXPERTS_PER_GROUP = 4
N_EXPERTS = N_GROUPS * EXPERTS_PER_GROUP
D_EXPERT = 512
EPS = 1e-6
NEG = -1e30

LANES = 128
SUBLANES = 8
VMEM_LIMIT_BYTES = 56 * 1024 * 1024

QK_WIDTH = ATTN_WIDTH + KV_WIDTH
OFF_V = QK_WIDTH
OFF_GLA = OFF_V + KV_WIDTH
OFF_ML = OFF_GLA + 4 * GLA_WIDTH
OFF_SMALL = OFF_ML + 4 * MLSTM_WIDTH
IN_PERM_WIDTH = OFF_SMALL + LANES
SMALL_GATE_LANE = 2 * GLA_RANK

ROUTE_BLOCK = 1024
ROUTE_TILE = 128
TILES_PER_BLOCK = (ROUTE_BLOCK + N_GROUPS * (ROUTE_TILE - 1)) // ROUTE_TILE
ROUTE_ROWS = TILES_PER_BLOCK * ROUTE_TILE
ROUTE_W_LANE = 8
ROUTE_LO_SHIFT = 32
ROUTE_SUBTILE = 256


def _dot(a, b):
    return jnp.dot(a, b, preferred_element_type=F32)


def _dot_nt(a, b):
    return lax.dot_general(a, b, (((1,), (1,)), ((), ())), preferred_element_type=F32)


def _dot_tn(a, b):
    return lax.dot_general(a, b, (((0,), (0,)), ((), ())), preferred_element_type=F32)


def _split(a):
    hi = a.astype(BF16)
    lo = (a - hi.astype(F32)).astype(BF16)
    return hi, lo


def _dot3(a, w_hi, w_lo):
    a_hi, a_lo = _split(a)
    return _dot(a_hi, w_hi) + _dot(a_lo, w_hi) + _dot(a_hi, w_lo)


def _log_sigmoid(x):
    return jnp.minimum(x, 0.0) - jnp.log(1.0 + jnp.exp(-jnp.abs(x)))


def _sigmoid(x):
    return 1.0 / (1.0 + jnp.exp(-x))


def _rms(x, g):
    return x * lax.rsqrt(jnp.mean(x * x, axis=-1, keepdims=True) + EPS) * g


def _params(*semantics):
    return pltpu.CompilerParams(dimension_semantics=semantics, vmem_limit_bytes=VMEM_LIMIT_BYTES)


def _in_proj_kernel(x_ref, g_ref, w_ref, cost_ref, sint_ref, gq_ref, gk_ref,
                    qt_ref, k_ref, vt_ref, gla_ref, ml_ref, small_ref):
    tm = x_ref.shape[0]
    h = _rms(x_ref[...], g_ref[...])
    z = _dot(h.astype(BF16), w_ref[...])

    heads = LANES // HEAD_DIM
    half = ROPE_AXIS_DIM // 2
    gq = jnp.concatenate([gq_ref[...]] * (tm // LANES), axis=1)
    cost = cost_ref[...]
    sint = sint_ref[...]
    gk = jnp.concatenate([gk_ref[...]] * (tm // LANES), axis=1)

    def norm_rope_t(c, gain):
        zt = z[:, c * LANES:(c + 1) * LANES].T
        z3 = zt.reshape(heads, HEAD_DIM, tm)
        inv = lax.rsqrt(jnp.mean(z3 * z3, axis=1, keepdims=True) + EPS)
        y = (z3 * inv).reshape(LANES, tm) * gain
        partner = jnp.concatenate(
            [y[(r ^ 1) * half:((r ^ 1) + 1) * half, :] for r in range(LANES // half)], axis=0)
        return y * cost + partner * sint

    for c in range(ATTN_WIDTH // LANES):
        qt_ref[0, c * LANES:(c + 1) * LANES, :] = norm_rope_t(c, gq).astype(BF16)
    k_ref[0] = norm_rope_t(ATTN_WIDTH // LANES, gk).T.astype(BF16)
    vt_ref[0] = z[:, OFF_V:OFF_V + KV_WIDTH].T.astype(BF16)
    gla_ref[0] = z[:, OFF_GLA:OFF_ML]
    ml_ref[0] = z[:, OFF_ML:OFF_SMALL]
    small_ref[0] = z[:, OFF_SMALL:IN_PERM_WIDTH]


def _in_proj(x2, lp, rope, B, T, tm=512):
    N, D = x2.shape
    tpb = T // tm
    const = lambda i: (0, 0)
    tok3 = lambda i: (i // tpb, i % tpb, 0)
    tokT = lambda i: (i // tpb, 0, i % tpb)
    layer = lp["layer"]
    return pl.pallas_call(
        _in_proj_kernel,
        grid=(N // tm,),
        in_specs=[
            pl.BlockSpec((tm, D), lambda i: (i, 0)),
            pl.BlockSpec((1, D), const),
            pl.BlockSpec((None, D, IN_PERM_WIDTH), lambda i: (layer, 0, 0)),
            pl.BlockSpec((LANES, tm), lambda i: (0, i % tpb)),
            pl.BlockSpec((LANES, tm), lambda i: (0, i % tpb)),
            pl.BlockSpec((LANES, LANES), const),
            pl.BlockSpec((LANES, LANES), const),
        ],
        out_specs=[
            pl.BlockSpec((1, ATTN_WIDTH, tm), tokT),
            pl.BlockSpec((1, tm, KV_WIDTH), tok3),
            pl.BlockSpec((1, KV_WIDTH, tm), tokT),
            pl.BlockSpec((1, tm, 4 * GLA_WIDTH), tok3),
            pl.BlockSpec((1, tm, 4 * MLSTM_WIDTH), tok3),
            pl.BlockSpec((1, tm, LANES), tok3),
        ],
        out_shape=[
            jax.ShapeDtypeStruct((B, ATTN_WIDTH, T), BF16),
            jax.ShapeDtypeStruct((B, T, KV_WIDTH), BF16),
            jax.ShapeDtypeStruct((B, KV_WIDTH, T), BF16),
            jax.ShapeDtypeStruct((B, T, 4 * GLA_WIDTH), F32),
            jax.ShapeDtypeStruct((B, T, 4 * MLSTM_WIDTH), F32),
            jax.ShapeDtypeStruct((B, T, LANES), F32),
        ],
        compiler_params=_params("arbitrary"),
        name="in_proj",
    )(x2, lp["g_mix"], lp["w_in"], rope["cos_t"], rope["sin_t"], lp["q_gain_t"], lp["k_gain_t"])


ATTN_SAFE_LOGIT = 40.0
LOG2E = 1.4426950408889634


def _attn_kernel(safe_ref, qt_ref, k_ref, vt_ref, o_ref, *, tk):
    tq = qt_ref.shape[2]
    T = k_ref.shape[1]
    G = ATTN_HEADS // ATTN_KV_HEADS
    n = G * tq
    zeros = jnp.zeros((HEAD_DIM, n), BF16)

    def q_operand(j):
        base = j * G * HEAD_DIM
        qs = jnp.concatenate(
            [qt_ref[0, base + h * HEAD_DIM:base + (h + 1) * HEAD_DIM, :] for h in range(G)], axis=1)
        return jnp.concatenate([qs, zeros] if j == 0 else [zeros, qs], axis=0)

    def finish(j, acc, l):
        base = j * G * HEAD_DIM
        o = acc * (1.0 / l)
        ot = jnp.concatenate([o[:, h * tq:(h + 1) * tq] for h in range(G)], axis=0)
        o_ref[0, :, base:base + G * HEAD_DIM] = ot.T.astype(BF16)

    @pl.when(safe_ref[0] == 1)
    def _():
        qps = [q_operand(j) for j in range(ATTN_KV_HEADS)]
        units = [(c, j) for c in range(T // tk) for j in range(ATTN_KV_HEADS)]

        def scores(u):
            c, j = units[u]
            return _dot(k_ref[0, c * tk:(c + 1) * tk, :], qps[j])

        l8 = [jnp.zeros((SUBLANES, n), F32)] * ATTN_KV_HEADS
        acc = [jnp.zeros((HEAD_DIM, n), F32)] * ATTN_KV_HEADS
        st = scores(0)
        for u, (c, j) in enumerate(units):
            st_next = scores(u + 1) if u + 1 < len(units) else None
            p = jnp.exp2(st)
            l8[j] = l8[j] + jnp.sum(p.reshape(tk // SUBLANES, SUBLANES, n), axis=0)
            vc = vt_ref[0, j * HEAD_DIM:(j + 1) * HEAD_DIM, c * tk:(c + 1) * tk]
            acc[j] = acc[j] + _dot(vc, p.astype(BF16))
            st = st_next
        for j in range(ATTN_KV_HEADS):
            finish(j, acc[j], jnp.sum(l8[j], axis=0, keepdims=True))

    @pl.when(safe_ref[0] == 0)
    def _():
        for j in range(ATTN_KV_HEADS):
            qp = q_operand(j)

            def body(c, carry, qp=qp, j=j):
                m, l, acc = carry
                off = pl.multiple_of(c * tk, tk)
                st = _dot(k_ref[0, pl.ds(off, tk), :], qp)
                m_new = jnp.maximum(m, jnp.max(st, axis=0, keepdims=True))
                alpha = jnp.exp2(m - m_new)
                p = jnp.exp2(st - m_new)
                l = alpha * l + jnp.sum(p, axis=0, keepdims=True)
                vc = vt_ref[0, j * HEAD_DIM:(j + 1) * HEAD_DIM, pl.ds(off, tk)]
                return m_new, l, alpha * acc + _dot(vc, p.astype(BF16))

            init = (jnp.full((1, n), NEG, F32), jnp.zeros((1, n), F32), jnp.zeros((HEAD_DIM, n), F32))
            _, l, acc = lax.fori_loop(0, T // tk, body, init)
            finish(j, acc, l)


def _attention(safe, qt, k, vt, tq=256, tk=128):
    B, _, T = qt.shape
    tk = min(tk, T)
    return pl.pallas_call(
        functools.partial(_attn_kernel, tk=tk),
        grid_spec=pltpu.PrefetchScalarGridSpec(
            num_scalar_prefetch=1,
            grid=(B, T // tq),
            in_specs=[
                pl.BlockSpec((1, ATTN_WIDTH, tq), lambda b, i, s: (b, 0, i)),
                pl.BlockSpec((1, T, KV_WIDTH), lambda b, i, s: (b, 0, 0)),
                pl.BlockSpec((1, KV_WIDTH, T), lambda b, i, s: (b, 0, 0)),
            ],
            out_specs=pl.BlockSpec((1, tq, ATTN_WIDTH), lambda b, i, s: (b, i, 0)),
        ),
        out_shape=jax.ShapeDtypeStruct((B, T, ATTN_WIDTH), BF16),
        compiler_params=_params("arbitrary", "arbitrary"),
        name="attention",
    )(safe, qt, k, vt)


def _chunk_scan(x, pos, op, fill, reverse):
    rows = x.shape[0]
    s = 1
    while s < CHUNK:
        if reverse:
            shifted = jnp.where(pos < CHUNK - s, pltpu.roll(x, rows - s, 0), fill)
        else:
            shifted = jnp.where(pos >= s, pltpu.roll(x, s, 0), fill)
        x = op(x, shifted)
        s *= 2
    return x


def _chunk_pos(rows):
    return lax.broadcasted_iota(jnp.int32, (rows, LANES), 0) % CHUNK


def _chunk_row(a, reverse_dir, idx_fwd, idx_bwd):
    rows = a.shape[0]
    a3 = a.reshape(rows // CHUNK, CHUNK, LANES)
    i = idx_bwd if reverse_dir else idx_fwd
    return jnp.broadcast_to(a3[:, i:i + 1, :], a3.shape).reshape(rows, LANES)


def _stack_heads(x):
    lane = lax.broadcasted_iota(jnp.int32, x.shape, 1)
    zero = jnp.zeros_like(x)
    return jnp.concatenate([jnp.where(lane < HEAD_DIM, x, zero), jnp.where(lane >= HEAD_DIM, x, zero)], axis=0)


def _select_heads(x):
    c = x.shape[0] // 2
    lane = lax.broadcasted_iota(jnp.int32, (c, x.shape[1]), 1)
    return jnp.where(lane < HEAD_DIM, x[:c], x[c:])


def _stacked_causal_masks():
    ci = lax.broadcasted_iota(jnp.int32, (2 * CHUNK, CHUNK), 0) % CHUNK
    si = lax.broadcasted_iota(jnp.int32, (2 * CHUNK, CHUNK), 1)
    return ci >= si, ci <= si


def _pair_blockdiag(width):
    r = lax.broadcasted_iota(jnp.int32, (LANES, width), 0) // HEAD_DIM
    c = lax.broadcasted_iota(jnp.int32, (LANES, width), 1) // (width // 2)
    return r == c


def _head_rms(o, gain):
    lane = lax.broadcasted_iota(jnp.int32, o.shape, 1)
    lo = lane < HEAD_DIM
    sq = o * o
    s_lo = jnp.sum(jnp.where(lo, sq, 0.0), axis=-1, keepdims=True)
    s_hi = jnp.sum(jnp.where(lo, 0.0, sq), axis=-1, keepdims=True)
    ms = jnp.where(lo, s_lo, s_hi) * (1.0 / HEAD_DIM)
    return o * lax.rsqrt(ms + EPS) * gain


PREP_ROWS = 512
GLA_CHUNKS_PER_STEP = 4


def _gla_kernel(q_ref, k_ref, v_ref, g_ref, small_ref, wdh_ref, wdl_ref, bd_ref, gain_ref,
                o_ref, qe_s, ke_s, dec_s, of_s, ob_s):
    T = q_ref.shape[1]
    nc = T // CHUNK
    R = min(PREP_ROWS, T)
    cpt = R // CHUNK
    pos = _chunk_pos(R)
    w_hi = jnp.concatenate([wdh_ref[0], wdh_ref[1]], axis=1)
    w_lo = jnp.concatenate([wdl_ref[0], wdl_ref[1]], axis=1)
    bias = jnp.concatenate([bd_ref[0], bd_ref[1]], axis=1)

    def prep(t, _):
        r0 = pl.multiple_of(t * R, R)
        q = q_ref[0, pl.ds(r0, R), :] * HEAD_DIM ** -0.5
        k = k_ref[0, pl.ds(r0, R), :]
        la2 = _log_sigmoid(_dot3(small_ref[0, pl.ds(r0, R), :], w_hi, w_lo) + bias) * (1.0 / GLA_TAU)
        for d in range(2):
            b = _chunk_scan(la2[:, d * LANES:(d + 1) * LANES], pos, jnp.add, 0.0, reverse=bool(d))
            b_mid = _chunk_row(b, d, CHUNK // 2 - 1, CHUNK // 2)
            b_last = _chunk_row(b, d, CHUNK - 1, 0)
            qe_s[d, pl.ds(r0, R), :] = (q * jnp.exp(b - b_mid)).astype(BF16)
            ke_s[d, pl.ds(r0, R), :] = (k * jnp.exp(b_mid - b)).astype(BF16)
            c8 = pl.ds(pl.multiple_of(t * cpt * SUBLANES, SUBLANES), cpt * SUBLANES)
            for kind, val in enumerate((b_last, b_last - b_mid, b_mid)):
                rows8 = val.reshape(cpt, CHUNK, LANES)[:, :SUBLANES, :].reshape(cpt * SUBLANES, LANES)
                dec_s[d, kind, c8, :] = jnp.exp(rows8)
        return 0

    lax.fori_loop(0, T // R, prep, 0)

    outs = (of_s, ob_s)
    masks = _stacked_causal_masks()
    blockdiag = _pair_blockdiag(LANES)
    G = GLA_CHUNKS_PER_STEP

    def step(it, states):
        units = [(d, g) for d in range(2) for g in range(G)]
        ops = {}
        for d, g in units:
            c = it * G + g
            c = (nc - 1 - c) if d else c
            r0 = pl.multiple_of(c * CHUNK, CHUNK)
            c8 = pl.ds(pl.multiple_of(c * SUBLANES, SUBLANES), SUBLANES)
            ops[d, g] = dict(
                r0=r0,
                qe=qe_s[d, pl.ds(r0, CHUNK), :], ke=ke_s[d, pl.ds(r0, CHUNK), :],
                v=v_ref[0, pl.ds(r0, CHUNK), :].astype(BF16),
                dec=dec_s[d, 0, c8, :][0:1, :], to_end=dec_s[d, 1, c8, :][0:1, :],
                from_start=dec_s[d, 2, c8, :][0:1, :])
        a2, kv = {}, {}
        for u in units:
            o = ops[u]
            a2[u] = _dot_nt(_stack_heads(o["qe"]), o["ke"])
            kv[u] = _dot_tn(o["v"], o["ke"]) * o["to_end"]
        st_in = {}
        new_states = []
        for d in range(2):
            st = states[d]
            for g in range(G):
                st_in[d, g] = (st * ops[d, g]["from_start"]).astype(BF16)
                st = st * ops[d, g]["dec"] + jnp.where(blockdiag, kv[d, g], 0.0)
            new_states.append(st)
        for u in units:
            o = ops[u]
            a = jnp.where(masks[u[0]], a2[u], 0.0).astype(BF16)
            intra = _select_heads(_dot(a, o["v"]))
            outs[u[0]][pl.ds(o["r0"], CHUNK), :] = intra + _dot_nt(o["qe"], st_in[u])
        return tuple(new_states)

    zero = jnp.zeros((LANES, LANES), F32)
    lax.fori_loop(0, nc // G, step, (zero, zero))

    def post(t, _):
        r0 = pl.multiple_of(t * R, R)
        o = _head_rms(of_s[pl.ds(r0, R), :] + ob_s[pl.ds(r0, R), :], gain_ref[...])
        g = g_ref[0, pl.ds(r0, R), :]
        o_ref[0, pl.ds(r0, R), :] = (o * (g * _sigmoid(g))).astype(BF16)
        return 0

    lax.fori_loop(0, T // R, post, 0)


def _gla(gla, small, lp):
    B, T, _ = gla.shape
    pairs = GLA_WIDTH // LANES
    col = lambda off: (lambda b, hp: (b, 0, off * pairs + hp))
    return pl.pallas_call(
        _gla_kernel,
        grid=(B, pairs),
        in_specs=[
            pl.BlockSpec((1, T, LANES), col(0)),
            pl.BlockSpec((1, T, LANES), col(1)),
            pl.BlockSpec((1, T, LANES), col(2)),
            pl.BlockSpec((1, T, LANES), col(3)),
            pl.BlockSpec((1, T, LANES), lambda b, hp: (b, 0, 0)),
            pl.BlockSpec((2, LANES, LANES), lambda b, hp: (0, 0, hp)),
            pl.BlockSpec((2, LANES, LANES), lambda b, hp: (0, 0, hp)),
            pl.BlockSpec((2, 1, LANES), lambda b, hp: (0, 0, hp)),
            pl.BlockSpec((1, LANES), lambda b, hp: (0, 0)),
        ],
        out_specs=pl.BlockSpec((1, T, LANES), lambda b, hp: (b, 0, hp)),
        out_shape=jax.ShapeDtypeStruct((B, T, GLA_WIDTH), BF16),
        scratch_shapes=[
            pltpu.VMEM((2, T, LANES), BF16), pltpu.VMEM((2, T, LANES), BF16),
            pltpu.VMEM((2, 3, (T // CHUNK) * SUBLANES, LANES), F32),
            pltpu.VMEM((T, LANES), F32), pltpu.VMEM((T, LANES), F32),
        ],
        compiler_params=_params("arbitrary", "arbitrary"),
        name="gla",
    )(gla, gla, gla, gla, small, lp["wd_hi"], lp["wd_lo"], lp["bd"], lp["gla_gain"])


GATE_I = SMALL_GATE_LANE
GATE_F = SMALL_GATE_LANE + MLSTM_HEADS


def _gate_lane(d, hh):
    return GATE_F + 2 * MLSTM_HEADS * d + hh


MLSTM_CHUNKS_PER_STEP = 8


def _chunk_rows8(a, row):
    n = a.shape[0] // CHUNK
    a3 = a.reshape(n, CHUNK, LANES)
    full = jnp.broadcast_to(a3[:, row:row + 1, :], a3.shape)
    return full[:, :SUBLANES, :].reshape(n * SUBLANES, LANES)


def _expand_rows8(a8):
    n, w = a8.shape[0] // SUBLANES, a8.shape[1]
    a3 = a8.reshape(n, SUBLANES, w)[:, 0:1, :]
    return jnp.broadcast_to(a3, (n, CHUNK, w)).reshape(n * CHUNK, w)


def _gate_select():
    src = lax.broadcasted_iota(jnp.int32, (LANES, 2 * LANES), 0)
    dst = lax.broadcasted_iota(jnp.int32, (LANES, 2 * LANES), 1)
    want = GATE_F + 2 * MLSTM_HEADS * (dst // LANES) + (dst % LANES) // HEAD_DIM
    return jnp.where(src == want, 1.0, 0.0).astype(BF16)


def _gate_broadcast(x, sel, gate_mask, pieces=2):
    x = jnp.where(gate_mask, x, 0.0)
    if pieces == 1:
        return _dot(x.astype(BF16), sel)
    hi, lo = _split(x)
    return _dot(hi, sel) + _dot(lo, sel)


def _mlstm_kernel(q_ref, k_ref, v_ref, og_ref, small_ref, wq_ref, wk_ref, bq_ref, bk_ref,
                  gbias_ref, gain_ref, o_ref, q_s, k_s, b_s, r_s, cm_s, fl1_s, bl_s, rl_s, mf_s, mb_s,
                  wp_s, qi_s, kw_s, rt_s, st_s, of_s, ob_s):
    T = q_ref.shape[1]
    nc = T // CHUNK
    R = min(PREP_ROWS, T)
    hp = pl.program_id(1)
    G = MLSTM_CHUNKS_PER_STEP
    cpt = R // CHUNK

    row = lax.broadcasted_iota(jnp.int32, (T, LANES), 0)
    for src, w_ref, b_ref, dst, scale in ((q_ref, wq_ref, bq_ref, q_s, 1.0),
                                          (k_ref, wk_ref, bk_ref, k_s, HEAD_DIM ** -0.5)):
        xc = src[0]
        prev = jnp.where(row >= 1, pltpu.roll(xc, 1, 0), 0.0)
        nxt = jnp.where(row < T - 1, pltpu.roll(xc, T - 1, 0), 0.0)
        y = prev * w_ref[0:1, :] + xc * w_ref[1:2, :] + nxt * w_ref[2:3, :] + b_ref[...]
        dst[...] = (y * _sigmoid(y) * scale).astype(BF16)

    pos = _chunk_pos(R)
    lane = lax.broadcasted_iota(jnp.int32, (R, LANES), 1)
    is_bwd = lane >= GATE_I + 2 * MLSTM_HEADS
    is_bwd8 = lax.broadcasted_iota(jnp.int32, (cpt * SUBLANES, LANES), 1) >= GATE_I + 2 * MLSTM_HEADS
    heads_per_pair = LANES // HEAD_DIM
    shift = (LANES - heads_per_pair * hp) % LANES
    gate_lanes = [_gate_lane(d, hh) for d in range(2) for hh in range(2)]
    gate_mask = functools.reduce(jnp.logical_or, [lane == l for l in gate_lanes])
    lane8 = lax.broadcasted_iota(jnp.int32, (cpt * SUBLANES, LANES), 1)
    gate_mask8 = functools.reduce(jnp.logical_or, [lane8 == l for l in gate_lanes])

    def prep(t, _):
        r0 = pl.multiple_of(t * R, R)
        gc = pltpu.roll(small_ref[0, pl.ds(r0, R), :] + gbias_ref[...], shift, 1)
        logf = _log_sigmoid(gc)
        pre = _chunk_scan(logf, pos, jnp.add, 0.0, False)
        b = jnp.where(is_bwd, _chunk_row(pre, 0, CHUNK - 1, CHUNK - 1) - pre + logf, pre)
        r = pltpu.roll(gc, MLSTM_HEADS, 1) - b
        cm = jnp.where(is_bwd, _chunk_scan(r, pos, jnp.maximum, NEG, True),
                       _chunk_scan(r, pos, jnp.maximum, NEG, False))
        b_s[pl.ds(r0, R), :] = b
        r_s[pl.ds(r0, R), :] = r
        cm_s[pl.ds(r0, R), :] = cm
        rt_s[:, pl.ds(r0, R)] = r.T
        c8 = pl.ds(pl.multiple_of(t * cpt * SUBLANES, SUBLANES), cpt * SUBLANES)
        bl_s[c8, :] = jnp.where(is_bwd8, _chunk_rows8(b, 0), _chunk_rows8(b, CHUNK - 1))
        rl_s[c8, :] = jnp.where(is_bwd8, _chunk_rows8(cm, 0), _chunk_rows8(cm, CHUNK - 1))
        return 0

    lax.fori_loop(0, T // R, prep, 0)

    def m_chain(n, carry):
        mf, mb = carry
        rf = pl.ds(pl.multiple_of(n * SUBLANES, SUBLANES), SUBLANES)
        rb = pl.ds(pl.multiple_of((nc - 1 - n) * SUBLANES, SUBLANES), SUBLANES)
        mf_s[rf, :] = mf
        mb_s[rb, :] = mb
        return (bl_s[rf, :] + jnp.maximum(mf, rl_s[rf, :]), bl_s[rb, :] + jnp.maximum(mb, rl_s[rb, :]))

    m0 = jnp.full((SUBLANES, LANES), NEG, F32)
    lax.fori_loop(0, nc, m_chain, (m0, m0))

    expo_s, floor_s = (b_s, cm_s), (r_s, fl1_s)
    sel = _gate_select()

    def weights(t, _):
        r0 = pl.multiple_of(t * R, R)
        c8 = pl.ds(pl.multiple_of(t * cpt * SUBLANES, SUBLANES), cpt * SUBLANES)
        rows = pl.ds(r0, R)
        m_in8 = jnp.where(is_bwd8, mb_s[c8, :], mf_s[c8, :])
        bl8 = bl_s[c8, :]
        m_out8 = bl8 + jnp.maximum(m_in8, rl_s[c8, :])
        m_in = _expand_rows8(m_in8)
        mx = jnp.maximum(m_in, cm_s[rows, :])
        wp_b = jnp.exp(_gate_broadcast(bl8 + m_in8 - m_out8, sel, gate_mask8))
        mx_b = _gate_broadcast(mx, sel, gate_mask)
        floor = jnp.exp(-_gate_broadcast(b_s[rows, :] + mx, sel, gate_mask))
        w_inter = _gate_broadcast(jnp.exp(m_in - mx), sel, gate_mask, pieces=1)
        wk = _gate_broadcast(jnp.exp(_expand_rows8(bl8 - m_out8) + r_s[rows, :]), sel, gate_mask, pieces=1)
        q = q_s[rows, :].astype(F32)
        k = k_s[rows, :].astype(F32)
        for d in range(2):
            half = slice(d * LANES, (d + 1) * LANES)
            qi_s[d, rows, :] = (q * w_inter[:, half]).astype(BF16)
            kw_s[d, rows, :] = (k * wk[:, half]).astype(BF16)
            wp_s[d, c8, :] = wp_b[:, half]
            expo_s[d][rows, :] = -mx_b[:, half]
            floor_s[d][rows, :] = floor[:, half]
        return 0

    lax.fori_loop(0, T // R, weights, 0)

    outs = (of_s, ob_s)
    ci = lax.broadcasted_iota(jnp.int32, (CHUNK, LANES), 0)
    si = lax.broadcasted_iota(jnp.int32, (CHUNK, LANES), 1) % CHUNK
    masks = (ci >= si, ci <= si)
    br = lax.broadcasted_iota(jnp.int32, (LANES, 2 * LANES), 0) // HEAD_DIM
    bc = (lax.broadcasted_iota(jnp.int32, (LANES, 2 * LANES), 1) % LANES) // HEAD_DIM
    blockdiag = br == bc
    ones = jnp.ones((CHUNK, LANES), BF16)
    st_s[...] = jnp.zeros(st_s.shape, F32)
    gate_rows = slice(GATE_I, GATE_I + 4 * MLSTM_HEADS)

    def step(it, _):
        units = [(d, g) for d in range(2) for g in range(G)]
        ops = {}
        for d in range(2):
            first = (nc - (it + 1) * G) if d else it * G
            rt = rt_s[gate_rows, pl.ds(pl.multiple_of(first * CHUNK, G * CHUNK), G * CHUNK)]
            for g in range(G):
                local = (G - 1 - g) if d else g
                c = first + local
                rows = pl.ds(pl.multiple_of(c * CHUNK, CHUNK), CHUNK)
                c8 = pl.ds(pl.multiple_of(c * SUBLANES, SUBLANES), SUBLANES)
                r_row = jnp.concatenate(
                    [rt[_gate_lane(d, hh) - GATE_I:_gate_lane(d, hh) - GATE_I + 1,
                        local * CHUNK:(local + 1) * CHUNK] for hh in range(2)], axis=1)
                wp = wp_s[d, c8, :][0:1, :]
                ops[d, g] = dict(
                    rows=rows, q=q_s[rows, :], k=k_s[rows, :], qi=qi_s[d, rows, :], kw=kw_s[d, rows, :],
                    va=jnp.concatenate([v_ref[0, rows, :].astype(BF16), ones], axis=1),
                    expo=expo_s[d][rows, :] + r_row, floor=floor_s[d][rows, :],
                    wp=jnp.concatenate([wp, wp], axis=1))
        qk, kv = {}, {}
        for u in units:
            o = ops[u]
            qk[u] = _dot_nt(o["q"], _stack_heads(o["k"]))
            kv[u] = _dot_tn(o["kw"], o["va"])
        st_in = {}
        for d in range(2):
            st = st_s[d]
            for g in range(G):
                st_in[d, g] = st.astype(BF16)
                st = st * ops[d, g]["wp"] + jnp.where(blockdiag, kv[d, g], 0.0)
            st_s[d] = st
        for u in units:
            d = u[0]
            o = ops[u]
            smat = (qk[u] * jnp.where(masks[d], jnp.exp(o["expo"]), 0.0)).astype(BF16)
            va_bd = jnp.where(blockdiag, jnp.concatenate([o["va"], o["va"]], axis=0), jnp.zeros((), BF16))
            num = _dot(jnp.concatenate([o["qi"], smat], axis=1), jnp.concatenate([st_in[u], va_bd], axis=0))
            den = jnp.maximum(jnp.abs(num[:, LANES:]), o["floor"])
            outs[d][o["rows"], :] = num[:, :LANES] / den
        return 0

    lax.fori_loop(0, nc // G, step, 0)

    def post(t, _):
        r0 = pl.multiple_of(t * R, R)
        h = _head_rms(of_s[pl.ds(r0, R), :] + ob_s[pl.ds(r0, R), :], gain_ref[...])
        o_ref[0, pl.ds(r0, R), :] = (h * _sigmoid(og_ref[0, pl.ds(r0, R), :])).astype(BF16)
        return 0

    lax.fori_loop(0, T // R, post, 0)


def _mlstm(ml, small, lp):
    B, T, _ = ml.shape
    pairs = MLSTM_WIDTH // LANES
    col = lambda off: (lambda b, hp: (b, 0, off * pairs + hp))
    const = lambda b, hp: (0, 0)
    return pl.pallas_call(
        _mlstm_kernel,
        grid=(B, pairs),
        in_specs=[
            pl.BlockSpec((1, T, LANES), col(0)),
            pl.BlockSpec((1, T, LANES), col(1)),
            pl.BlockSpec((1, T, LANES), col(2)),
            pl.BlockSpec((1, T, LANES), col(3)),
            pl.BlockSpec((1, T, LANES), lambda b, hp: (b, 0, 0)),
            pl.BlockSpec((3, LANES), lambda b, hp: (0, hp)),
            pl.BlockSpec((3, LANES), lambda b, hp: (0, pairs + hp)),
            pl.BlockSpec((1, LANES), lambda b, hp: (0, hp)),
            pl.BlockSpec((1, LANES), lambda b, hp: (0, pairs + hp)),
            pl.BlockSpec((1, LANES), const),
            pl.BlockSpec((1, LANES), const),
        ],
        out_specs=pl.BlockSpec((1, T, LANES), lambda b, hp: (b, 0, hp)),
        out_shape=jax.ShapeDtypeStruct((B, T, MLSTM_WIDTH), BF16),
        scratch_shapes=(
            [pltpu.VMEM((T, LANES), BF16)] * 2
            + [pltpu.VMEM((T, LANES), F32)] * 4
            + [pltpu.VMEM(((T // CHUNK) * SUBLANES, LANES), F32)] * 4
            + [pltpu.VMEM((2, (T // CHUNK) * SUBLANES, LANES), F32)]
            + [pltpu.VMEM((2, T, LANES), BF16)] * 2
            + [pltpu.VMEM((LANES, T), F32),
               pltpu.VMEM((2, LANES, 2 * LANES), F32),
               pltpu.VMEM((T, LANES), F32), pltpu.VMEM((T, LANES), F32)]),
        compiler_params=_params("arbitrary", "arbitrary"),
        name="mlstm",
    )(ml, ml, ml, ml, small, lp["conv_w"], lp["conv_w"], lp["conv_b"], lp["conv_b"],
      lp["gate_bias"], lp["ml_gain"])


def _first_argmax(vals, row):
    mx = jnp.max(vals, axis=0, keepdims=True)
    idx = jnp.min(jnp.where(vals == mx, row, vals.shape[0]), axis=0, keepdims=True)
    return mx, idx


def _out_route_kernel(x_ref, a_ref, gl_ref, ml_ref, w_ref, g_ref, wr_ref, br_ref,
                      x1_ref, t_ref, route_ref, routet_ref):
    tm = x_ref.shape[0]
    sub = min(tm, ROUTE_SUBTILE)
    spans = [slice(r, r + sub) for r in range(0, tm, sub)]
    x1s, logits = [], []
    for rs in spans:
        x1 = (x_ref[rs, :]
              + _dot(a_ref[rs, :], w_ref[0:ATTN_WIDTH, :])
              + _dot(gl_ref[rs, :], w_ref[ATTN_WIDTH:ATTN_WIDTH + GLA_WIDTH, :])
              + _dot(ml_ref[rs, :], w_ref[ATTN_WIDTH + GLA_WIDTH:, :]))
        x1_ref[rs, :] = x1
        x1s.append(x1)
    for rs, x1 in zip(spans, x1s):
        t_hi, t_lo = _split(_rms(x1, g_ref[...]))
        t_ref[rs, 0:t_hi.shape[1]] = t_hi
        both = _dot(t_hi, wr_ref[...])
        logits.append((both[:, :LANES] + both[:, LANES:] + _dot(t_lo, wr_ref[:, :LANES])) + br_ref[...])
    grow = lax.broadcasted_iota(jnp.int32, (SUBLANES, sub), 0)
    erow = lax.broadcasted_iota(jnp.int32, (N_EXPERTS, sub), 0)
    for rs, lg in zip(spans, logits):
        lt = lg.T
        gl = jnp.where(grow < N_GROUPS, lt[0:SUBLANES, :], -jnp.inf)
        gmax, gi = _first_argmax(gl, grow)
        g_prob = 1.0 / jnp.sum(jnp.exp(gl - gmax), axis=0, keepdims=True)
        el = jnp.where(erow // EXPERTS_PER_GROUP == gi, lt[ROUTE_W_LANE:ROUTE_W_LANE + N_EXPERTS, :], -jnp.inf)
        v1, i1 = _first_argmax(el, erow)
        v2, i2 = _first_argmax(jnp.where(erow == i1, -jnp.inf, el), erow)
        e2 = jnp.exp(v2 - v1)
        w1 = g_prob / (1.0 + e2)
        w2 = g_prob * e2 / (1.0 + e2)
        comb = jnp.where(erow == i1, w1, jnp.where(erow == i2, w2, 0.0))
        head = jnp.where(grow == 0, gi.astype(F32), 0.0)
        route_t = jnp.concatenate(
            [head, comb, jnp.zeros((LANES - SUBLANES - N_EXPERTS, sub), F32)], axis=0)
        route_ref[rs, :] = route_t.T
        routet_ref[:, rs] = head


def _dispatch_kernel(t_ref, route_ref, routet_ref, xs_ref, cws_ref, pos_ref, cnt_ref):
    nb = ROUTE_BLOCK
    gi_row = routet_ref[0:1, :]
    sub = lax.broadcasted_iota(jnp.int32, (SUBLANES, nb), 0).astype(F32)
    onehot = (sub == gi_row)
    ri = lax.broadcasted_iota(jnp.int32, (nb, nb), 0)
    cj = lax.broadcasted_iota(jnp.int32, (nb, nb), 1)
    before = (ri < cj).astype(BF16)
    rank = _dot(onehot.astype(BF16), before)
    counts = jnp.broadcast_to(jnp.sum(onehot.astype(F32), axis=-1, keepdims=True), (SUBLANES, LANES))
    padded = jnp.ceil(counts * (1.0 / ROUTE_TILE)) * ROUTE_TILE
    srow = lax.broadcasted_iota(jnp.int32, (SUBLANES, LANES), 0)
    incl = padded
    s = 1
    while s < SUBLANES:
        incl = incl + jnp.where(srow >= s, pltpu.roll(incl, s, 0), 0.0)
        s *= 2
    start = incl - padded
    pos_row = jnp.sum(jnp.where(onehot, start[:, 0:1] + rank, 0.0), axis=0, keepdims=True)
    cnt_ref[0] = counts.astype(jnp.int32)

    pos_hi = jnp.floor(pos_row * (1.0 / ROUTE_TILE))
    prow = lax.broadcasted_iota(jnp.int32, (2 * SUBLANES, nb), 0)
    pieces = jnp.where(prow == 0, pos_hi, jnp.where(prow == 1, pos_row - ROUTE_TILE * pos_hi, 0.0))
    wrow = lax.broadcasted_iota(jnp.int32, (2 * SUBLANES, LANES), 0)
    weights = jnp.where(wrow == 0, float(ROUTE_TILE), jnp.where(wrow == 1, 1.0, 0.0))
    pos_ref[...] = _dot_tn(pieces.astype(BF16), weights.astype(BF16))

    route = route_ref[...]
    lane = lax.broadcasted_iota(jnp.int32, (nb, LANES), 1)
    comb = jnp.where(lane >= ROUTE_W_LANE, route, 0.0)
    c_hi, c_lo = _split(comb)
    c_lo2 = (comb - c_hi.astype(F32) - c_lo.astype(F32)).astype(BF16)
    d_model = xs_ref.shape[1]
    t_ref[:, d_model:] = (c_hi.astype(F32) + pltpu.roll(c_lo.astype(F32), ROUTE_LO_SHIFT, 1)
                          + pltpu.roll(c_lo2.astype(F32), 2 * ROUTE_LO_SHIFT, 1)).astype(BF16)
    tb = t_ref[...]
    for r in range(TILES_PER_BLOCK):
        rows = (lax.broadcasted_iota(jnp.int32, (ROUTE_TILE, nb), 0) + r * ROUTE_TILE).astype(F32)
        perm = (rows == pos_row).astype(BF16)
        moved = _dot(perm, tb)
        xs_ref[r * ROUTE_TILE:(r + 1) * ROUTE_TILE, :] = moved[:, :d_model].astype(BF16)
        cws_ref[r * ROUTE_TILE:(r + 1) * ROUTE_TILE, :] = moved[:, d_model:]


def _route_dispatch_kernel(x_ref, a_ref, gl_ref, ml_ref, w_ref, g_ref, wr_ref, br_ref,
                           x1_ref, xs_ref, cws_ref, pos_ref, cnt_ref, t_s, route_s, routet_s):
    _out_route_kernel(x_ref, a_ref, gl_ref, ml_ref, w_ref, g_ref, wr_ref, br_ref,
                      x1_ref, t_s, route_s, routet_s)
    _dispatch_kernel(t_s, route_s, routet_s, xs_ref, cws_ref, pos_ref, cnt_ref)


def _route_dispatch(x2, attn, gla_o, ml_o, lp):
    N, D = x2.shape
    nblk = N // ROUTE_BLOCK
    const = lambda i: (0, 0)
    tok = lambda i: (i, 0)
    return pl.pallas_call(
        _route_dispatch_kernel,
        grid=(nblk,),
        in_specs=[
            pl.BlockSpec((ROUTE_BLOCK, D), tok),
            pl.BlockSpec((ROUTE_BLOCK, ATTN_WIDTH), tok),
            pl.BlockSpec((ROUTE_BLOCK, GLA_WIDTH), tok),
            pl.BlockSpec((ROUTE_BLOCK, MLSTM_WIDTH), tok),
            pl.BlockSpec((ATTN_WIDTH + GLA_WIDTH + MLSTM_WIDTH, D), const),
            pl.BlockSpec((1, D), const),
            pl.BlockSpec((D, 2 * LANES), const),
            pl.BlockSpec((1, LANES), const),
        ],
        out_specs=[
            pl.BlockSpec((ROUTE_BLOCK, D), tok),
            pl.BlockSpec((ROUTE_ROWS, D), tok),
            pl.BlockSpec((ROUTE_ROWS, LANES), tok),
            pl.BlockSpec((ROUTE_BLOCK, LANES), tok),
            pl.BlockSpec((1, SUBLANES, LANES), lambda i: (i, 0, 0)),
        ],
        out_shape=[
            jax.ShapeDtypeStruct((N, D), F32),
            jax.ShapeDtypeStruct((nblk * ROUTE_ROWS, D), BF16),
            jax.ShapeDtypeStruct((nblk * ROUTE_ROWS, LANES), F32),
            jax.ShapeDtypeStruct((N, LANES), F32),
            jax.ShapeDtypeStruct((nblk, SUBLANES, LANES), jnp.int32),
        ],
        scratch_shapes=[
            pltpu.VMEM((ROUTE_BLOCK, D + LANES), BF16),
            pltpu.VMEM((ROUTE_BLOCK, LANES), F32),
            pltpu.VMEM((SUBLANES, ROUTE_BLOCK), F32),
        ],
        compiler_params=_params("arbitrary"),
        name="route_dispatch",
    )(x2, attn, gla_o, ml_o, lp["w_out"], lp["g_ffn"], lp["w_route"], lp["b_route"])


def _tile_schedule(cnt):
    nblk = cnt.shape[0]
    ntile = (cnt + ROUTE_TILE - 1) // ROUTE_TILE
    end = jnp.cumsum(ntile, axis=1)
    r = jnp.arange(TILES_PER_BLOCK, dtype=jnp.int32)
    grp = jnp.sum(r[None, :, None] >= end[:, None, :], axis=-1)
    grp = grp.reshape(-1).astype(jnp.int32)
    tile = jnp.arange(nblk * TILES_PER_BLOCK, dtype=jnp.int32)
    order = jnp.argsort(grp * (nblk * TILES_PER_BLOCK) + tile).astype(jnp.int32)
    n_active = jnp.sum(grp < N_GROUPS).astype(jnp.int32)
    g_sorted = grp[order]
    last_group = g_sorted[jnp.maximum(n_active - 1, 0)]
    g_sorted = jnp.where(g_sorted < N_GROUPS, g_sorted, last_group)
    return order, g_sorted, n_active[None]


def _expert_kernel(trow_ref, tgrp_ref, nact_ref, xs_ref, cws_ref, wg_ref, wu_ref, wd_ref, ys_ref,
                   wg_s, wu_s, wd_s):
    i = pl.program_id(0)

    @pl.when((i == 0) | (tgrp_ref[i] != tgrp_ref[jnp.maximum(i - 1, 0)]))
    def _():
        for j in range(EXPERTS_PER_GROUP):
            wg_s[j] = wg_ref[j].astype(BF16)
            wu_s[j] = wu_ref[j].astype(BF16)
            wd_s[j] = wd_ref[j].astype(BF16)

    @pl.when(i < nact_ref[0])
    def _():
        x = xs_ref[...]
        cws = cws_ref[...]
        lane = lax.broadcasted_iota(jnp.int32, cws.shape, 1)
        first = ROUTE_W_LANE + tgrp_ref[i] * EXPERTS_PER_GROUP
        y = jnp.zeros(ys_ref.shape, F32)

        def hidden(j):
            return _dot(x, wg_s[j]), _dot(x, wu_s[j])

        h = hidden(0)
        for j in range(EXPERTS_PER_GROUP):
            h_next = hidden(j + 1) if j + 1 < EXPERTS_PER_GROUP else None
            off = lane - (first + j)
            sel = (off == 0) | (off == ROUTE_LO_SHIFT) | (off == 2 * ROUTE_LO_SHIFT)
            wj = jnp.sum(jnp.where(sel, cws, 0.0), axis=-1, keepdims=True)
            a = (h[0] * _sigmoid(h[0]) * h[1]).astype(BF16)
            y = y + wj * _dot(a, wd_s[j])
            h = h_next
        ys_ref[...] = y.astype(BF16)

    @pl.when(i >= nact_ref[0])
    def _():
        ys_ref[...] = jnp.zeros(ys_ref.shape, BF16)


def _experts(xs, cws, order, grp, n_active, lp):
    rows, D = xs.shape
    n_tiles = rows // ROUTE_TILE
    tile = lambda i, trow, tgrp, nact: (trow[i], 0)
    layer_groups = lp["layer"] * N_GROUPS
    wsel = lambda i, trow, tgrp, nact: (layer_groups + tgrp[i], 0, 0)
    once = pl.Buffered(1)
    return pl.pallas_call(
        _expert_kernel,
        grid_spec=pltpu.PrefetchScalarGridSpec(
            num_scalar_prefetch=3,
            grid=(n_tiles,),
            in_specs=[
                pl.BlockSpec((ROUTE_TILE, D), tile),
                pl.BlockSpec((ROUTE_TILE, LANES), tile),
                pl.BlockSpec((EXPERTS_PER_GROUP, D, D_EXPERT), wsel, pipeline_mode=once),
                pl.BlockSpec((EXPERTS_PER_GROUP, D, D_EXPERT), wsel, pipeline_mode=once),
                pl.BlockSpec((EXPERTS_PER_GROUP, D_EXPERT, D), wsel, pipeline_mode=once),
            ],
            out_specs=pl.BlockSpec((ROUTE_TILE, D), tile),
            scratch_shapes=[
                pltpu.VMEM((EXPERTS_PER_GROUP, D, D_EXPERT), BF16),
                pltpu.VMEM((EXPERTS_PER_GROUP, D, D_EXPERT), BF16),
                pltpu.VMEM((EXPERTS_PER_GROUP, D_EXPERT, D), BF16),
            ],
        ),
        out_shape=jax.ShapeDtypeStruct((rows, D), BF16),
        compiler_params=_params("arbitrary"),
        name="experts",
    )(order, grp, n_active, xs, cws, lp["w_gate"], lp["w_up"], lp["w_down"])


def _combine_kernel(ys_ref, pos_ref, x1_ref, p_ref, g_ref, wpg_ref, wpp_ref, gfin_ref, o_ref,
                    *, embed, final):
    tm = x1_ref.shape[0]
    pos = pos_ref[...]
    lane = lax.broadcasted_iota(jnp.int32, (tm, LANES), 1).astype(F32)
    perm_t = jnp.concatenate(
        [(lane + r * LANES == pos).astype(BF16) for r in range(ROUTE_ROWS // LANES)], axis=1)
    x = x1_ref[...] + _dot(perm_t, ys_ref[...])
    if embed:
        gate = _sigmoid(_dot(_rms(x, g_ref[...]).astype(BF16), wpg_ref[...]))
        x = x + gate * _dot(p_ref[...].astype(BF16), wpp_ref[...])
    if final:
        x = _rms(x, gfin_ref[...])
    o_ref[...] = x


def _combine(ys, pos, x1, p3, lp, g_final, embed=True, final=False, tm=512):
    N, D = x1.shape
    inner = ROUTE_BLOCK // tm
    layer = lp["layer"]
    tok = lambda b, i: (b * inner + i, 0)
    const = lambda b, i: (0, 0)
    return pl.pallas_call(
        functools.partial(_combine_kernel, embed=embed, final=final),
        grid=(N // ROUTE_BLOCK, inner),
        in_specs=[
            pl.BlockSpec((ROUTE_ROWS, D), lambda b, i: (b, 0)),
            pl.BlockSpec((tm, LANES), tok),
            pl.BlockSpec((tm, D), tok),
            pl.BlockSpec((None, tm, p3.shape[2]), lambda b, i: (layer, b * inner + i, 0)),
            pl.BlockSpec((1, D), const),
            pl.BlockSpec((D, D), const),
            pl.BlockSpec((p3.shape[2], D), const),
            pl.BlockSpec((1, D), const),
        ],
        out_specs=pl.BlockSpec((tm, D), tok),
        out_shape=jax.ShapeDtypeStruct((N, D), F32),
        compiler_params=_params("arbitrary", "arbitrary"),
        name="combine",
    )(ys, pos, x1, p3, lp["g_ple"], lp["w_pg"], lp["w_pp"], g_final)


def _mix_out_moe(x2, attn, gla_o, ml_o, lp):
    x1, xs, cws, pos, cnt = _route_dispatch(x2, attn, gla_o, ml_o, lp)
    order, grp, n_active = _tile_schedule(cnt[:, :N_GROUPS, 0])
    return x1, _experts(xs, cws, order, grp, n_active, lp), pos


def _moe_debug(x2, lp):
    N, D = x2.shape
    zeros = lambda w: jnp.zeros((N, w), BF16)
    x1, ys, pos = _mix_out_moe(x2, zeros(ATTN_WIDTH), zeros(GLA_WIDTH), zeros(MLSTM_WIDTH), lp)
    p3 = jnp.zeros((lp["layer"] + 1, N, lp["w_pp"].shape[0]), F32)
    return _combine(ys, pos, x1, p3, lp, lp["g_ple"], embed=False) - x1


def _permute_in_cols(w):
    glr0 = ATTN_WIDTH + 2 * KV_WIDTH + 4 * GLA_WIDTH
    ml0 = glr0 + 2 * GLA_RANK
    mg0 = ml0 + 4 * MLSTM_WIDTH
    end = mg0 + 4 * MLSTM_HEADS
    assert end == w.shape[-1]
    pad = jnp.zeros(w.shape[:-1] + (IN_PERM_WIDTH - end,), BF16)
    parts = [w[..., :glr0], w[..., ml0:mg0], w[..., glr0:ml0], w[..., mg0:end]]
    return jnp.concatenate([part.astype(BF16) for part in parts] + [pad], axis=-1)


def _rope_tables(T):
    t = np.arange(T)
    inv = ROPE_THETA ** (-np.arange(0, ROPE_AXIS_DIM, 2, dtype=np.float64) / ROPE_AXIS_DIM)
    ang_r = (t // GRID_W)[None, :] * inv[:, None]
    ang_c = (t % GRID_W)[None, :] * inv[:, None]
    cos_h = np.concatenate([np.cos(ang_r), np.cos(ang_r), np.cos(ang_c), np.cos(ang_c)], axis=0)
    sin_h = np.concatenate([-np.sin(ang_r), np.sin(ang_r), -np.sin(ang_c), np.sin(ang_c)], axis=0)
    reps = LANES // HEAD_DIM
    return dict(cos_t=jnp.asarray(np.tile(cos_h, (reps, 1)), F32),
                sin_t=jnp.asarray(np.tile(sin_h, (reps, 1)), F32))


def kernel(x, p, norm_mix_g, w_in, attn_q_norm_g, attn_k_norm_g, gla_w_decay, gla_b_decay,
           gla_out_norm_g, mlstm_conv_w, mlstm_conv_b, mlstm_b_input, mlstm_b_forget,
           mlstm_out_norm_g, w_out, norm_ffn_g, w_group, b_group, w_router, b_router,
           w_expert_gate, w_expert_up, w_expert_down, norm_ple_g, w_ple_gate, w_ple_proj,
           final_norm_g):
    params = dict(
        norm_mix_g=norm_mix_g, w_in=w_in, attn_q_norm_g=attn_q_norm_g, attn_k_norm_g=attn_k_norm_g,
        gla_w_decay=gla_w_decay, gla_b_decay=gla_b_decay, gla_out_norm_g=gla_out_norm_g,
        mlstm_conv_w=mlstm_conv_w, mlstm_conv_b=mlstm_conv_b, mlstm_b_input=mlstm_b_input,
        mlstm_b_forget=mlstm_b_forget, mlstm_out_norm_g=mlstm_out_norm_g, w_out=w_out,
        norm_ffn_g=norm_ffn_g, w_group=w_group, b_group=b_group, w_router=w_router,
        b_router=b_router, w_expert_gate=w_expert_gate, w_expert_up=w_expert_up,
        w_expert_down=w_expert_down, norm_ple_g=norm_ple_g, w_ple_gate=w_ple_gate,
        w_ple_proj=w_ple_proj)
    B, T, D = x.shape
    rope = _rope_tables(T)
    N = B * T
    depth = w_in.shape[0]
    x2 = x.reshape(N, D)
    g_final = final_norm_g[None, :]
    p3 = p.reshape(depth, N, p.shape[-1])
    for i in range(depth):
        lp = _layer_params(params, i)
        qt, k, vt, gla, ml, small = _in_proj(x2, lp, rope, B, T)
        attn = _attention(lp["attn_safe"], qt, k, vt).reshape(N, ATTN_WIDTH)
        gla_o = _gla(gla, small, lp).reshape(N, GLA_WIDTH)
        ml_o = _mlstm(ml, small, lp).reshape(N, MLSTM_WIDTH)
        x1, ys, pos = _mix_out_moe(x2, attn, gla_o, ml_o, lp)
        x2 = _combine(ys, pos, x1, p3, lp, g_final, final=(i == depth - 1))
    return x2.reshape(B, T, D)


def _split_w(w):
    hi = w.astype(BF16)
    return hi, (w - hi.astype(F32)).astype(BF16)


def _stacked_experts(w):
    return w.reshape((w.shape[0] * w.shape[1],) + w.shape[2:])


def _layer_params(p, i):
    D = p["w_in"].shape[1]
    gq, gk = p["attn_q_norm_g"][i], p["attn_k_norm_g"][i]
    q_gain = jnp.tile(gq, LANES // HEAD_DIM) * (HEAD_DIM ** -0.5 * LOG2E)
    logit_bound = HEAD_DIM ** 0.5 * jnp.max(jnp.abs(gq)) * jnp.max(jnp.abs(gk))
    attn_safe = (logit_bound <= ATTN_SAFE_LOGIT).astype(jnp.int32)[None]
    wd = jnp.zeros((2, LANES, GLA_WIDTH), F32)
    wd = wd.at[0, :GLA_RANK].set(p["gla_w_decay"][i, 0]).at[1, GLA_RANK:2 * GLA_RANK].set(p["gla_w_decay"][i, 1])
    wd_hi, wd_lo = _split_w(wd)
    gate_bias = jnp.zeros((LANES,), F32).at[SMALL_GATE_LANE:SMALL_GATE_LANE + 4 * MLSTM_HEADS].set(
        jnp.concatenate([p["mlstm_b_input"][i, 0], p["mlstm_b_forget"][i, 0],
                         p["mlstm_b_input"][i, 1], p["mlstm_b_forget"][i, 1]]))
    w_route = jnp.zeros((D, LANES), F32)
    w_route = w_route.at[:, :N_GROUPS].set(p["w_group"][i])
    w_route = w_route.at[:, ROUTE_W_LANE:ROUTE_W_LANE + N_EXPERTS].set(p["w_router"][i])
    wr_hi, wr_lo = _split_w(w_route)
    b_route = jnp.zeros((LANES,), F32).at[:N_GROUPS].set(p["b_group"][i])
    b_route = b_route.at[ROUTE_W_LANE:ROUTE_W_LANE + N_EXPERTS].set(p["b_router"][i])
    return dict(
        g_mix=p["norm_mix_g"][i][None, :],
        w_in=_permute_in_cols(p["w_in"]),
        q_gain_t=jnp.broadcast_to(q_gain[:, None], (LANES, LANES)),
        k_gain_t=jnp.broadcast_to(jnp.tile(gk, LANES // HEAD_DIM)[:, None], (LANES, LANES)),
        attn_safe=attn_safe,
        wd_hi=wd_hi, wd_lo=wd_lo,
        bd=p["gla_b_decay"][i][:, None, :],
        gla_gain=jnp.tile(p["gla_out_norm_g"][i], LANES // HEAD_DIM)[None, :],
        conv_w=p["mlstm_conv_w"][i],
        conv_b=p["mlstm_conv_b"][i][None, :],
        gate_bias=gate_bias[None, :],
        ml_gain=jnp.tile(p["mlstm_out_norm_g"][i], LANES // HEAD_DIM)[None, :],
        w_out=p["w_out"][i].astype(BF16),
        g_ffn=p["norm_ffn_g"][i][None, :],
        w_route=jnp.concatenate([wr_hi, wr_lo], axis=1), b_route=b_route[None, :],
        layer=i,
        w_gate=_stacked_experts(p["w_expert_gate"]),
        w_up=_stacked_experts(p["w_expert_up"]),
        w_down=_stacked_experts(p["w_expert_down"]),
        g_ple=p["norm_ple_g"][i][None, :],
        w_pg=p["w_ple_gate"][i].astype(BF16),
        w_pp=p["w_ple_proj"][i].astype(BF16),
    )
```

```python
import functools

import jax
import jax.numpy as jnp
import numpy as np
from jax import lax
from jax.experimental import pallas as pl
from jax.experimental.pallas import tpu as pltpu

F32 = jnp.float32
BF16 = jnp.bfloat16

GRID_W = 64
HEAD_DIM = 64
ATTN_HEADS = 8
ATTN_KV_HEADS = 2
GLA_HEADS = 4
MLSTM_HEADS = 4
ATTN_WIDTH = ATTN_HEADS * HEAD_DIM
KV_WIDTH = ATTN_KV_HEADS * HEAD_DIM
GLA_WIDTH = GLA_HEADS * HEAD_DIM
MLSTM_WIDTH = MLSTM_HEADS * HEAD_DIM
GLA_RANK = 16
GLA_TAU = 16.0
CHUNK = 64
ROPE_THETA = 10000.0
ROPE_AXIS_DIM = HEAD_DIM // 2
N_GROUPS = 4
EXPERTS_PER_GROUP = 4
N_EXPERTS = N_GROUPS * EXPERTS_PER_GROUP
D_EXPERT = 512
EPS = 1e-6
NEG = -1e30

LANES = 128
SUBLANES = 8
VMEM_LIMIT_BYTES = 56 * 1024 * 1024

QK_WIDTH = ATTN_WIDTH + KV_WIDTH
OFF_V = QK_WIDTH
OFF_GLA = OFF_V + KV_WIDTH
OFF_ML = OFF_GLA + 4 * GLA_WIDTH
OFF_SMALL = OFF_ML + 4 * MLSTM_WIDTH
IN_PERM_WIDTH = OFF_SMALL + LANES
SMALL_GATE_LANE = 2 * GLA_RANK

ROUTE_BLOCK = 1024
ROUTE_TILE = 128
TILES_PER_BLOCK = (ROUTE_BLOCK + N_GROUPS * (ROUTE_TILE - 1)) // ROUTE_TILE
ROUTE_ROWS = TILES_PER_BLOCK * ROUTE_TILE
ROUTE_W_LANE = 8
ROUTE_LO_SHIFT = 32
ROUTE_SUBTILE = 256


def _dot(a, b):
    return jnp.dot(a, b, preferred_element_type=F32)


def _dot_nt(a, b):
    return lax.dot_general(a, b, (((1,), (1,)), ((), ())), preferred_element_type=F32)


def _dot_tn(a, b):
    return lax.dot_general(a, b, (((0,), (0,)), ((), ())), preferred_element_type=F32)


def _split(a):
    hi = a.astype(BF16)
    lo = (a - hi.astype(F32)).astype(BF16)
    return hi, lo


def _dot3(a, w_hi, w_lo):
    a_hi, a_lo = _split(a)
    return _dot(a_hi, w_hi) + _dot(a_lo, w_hi) + _dot(a_hi, w_lo)


def _log_sigmoid(x):
    return jnp.minimum(x, 0.0) - jnp.log(1.0 + jnp.exp(-jnp.abs(x)))


def _sigmoid(x):
    return 1.0 / (1.0 + jnp.exp(-x))


def _rms(x, g):
    return x * lax.rsqrt(jnp.mean(x * x, axis=-1, keepdims=True) + EPS) * g


def _params(*semantics):
    return pltpu.CompilerParams(dimension_semantics=semantics, vmem_limit_bytes=VMEM_LIMIT_BYTES)


def _in_proj_kernel(x_ref, g_ref, w_ref, cost_ref, sint_ref, gq_ref, gk_ref,
                    qt_ref, k_ref, vt_ref, gla_ref, ml_ref, small_ref):
    tm = x_ref.shape[0]
    h = _rms(x_ref[...], g_ref[...])
    z = _dot(h.astype(BF16), w_ref[...])

    heads = LANES // HEAD_DIM
    half = ROPE_AXIS_DIM // 2
    gq = jnp.concatenate([gq_ref[...]] * (tm // LANES), axis=1)
    cost = cost_ref[...]
    sint = sint_ref[...]
    gk = jnp.concatenate([gk_ref[...]] * (tm // LANES), axis=1)

    def norm_rope_t(c, gain):
        zt = z[:, c * LANES:(c + 1) * LANES].T
        z3 = zt.reshape(heads, HEAD_DIM, tm)
        inv = lax.rsqrt(jnp.mean(z3 * z3, axis=1, keepdims=True) + EPS)
        y = (z3 * inv).reshape(LANES, tm) * gain
        partner = jnp.concatenate(
            [y[(r ^ 1) * half:((r ^ 1) + 1) * half, :] for r in range(LANES // half)], axis=0)
        return y * cost + partner * sint

    for c in range(ATTN_WIDTH // LANES):
        qt_ref[0, c * LANES:(c + 1) * LANES, :] = norm_rope_t(c, gq).astype(BF16)
    k_ref[0] = norm_rope_t(ATTN_WIDTH // LANES, gk).T.astype(BF16)
    vt_ref[0] = z[:, OFF_V:OFF_V + KV_WIDTH].T.astype(BF16)
    gla_ref[0] = z[:, OFF_GLA:OFF_ML]
    ml_ref[0] = z[:, OFF_ML:OFF_SMALL]
    small_ref[0] = z[:, OFF_SMALL:IN_PERM_WIDTH]


def _in_proj(x2, lp, rope, B, T, tm=512):
    N, D = x2.shape
    tpb = T // tm
    const = lambda i: (0, 0)
    tok3 = lambda i: (i // tpb, i % tpb, 0)
    tokT = lambda i: (i // tpb, 0, i % tpb)
    layer = lp["layer"]
    return pl.pallas_call(
        _in_proj_kernel,
        grid=(N // tm,),
        in_specs=[
            pl.BlockSpec((tm, D), lambda i: (i, 0)),
            pl.BlockSpec((1, D), const),
            pl.BlockSpec((None, D, IN_PERM_WIDTH), lambda i: (layer, 0, 0)),
            pl.BlockSpec((LANES, tm), lambda i: (0, i % tpb)),
            pl.BlockSpec((LANES, tm), lambda i: (0, i % tpb)),
            pl.BlockSpec((LANES, LANES), const),
            pl.BlockSpec((LANES, LANES), const),
        ],
        out_specs=[
            pl.BlockSpec((1, ATTN_WIDTH, tm), tokT),
            pl.BlockSpec((1, tm, KV_WIDTH), tok3),
            pl.BlockSpec((1, KV_WIDTH, tm), tokT),
            pl.BlockSpec((1, tm, 4 * GLA_WIDTH), tok3),
            pl.BlockSpec((1, tm, 4 * MLSTM_WIDTH), tok3),
            pl.BlockSpec((1, tm, LANES), tok3),
        ],
        out_shape=[
            jax.ShapeDtypeStruct((B, ATTN_WIDTH, T), BF16),
            jax.ShapeDtypeStruct((B, T, KV_WIDTH), BF16),
            jax.ShapeDtypeStruct((B, KV_WIDTH, T), BF16),
            jax.ShapeDtypeStruct((B, T, 4 * GLA_WIDTH), F32),
            jax.ShapeDtypeStruct((B, T, 4 * MLSTM_WIDTH), F32),
            jax.ShapeDtypeStruct((B, T, LANES), F32),
        ],
        compiler_params=_params("arbitrary"),
        name="in_proj",
    )(x2, lp["g_mix"], lp["w_in"], rope["cos_t"], rope["sin_t"], lp["q_gain_t"], lp["k_gain_t"])


ATTN_SAFE_LOGIT = 40.0
LOG2E = 1.4426950408889634


def _attn_kernel(safe_ref, qt_ref, k_ref, vt_ref, o_ref, *, tk):
    tq = qt_ref.shape[2]
    T = k_ref.shape[1]
    G = ATTN_HEADS // ATTN_KV_HEADS
    n = G * tq
    zeros = jnp.zeros((HEAD_DIM, n), BF16)

    def q_operand(j):
        base = j * G * HEAD_DIM
        qs = jnp.concatenate(
            [qt_ref[0, base + h * HEAD_DIM:base + (h + 1) * HEAD_DIM, :] for h in range(G)], axis=1)
        return jnp.concatenate([qs, zeros] if j == 0 else [zeros, qs], axis=0)

    def finish(j, acc, l):
        base = j * G * HEAD_DIM
        o = acc * (1.0 / l)
        ot = jnp.concatenate([o[:, h * tq:(h + 1) * tq] for h in range(G)], axis=0)
        o_ref[0, :, base:base + G * HEAD_DIM] = ot.T.astype(BF16)

    @pl.when(safe_ref[0] == 1)
    def _():
        qps = [q_operand(j) for j in range(ATTN_KV_HEADS)]
        units = [(c, j) for c in range(T // tk) for j in range(ATTN_KV_HEADS)]

        def scores(u):
            c, j = units[u]
            return _dot(k_ref[0, c * tk:(c + 1) * tk, :], qps[j])

        l8 = [jnp.zeros((SUBLANES, n), F32)] * ATTN_KV_HEADS
        acc = [jnp.zeros((HEAD_DIM, n), F32)] * ATTN_KV_HEADS
        st = scores(0)
        for u, (c, j) in enumerate(units):
            st_next = scores(u + 1) if u + 1 < len(units) else None
            p = jnp.exp2(st)
            l8[j] = l8[j] + jnp.sum(p.reshape(tk // SUBLANES, SUBLANES, n), axis=0)
            vc = vt_ref[0, j * HEAD_DIM:(j + 1) * HEAD_DIM, c * tk:(c + 1) * tk]
            acc[j] = acc[j] + _dot(vc, p.astype(BF16))
            st = st_next
        for j in range(ATTN_KV_HEADS):
            finish(j, acc[j], jnp.sum(l8[j], axis=0, keepdims=True))

    @pl.when(safe_ref[0] == 0)
    def _():
        for j in range(ATTN_KV_HEADS):
            qp = q_operand(j)

            def body(c, carry, qp=qp, j=j):
                m, l, acc = carry
                off = pl.multiple_of(c * tk, tk)
                st = _dot(k_ref[0, pl.ds(off, tk), :], qp)
                m_new = jnp.maximum(m, jnp.max(st, axis=0, keepdims=True))
                alpha = jnp.exp2(m - m_new)
                p = jnp.exp2(st - m_new)
                l = alpha * l + jnp.sum(p, axis=0, keepdims=True)
                vc = vt_ref[0, j * HEAD_DIM:(j + 1) * HEAD_DIM, pl.ds(off, tk)]
                return m_new, l, alpha * acc + _dot(vc, p.astype(BF16))

            init = (jnp.full((1, n), NEG, F32), jnp.zeros((1, n), F32), jnp.zeros((HEAD_DIM, n), F32))
            _, l, acc = lax.fori_loop(0, T // tk, body, init)
            finish(j, acc, l)


def _attention(safe, qt, k, vt, tq=256, tk=128):
    B, _, T = qt.shape
    tk = min(tk, T)
    return pl.pallas_call(
        functools.partial(_attn_kernel, tk=tk),
        grid_spec=pltpu.PrefetchScalarGridSpec(
            num_scalar_prefetch=1,
            grid=(B, T // tq),
            in_specs=[
                pl.BlockSpec((1, ATTN_WIDTH, tq), lambda b, i, s: (b, 0, i)),
                pl.BlockSpec((1, T, KV_WIDTH), lambda b, i, s: (b, 0, 0)),
                pl.BlockSpec((1, KV_WIDTH, T), lambda b, i, s: (b, 0, 0)),
            ],
            out_specs=pl.BlockSpec((1, tq, ATTN_WIDTH), lambda b, i, s: (b, i, 0)),
        ),
        out_shape=jax.ShapeDtypeStruct((B, T, ATTN_WIDTH), BF16),
        compiler_params=_params("arbitrary", "arbitrary"),
        name="attention",
    )(safe, qt, k, vt)


def _chunk_scan(x, pos, op, fill, reverse):
    rows = x.shape[0]
    s = 1
    while s < CHUNK:
        if reverse:
            shifted = jnp.where(pos < CHUNK - s, pltpu.roll(x, rows - s, 0), fill)
        else:
            shifted = jnp.where(pos >= s, pltpu.roll(x, s, 0), fill)
        x = op(x, shifted)
        s *= 2
    return x


def _chunk_pos(rows):
    return lax.broadcasted_iota(jnp.int32, (rows, LANES), 0) % CHUNK


def _chunk_row(a, reverse_dir, idx_fwd, idx_bwd):
    rows = a.shape[0]
    a3 = a.reshape(rows // CHUNK, CHUNK, LANES)
    i = idx_bwd if reverse_dir else idx_fwd
    return jnp.broadcast_to(a3[:, i:i + 1, :], a3.shape).reshape(rows, LANES)


def _stack_heads(x):
    lane = lax.broadcasted_iota(jnp.int32, x.shape, 1)
    zero = jnp.zeros_like(x)
    return jnp.concatenate([jnp.where(lane < HEAD_DIM, x, zero), jnp.where(lane >= HEAD_DIM, x, zero)], axis=0)


def _select_heads(x):
    c = x.shape[0] // 2
    lane = lax.broadcasted_iota(jnp.int32, (c, x.shape[1]), 1)
    return jnp.where(lane < HEAD_DIM, x[:c], x[c:])


def _stacked_causal_masks():
    ci = lax.broadcasted_iota(jnp.int32, (2 * CHUNK, CHUNK), 0) % CHUNK
    si = lax.broadcasted_iota(jnp.int32, (2 * CHUNK, CHUNK), 1)
    return ci >= si, ci <= si


def _pair_blockdiag(width):
    r = lax.broadcasted_iota(jnp.int32, (LANES, width), 0) // HEAD_DIM
    c = lax.broadcasted_iota(jnp.int32, (LANES, width), 1) // (width // 2)
    return r == c


def _head_rms(o, gain):
    lane = lax.broadcasted_iota(jnp.int32, o.shape, 1)
    lo = lane < HEAD_DIM
    sq = o * o
    s_lo = jnp.sum(jnp.where(lo, sq, 0.0), axis=-1, keepdims=True)
    s_hi = jnp.sum(jnp.where(lo, 0.0, sq), axis=-1, keepdims=True)
    ms = jnp.where(lo, s_lo, s_hi) * (1.0 / HEAD_DIM)
    return o * lax.rsqrt(ms + EPS) * gain


PREP_ROWS = 1024
GLA_CHUNKS_PER_STEP = 8


def _gla_kernel(q_ref, k_ref, v_ref, g_ref, small_ref, wdh_ref, wdl_ref, bd_ref, gain_ref,
                o_ref, qe_s, ke_s, dec_s, of_s, ob_s):
    T = q_ref.shape[1]
    nc = T // CHUNK
    R = min(PREP_ROWS, T)
    cpt = R // CHUNK
    pos = _chunk_pos(R)
    w_hi = jnp.concatenate([wdh_ref[0], wdh_ref[1]], axis=1)
    w_lo = jnp.concatenate([wdl_ref[0], wdl_ref[1]], axis=1)
    bias = jnp.concatenate([bd_ref[0], bd_ref[1]], axis=1)

    def prep(t, _):
        r0 = pl.multiple_of(t * R, R)
        q = q_ref[0, pl.ds(r0, R), :] * HEAD_DIM ** -0.5
        k = k_ref[0, pl.ds(r0, R), :]
        la2 = _log_sigmoid(_dot3(small_ref[0, pl.ds(r0, R), :], w_hi, w_lo) + bias) * (1.0 / GLA_TAU)
        for d in range(2):
            b = _chunk_scan(la2[:, d * LANES:(d + 1) * LANES], pos, jnp.add, 0.0, reverse=bool(d))
            b_mid = _chunk_row(b, d, CHUNK // 2 - 1, CHUNK // 2)
            b_last = _chunk_row(b, d, CHUNK - 1, 0)
            qe_s[d, pl.ds(r0, R), :] = (q * jnp.exp(b - b_mid)).astype(BF16)
            ke_s[d, pl.ds(r0, R), :] = (k * jnp.exp(b_mid - b)).astype(BF16)
            c8 = pl.ds(pl.multiple_of(t * cpt * SUBLANES, SUBLANES), cpt * SUBLANES)
            for kind, val in enumerate((b_last, b_last - b_mid, b_mid)):
                rows8 = val.reshape(cpt, CHUNK, LANES)[:, :SUBLANES, :].reshape(cpt * SUBLANES, LANES)
                dec_s[d, kind, c8, :] = jnp.exp(rows8)
        return 0

    lax.fori_loop(0, T // R, prep, 0)

    outs = (of_s, ob_s)
    masks = _stacked_causal_masks()
    blockdiag = _pair_blockdiag(LANES)
    G = GLA_CHUNKS_PER_STEP

    def step(it, states):
        units = [(d, g) for d in range(2) for g in range(G)]
        ops = {}
        for d, g in units:
            c = it * G + g
            c = (nc - 1 - c) if d else c
            r0 = pl.multiple_of(c * CHUNK, CHUNK)
            c8 = pl.ds(pl.multiple_of(c * SUBLANES, SUBLANES), SUBLANES)
            ops[d, g] = dict(
                r0=r0,
                qe=qe_s[d, pl.ds(r0, CHUNK), :], ke=ke_s[d, pl.ds(r0, CHUNK), :],
                v=v_ref[0, pl.ds(r0, CHUNK), :].astype(BF16),
                dec=dec_s[d, 0, c8, :][0:1, :], to_end=dec_s[d, 1, c8, :][0:1, :],
                from_start=dec_s[d, 2, c8, :][0:1, :])
        a2, kv = {}, {}
        for u in units:
            o = ops[u]
            a2[u] = _dot_nt(_stack_heads(o["qe"]), o["ke"])
            kv[u] = _dot_tn(o["v"], o["ke"]) * o["to_end"]
        st_in = {}
        new_states = []
        for d in range(2):
            st = states[d]
            for g in range(G):
                st_in[d, g] = (st * ops[d, g]["from_start"]).astype(BF16)
                st = st * ops[d, g]["dec"] + jnp.where(blockdiag, kv[d, g], 0.0)
            new_states.append(st)
        for u in units:
            o = ops[u]
            a = jnp.where(masks[u[0]], a2[u], 0.0).astype(BF16)
            intra = _select_heads(_dot(a, o["v"]))
            outs[u[0]][pl.ds(o["r0"], CHUNK), :] = intra + _dot_nt(o["qe"], st_in[u])
        return tuple(new_states)

    zero = jnp.zeros((LANES, LANES), F32)
    lax.fori_loop(0, nc // G, step, (zero, zero))

    def post(t, _):
        r0 = pl.multiple_of(t * R, R)
        o = _head_rms(of_s[pl.ds(r0, R), :] + ob_s[pl.ds(r0, R), :], gain_ref[...])
        g = g_ref[0, pl.ds(r0, R), :]
        o_ref[0, pl.ds(r0, R), :] = (o * (g * _sigmoid(g))).astype(BF16)
        return 0

    lax.fori_loop(0, T // R, post, 0)


def _gla(gla, small, lp):
    B, T, _ = gla.shape
    pairs = GLA_WIDTH // LANES
    col = lambda off: (lambda b, hp: (b, 0, off * pairs + hp))
    return pl.pallas_call(
        _gla_kernel,
        grid=(B, pairs),
        in_specs=[
            pl.BlockSpec((1, T, LANES), col(0)),
            pl.BlockSpec((1, T, LANES), col(1)),
            pl.BlockSpec((1, T, LANES), col(2)),
            pl.BlockSpec((1, T, LANES), col(3)),
            pl.BlockSpec((1, T, LANES), lambda b, hp: (b, 0, 0)),
            pl.BlockSpec((2, LANES, LANES), lambda b, hp: (0, 0, hp)),
            pl.BlockSpec((2, LANES, LANES), lambda b, hp: (0, 0, hp)),
            pl.BlockSpec((2, 1, LANES), lambda b, hp: (0, 0, hp)),
            pl.BlockSpec((1, LANES), lambda b, hp: (0, 0)),
        ],
        out_specs=pl.BlockSpec((1, T, LANES), lambda b, hp: (b, 0, hp)),
        out_shape=jax.ShapeDtypeStruct((B, T, GLA_WIDTH), BF16),
        scratch_shapes=[
            pltpu.VMEM((2, T, LANES), BF16), pltpu.VMEM((2, T, LANES), BF16),
            pltpu.VMEM((2, 3, (T // CHUNK) * SUBLANES, LANES), F32),
            pltpu.VMEM((T, LANES), F32), pltpu.VMEM((T, LANES), F32),
        ],
        compiler_params=_params("arbitrary", "arbitrary"),
        name="gla",
    )(gla, gla, gla, gla, small, lp["wd_hi"], lp["wd_lo"], lp["bd"], lp["gla_gain"])


GATE_I = SMALL_GATE_LANE
GATE_F = SMALL_GATE_LANE + MLSTM_HEADS


def _gate_lane(d, hh):
    return GATE_F + 2 * MLSTM_HEADS * d + hh


MLSTM_CHUNKS_PER_STEP = 8


def _chunk_rows8(a, row):
    n = a.shape[0] // CHUNK
    a3 = a.reshape(n, CHUNK, LANES)
    full = jnp.broadcast_to(a3[:, row:row + 1, :], a3.shape)
    return full[:, :SUBLANES, :].reshape(n * SUBLANES, LANES)


def _expand_rows8(a8):
    n, w = a8.shape[0] // SUBLANES, a8.shape[1]
    a3 = a8.reshape(n, SUBLANES, w)[:, 0:1, :]
    return jnp.broadcast_to(a3, (n, CHUNK, w)).reshape(n * CHUNK, w)


def _gate_select():
    src = lax.broadcasted_iota(jnp.int32, (LANES, 2 * LANES), 0)
    dst = lax.broadcasted_iota(jnp.int32, (LANES, 2 * LANES), 1)
    want = GATE_F + 2 * MLSTM_HEADS * (dst // LANES) + (dst % LANES) // HEAD_DIM
    return jnp.where(src == want, 1.0, 0.0).astype(BF16)


def _gate_broadcast(x, sel, gate_mask, pieces=2):
    x = jnp.where(gate_mask, x, 0.0)
    if pieces == 1:
        return _dot(x.astype(BF16), sel)
    hi, lo = _split(x)
    return _dot(hi, sel) + _dot(lo, sel)


def _mlstm_kernel(q_ref, k_ref, v_ref, og_ref, small_ref, wq_ref, wk_ref, bq_ref, bk_ref,
                  gbias_ref, gain_ref, o_ref, q_s, k_s, b_s, r_s, cm_s, fl1_s, bl_s, rl_s, mf_s, mb_s,
                  wp_s, qi_s, kw_s, rt_s, st_s, of_s, ob_s):
    T = q_ref.shape[1]
    nc = T // CHUNK
    R = min(PREP_ROWS, T)
    hp = pl.program_id(1)
    G = MLSTM_CHUNKS_PER_STEP
    cpt = R // CHUNK

    row = lax.broadcasted_iota(jnp.int32, (T, LANES), 0)
    for src, w_ref, b_ref, dst, scale in ((q_ref, wq_ref, bq_ref, q_s, 1.0),
                                          (k_ref, wk_ref, bk_ref, k_s, HEAD_DIM ** -0.5)):
        xc = src[0]
        prev = jnp.where(row >= 1, pltpu.roll(xc, 1, 0), 0.0)
        nxt = jnp.where(row < T - 1, pltpu.roll(xc, T - 1, 0), 0.0)
        y = prev * w_ref[0:1, :] + xc * w_ref[1:2, :] + nxt * w_ref[2:3, :] + b_ref[...]
        dst[...] = (y * _sigmoid(y) * scale).astype(BF16)

    pos = _chunk_pos(R)
    lane = lax.broadcasted_iota(jnp.int32, (R, LANES), 1)
    is_bwd = lane >= GATE_I + 2 * MLSTM_HEADS
    is_bwd8 = lax.broadcasted_iota(jnp.int32, (cpt * SUBLANES, LANES), 1) >= GATE_I + 2 * MLSTM_HEADS
    heads_per_pair = LANES // HEAD_DIM
    shift = (LANES - heads_per_pair * hp) % LANES
    gate_lanes = [_gate_lane(d, hh) for d in range(2) for hh in range(2)]
    gate_mask = functools.reduce(jnp.logical_or, [lane == l for l in gate_lanes])
    lane8 = lax.broadcasted_iota(jnp.int32, (cpt * SUBLANES, LANES), 1)
    gate_mask8 = functools.reduce(jnp.logical_or, [lane8 == l for l in gate_lanes])

    def prep(t, _):
        r0 = pl.multiple_of(t * R, R)
        gc = pltpu.roll(small_ref[0, pl.ds(r0, R), :] + gbias_ref[...], shift, 1)
        logf = _log_sigmoid(gc)
        pre = _chunk_scan(logf, pos, jnp.add, 0.0, False)
        b = jnp.where(is_bwd, _chunk_row(pre, 0, CHUNK - 1, CHUNK - 1) - pre + logf, pre)
        r = pltpu.roll(gc, MLSTM_HEADS, 1) - b
        cm = jnp.where(is_bwd, _chunk_scan(r, pos, jnp.maximum, NEG, True),
                       _chunk_scan(r, pos, jnp.maximum, NEG, False))
        b_s[pl.ds(r0, R), :] = b
        r_s[pl.ds(r0, R), :] = r
        cm_s[pl.ds(r0, R), :] = cm
        rt_s[:, pl.ds(r0, R)] = r.T
        c8 = pl.ds(pl.multiple_of(t * cpt * SUBLANES, SUBLANES), cpt * SUBLANES)
        bl_s[c8, :] = jnp.where(is_bwd8, _chunk_rows8(b, 0), _chunk_rows8(b, CHUNK - 1))
        rl_s[c8, :] = jnp.where(is_bwd8, _chunk_rows8(cm, 0), _chunk_rows8(cm, CHUNK - 1))
        return 0

    lax.fori_loop(0, T // R, prep, 0)

    def m_chain(n, carry):
        mf, mb = carry
        rf = pl.ds(pl.multiple_of(n * SUBLANES, SUBLANES), SUBLANES)
        rb = pl.ds(pl.multiple_of((nc - 1 - n) * SUBLANES, SUBLANES), SUBLANES)
        mf_s[rf, :] = mf
        mb_s[rb, :] = mb
        return (bl_s[rf, :] + jnp.maximum(mf, rl_s[rf, :]), bl_s[rb, :] + jnp.maximum(mb, rl_s[rb, :]))

    m0 = jnp.full((SUBLANES, LANES), NEG, F32)
    lax.fori_loop(0, nc, m_chain, (m0, m0))

    expo_s, floor_s = (b_s, cm_s), (r_s, fl1_s)
    sel = _gate_select()

    def weights(t, _):
        r0 = pl.multiple_of(t * R, R)
        c8 = pl.ds(pl.multiple_of(t * cpt * SUBLANES, SUBLANES), cpt * SUBLANES)
        rows = pl.ds(r0, R)
        m_in8 = jnp.where(is_bwd8, mb_s[c8, :], mf_s[c8, :])
        bl8 = bl_s[c8, :]
        m_out8 = bl8 + jnp.maximum(m_in8, rl_s[c8, :])
        m_in = _expand_rows8(m_in8)
        mx = jnp.maximum(m_in, cm_s[rows, :])
        wp_b = jnp.exp(_gate_broadcast(bl8 + m_in8 - m_out8, sel, gate_mask8))
        mx_b = _gate_broadcast(mx, sel, gate_mask)
        floor = jnp.exp(-_gate_broadcast(b_s[rows, :] + mx, sel, gate_mask))
        w_inter = _gate_broadcast(jnp.exp(m_in - mx), sel, gate_mask, pieces=1)
        wk = _gate_broadcast(jnp.exp(_expand_rows8(bl8 - m_out8) + r_s[rows, :]), sel, gate_mask, pieces=1)
        q = q_s[rows, :].astype(F32)
        k = k_s[rows, :].astype(F32)
        for d in range(2):
            half = slice(d * LANES, (d + 1) * LANES)
            qi_s[d, rows, :] = (q * w_inter[:, half]).astype(BF16)
            kw_s[d, rows, :] = (k * wk[:, half]).astype(BF16)
            wp_s[d, c8, :] = wp_b[:, half]
            expo_s[d][rows, :] = -mx_b[:, half]
            floor_s[d][rows, :] = floor[:, half]
        return 0

    lax.fori_loop(0, T // R, weights, 0)

    outs = (of_s, ob_s)
    ci = lax.broadcasted_iota(jnp.int32, (CHUNK, LANES), 0)
    si = lax.broadcasted_iota(jnp.int32, (CHUNK, LANES), 1) % CHUNK
    masks = (ci >= si, ci <= si)
    br = lax.broadcasted_iota(jnp.int32, (LANES, 2 * LANES), 0) // HEAD_DIM
    bc = (lax.broadcasted_iota(jnp.int32, (LANES, 2 * LANES), 1) % LANES) // HEAD_DIM
    blockdiag = br == bc
    ones = jnp.ones((CHUNK, LANES), BF16)
    st_s[...] = jnp.zeros(st_s.shape, F32)
    gate_rows = slice(GATE_I, GATE_I + 4 * MLSTM_HEADS)

    def step(it, _):
        units = [(d, g) for d in range(2) for g in range(G)]
        ops = {}
        for d in range(2):
            first = (nc - (it + 1) * G) if d else it * G
            rt = rt_s[gate_rows, pl.ds(pl.multiple_of(first * CHUNK, G * CHUNK), G * CHUNK)]
            for g in range(G):
                local = (G - 1 - g) if d else g
                c = first + local
                rows = pl.ds(pl.multiple_of(c * CHUNK, CHUNK), CHUNK)
                c8 = pl.ds(pl.multiple_of(c * SUBLANES, SUBLANES), SUBLANES)
                r_row = jnp.concatenate(
                    [rt[_gate_lane(d, hh) - GATE_I:_gate_lane(d, hh) - GATE_I + 1,
                        local * CHUNK:(local + 1) * CHUNK] for hh in range(2)], axis=1)
                wp = wp_s[d, c8, :][0:1, :]
                ops[d, g] = dict(
                    rows=rows, q=q_s[rows, :], k=k_s[rows, :], qi=qi_s[d, rows, :], kw=kw_s[d, rows, :],
                    va=jnp.concatenate([v_ref[0, rows, :].astype(BF16), ones], axis=1),
                    expo=expo_s[d][rows, :] + r_row, floor=floor_s[d][rows, :],
                    wp=jnp.concatenate([wp, wp], axis=1))
        qk, kv = {}, {}
        for u in units:
            o = ops[u]
            qk[u] = _dot_nt(o["q"], _stack_heads(o["k"]))
            kv[u] = _dot_tn(o["kw"], o["va"])
        st_in = {}
        for d in range(2):
            st = st_s[d]
            for g in range(G):
                st_in[d, g] = st.astype(BF16)
                st = st * ops[d, g]["wp"] + jnp.where(blockdiag, kv[d, g], 0.0)
            st_s[d] = st
        for u in units:
            d = u[0]
            o = ops[u]
            smat = (qk[u] * jnp.where(masks[d], jnp.exp(o["expo"]), 0.0)).astype(BF16)
            va_bd = jnp.where(blockdiag, jnp.concatenate([o["va"], o["va"]], axis=0), jnp.zeros((), BF16))
            num = _dot(jnp.concatenate([o["qi"], smat], axis=1), jnp.concatenate([st_in[u], va_bd], axis=0))
            den = jnp.maximum(jnp.abs(num[:, LANES:]), o["floor"])
            outs[d][o["rows"], :] = num[:, :LANES] / den
        return 0

    lax.fori_loop(0, nc // G, step, 0)

    def post(t, _):
        r0 = pl.multiple_of(t * R, R)
        h = _head_rms(of_s[pl.ds(r0, R), :] + ob_s[pl.ds(r0, R), :], gain_ref[...])
        o_ref[0, pl.ds(r0, R), :] = (h * _sigmoid(og_ref[0, pl.ds(r0, R), :])).astype(BF16)
        return 0

    lax.fori_loop(0, T // R, post, 0)


def _mlstm(ml, small, lp):
    B, T, _ = ml.shape
    pairs = MLSTM_WIDTH // LANES
    col = lambda off: (lambda b, hp: (b, 0, off * pairs + hp))
    const = lambda b, hp: (0, 0)
    return pl.pallas_call(
        _mlstm_kernel,
        grid=(B, pairs),
        in_specs=[
            pl.BlockSpec((1, T, LANES), col(0)),
            pl.BlockSpec((1, T, LANES), col(1)),
            pl.BlockSpec((1, T, LANES), col(2)),
            pl.BlockSpec((1, T, LANES), col(3)),
            pl.BlockSpec((1, T, LANES), lambda b, hp: (b, 0, 0)),
            pl.BlockSpec((3, LANES), lambda b, hp: (0, hp)),
            pl.BlockSpec((3, LANES), lambda b, hp: (0, pairs + hp)),
            pl.BlockSpec((1, LANES), lambda b, hp: (0, hp)),
            pl.BlockSpec((1, LANES), lambda b, hp: (0, pairs + hp)),
            pl.BlockSpec((1, LANES), const),
            pl.BlockSpec((1, LANES), const),
        ],
        out_specs=pl.BlockSpec((1, T, LANES), lambda b, hp: (b, 0, hp)),
        out_shape=jax.ShapeDtypeStruct((B, T, MLSTM_WIDTH), BF16),
        scratch_shapes=(
            [pltpu.VMEM((T, LANES), BF16)] * 2
            + [pltpu.VMEM((T, LANES), F32)] * 4
            + [pltpu.VMEM(((T // CHUNK) * SUBLANES, LANES), F32)] * 4
            + [pltpu.VMEM((2, (T // CHUNK) * SUBLANES, LANES), F32)]
            + [pltpu.VMEM((2, T, LANES), BF16)] * 2
            + [pltpu.VMEM((LANES, T), F32),
               pltpu.VMEM((2, LANES, 2 * LANES), F32),
               pltpu.VMEM((T, LANES), F32), pltpu.VMEM((T, LANES), F32)]),
        compiler_params=_params("arbitrary", "arbitrary"),
        name="mlstm",
    )(ml, ml, ml, ml, small, lp["conv_w"], lp["conv_w"], lp["conv_b"], lp["conv_b"],
      lp["gate_bias"], lp["ml_gain"])


def _first_argmax(vals, row):
    mx = jnp.max(vals, axis=0, keepdims=True)
    idx = jnp.min(jnp.where(vals == mx, row, vals.shape[0]), axis=0, keepdims=True)
    return mx, idx


def _out_route_kernel(x_ref, a_ref, gl_ref, ml_ref, w_ref, g_ref, wr_ref, br_ref,
                      x1_ref, t_ref, route_ref, routet_ref):
    tm = x_ref.shape[0]
    sub = min(tm, ROUTE_SUBTILE)
    spans = [slice(r, r + sub) for r in range(0, tm, sub)]
    x1s, logits = [], []
    for rs in spans:
        x1 = (x_ref[rs, :]
              + _dot(a_ref[rs, :], w_ref[0:ATTN_WIDTH, :])
              + _dot(gl_ref[rs, :], w_ref[ATTN_WIDTH:ATTN_WIDTH + GLA_WIDTH, :])
              + _dot(ml_ref[rs, :], w_ref[ATTN_WIDTH + GLA_WIDTH:, :]))
        x1_ref[rs, :] = x1
        x1s.append(x1)
    for rs, x1 in zip(spans, x1s):
        t_hi, t_lo = _split(_rms(x1, g_ref[...]))
        t_ref[rs, 0:t_hi.shape[1]] = t_hi
        both = _dot(t_hi, wr_ref[...])
        logits.append((both[:, :LANES] + both[:, LANES:] + _dot(t_lo, wr_ref[:, :LANES])) + br_ref[...])
    grow = lax.broadcasted_iota(jnp.int32, (SUBLANES, sub), 0)
    erow = lax.broadcasted_iota(jnp.int32, (N_EXPERTS, sub), 0)
    for rs, lg in zip(spans, logits):
        lt = lg.T
        gl = jnp.where(grow < N_GROUPS, lt[0:SUBLANES, :], -jnp.inf)
        gmax, gi = _first_argmax(gl, grow)
        g_prob = 1.0 / jnp.sum(jnp.exp(gl - gmax), axis=0, keepdims=True)
        el = jnp.where(erow // EXPERTS_PER_GROUP == gi, lt[ROUTE_W_LANE:ROUTE_W_LANE + N_EXPERTS, :], -jnp.inf)
        v1, i1 = _first_argmax(el, erow)
        v2, i2 = _first_argmax(jnp.where(erow == i1, -jnp.inf, el), erow)
        e2 = jnp.exp(v2 - v1)
        w1 = g_prob / (1.0 + e2)
        w2 = g_prob * e2 / (1.0 + e2)
        comb = jnp.where(erow == i1, w1, jnp.where(erow == i2, w2, 0.0))
        head = jnp.where(grow == 0, gi.astype(F32), 0.0)
        route_t = jnp.concatenate(
            [head, comb, jnp.zeros((LANES - SUBLANES - N_EXPERTS, sub), F32)], axis=0)
        route_ref[rs, :] = route_t.T
        routet_ref[:, rs] = head


def _dispatch_kernel(t_ref, route_ref, routet_ref, xs_ref, cws_ref, pos_ref, cnt_ref):
    nb = ROUTE_BLOCK
    gi_row = routet_ref[0:1, :]
    sub = lax.broadcasted_iota(jnp.int32, (SUBLANES, nb), 0).astype(F32)
    onehot = (sub == gi_row)
    ri = lax.broadcasted_iota(jnp.int32, (nb, nb), 0)
    cj = lax.broadcasted_iota(jnp.int32, (nb, nb), 1)
    before = (ri < cj).astype(BF16)
    rank = _dot(onehot.astype(BF16), before)
    counts = jnp.broadcast_to(jnp.sum(onehot.astype(F32), axis=-1, keepdims=True), (SUBLANES, LANES))
    padded = jnp.ceil(counts * (1.0 / ROUTE_TILE)) * ROUTE_TILE
    srow = lax.broadcasted_iota(jnp.int32, (SUBLANES, LANES), 0)
    incl = padded
    s = 1
    while s < SUBLANES:
        incl = incl + jnp.where(srow >= s, pltpu.roll(incl, s, 0), 0.0)
        s *= 2
    start = incl - padded
    pos_row = jnp.sum(jnp.where(onehot, start[:, 0:1] + rank, 0.0), axis=0, keepdims=True)
    cnt_ref[0] = counts.astype(jnp.int32)

    pos_hi = jnp.floor(pos_row * (1.0 / ROUTE_TILE))
    prow = lax.broadcasted_iota(jnp.int32, (2 * SUBLANES, nb), 0)
    pieces = jnp.where(prow == 0, pos_hi, jnp.where(prow == 1, pos_row - ROUTE_TILE * pos_hi, 0.0))
    wrow = lax.broadcasted_iota(jnp.int32, (2 * SUBLANES, LANES), 0)
    weights = jnp.where(wrow == 0, float(ROUTE_TILE), jnp.where(wrow == 1, 1.0, 0.0))
    pos_ref[...] = _dot_tn(pieces.astype(BF16), weights.astype(BF16))

    route = route_ref[...]
    lane = lax.broadcasted_iota(jnp.int32, (nb, LANES), 1)
    comb = jnp.where(lane >= ROUTE_W_LANE, route, 0.0)
    c_hi, c_lo = _split(comb)
    c_lo2 = (comb - c_hi.astype(F32) - c_lo.astype(F32)).astype(BF16)
    d_model = xs_ref.shape[1]
    t_ref[:, d_model:] = (c_hi.astype(F32) + pltpu.roll(c_lo.astype(F32), ROUTE_LO_SHIFT, 1)
                          + pltpu.roll(c_lo2.astype(F32), 2 * ROUTE_LO_SHIFT, 1)).astype(BF16)
    tb = t_ref[...]
    for r in range(TILES_PER_BLOCK):
        rows = (lax.broadcasted_iota(jnp.int32, (ROUTE_TILE, nb), 0) + r * ROUTE_TILE).astype(F32)
        perm = (rows == pos_row).astype(BF16)
        moved = _dot(perm, tb)
        xs_ref[r * ROUTE_TILE:(r + 1) * ROUTE_TILE, :] = moved[:, :d_model].astype(BF16)
        cws_ref[r * ROUTE_TILE:(r + 1) * ROUTE_TILE, :] = moved[:, d_model:]


def _route_dispatch_kernel(x_ref, a_ref, gl_ref, ml_ref, w_ref, g_ref, wr_ref, br_ref,
                           x1_ref, xs_ref, cws_ref, pos_ref, cnt_ref, t_s, route_s, routet_s):
    _out_route_kernel(x_ref, a_ref, gl_ref, ml_ref, w_ref, g_ref, wr_ref, br_ref,
                      x1_ref, t_s, route_s, routet_s)
    _dispatch_kernel(t_s, route_s, routet_s, xs_ref, cws_ref, pos_ref, cnt_ref)


def _route_dispatch(x2, attn, gla_o, ml_o, lp):
    N, D = x2.shape
    nblk = N // ROUTE_BLOCK
    const = lambda i: (0, 0)
    tok = lambda i: (i, 0)
    return pl.pallas_call(
        _route_dispatch_kernel,
        grid=(nblk,),
        in_specs=[
            pl.BlockSpec((ROUTE_BLOCK, D), tok),
            pl.BlockSpec((ROUTE_BLOCK, ATTN_WIDTH), tok),
            pl.BlockSpec((ROUTE_BLOCK, GLA_WIDTH), tok),
            pl.BlockSpec((ROUTE_BLOCK, MLSTM_WIDTH), tok),
            pl.BlockSpec((ATTN_WIDTH + GLA_WIDTH + MLSTM_WIDTH, D), const),
            pl.BlockSpec((1, D), const),
            pl.BlockSpec((D, 2 * LANES), const),
            pl.BlockSpec((1, LANES), const),
        ],
        out_specs=[
            pl.BlockSpec((ROUTE_BLOCK, D), tok),
            pl.BlockSpec((ROUTE_ROWS, D), tok),
            pl.BlockSpec((ROUTE_ROWS, LANES), tok),
            pl.BlockSpec((ROUTE_BLOCK, LANES), tok),
            pl.BlockSpec((1, SUBLANES, LANES), lambda i: (i, 0, 0)),
        ],
        out_shape=[
            jax.ShapeDtypeStruct((N, D), F32),
            jax.ShapeDtypeStruct((nblk * ROUTE_ROWS, D), BF16),
            jax.ShapeDtypeStruct((nblk * ROUTE_ROWS, LANES), F32),
            jax.ShapeDtypeStruct((N, LANES), F32),
            jax.ShapeDtypeStruct((nblk, SUBLANES, LANES), jnp.int32),
        ],
        scratch_shapes=[
            pltpu.VMEM((ROUTE_BLOCK, D + LANES), BF16),
            pltpu.VMEM((ROUTE_BLOCK, LANES), F32),
            pltpu.VMEM((SUBLANES, ROUTE_BLOCK), F32),
        ],
        compiler_params=_params("arbitrary"),
        name="route_dispatch",
    )(x2, attn, gla_o, ml_o, lp["w_out"], lp["g_ffn"], lp["w_route"], lp["b_route"])


def _tile_schedule(cnt):
    nblk = cnt.shape[0]
    ntile = (cnt + ROUTE_TILE - 1) // ROUTE_TILE
    end = jnp.cumsum(ntile, axis=1)
    r = jnp.arange(TILES_PER_BLOCK, dtype=jnp.int32)
    grp = jnp.sum(r[None, :, None] >= end[:, None, :], axis=-1)
    grp = grp.reshape(-1).astype(jnp.int32)
    tile = jnp.arange(nblk * TILES_PER_BLOCK, dtype=jnp.int32)
    order = jnp.argsort(grp * (nblk * TILES_PER_BLOCK) + tile).astype(jnp.int32)
    n_active = jnp.sum(grp < N_GROUPS).astype(jnp.int32)
    g_sorted = grp[order]
    last_group = g_sorted[jnp.maximum(n_active - 1, 0)]
    g_sorted = jnp.where(g_sorted < N_GROUPS, g_sorted, last_group)
    return order, g_sorted, n_active[None]


def _expert_kernel(trow_ref, tgrp_ref, nact_ref, xs_ref, cws_ref, wg_ref, wu_ref, wd_ref, ys_ref,
                   wg_s, wu_s, wd_s):
    i = pl.program_id(0)

    @pl.when((i == 0) | (tgrp_ref[i] != tgrp_ref[jnp.maximum(i - 1, 0)]))
    def _():
        for j in range(EXPERTS_PER_GROUP):
            wg_s[j] = wg_ref[j].astype(BF16)
            wu_s[j] = wu_ref[j].astype(BF16)
            wd_s[j] = wd_ref[j].astype(BF16)

    @pl.when(i < nact_ref[0])
    def _():
        x = xs_ref[...]
        cws = cws_ref[...]
        lane = lax.broadcasted_iota(jnp.int32, cws.shape, 1)
        first = ROUTE_W_LANE + tgrp_ref[i] * EXPERTS_PER_GROUP
        y = jnp.zeros(ys_ref.shape, F32)

        def hidden(j):
            return _dot(x, wg_s[j]), _dot(x, wu_s[j])

        h = hidden(0)
        for j in range(EXPERTS_PER_GROUP):
            h_next = hidden(j + 1) if j + 1 < EXPERTS_PER_GROUP else None
            off = lane - (first + j)
            sel = (off == 0) | (off == ROUTE_LO_SHIFT) | (off == 2 * ROUTE_LO_SHIFT)
            wj = jnp.sum(jnp.where(sel, cws, 0.0), axis=-1, keepdims=True)
            a = (h[0] * _sigmoid(h[0]) * h[1]).astype(BF16)
            y = y + wj * _dot(a, wd_s[j])
            h = h_next
        ys_ref[...] = y.astype(BF16)

    @pl.when(i >= nact_ref[0])
    def _():
        ys_ref[...] = jnp.zeros(ys_ref.shape, BF16)


def _experts(xs, cws, order, grp, n_active, lp):
    rows, D = xs.shape
    n_tiles = rows // ROUTE_TILE
    tile = lambda i, trow, tgrp, nact: (trow[i], 0)
    layer_groups = lp["layer"] * N_GROUPS
    wsel = lambda i, trow, tgrp, nact: (layer_groups + tgrp[i], 0, 0)
    once = pl.Buffered(1)
    return pl.pallas_call(
        _expert_kernel,
        grid_spec=pltpu.PrefetchScalarGridSpec(
            num_scalar_prefetch=3,
            grid=(n_tiles,),
            in_specs=[
                pl.BlockSpec((ROUTE_TILE, D), tile),
                pl.BlockSpec((ROUTE_TILE, LANES), tile),
                pl.BlockSpec((EXPERTS_PER_GROUP, D, D_EXPERT), wsel, pipeline_mode=once),
                pl.BlockSpec((EXPERTS_PER_GROUP, D, D_EXPERT), wsel, pipeline_mode=once),
                pl.BlockSpec((EXPERTS_PER_GROUP, D_EXPERT, D), wsel, pipeline_mode=once),
            ],
            out_specs=pl.BlockSpec((ROUTE_TILE, D), tile),
            scratch_shapes=[
                pltpu.VMEM((EXPERTS_PER_GROUP, D, D_EXPERT), BF16),
                pltpu.VMEM((EXPERTS_PER_GROUP, D, D_EXPERT), BF16),
                pltpu.VMEM((EXPERTS_PER_GROUP, D_EXPERT, D), BF16),
            ],
        ),
        out_shape=jax.ShapeDtypeStruct((rows, D), BF16),
        compiler_params=_params("arbitrary"),
        name="experts",
    )(order, grp, n_active, xs, cws, lp["w_gate"], lp["w_up"], lp["w_down"])


def _combine_kernel(ys_ref, pos_ref, x1_ref, p_ref, g_ref, wpg_ref, wpp_ref, gfin_ref, o_ref,
                    *, final):
    tm = x1_ref.shape[0]
    pos = pos_ref[...]
    lane = lax.broadcasted_iota(jnp.int32, (tm, LANES), 1).astype(F32)
    perm_t = jnp.concatenate(
        [(lane + r * LANES == pos).astype(BF16) for r in range(ROUTE_ROWS // LANES)], axis=1)
    x = x1_ref[...] + _dot(perm_t, ys_ref[...])
    gate = _sigmoid(_dot(_rms(x, g_ref[...]).astype(BF16), wpg_ref[...]))
    x = x + gate * _dot(p_ref[...].astype(BF16), wpp_ref[...])
    if final:
        x = _rms(x, gfin_ref[...])
    o_ref[...] = x


def _combine(ys, pos, x1, p3, lp, g_final, final=False, tm=512):
    N, D = x1.shape
    inner = ROUTE_BLOCK // tm
    layer = lp["layer"]
    tok = lambda b, i: (b * inner + i, 0)
    const = lambda b, i: (0, 0)
    return pl.pallas_call(
        functools.partial(_combine_kernel, final=final),
        grid=(N // ROUTE_BLOCK, inner),
        in_specs=[
            pl.BlockSpec((ROUTE_ROWS, D), lambda b, i: (b, 0)),
            pl.BlockSpec((tm, LANES), tok),
            pl.BlockSpec((tm, D), tok),
            pl.BlockSpec((None, tm, p3.shape[2]), lambda b, i: (layer, b * inner + i, 0)),
            pl.BlockSpec((1, D), const),
            pl.BlockSpec((D, D), const),
            pl.BlockSpec((p3.shape[2], D), const),
            pl.BlockSpec((1, D), const),
        ],
        out_specs=pl.BlockSpec((tm, D), tok),
        out_shape=jax.ShapeDtypeStruct((N, D), F32),
        compiler_params=_params("arbitrary", "arbitrary"),
        name="combine",
    )(ys, pos, x1, p3, lp["g_ple"], lp["w_pg"], lp["w_pp"], g_final)


def _mix_out_moe(x2, attn, gla_o, ml_o, lp):
    x1, xs, cws, pos, cnt = _route_dispatch(x2, attn, gla_o, ml_o, lp)
    order, grp, n_active = _tile_schedule(cnt[:, :N_GROUPS, 0])
    return x1, _experts(xs, cws, order, grp, n_active, lp), pos


def _permute_in_cols(w):
    glr0 = ATTN_WIDTH + 2 * KV_WIDTH + 4 * GLA_WIDTH
    ml0 = glr0 + 2 * GLA_RANK
    mg0 = ml0 + 4 * MLSTM_WIDTH
    end = mg0 + 4 * MLSTM_HEADS
    assert end == w.shape[-1]
    pad = jnp.zeros(w.shape[:-1] + (IN_PERM_WIDTH - end,), BF16)
    parts = [w[..., :glr0], w[..., ml0:mg0], w[..., glr0:ml0], w[..., mg0:end]]
    return jnp.concatenate([part.astype(BF16) for part in parts] + [pad], axis=-1)


def _rope_tables(T):
    t = np.arange(T)
    inv = ROPE_THETA ** (-np.arange(0, ROPE_AXIS_DIM, 2, dtype=np.float64) / ROPE_AXIS_DIM)
    ang_r = (t // GRID_W)[None, :] * inv[:, None]
    ang_c = (t % GRID_W)[None, :] * inv[:, None]
    cos_h = np.concatenate([np.cos(ang_r), np.cos(ang_r), np.cos(ang_c), np.cos(ang_c)], axis=0)
    sin_h = np.concatenate([-np.sin(ang_r), np.sin(ang_r), -np.sin(ang_c), np.sin(ang_c)], axis=0)
    reps = LANES // HEAD_DIM
    return dict(cos_t=jnp.asarray(np.tile(cos_h, (reps, 1)), F32),
                sin_t=jnp.asarray(np.tile(sin_h, (reps, 1)), F32))


def kernel(x, p, norm_mix_g, w_in, attn_q_norm_g, attn_k_norm_g, gla_w_decay, gla_b_decay,
           gla_out_norm_g, mlstm_conv_w, mlstm_conv_b, mlstm_b_input, mlstm_b_forget,
           mlstm_out_norm_g, w_out, norm_ffn_g, w_group, b_group, w_router, b_router,
           w_expert_gate, w_expert_up, w_expert_down, norm_ple_g, w_ple_gate, w_ple_proj,
           final_norm_g):
    params = dict(
        norm_mix_g=norm_mix_g, w_in=w_in, attn_q_norm_g=attn_q_norm_g, attn_k_norm_g=attn_k_norm_g,
        gla_w_decay=gla_w_decay, gla_b_decay=gla_b_decay, gla_out_norm_g=gla_out_norm_g,
        mlstm_conv_w=mlstm_conv_w, mlstm_conv_b=mlstm_conv_b, mlstm_b_input=mlstm_b_input,
        mlstm_b_forget=mlstm_b_forget, mlstm_out_norm_g=mlstm_out_norm_g, w_out=w_out,
        norm_ffn_g=norm_ffn_g, w_group=w_group, b_group=b_group, w_router=w_router,
        b_router=b_router, w_expert_gate=w_expert_gate, w_expert_up=w_expert_up,
        w_expert_down=w_expert_down, norm_ple_g=norm_ple_g, w_ple_gate=w_ple_gate,
        w_ple_proj=w_ple_proj)
    B, T, D = x.shape
    rope = _rope_tables(T)
    N = B * T
    depth = w_in.shape[0]
    x2 = x.reshape(N, D)
    g_final = final_norm_g[None, :]
    p3 = p.reshape(depth, N, p.shape[-1])
    for i in range(depth):
        lp = _layer_params(params, i)
        qt, k, vt, gla, ml, small = _in_proj(x2, lp, rope, B, T)
        attn = _attention(lp["attn_safe"], qt, k, vt).reshape(N, ATTN_WIDTH)
        gla_o = _gla(gla, small, lp).reshape(N, GLA_WIDTH)
        ml_o = _mlstm(ml, small, lp).reshape(N, MLSTM_WIDTH)
        x1, ys, pos = _mix_out_moe(x2, attn, gla_o, ml_o, lp)
        x2 = _combine(ys, pos, x1, p3, lp, g_final, final=(i == depth - 1))
    return x2.reshape(B, T, D)


def _split_w(w):
    hi = w.astype(BF16)
    return hi, (w - hi.astype(F32)).astype(BF16)


def _stacked_experts(w):
    return w.reshape((w.shape[0] * w.shape[1],) + w.shape[2:])


def _layer_params(p, i):
    D = p["w_in"].shape[1]
    gq, gk = p["attn_q_norm_g"][i], p["attn_k_norm_g"][i]
    q_gain = jnp.tile(gq, LANES // HEAD_DIM) * (HEAD_DIM ** -0.5 * LOG2E)
    logit_bound = HEAD_DIM ** 0.5 * jnp.max(jnp.abs(gq)) * jnp.max(jnp.abs(gk))
    attn_safe = (logit_bound <= ATTN_SAFE_LOGIT).astype(jnp.int32)[None]
    wd = jnp.zeros((2, LANES, GLA_WIDTH), F32)
    wd = wd.at[0, :GLA_RANK].set(p["gla_w_decay"][i, 0]).at[1, GLA_RANK:2 * GLA_RANK].set(p["gla_w_decay"][i, 1])
    wd_hi, wd_lo = _split_w(wd)
    gate_bias = jnp.zeros((LANES,), F32).at[SMALL_GATE_LANE:SMALL_GATE_LANE + 4 * MLSTM_HEADS].set(
        jnp.concatenate([p["mlstm_b_input"][i, 0], p["mlstm_b_forget"][i, 0],
                         p["mlstm_b_input"][i, 1], p["mlstm_b_forget"][i, 1]]))
    w_route = jnp.zeros((D, LANES), F32)
    w_route = w_route.at[:, :N_GROUPS].set(p["w_group"][i])
    w_route = w_route.at[:, ROUTE_W_LANE:ROUTE_W_LANE + N_EXPERTS].set(p["w_router"][i])
    wr_hi, wr_lo = _split_w(w_route)
    b_route = jnp.zeros((LANES,), F32).at[:N_GROUPS].set(p["b_group"][i])
    b_route = b_route.at[ROUTE_W_LANE:ROUTE_W_LANE + N_EXPERTS].set(p["b_router"][i])
    return dict(
        g_mix=p["norm_mix_g"][i][None, :],
        w_in=_permute_in_cols(p["w_in"]),
        q_gain_t=jnp.broadcast_to(q_gain[:, None], (LANES, LANES)),
        k_gain_t=jnp.broadcast_to(jnp.tile(gk, LANES // HEAD_DIM)[:, None], (LANES, LANES)),
        attn_safe=attn_safe,
        wd_hi=wd_hi, wd_lo=wd_lo,
        bd=p["gla_b_decay"][i][:, None, :],
        gla_gain=jnp.tile(p["gla_out_norm_g"][i], LANES // HEAD_DIM)[None, :],
        conv_w=p["mlstm_conv_w"][i],
        conv_b=p["mlstm_conv_b"][i][None, :],
        gate_bias=gate_bias[None, :],
        ml_gain=jnp.tile(p["mlstm_out_norm_g"][i], LANES // HEAD_DIM)[None, :],
        w_out=p["w_out"][i].astype(BF16),
        g_ffn=p["norm_ffn_g"][i][None, :],
        w_route=jnp.concatenate([wr_hi, wr_lo], axis=1), b_route=b_route[None, :],
        layer=i,
        w_gate=_stacked_experts(p["w_expert_gate"]),
        w_up=_stacked_experts(p["w_expert_up"]),
        w_down=_stacked_experts(p["w_expert_down"]),
        g_ple=p["norm_ple_g"][i][None, :],
        w_pg=p["w_ple_gate"][i].astype(BF16),
        w_pp=p["w_ple_proj"][i].astype(BF16),
    )
```

```python
import functools

import jax
import jax.numpy as jnp
import numpy as np
from jax import lax
from jax.experimental import pallas as pl
from jax.experimental.pallas import tpu as pltpu

F32 = jnp.float32
BF16 = jnp.bfloat16

GRID_W = 64
HEAD_DIM = 64
ATTN_HEADS = 8
ATTN_KV_HEADS = 2
GLA_HEADS = 4
MLSTM_HEADS = 4
ATTN_WIDTH = ATTN_HEADS * HEAD_DIM
KV_WIDTH = ATTN_KV_HEADS * HEAD_DIM
GLA_WIDTH = GLA_HEADS * HEAD_DIM
MLSTM_WIDTH = MLSTM_HEADS * HEAD_DIM
GLA_RANK = 16
GLA_TAU = 16.0
CHUNK = 64
ROPE_THETA = 10000.0
ROPE_AXIS_DIM = HEAD_DIM // 2
N_GROUPS = 4
EXPERTS_PER_GROUP = 4
N_EXPERTS = N_GROUPS * EXPERTS_PER_GROUP
D_EXPERT = 512
EPS = 1e-6
NEG = -1e30

LANES = 128
SUBLANES = 8
VMEM_LIMIT_BYTES = 56 * 1024 * 1024

QK_WIDTH = ATTN_WIDTH + KV_WIDTH
OFF_V = QK_WIDTH
OFF_GLA = OFF_V + KV_WIDTH
OFF_ML = OFF_GLA + 4 * GLA_WIDTH
OFF_SMALL = OFF_ML + 4 * MLSTM_WIDTH
IN_PERM_WIDTH = OFF_SMALL + LANES
SMALL_GATE_LANE = 2 * GLA_RANK

ROUTE_BLOCK = 1024
ROUTE_TILE = 128
TILES_PER_BLOCK = (ROUTE_BLOCK + N_GROUPS * (ROUTE_TILE - 1)) // ROUTE_TILE
ROUTE_ROWS = TILES_PER_BLOCK * ROUTE_TILE
ROUTE_W_LANE = 8
ROUTE_LO_SHIFT = 32
ROUTE_SUBTILE = 256


def _dot(a, b):
    return jnp.dot(a, b, preferred_element_type=F32)


def _dot_nt(a, b):
    return lax.dot_general(a, b, (((1,), (1,)), ((), ())), preferred_element_type=F32)


def _dot_tn(a, b):
    return lax.dot_general(a, b, (((0,), (0,)), ((), ())), preferred_element_type=F32)


def _split(a):
    hi = a.astype(BF16)
    lo = (a - hi.astype(F32)).astype(BF16)
    return hi, lo


def _dot3(a, w_hi, w_lo):
    a_hi, a_lo = _split(a)
    return _dot(a_hi, w_hi) + _dot(a_lo, w_hi) + _dot(a_hi, w_lo)


def _log_sigmoid(x):
    return jnp.minimum(x, 0.0) - jnp.log(1.0 + jnp.exp(-jnp.abs(x)))


def _sigmoid(x):
    return 1.0 / (1.0 + jnp.exp(-x))


def _rms(x, g):
    return x * lax.rsqrt(jnp.mean(x * x, axis=-1, keepdims=True) + EPS) * g


def _params(*semantics):
    return pltpu.CompilerParams(dimension_semantics=semantics, vmem_limit_bytes=VMEM_LIMIT_BYTES)


def _in_proj_kernel(x_ref, g_ref, w_ref, cost_ref, sint_ref, gq_ref, gk_ref,
                    qt_ref, k_ref, vt_ref, gla_ref, ml_ref, small_ref):
    tm = x_ref.shape[0]
    h = _rms(x_ref[...], g_ref[...])
    z = _dot(h.astype(BF16), w_ref[...])

    heads = LANES // HEAD_DIM
    half = ROPE_AXIS_DIM // 2
    gq = jnp.concatenate([gq_ref[...]] * (tm // LANES), axis=1)
    cost = cost_ref[...]
    sint = sint_ref[...]
    gk = jnp.concatenate([gk_ref[...]] * (tm // LANES), axis=1)

    def norm_rope_t(c, gain):
        zt = z[:, c * LANES:(c + 1) * LANES].T
        z3 = zt.reshape(heads, HEAD_DIM, tm)
        inv = lax.rsqrt(jnp.mean(z3 * z3, axis=1, keepdims=True) + EPS)
        y = (z3 * inv).reshape(LANES, tm) * gain
        partner = jnp.concatenate(
            [y[(r ^ 1) * half:((r ^ 1) + 1) * half, :] for r in range(LANES // half)], axis=0)
        return y * cost + partner * sint

    for c in range(ATTN_WIDTH // LANES):
        qt_ref[0, c * LANES:(c + 1) * LANES, :] = norm_rope_t(c, gq).astype(BF16)
    k_ref[0] = norm_rope_t(ATTN_WIDTH // LANES, gk).T.astype(BF16)
    vt_ref[0] = z[:, OFF_V:OFF_V + KV_WIDTH].T.astype(BF16)
    gla_ref[0] = z[:, OFF_GLA:OFF_ML]
    ml_ref[0] = z[:, OFF_ML:OFF_SMALL]
    small_ref[0] = z[:, OFF_SMALL:IN_PERM_WIDTH]


def _in_proj(x2, lp, rope, B, T, tm=512):
    N, D = x2.shape
    tpb = T // tm
    const = lambda i: (0, 0)
    tok3 = lambda i: (i // tpb, i % tpb, 0)
    tokT = lambda i: (i // tpb, 0, i % tpb)
    layer = lp["layer"]
    return pl.pallas_call(
        _in_proj_kernel,
        grid=(N // tm,),
        in_specs=[
            pl.BlockSpec((tm, D), lambda i: (i, 0)),
            pl.BlockSpec((1, D), const),
            pl.BlockSpec((None, D, IN_PERM_WIDTH), lambda i: (layer, 0, 0)),
            pl.BlockSpec((LANES, tm), lambda i: (0, i % tpb)),
            pl.BlockSpec((LANES, tm), lambda i: (0, i % tpb)),
            pl.BlockSpec((LANES, LANES), const),
            pl.BlockSpec((LANES, LANES), const),
        ],
        out_specs=[
            pl.BlockSpec((1, ATTN_WIDTH, tm), tokT),
            pl.BlockSpec((1, tm, KV_WIDTH), tok3),
            pl.BlockSpec((1, KV_WIDTH, tm), tokT),
            pl.BlockSpec((1, tm, 4 * GLA_WIDTH), tok3),
            pl.BlockSpec((1, tm, 4 * MLSTM_WIDTH), tok3),
            pl.BlockSpec((1, tm, LANES), tok3),
        ],
        out_shape=[
            jax.ShapeDtypeStruct((B, ATTN_WIDTH, T), BF16),
            jax.ShapeDtypeStruct((B, T, KV_WIDTH), BF16),
            jax.ShapeDtypeStruct((B, KV_WIDTH, T), BF16),
            jax.ShapeDtypeStruct((B, T, 4 * GLA_WIDTH), F32),
            jax.ShapeDtypeStruct((B, T, 4 * MLSTM_WIDTH), F32),
            jax.ShapeDtypeStruct((B, T, LANES), F32),
        ],
        compiler_params=_params("arbitrary"),
        name="in_proj",
    )(x2, lp["g_mix"], lp["w_in"], rope["cos_t"], rope["sin_t"], lp["q_gain_t"], lp["k_gain_t"])


ATTN_SAFE_LOGIT = 40.0
LOG2E = 1.4426950408889634


def _attn_kernel(safe_ref, qt_ref, k_ref, vt_ref, o_ref, *, tk):
    tq = qt_ref.shape[2]
    T = k_ref.shape[1]
    G = ATTN_HEADS // ATTN_KV_HEADS
    n = G * tq
    zeros = jnp.zeros((HEAD_DIM, n), BF16)

    def q_operand(j):
        base = j * G * HEAD_DIM
        qs = jnp.concatenate(
            [qt_ref[0, base + h * HEAD_DIM:base + (h + 1) * HEAD_DIM, :] for h in range(G)], axis=1)
        return jnp.concatenate([qs, zeros] if j == 0 else [zeros, qs], axis=0)

    def finish(j, acc, l):
        base = j * G * HEAD_DIM
        o = acc * (1.0 / l)
        ot = jnp.concatenate([o[:, h * tq:(h + 1) * tq] for h in range(G)], axis=0)
        o_ref[0, :, base:base + G * HEAD_DIM] = ot.T.astype(BF16)

    @pl.when(safe_ref[0] == 1)
    def _():
        qps = [q_operand(j) for j in range(ATTN_KV_HEADS)]
        units = [(c, j) for c in range(T // tk) for j in range(ATTN_KV_HEADS)]

        def scores(u):
            c, j = units[u]
            return _dot(k_ref[0, c * tk:(c + 1) * tk, :], qps[j])

        l8 = [jnp.zeros((SUBLANES, n), F32)] * ATTN_KV_HEADS
        acc = [jnp.zeros((HEAD_DIM, n), F32)] * ATTN_KV_HEADS
        st = scores(0)
        for u, (c, j) in enumerate(units):
            st_next = scores(u + 1) if u + 1 < len(units) else None
            p = jnp.exp2(st)
            l8[j] = l8[j] + jnp.sum(p.reshape(tk // SUBLANES, SUBLANES, n), axis=0)
            vc = vt_ref[0, j * HEAD_DIM:(j + 1) * HEAD_DIM, c * tk:(c + 1) * tk]
            acc[j] = acc[j] + _dot(vc, p.astype(BF16))
            st = st_next
        for j in range(ATTN_KV_HEADS):
            finish(j, acc[j], jnp.sum(l8[j], axis=0, keepdims=True))

    @pl.when(safe_ref[0] == 0)
    def _():
        for j in range(ATTN_KV_HEADS):
            qp = q_operand(j)

            def body(c, carry, qp=qp, j=j):
                m, l, acc = carry
                off = pl.multiple_of(c * tk, tk)
                st = _dot(k_ref[0, pl.ds(off, tk), :], qp)
                m_new = jnp.maximum(m, jnp.max(st, axis=0, keepdims=True))
                alpha = jnp.exp2(m - m_new)
                p = jnp.exp2(st - m_new)
                l = alpha * l + jnp.sum(p, axis=0, keepdims=True)
                vc = vt_ref[0, j * HEAD_DIM:(j + 1) * HEAD_DIM, pl.ds(off, tk)]
                return m_new, l, alpha * acc + _dot(vc, p.astype(BF16))

            init = (jnp.full((1, n), NEG, F32), jnp.zeros((1, n), F32), jnp.zeros((HEAD_DIM, n), F32))
            _, l, acc = lax.fori_loop(0, T // tk, body, init)
            finish(j, acc, l)


def _attention(safe, qt, k, vt, tq=256, tk=128):
    B, _, T = qt.shape
    tk = min(tk, T)
    return pl.pallas_call(
        functools.partial(_attn_kernel, tk=tk),
        grid_spec=pltpu.PrefetchScalarGridSpec(
            num_scalar_prefetch=1,
            grid=(B, T // tq),
            in_specs=[
                pl.BlockSpec((1, ATTN_WIDTH, tq), lambda b, i, s: (b, 0, i)),
                pl.BlockSpec((1, T, KV_WIDTH), lambda b, i, s: (b, 0, 0)),
                pl.BlockSpec((1, KV_WIDTH, T), lambda b, i, s: (b, 0, 0)),
            ],
            out_specs=pl.BlockSpec((1, tq, ATTN_WIDTH), lambda b, i, s: (b, i, 0)),
        ),
        out_shape=jax.ShapeDtypeStruct((B, T, ATTN_WIDTH), BF16),
        compiler_params=_params("arbitrary", "arbitrary"),
        name="attention",
    )(safe, qt, k, vt)


def _chunk_scan(x, pos, op, fill, reverse):
    rows = x.shape[0]
    s = 1
    while s < CHUNK:
        if reverse:
            shifted = jnp.where(pos < CHUNK - s, pltpu.roll(x, rows - s, 0), fill)
        else:
            shifted = jnp.where(pos >= s, pltpu.roll(x, s, 0), fill)
        x = op(x, shifted)
        s *= 2
    return x


def _chunk_pos(rows):
    return lax.broadcasted_iota(jnp.int32, (rows, LANES), 0) % CHUNK


def _chunk_row(a, reverse_dir, idx_fwd, idx_bwd):
    rows = a.shape[0]
    a3 = a.reshape(rows // CHUNK, CHUNK, LANES)
    i = idx_bwd if reverse_dir else idx_fwd
    return jnp.broadcast_to(a3[:, i:i + 1, :], a3.shape).reshape(rows, LANES)


def _stack_heads(x):
    lane = lax.broadcasted_iota(jnp.int32, x.shape, 1)
    zero = jnp.zeros_like(x)
    return jnp.concatenate([jnp.where(lane < HEAD_DIM, x, zero), jnp.where(lane >= HEAD_DIM, x, zero)], axis=0)


def _select_heads(x):
    c = x.shape[0] // 2
    lane = lax.broadcasted_iota(jnp.int32, (c, x.shape[1]), 1)
    return jnp.where(lane < HEAD_DIM, x[:c], x[c:])


def _stacked_causal_masks():
    ci = lax.broadcasted_iota(jnp.int32, (2 * CHUNK, CHUNK), 0) % CHUNK
    si = lax.broadcasted_iota(jnp.int32, (2 * CHUNK, CHUNK), 1)
    return ci >= si, ci <= si


def _pair_blockdiag(width):
    r = lax.broadcasted_iota(jnp.int32, (LANES, width), 0) // HEAD_DIM
    c = lax.broadcasted_iota(jnp.int32, (LANES, width), 1) // (width // 2)
    return r == c


def _head_rms(o, gain):
    lane = lax.broadcasted_iota(jnp.int32, o.shape, 1)
    lo = lane < HEAD_DIM
    sq = o * o
    s_lo = jnp.sum(jnp.where(lo, sq, 0.0), axis=-1, keepdims=True)
    s_hi = jnp.sum(jnp.where(lo, 0.0, sq), axis=-1, keepdims=True)
    ms = jnp.where(lo, s_lo, s_hi) * (1.0 / HEAD_DIM)
    return o * lax.rsqrt(ms + EPS) * gain


PREP_ROWS = 1024
GLA_CHUNKS_PER_STEP = 16


def _gla_kernel(q_ref, k_ref, v_ref, g_ref, small_ref, wdh_ref, wdl_ref, bd_ref, gain_ref,
                o_ref, qe_s, ke_s, dec_s, of_s, ob_s):
    T = q_ref.shape[1]
    nc = T // CHUNK
    R = min(PREP_ROWS, T)
    cpt = R // CHUNK
    pos = _chunk_pos(R)
    w_hi = jnp.concatenate([wdh_ref[0], wdh_ref[1]], axis=1)
    w_lo = jnp.concatenate([wdl_ref[0], wdl_ref[1]], axis=1)
    bias = jnp.concatenate([bd_ref[0], bd_ref[1]], axis=1)

    def prep(t, _):
        r0 = pl.multiple_of(t * R, R)
        q = q_ref[0, pl.ds(r0, R), :] * HEAD_DIM ** -0.5
        k = k_ref[0, pl.ds(r0, R), :]
        la2 = _log_sigmoid(_dot3(small_ref[0, pl.ds(r0, R), :], w_hi, w_lo) + bias) * (1.0 / GLA_TAU)
        for d in range(2):
            b = _chunk_scan(la2[:, d * LANES:(d + 1) * LANES], pos, jnp.add, 0.0, reverse=bool(d))
            b_mid = _chunk_row(b, d, CHUNK // 2 - 1, CHUNK // 2)
            b_last = _chunk_row(b, d, CHUNK - 1, 0)
            qe_s[d, pl.ds(r0, R), :] = (q * jnp.exp(b - b_mid)).astype(BF16)
            ke_s[d, pl.ds(r0, R), :] = (k * jnp.exp(b_mid - b)).astype(BF16)
            c8 = pl.ds(pl.multiple_of(t * cpt * SUBLANES, SUBLANES), cpt * SUBLANES)
            for kind, val in enumerate((b_last, b_last - b_mid, b_mid)):
                rows8 = val.reshape(cpt, CHUNK, LANES)[:, :SUBLANES, :].reshape(cpt * SUBLANES, LANES)
                dec_s[d, kind, c8, :] = jnp.exp(rows8)
        return 0

    lax.fori_loop(0, T // R, prep, 0)

    outs = (of_s, ob_s)
    masks = _stacked_causal_masks()
    blockdiag = _pair_blockdiag(LANES)
    G = GLA_CHUNKS_PER_STEP

    def step(it, states):
        units = [(d, g) for d in range(2) for g in range(G)]
        ops = {}
        for d, g in units:
            c = it * G + g
            c = (nc - 1 - c) if d else c
            r0 = pl.multiple_of(c * CHUNK, CHUNK)
            c8 = pl.ds(pl.multiple_of(c * SUBLANES, SUBLANES), SUBLANES)
            ops[d, g] = dict(
                r0=r0,
                qe=qe_s[d, pl.ds(r0, CHUNK), :], ke=ke_s[d, pl.ds(r0, CHUNK), :],
                v=v_ref[0, pl.ds(r0, CHUNK), :].astype(BF16),
                dec=dec_s[d, 0, c8, :][0:1, :], to_end=dec_s[d, 1, c8, :][0:1, :],
                from_start=dec_s[d, 2, c8, :][0:1, :])
        a2, kv = {}, {}
        for u in units:
            o = ops[u]
            a2[u] = _dot_nt(_stack_heads(o["qe"]), o["ke"])
            kv[u] = _dot_tn(o["v"], o["ke"]) * o["to_end"]
        st_in = {}
        new_states = []
        for d in range(2):
            st = states[d]
            for g in range(G):
                st_in[d, g] = (st * ops[d, g]["from_start"]).astype(BF16)
                st = st * ops[d, g]["dec"] + jnp.where(blockdiag, kv[d, g], 0.0)
            new_states.append(st)
        for u in units:
            o = ops[u]
            a = jnp.where(masks[u[0]], a2[u], 0.0).astype(BF16)
            intra = _select_heads(_dot(a, o["v"]))
            outs[u[0]][pl.ds(o["r0"], CHUNK), :] = intra + _dot_nt(o["qe"], st_in[u])
        return tuple(new_states)

    zero = jnp.zeros((LANES, LANES), F32)
    lax.fori_loop(0, nc // G, step, (zero, zero))

    def post(t, _):
        r0 = pl.multiple_of(t * R, R)
        o = _head_rms(of_s[pl.ds(r0, R), :] + ob_s[pl.ds(r0, R), :], gain_ref[...])
        g = g_ref[0, pl.ds(r0, R), :]
        o_ref[0, pl.ds(r0, R), :] = (o * (g * _sigmoid(g))).astype(BF16)
        return 0

    lax.fori_loop(0, T // R, post, 0)


def _gla(gla, small, lp):
    B, T, _ = gla.shape
    pairs = GLA_WIDTH // LANES
    col = lambda off: (lambda b, hp: (b, 0, off * pairs + hp))
    return pl.pallas_call(
        _gla_kernel,
        grid=(B, pairs),
        in_specs=[
            pl.BlockSpec((1, T, LANES), col(0)),
            pl.BlockSpec((1, T, LANES), col(1)),
            pl.BlockSpec((1, T, LANES), col(2)),
            pl.BlockSpec((1, T, LANES), col(3)),
            pl.BlockSpec((1, T, LANES), lambda b, hp: (b, 0, 0)),
            pl.BlockSpec((2, LANES, LANES), lambda b, hp: (0, 0, hp)),
            pl.BlockSpec((2, LANES, LANES), lambda b, hp: (0, 0, hp)),
            pl.BlockSpec((2, 1, LANES), lambda b, hp: (0, 0, hp)),
            pl.BlockSpec((1, LANES), lambda b, hp: (0, 0)),
        ],
        out_specs=pl.BlockSpec((1, T, LANES), lambda b, hp: (b, 0, hp)),
        out_shape=jax.ShapeDtypeStruct((B, T, GLA_WIDTH), BF16),
        scratch_shapes=[
            pltpu.VMEM((2, T, LANES), BF16), pltpu.VMEM((2, T, LANES), BF16),
            pltpu.VMEM((2, 3, (T // CHUNK) * SUBLANES, LANES), F32),
            pltpu.VMEM((T, LANES), F32), pltpu.VMEM((T, LANES), F32),
        ],
        compiler_params=_params("arbitrary", "arbitrary"),
        name="gla",
    )(gla, gla, gla, gla, small, lp["wd_hi"], lp["wd_lo"], lp["bd"], lp["gla_gain"])


GATE_I = SMALL_GATE_LANE
GATE_F = SMALL_GATE_LANE + MLSTM_HEADS


def _gate_lane(d, hh):
    return GATE_F + 2 * MLSTM_HEADS * d + hh


MLSTM_CHUNKS_PER_STEP = 16


def _chunk_rows8(a, row):
    n = a.shape[0] // CHUNK
    a3 = a.reshape(n, CHUNK, LANES)
    full = jnp.broadcast_to(a3[:, row:row + 1, :], a3.shape)
    return full[:, :SUBLANES, :].reshape(n * SUBLANES, LANES)


def _expand_rows8(a8):
    n, w = a8.shape[0] // SUBLANES, a8.shape[1]
    a3 = a8.reshape(n, SUBLANES, w)[:, 0:1, :]
    return jnp.broadcast_to(a3, (n, CHUNK, w)).reshape(n * CHUNK, w)


def _gate_select():
    src = lax.broadcasted_iota(jnp.int32, (LANES, 2 * LANES), 0)
    dst = lax.broadcasted_iota(jnp.int32, (LANES, 2 * LANES), 1)
    want = GATE_F + 2 * MLSTM_HEADS * (dst // LANES) + (dst % LANES) // HEAD_DIM
    return jnp.where(src == want, 1.0, 0.0).astype(BF16)


def _gate_broadcast(x, sel, gate_mask, pieces=2):
    x = jnp.where(gate_mask, x, 0.0)
    if pieces == 1:
        return _dot(x.astype(BF16), sel)
    hi, lo = _split(x)
    return _dot(hi, sel) + _dot(lo, sel)


def _mlstm_kernel(q_ref, k_ref, v_ref, og_ref, small_ref, wq_ref, wk_ref, bq_ref, bk_ref,
                  gbias_ref, gain_ref, o_ref, q_s, k_s, b_s, r_s, cm_s, fl1_s, bl_s, rl_s, mf_s, mb_s,
                  wp_s, qi_s, kw_s, rt_s, st_s, of_s, ob_s):
    T = q_ref.shape[1]
    nc = T // CHUNK
    R = min(PREP_ROWS, T)
    hp = pl.program_id(1)
    G = MLSTM_CHUNKS_PER_STEP
    cpt = R // CHUNK

    row = lax.broadcasted_iota(jnp.int32, (T, LANES), 0)
    for src, w_ref, b_ref, dst, scale in ((q_ref, wq_ref, bq_ref, q_s, 1.0),
                                          (k_ref, wk_ref, bk_ref, k_s, HEAD_DIM ** -0.5)):
        xc = src[0]
        prev = jnp.where(row >= 1, pltpu.roll(xc, 1, 0), 0.0)
        nxt = jnp.where(row < T - 1, pltpu.roll(xc, T - 1, 0), 0.0)
        y = prev * w_ref[0:1, :] + xc * w_ref[1:2, :] + nxt * w_ref[2:3, :] + b_ref[...]
        dst[...] = (y * _sigmoid(y) * scale).astype(BF16)

    pos = _chunk_pos(R)
    lane = lax.broadcasted_iota(jnp.int32, (R, LANES), 1)
    is_bwd = lane >= GATE_I + 2 * MLSTM_HEADS
    is_bwd8 = lax.broadcasted_iota(jnp.int32, (cpt * SUBLANES, LANES), 1) >= GATE_I + 2 * MLSTM_HEADS
    heads_per_pair = LANES // HEAD_DIM
    shift = (LANES - heads_per_pair * hp) % LANES
    gate_lanes = [_gate_lane(d, hh) for d in range(2) for hh in range(2)]
    gate_mask = functools.reduce(jnp.logical_or, [lane == l for l in gate_lanes])
    lane8 = lax.broadcasted_iota(jnp.int32, (cpt * SUBLANES, LANES), 1)
    gate_mask8 = functools.reduce(jnp.logical_or, [lane8 == l for l in gate_lanes])

    def prep(t, _):
        r0 = pl.multiple_of(t * R, R)
        gc = pltpu.roll(small_ref[0, pl.ds(r0, R), :] + gbias_ref[...], shift, 1)
        logf = _log_sigmoid(gc)
        pre = _chunk_scan(logf, pos, jnp.add, 0.0, False)
        b = jnp.where(is_bwd, _chunk_row(pre, 0, CHUNK - 1, CHUNK - 1) - pre + logf, pre)
        r = pltpu.roll(gc, MLSTM_HEADS, 1) - b
        cm = jnp.where(is_bwd, _chunk_scan(r, pos, jnp.maximum, NEG, True),
                       _chunk_scan(r, pos, jnp.maximum, NEG, False))
        b_s[pl.ds(r0, R), :] = b
        r_s[pl.ds(r0, R), :] = r
        cm_s[pl.ds(r0, R), :] = cm
        rt_s[:, pl.ds(r0, R)] = r.T
        c8 = pl.ds(pl.multiple_of(t * cpt * SUBLANES, SUBLANES), cpt * SUBLANES)
        bl_s[c8, :] = jnp.where(is_bwd8, _chunk_rows8(b, 0), _chunk_rows8(b, CHUNK - 1))
        rl_s[c8, :] = jnp.where(is_bwd8, _chunk_rows8(cm, 0), _chunk_rows8(cm, CHUNK - 1))
        return 0

    lax.fori_loop(0, T // R, prep, 0)

    def m_chain(n, carry):
        mf, mb = carry
        rf = pl.ds(pl.multiple_of(n * SUBLANES, SUBLANES), SUBLANES)
        rb = pl.ds(pl.multiple_of((nc - 1 - n) * SUBLANES, SUBLANES), SUBLANES)
        mf_s[rf, :] = mf
        mb_s[rb, :] = mb
        return (bl_s[rf, :] + jnp.maximum(mf, rl_s[rf, :]), bl_s[rb, :] + jnp.maximum(mb, rl_s[rb, :]))

    m0 = jnp.full((SUBLANES, LANES), NEG, F32)
    lax.fori_loop(0, nc, m_chain, (m0, m0))

    expo_s, floor_s = (b_s, cm_s), (r_s, fl1_s)
    sel = _gate_select()

    def weights(t, _):
        r0 = pl.multiple_of(t * R, R)
        c8 = pl.ds(pl.multiple_of(t * cpt * SUBLANES, SUBLANES), cpt * SUBLANES)
        rows = pl.ds(r0, R)
        m_in8 = jnp.where(is_bwd8, mb_s[c8, :], mf_s[c8, :])
        bl8 = bl_s[c8, :]
        m_out8 = bl8 + jnp.maximum(m_in8, rl_s[c8, :])
        m_in = _expand_rows8(m_in8)
        mx = jnp.maximum(m_in, cm_s[rows, :])
        wp_b = jnp.exp(_gate_broadcast(bl8 + m_in8 - m_out8, sel, gate_mask8))
        mx_b = _gate_broadcast(mx, sel, gate_mask)
        floor = jnp.exp(-_gate_broadcast(b_s[rows, :] + mx, sel, gate_mask))
        w_inter = _gate_broadcast(jnp.exp(m_in - mx), sel, gate_mask, pieces=1)
        wk = _gate_broadcast(jnp.exp(_expand_rows8(bl8 - m_out8) + r_s[rows, :]), sel, gate_mask, pieces=1)
        q = q_s[rows, :].astype(F32)
        k = k_s[rows, :].astype(F32)
        for d in range(2):
            half = slice(d * LANES, (d + 1) * LANES)
            qi_s[d, rows, :] = (q * w_inter[:, half]).astype(BF16)
            kw_s[d, rows, :] = (k * wk[:, half]).astype(BF16)
            wp_s[d, c8, :] = wp_b[:, half]
            expo_s[d][rows, :] = -mx_b[:, half]
            floor_s[d][rows, :] = floor[:, half]
        return 0

    lax.fori_loop(0, T // R, weights, 0)

    outs = (of_s, ob_s)
    ci = lax.broadcasted_iota(jnp.int32, (CHUNK, LANES), 0)
    si = lax.broadcasted_iota(jnp.int32, (CHUNK, LANES), 1) % CHUNK
    masks = (ci >= si, ci <= si)
    br = lax.broadcasted_iota(jnp.int32, (LANES, 2 * LANES), 0) // HEAD_DIM
    bc = (lax.broadcasted_iota(jnp.int32, (LANES, 2 * LANES), 1) % LANES) // HEAD_DIM
    blockdiag = br == bc
    ones = jnp.ones((CHUNK, LANES), BF16)
    st_s[...] = jnp.zeros(st_s.shape, F32)
    gate_rows = slice(GATE_I, GATE_I + 4 * MLSTM_HEADS)

    def step(it, _):
        units = [(d, g) for d in range(2) for g in range(G)]
        ops = {}
        for d in range(2):
            first = (nc - (it + 1) * G) if d else it * G
            rt = rt_s[gate_rows, pl.ds(pl.multiple_of(first * CHUNK, G * CHUNK), G * CHUNK)]
            for g in range(G):
                local = (G - 1 - g) if d else g
                c = first + local
                rows = pl.ds(pl.multiple_of(c * CHUNK, CHUNK), CHUNK)
                c8 = pl.ds(pl.multiple_of(c * SUBLANES, SUBLANES), SUBLANES)
                r_row = jnp.concatenate(
                    [rt[_gate_lane(d, hh) - GATE_I:_gate_lane(d, hh) - GATE_I + 1,
                        local * CHUNK:(local + 1) * CHUNK] for hh in range(2)], axis=1)
                wp = wp_s[d, c8, :][0:1, :]
                ops[d, g] = dict(
                    rows=rows, q=q_s[rows, :], k=k_s[rows, :], qi=qi_s[d, rows, :], kw=kw_s[d, rows, :],
                    va=jnp.concatenate([v_ref[0, rows, :].astype(BF16), ones], axis=1),
                    expo=expo_s[d][rows, :] + r_row, floor=floor_s[d][rows, :],
                    wp=jnp.concatenate([wp, wp], axis=1))
        qk, kv = {}, {}
        for u in units:
            o = ops[u]
            qk[u] = _dot_nt(o["q"], _stack_heads(o["k"]))
            kv[u] = _dot_tn(o["kw"], o["va"])
        st_in = {}
        for d in range(2):
            st = st_s[d]
            for g in range(G):
                st_in[d, g] = st.astype(BF16)
                st = st * ops[d, g]["wp"] + jnp.where(blockdiag, kv[d, g], 0.0)
            st_s[d] = st
        for u in units:
            d = u[0]
            o = ops[u]
            smat = (qk[u] * jnp.where(masks[d], jnp.exp(o["expo"]), 0.0)).astype(BF16)
            va_bd = jnp.where(blockdiag, jnp.concatenate([o["va"], o["va"]], axis=0), jnp.zeros((), BF16))
            num = _dot(jnp.concatenate([o["qi"], smat], axis=1), jnp.concatenate([st_in[u], va_bd], axis=0))
            den = jnp.maximum(jnp.abs(num[:, LANES:]), o["floor"])
            outs[d][o["rows"], :] = num[:, :LANES] / den
        return 0

    lax.fori_loop(0, nc // G, step, 0)

    def post(t, _):
        r0 = pl.multiple_of(t * R, R)
        h = _head_rms(of_s[pl.ds(r0, R), :] + ob_s[pl.ds(r0, R), :], gain_ref[...])
        o_ref[0, pl.ds(r0, R), :] = (h * _sigmoid(og_ref[0, pl.ds(r0, R), :])).astype(BF16)
        return 0

    lax.fori_loop(0, T // R, post, 0)


def _mlstm(ml, small, lp):
    B, T, _ = ml.shape
    pairs = MLSTM_WIDTH // LANES
    col = lambda off: (lambda b, hp: (b, 0, off * pairs + hp))
    const = lambda b, hp: (0, 0)
    return pl.pallas_call(
        _mlstm_kernel,
        grid=(B, pairs),
        in_specs=[
            pl.BlockSpec((1, T, LANES), col(0)),
            pl.BlockSpec((1, T, LANES), col(1)),
            pl.BlockSpec((1, T, LANES), col(2)),
            pl.BlockSpec((1, T, LANES), col(3)),
            pl.BlockSpec((1, T, LANES), lambda b, hp: (b, 0, 0)),
            pl.BlockSpec((3, LANES), lambda b, hp: (0, hp)),
            pl.BlockSpec((3, LANES), lambda b, hp: (0, pairs + hp)),
            pl.BlockSpec((1, LANES), lambda b, hp: (0, hp)),
            pl.BlockSpec((1, LANES), lambda b, hp: (0, pairs + hp)),
            pl.BlockSpec((1, LANES), const),
            pl.BlockSpec((1, LANES), const),
        ],
        out_specs=pl.BlockSpec((1, T, LANES), lambda b, hp: (b, 0, hp)),
        out_shape=jax.ShapeDtypeStruct((B, T, MLSTM_WIDTH), BF16),
        scratch_shapes=(
            [pltpu.VMEM((T, LANES), BF16)] * 2
            + [pltpu.VMEM((T, LANES), F32)] * 4
            + [pltpu.VMEM(((T // CHUNK) * SUBLANES, LANES), F32)] * 4
            + [pltpu.VMEM((2, (T // CHUNK) * SUBLANES, LANES), F32)]
            + [pltpu.VMEM((2, T, LANES), BF16)] * 2
            + [pltpu.VMEM((LANES, T), F32),
               pltpu.VMEM((2, LANES, 2 * LANES), F32),
               pltpu.VMEM((T, LANES), F32), pltpu.VMEM((T, LANES), F32)]),
        compiler_params=_params("arbitrary", "arbitrary"),
        name="mlstm",
    )(ml, ml, ml, ml, small, lp["conv_w"], lp["conv_w"], lp["conv_b"], lp["conv_b"],
      lp["gate_bias"], lp["ml_gain"])


def _first_argmax(vals, row):
    mx = jnp.max(vals, axis=0, keepdims=True)
    idx = jnp.min(jnp.where(vals == mx, row, vals.shape[0]), axis=0, keepdims=True)
    return mx, idx


def _out_route_kernel(x_ref, a_ref, gl_ref, ml_ref, w_ref, g_ref, wr_ref, br_ref,
                      x1_ref, t_ref, route_ref, routet_ref):
    tm = x_ref.shape[0]
    sub = min(tm, ROUTE_SUBTILE)
    spans = [slice(r, r + sub) for r in range(0, tm, sub)]
    x1s, logits = [], []
    for rs in spans:
        x1 = (x_ref[rs, :]
              + _dot(a_ref[rs, :], w_ref[0:ATTN_WIDTH, :])
              + _dot(gl_ref[rs, :], w_ref[ATTN_WIDTH:ATTN_WIDTH + GLA_WIDTH, :])
              + _dot(ml_ref[rs, :], w_ref[ATTN_WIDTH + GLA_WIDTH:, :]))
        x1_ref[rs, :] = x1
        x1s.append(x1)
    for rs, x1 in zip(spans, x1s):
        t_hi, t_lo = _split(_rms(x1, g_ref[...]))
        t_ref[rs, 0:t_hi.shape[1]] = t_hi
        both = _dot(t_hi, wr_ref[...])
        logits.append((both[:, :LANES] + both[:, LANES:] + _dot(t_lo, wr_ref[:, :LANES])) + br_ref[...])
    grow = lax.broadcasted_iota(jnp.int32, (SUBLANES, sub), 0)
    erow = lax.broadcasted_iota(jnp.int32, (N_EXPERTS, sub), 0)
    for rs, lg in zip(spans, logits):
        lt = lg.T
        gl = jnp.where(grow < N_GROUPS, lt[0:SUBLANES, :], -jnp.inf)
        gmax, gi = _first_argmax(gl, grow)
        g_prob = 1.0 / jnp.sum(jnp.exp(gl - gmax), axis=0, keepdims=True)
        el = jnp.where(erow // EXPERTS_PER_GROUP == gi, lt[ROUTE_W_LANE:ROUTE_W_LANE + N_EXPERTS, :], -jnp.inf)
        v1, i1 = _first_argmax(el, erow)
        v2, i2 = _first_argmax(jnp.where(erow == i1, -jnp.inf, el), erow)
        e2 = jnp.exp(v2 - v1)
        w1 = g_prob / (1.0 + e2)
        w2 = g_prob * e2 / (1.0 + e2)
        comb = jnp.where(erow == i1, w1, jnp.where(erow == i2, w2, 0.0))
        head = jnp.where(grow == 0, gi.astype(F32), 0.0)
        route_t = jnp.concatenate(
            [head, comb, jnp.zeros((LANES - SUBLANES - N_EXPERTS, sub), F32)], axis=0)
        route_ref[rs, :] = route_t.T
        routet_ref[:, rs] = head


def _dispatch_kernel(t_ref, route_ref, routet_ref, xs_ref, cws_ref, pos_ref, cnt_ref):
    nb = ROUTE_BLOCK
    gi_row = routet_ref[0:1, :]
    sub = lax.broadcasted_iota(jnp.int32, (SUBLANES, nb), 0).astype(F32)
    onehot = (sub == gi_row)
    ri = lax.broadcasted_iota(jnp.int32, (nb, nb), 0)
    cj = lax.broadcasted_iota(jnp.int32, (nb, nb), 1)
    before = (ri < cj).astype(BF16)
    rank = _dot(onehot.astype(BF16), before)
    counts = jnp.broadcast_to(jnp.sum(onehot.astype(F32), axis=-1, keepdims=True), (SUBLANES, LANES))
    padded = jnp.ceil(counts * (1.0 / ROUTE_TILE)) * ROUTE_TILE
    srow = lax.broadcasted_iota(jnp.int32, (SUBLANES, LANES), 0)
    incl = padded
    s = 1
    while s < SUBLANES:
        incl = incl + jnp.where(srow >= s, pltpu.roll(incl, s, 0), 0.0)
        s *= 2
    start = incl - padded
    pos_row = jnp.sum(jnp.where(onehot, start[:, 0:1] + rank, 0.0), axis=0, keepdims=True)
    cnt_ref[0] = counts.astype(jnp.int32)

    pos_hi = jnp.floor(pos_row * (1.0 / ROUTE_TILE))
    prow = lax.broadcasted_iota(jnp.int32, (2 * SUBLANES, nb), 0)
    pieces = jnp.where(prow == 0, pos_hi, jnp.where(prow == 1, pos_row - ROUTE_TILE * pos_hi, 0.0))
    wrow = lax.broadcasted_iota(jnp.int32, (2 * SUBLANES, LANES), 0)
    weights = jnp.where(wrow == 0, float(ROUTE_TILE), jnp.where(wrow == 1, 1.0, 0.0))
    pos_ref[...] = _dot_tn(pieces.astype(BF16), weights.astype(BF16))

    route = route_ref[...]
    lane = lax.broadcasted_iota(jnp.int32, (nb, LANES), 1)
    comb = jnp.where(lane >= ROUTE_W_LANE, route, 0.0)
    c_hi, c_lo = _split(comb)
    c_lo2 = (comb - c_hi.astype(F32) - c_lo.astype(F32)).astype(BF16)
    d_model = xs_ref.shape[1]
    t_ref[:, d_model:] = (c_hi.astype(F32) + pltpu.roll(c_lo.astype(F32), ROUTE_LO_SHIFT, 1)
                          + pltpu.roll(c_lo2.astype(F32), 2 * ROUTE_LO_SHIFT, 1)).astype(BF16)
    tb = t_ref[...]
    for r in range(TILES_PER_BLOCK):
        rows = (lax.broadcasted_iota(jnp.int32, (ROUTE_TILE, nb), 0) + r * ROUTE_TILE).astype(F32)
        perm = (rows == pos_row).astype(BF16)
        moved = _dot(perm, tb)
        xs_ref[r * ROUTE_TILE:(r + 1) * ROUTE_TILE, :] = moved[:, :d_model].astype(BF16)
        cws_ref[r * ROUTE_TILE:(r + 1) * ROUTE_TILE, :] = moved[:, d_model:]


def _route_dispatch_kernel(x_ref, a_ref, gl_ref, ml_ref, w_ref, g_ref, wr_ref, br_ref,
                           x1_ref, xs_ref, cws_ref, pos_ref, cnt_ref, t_s, route_s, routet_s):
    _out_route_kernel(x_ref, a_ref, gl_ref, ml_ref, w_ref, g_ref, wr_ref, br_ref,
                      x1_ref, t_s, route_s, routet_s)
    _dispatch_kernel(t_s, route_s, routet_s, xs_ref, cws_ref, pos_ref, cnt_ref)


def _route_dispatch(x2, attn, gla_o, ml_o, lp):
    N, D = x2.shape
    nblk = N // ROUTE_BLOCK
    const = lambda i: (0, 0)
    tok = lambda i: (i, 0)
    return pl.pallas_call(
        _route_dispatch_kernel,
        grid=(nblk,),
        in_specs=[
            pl.BlockSpec((ROUTE_BLOCK, D), tok),
            pl.BlockSpec((ROUTE_BLOCK, ATTN_WIDTH), tok),
            pl.BlockSpec((ROUTE_BLOCK, GLA_WIDTH), tok),
            pl.BlockSpec((ROUTE_BLOCK, MLSTM_WIDTH), tok),
            pl.BlockSpec((ATTN_WIDTH + GLA_WIDTH + MLSTM_WIDTH, D), const),
            pl.BlockSpec((1, D), const),
            pl.BlockSpec((D, 2 * LANES), const),
            pl.BlockSpec((1, LANES), const),
        ],
        out_specs=[
            pl.BlockSpec((ROUTE_BLOCK, D), tok),
            pl.BlockSpec((ROUTE_ROWS, D), tok),
            pl.BlockSpec((ROUTE_ROWS, LANES), tok),
            pl.BlockSpec((ROUTE_BLOCK, LANES), tok),
            pl.BlockSpec((1, SUBLANES, LANES), lambda i: (i, 0, 0)),
        ],
        out_shape=[
            jax.ShapeDtypeStruct((N, D), F32),
            jax.ShapeDtypeStruct((nblk * ROUTE_ROWS, D), BF16),
            jax.ShapeDtypeStruct((nblk * ROUTE_ROWS, LANES), F32),
            jax.ShapeDtypeStruct((N, LANES), F32),
            jax.ShapeDtypeStruct((nblk, SUBLANES, LANES), jnp.int32),
        ],
        scratch_shapes=[
            pltpu.VMEM((ROUTE_BLOCK, D + LANES), BF16),
            pltpu.VMEM((ROUTE_BLOCK, LANES), F32),
            pltpu.VMEM((SUBLANES, ROUTE_BLOCK), F32),
        ],
        compiler_params=_params("arbitrary"),
        name="route_dispatch",
    )(x2, attn, gla_o, ml_o, lp["w_out"], lp["g_ffn"], lp["w_route"], lp["b_route"])


def _tile_schedule(cnt):
    nblk = cnt.shape[0]
    ntile = (cnt + ROUTE_TILE - 1) // ROUTE_TILE
    end = jnp.cumsum(ntile, axis=1)
    r = jnp.arange(TILES_PER_BLOCK, dtype=jnp.int32)
    grp = jnp.sum(r[None, :, None] >= end[:, None, :], axis=-1)
    grp = grp.reshape(-1).astype(jnp.int32)
    tile = jnp.arange(nblk * TILES_PER_BLOCK, dtype=jnp.int32)
    order = jnp.argsort(grp * (nblk * TILES_PER_BLOCK) + tile).astype(jnp.int32)
    n_active = jnp.sum(grp < N_GROUPS).astype(jnp.int32)
    g_sorted = grp[order]
    last_group = g_sorted[jnp.maximum(n_active - 1, 0)]
    g_sorted = jnp.where(g_sorted < N_GROUPS, g_sorted, last_group)
    return order, g_sorted, n_active[None]


def _expert_kernel(trow_ref, tgrp_ref, nact_ref, xs_ref, cws_ref, wg_ref, wu_ref, wd_ref, ys_ref,
                   wg_s, wu_s, wd_s):
    i = pl.program_id(0)

    @pl.when((i == 0) | (tgrp_ref[i] != tgrp_ref[jnp.maximum(i - 1, 0)]))
    def _():
        for j in range(EXPERTS_PER_GROUP):
            wg_s[j] = wg_ref[j].astype(BF16)
            wu_s[j] = wu_ref[j].astype(BF16)
            wd_s[j] = wd_ref[j].astype(BF16)

    @pl.when(i < nact_ref[0])
    def _():
        x = xs_ref[...]
        cws = cws_ref[...]
        lane = lax.broadcasted_iota(jnp.int32, cws.shape, 1)
        first = ROUTE_W_LANE + tgrp_ref[i] * EXPERTS_PER_GROUP
        y = jnp.zeros(ys_ref.shape, F32)

        def hidden(j):
            return _dot(x, wg_s[j]), _dot(x, wu_s[j])

        h = hidden(0)
        for j in range(EXPERTS_PER_GROUP):
            h_next = hidden(j + 1) if j + 1 < EXPERTS_PER_GROUP else None
            off = lane - (first + j)
            sel = (off == 0) | (off == ROUTE_LO_SHIFT) | (off == 2 * ROUTE_LO_SHIFT)
            wj = jnp.sum(jnp.where(sel, cws, 0.0), axis=-1, keepdims=True)
            a = (h[0] * _sigmoid(h[0]) * h[1]).astype(BF16)
            y = y + wj * _dot(a, wd_s[j])
            h = h_next
        ys_ref[...] = y.astype(BF16)

    @pl.when(i >= nact_ref[0])
    def _():
        ys_ref[...] = jnp.zeros(ys_ref.shape, BF16)


def _experts(xs, cws, order, grp, n_active, lp):
    rows, D = xs.shape
    n_tiles = rows // ROUTE_TILE
    tile = lambda i, trow, tgrp, nact: (trow[i], 0)
    layer_groups = lp["layer"] * N_GROUPS
    wsel = lambda i, trow, tgrp, nact: (layer_groups + tgrp[i], 0, 0)
    once = pl.Buffered(1)
    return pl.pallas_call(
        _expert_kernel,
        grid_spec=pltpu.PrefetchScalarGridSpec(
            num_scalar_prefetch=3,
            grid=(n_tiles,),
            in_specs=[
                pl.BlockSpec((ROUTE_TILE, D), tile),
                pl.BlockSpec((ROUTE_TILE, LANES), tile),
                pl.BlockSpec((EXPERTS_PER_GROUP, D, D_EXPERT), wsel, pipeline_mode=once),
                pl.BlockSpec((EXPERTS_PER_GROUP, D, D_EXPERT), wsel, pipeline_mode=once),
                pl.BlockSpec((EXPERTS_PER_GROUP, D_EXPERT, D), wsel, pipeline_mode=once),
            ],
            out_specs=pl.BlockSpec((ROUTE_TILE, D), tile),
            scratch_shapes=[
                pltpu.VMEM((EXPERTS_PER_GROUP, D, D_EXPERT), BF16),
                pltpu.VMEM((EXPERTS_PER_GROUP, D, D_EXPERT), BF16),
                pltpu.VMEM((EXPERTS_PER_GROUP, D_EXPERT, D), BF16),
            ],
        ),
        out_shape=jax.ShapeDtypeStruct((rows, D), BF16),
        compiler_params=_params("arbitrary"),
        name="experts",
    )(order, grp, n_active, xs, cws, lp["w_gate"], lp["w_up"], lp["w_down"])


def _combine_kernel(ys_ref, pos_ref, x1_ref, p_ref, g_ref, wpg_ref, wpp_ref, gfin_ref, o_ref,
                    *, final):
    tm = x1_ref.shape[0]
    pos = pos_ref[...]
    lane = lax.broadcasted_iota(jnp.int32, (tm, LANES), 1).astype(F32)
    perm_t = jnp.concatenate(
        [(lane + r * LANES == pos).astype(BF16) for r in range(ROUTE_ROWS // LANES)], axis=1)
    x = x1_ref[...] + _dot(perm_t, ys_ref[...])
    gate = _sigmoid(_dot(_rms(x, g_ref[...]).astype(BF16), wpg_ref[...]))
    x = x + gate * _dot(p_ref[...].astype(BF16), wpp_ref[...])
    if final:
        x = _rms(x, gfin_ref[...])
    o_ref[...] = x


def _combine(ys, pos, x1, p3, lp, g_final, final=False, tm=512):
    N, D = x1.shape
    inner = ROUTE_BLOCK // tm
    layer = lp["layer"]
    tok = lambda b, i: (b * inner + i, 0)
    const = lambda b, i: (0, 0)
    return pl.pallas_call(
        functools.partial(_combine_kernel, final=final),
        grid=(N // ROUTE_BLOCK, inner),
        in_specs=[
            pl.BlockSpec((ROUTE_ROWS, D), lambda b, i: (b, 0)),
            pl.BlockSpec((tm, LANES), tok),
            pl.BlockSpec((tm, D), tok),
            pl.BlockSpec((None, tm, p3.shape[2]), lambda b, i: (layer, b * inner + i, 0)),
            pl.BlockSpec((1, D), const),
            pl.BlockSpec((D, D), const),
            pl.BlockSpec((p3.shape[2], D), const),
            pl.BlockSpec((1, D), const),
        ],
        out_specs=pl.BlockSpec((tm, D), tok),
        out_shape=jax.ShapeDtypeStruct((N, D), F32),
        compiler_params=_params("arbitrary", "arbitrary"),
        name="combine",
    )(ys, pos, x1, p3, lp["g_ple"], lp["w_pg"], lp["w_pp"], g_final)


def _mix_out_moe(x2, attn, gla_o, ml_o, lp):
    x1, xs, cws, pos, cnt = _route_dispatch(x2, attn, gla_o, ml_o, lp)
    order, grp, n_active = _tile_schedule(cnt[:, :N_GROUPS, 0])
    return x1, _experts(xs, cws, order, grp, n_active, lp), pos


def _permute_in_cols(w):
    glr0 = ATTN_WIDTH + 2 * KV_WIDTH + 4 * GLA_WIDTH
    ml0 = glr0 + 2 * GLA_RANK
    mg0 = ml0 + 4 * MLSTM_WIDTH
    end = mg0 + 4 * MLSTM_HEADS
    assert end == w.shape[-1]
    pad = jnp.zeros(w.shape[:-1] + (IN_PERM_WIDTH - end,), BF16)
    parts = [w[..., :glr0], w[..., ml0:mg0], w[..., glr0:ml0], w[..., mg0:end]]
    return jnp.concatenate([part.astype(BF16) for part in parts] + [pad], axis=-1)


def _rope_tables(T):
    t = np.arange(T)
    inv = ROPE_THETA ** (-np.arange(0, ROPE_AXIS_DIM, 2, dtype=np.float64) / ROPE_AXIS_DIM)
    ang_r = (t // GRID_W)[None, :] * inv[:, None]
    ang_c = (t % GRID_W)[None, :] * inv[:, None]
    cos_h = np.concatenate([np.cos(ang_r), np.cos(ang_r), np.cos(ang_c), np.cos(ang_c)], axis=0)
    sin_h = np.concatenate([-np.sin(ang_r), np.sin(ang_r), -np.sin(ang_c), np.sin(ang_c)], axis=0)
    reps = LANES // HEAD_DIM
    return dict(cos_t=jnp.asarray(np.tile(cos_h, (reps, 1)), F32),
                sin_t=jnp.asarray(np.tile(sin_h, (reps, 1)), F32))


def kernel(x, p, norm_mix_g, w_in, attn_q_norm_g, attn_k_norm_g, gla_w_decay, gla_b_decay,
           gla_out_norm_g, mlstm_conv_w, mlstm_conv_b, mlstm_b_input, mlstm_b_forget,
           mlstm_out_norm_g, w_out, norm_ffn_g, w_group, b_group, w_router, b_router,
           w_expert_gate, w_expert_up, w_expert_down, norm_ple_g, w_ple_gate, w_ple_proj,
           final_norm_g):
    params = dict(
        norm_mix_g=norm_mix_g, w_in=w_in, attn_q_norm_g=attn_q_norm_g, attn_k_norm_g=attn_k_norm_g,
        gla_w_decay=gla_w_decay, gla_b_decay=gla_b_decay, gla_out_norm_g=gla_out_norm_g,
        mlstm_conv_w=mlstm_conv_w, mlstm_conv_b=mlstm_conv_b, mlstm_b_input=mlstm_b_input,
        mlstm_b_forget=mlstm_b_forget, mlstm_out_norm_g=mlstm_out_norm_g, w_out=w_out,
        norm_ffn_g=norm_ffn_g, w_group=w_group, b_group=b_group, w_router=w_router,
        b_router=b_router, w_expert_gate=w_expert_gate, w_expert_up=w_expert_up,
        w_expert_down=w_expert_down, norm_ple_g=norm_ple_g, w_ple_gate=w_ple_gate,
        w_ple_proj=w_ple_proj)
    B, T, D = x.shape
    rope = _rope_tables(T)
    N = B * T
    depth = w_in.shape[0]
    x2 = x.reshape(N, D)
    g_final = final_norm_g[None, :]
    p3 = p.reshape(depth, N, p.shape[-1])
    for i in range(depth):
        lp = _layer_params(params, i)
        qt, k, vt, gla, ml, small = _in_proj(x2, lp, rope, B, T)
        attn = _attention(lp["attn_safe"], qt, k, vt).reshape(N, ATTN_WIDTH)
        gla_o = _gla(gla, small, lp).reshape(N, GLA_WIDTH)
        ml_o = _mlstm(ml, small, lp).reshape(N, MLSTM_WIDTH)
        x1, ys, pos = _mix_out_moe(x2, attn, gla_o, ml_o, lp)
        x2 = _combine(ys, pos, x1, p3, lp, g_final, final=(i == depth - 1))
    return x2.reshape(B, T, D)


def _split_w(w):
    hi = w.astype(BF16)
    return hi, (w - hi.astype(F32)).astype(BF16)


def _stacked_experts(w):
    return w.reshape((w.shape[0] * w.shape[1],) + w.shape[2:])


def _layer_params(p, i):
    D = p["w_in"].shape[1]
    gq, gk = p["attn_q_norm_g"][i], p["attn_k_norm_g"][i]
    q_gain = jnp.tile(gq, LANES // HEAD_DIM) * (HEAD_DIM ** -0.5 * LOG2E)
    logit_bound = HEAD_DIM ** 0.5 * jnp.max(jnp.abs(gq)) * jnp.max(jnp.abs(gk))
    attn_safe = (logit_bound <= ATTN_SAFE_LOGIT).astype(jnp.int32)[None]
    wd = jnp.zeros((2, LANES, GLA_WIDTH), F32)
    wd = wd.at[0, :GLA_RANK].set(p["gla_w_decay"][i, 0]).at[1, GLA_RANK:2 * GLA_RANK].set(p["gla_w_decay"][i, 1])
    wd_hi, wd_lo = _split_w(wd)
    gate_bias = jnp.zeros((LANES,), F32).at[SMALL_GATE_LANE:SMALL_GATE_LANE + 4 * MLSTM_HEADS].set(
        jnp.concatenate([p["mlstm_b_input"][i, 0], p["mlstm_b_forget"][i, 0],
                         p["mlstm_b_input"][i, 1], p["mlstm_b_forget"][i, 1]]))
    w_route = jnp.zeros((D, LANES), F32)
    w_route = w_route.at[:, :N_GROUPS].set(p["w_group"][i])
    w_route = w_route.at[:, ROUTE_W_LANE:ROUTE_W_LANE + N_EXPERTS].set(p["w_router"][i])
    wr_hi, wr_lo = _split_w(w_route)
    b_route = jnp.zeros((LANES,), F32).at[:N_GROUPS].set(p["b_group"][i])
    b_route = b_route.at[ROUTE_W_LANE:ROUTE_W_LANE + N_EXPERTS].set(p["b_router"][i])
    return dict(
        g_mix=p["norm_mix_g"][i][None, :],
        w_in=_permute_in_cols(p["w_in"]),
        q_gain_t=jnp.broadcast_to(q_gain[:, None], (LANES, LANES)),
        k_gain_t=jnp.broadcast_to(jnp.tile(gk, LANES // HEAD_DIM)[:, None], (LANES, LANES)),
        attn_safe=attn_safe,
        wd_hi=wd_hi, wd_lo=wd_lo,
        bd=p["gla_b_decay"][i][:, None, :],
        gla_gain=jnp.tile(p["gla_out_norm_g"][i], LANES // HEAD_DIM)[None, :],
        conv_w=p["mlstm_conv_w"][i],
        conv_b=p["mlstm_conv_b"][i][None, :],
        gate_bias=gate_bias[None, :],
        ml_gain=jnp.tile(p["mlstm_out_norm_g"][i], LANES // HEAD_DIM)[None, :],
        w_out=p["w_out"][i].astype(BF16),
        g_ffn=p["norm_ffn_g"][i][None, :],
        w_route=jnp.concatenate([wr_hi, wr_lo], axis=1), b_route=b_route[None, :],
        layer=i,
        w_gate=_stacked_experts(p["w_expert_gate"]),
        w_up=_stacked_experts(p["w_expert_up"]),
        w_down=_stacked_experts(p["w_expert_down"]),
        g_ple=p["norm_ple_g"][i][None, :],
        w_pg=p["w_ple_gate"][i].astype(BF16),
        w_pp=p["w_ple_proj"][i].astype(BF16),
    )
```

```python
import functools

import jax
import jax.numpy as jnp
import numpy as np
from jax import lax
from jax.experimental import pallas as pl
from jax.experimental.pallas import tpu as pltpu

F32 = jnp.float32
BF16 = jnp.bfloat16

GRID_W = 64
HEAD_DIM = 64
ATTN_HEADS = 8
ATTN_KV_HEADS = 2
GLA_HEADS = 4
MLSTM_HEADS = 4
ATTN_WIDTH = ATTN_HEADS * HEAD_DIM
KV_WIDTH = ATTN_KV_HEADS * HEAD_DIM
GLA_WIDTH = GLA_HEADS * HEAD_DIM
MLSTM_WIDTH = MLSTM_HEADS * HEAD_DIM
GLA_RANK = 16
GLA_TAU = 16.0
CHUNK = 64
ROPE_THETA = 10000.0
ROPE_AXIS_DIM = HEAD_DIM // 2
N_GROUPS = 4
EXPERTS_PER_GROUP = 4
N_EXPERTS = N_GROUPS * EXPERTS_PER_GROUP
D_EXPERT = 512
EPS = 1e-6
NEG = -1e30

LANES = 128
SUBLANES = 8
VMEM_LIMIT_BYTES = 56 * 1024 * 1024

QK_WIDTH = ATTN_WIDTH + KV_WIDTH
OFF_V = QK_WIDTH
OFF_GLA = OFF_V + KV_WIDTH
OFF_ML = OFF_GLA + 4 * GLA_WIDTH
OFF_SMALL = OFF_ML + 4 * MLSTM_WIDTH
IN_PERM_WIDTH = OFF_SMALL + LANES
SMALL_GATE_LANE = 2 * GLA_RANK

ROUTE_BLOCK = 1024
ROUTE_TILE = 128
TILES_PER_BLOCK = (ROUTE_BLOCK + N_GROUPS * (ROUTE_TILE - 1)) // ROUTE_TILE
ROUTE_ROWS = TILES_PER_BLOCK * ROUTE_TILE
ROUTE_W_LANE = 8
ROUTE_LO_SHIFT = 32
ROUTE_SUBTILE = 256


def _dot(a, b):
    return jnp.dot(a, b, preferred_element_type=F32)


def _dot_nt(a, b):
    return lax.dot_general(a, b, (((1,), (1,)), ((), ())), preferred_element_type=F32)


def _dot_tn(a, b):
    return lax.dot_general(a, b, (((0,), (0,)), ((), ())), preferred_element_type=F32)


def _split(a):
    hi = a.astype(BF16)
    lo = (a - hi.astype(F32)).astype(BF16)
    return hi, lo


def _dot3(a, w_hi, w_lo):
    a_hi, a_lo = _split(a)
    return _dot(a_hi, w_hi) + _dot(a_lo, w_hi) + _dot(a_hi, w_lo)


def _log_sigmoid(x):
    return jnp.minimum(x, 0.0) - jnp.log(1.0 + jnp.exp(-jnp.abs(x)))


def _sigmoid(x):
    return 1.0 / (1.0 + jnp.exp(-x))


def _rms(x, g):
    return x * lax.rsqrt(jnp.mean(x * x, axis=-1, keepdims=True) + EPS) * g


def _params(*semantics):
    return pltpu.CompilerParams(dimension_semantics=semantics, vmem_limit_bytes=VMEM_LIMIT_BYTES)


def _in_proj_kernel(x_ref, g_ref, w_ref, cost_ref, sint_ref, gq_ref, gk_ref,
                    qt_ref, k_ref, vt_ref, gla_ref, ml_ref, small_ref):
    tm = x_ref.shape[0]
    h = _rms(x_ref[...], g_ref[...])
    z = _dot(h.astype(BF16), w_ref[...])

    heads = LANES // HEAD_DIM
    half = ROPE_AXIS_DIM // 2
    gq = jnp.concatenate([gq_ref[...]] * (tm // LANES), axis=1)
    cost = cost_ref[...]
    sint = sint_ref[...]
    gk = jnp.concatenate([gk_ref[...]] * (tm // LANES), axis=1)

    def norm_rope_t(c, gain):
        zt = z[:, c * LANES:(c + 1) * LANES].T
        z3 = zt.reshape(heads, HEAD_DIM, tm)
        inv = lax.rsqrt(jnp.mean(z3 * z3, axis=1, keepdims=True) + EPS)
        y = (z3 * inv).reshape(LANES, tm) * gain
        partner = jnp.concatenate(
            [y[(r ^ 1) * half:((r ^ 1) + 1) * half, :] for r in range(LANES // half)], axis=0)
        return y * cost + partner * sint

    for c in range(ATTN_WIDTH // LANES):
        qt_ref[0, c * LANES:(c + 1) * LANES, :] = norm_rope_t(c, gq).astype(BF16)
    k_ref[0] = norm_rope_t(ATTN_WIDTH // LANES, gk).T.astype(BF16)
    vt_ref[0] = z[:, OFF_V:OFF_V + KV_WIDTH].T.astype(BF16)
    gla_ref[0] = z[:, OFF_GLA:OFF_ML]
    ml_ref[0] = z[:, OFF_ML:OFF_SMALL]
    small_ref[0] = z[:, OFF_SMALL:IN_PERM_WIDTH]


def _in_proj(x2, lp, rope, B, T, tm=512):
    N, D = x2.shape
    tpb = T // tm
    const = lambda i: (0, 0)
    tok3 = lambda i: (i // tpb, i % tpb, 0)
    tokT = lambda i: (i // tpb, 0, i % tpb)
    layer = lp["layer"]
    return pl.pallas_call(
        _in_proj_kernel,
        grid=(N // tm,),
        in_specs=[
            pl.BlockSpec((tm, D), lambda i: (i, 0)),
            pl.BlockSpec((1, D), const),
            pl.BlockSpec((None, D, IN_PERM_WIDTH), lambda i: (layer, 0, 0)),
            pl.BlockSpec((LANES, tm), lambda i: (0, i % tpb)),
            pl.BlockSpec((LANES, tm), lambda i: (0, i % tpb)),
            pl.BlockSpec((LANES, LANES), const),
            pl.BlockSpec((LANES, LANES), const),
        ],
        out_specs=[
            pl.BlockSpec((1, ATTN_WIDTH, tm), tokT),
            pl.BlockSpec((1, tm, KV_WIDTH), tok3),
            pl.BlockSpec((1, KV_WIDTH, tm), tokT),
            pl.BlockSpec((1, tm, 4 * GLA_WIDTH), tok3),
            pl.BlockSpec((1, tm, 4 * MLSTM_WIDTH), tok3),
            pl.BlockSpec((1, tm, LANES), tok3),
        ],
        out_shape=[
            jax.ShapeDtypeStruct((B, ATTN_WIDTH, T), BF16),
            jax.ShapeDtypeStruct((B, T, KV_WIDTH), BF16),
            jax.ShapeDtypeStruct((B, KV_WIDTH, T), BF16),
            jax.ShapeDtypeStruct((B, T, 4 * GLA_WIDTH), F32),
            jax.ShapeDtypeStruct((B, T, 4 * MLSTM_WIDTH), F32),
            jax.ShapeDtypeStruct((B, T, LANES), F32),
        ],
        compiler_params=_params("arbitrary"),
        name="in_proj",
    )(x2, lp["g_mix"], lp["w_in"], rope["cos_t"], rope["sin_t"], lp["q_gain_t"], lp["k_gain_t"])


ATTN_SAFE_LOGIT = 40.0
LOG2E = 1.4426950408889634


def _attn_kernel(safe_ref, qt_ref, k_ref, vt_ref, o_ref, *, tk):
    tq = qt_ref.shape[2]
    T = k_ref.shape[1]
    G = ATTN_HEADS // ATTN_KV_HEADS
    n = G * tq
    zeros = jnp.zeros((HEAD_DIM, n), BF16)

    def q_operand(j):
        base = j * G * HEAD_DIM
        qs = jnp.concatenate(
            [qt_ref[0, base + h * HEAD_DIM:base + (h + 1) * HEAD_DIM, :] for h in range(G)], axis=1)
        return jnp.concatenate([qs, zeros] if j == 0 else [zeros, qs], axis=0)

    def finish(j, acc, l):
        base = j * G * HEAD_DIM
        o = acc * (1.0 / l)
        ot = jnp.concatenate([o[:, h * tq:(h + 1) * tq] for h in range(G)], axis=0)
        o_ref[0, :, base:base + G * HEAD_DIM] = ot.T.astype(BF16)

    @pl.when(safe_ref[0] == 1)
    def _():
        qps = [q_operand(j) for j in range(ATTN_KV_HEADS)]
        units = [(c, j) for c in range(T // tk) for j in range(ATTN_KV_HEADS)]

        def scores(u):
            c, j = units[u]
            return _dot(k_ref[0, c * tk:(c + 1) * tk, :], qps[j])

        l8 = [jnp.zeros((SUBLANES, n), F32)] * ATTN_KV_HEADS
        acc = [jnp.zeros((HEAD_DIM, n), F32)] * ATTN_KV_HEADS
        st = scores(0)
        for u, (c, j) in enumerate(units):
            st_next = scores(u + 1) if u + 1 < len(units) else None
            p = jnp.exp2(st)
            l8[j] = l8[j] + jnp.sum(p.reshape(tk // SUBLANES, SUBLANES, n), axis=0)
            vc = vt_ref[0, j * HEAD_DIM:(j + 1) * HEAD_DIM, c * tk:(c + 1) * tk]
            acc[j] = acc[j] + _dot(vc, p.astype(BF16))
            st = st_next
        for j in range(ATTN_KV_HEADS):
            finish(j, acc[j], jnp.sum(l8[j], axis=0, keepdims=True))

    @pl.when(safe_ref[0] == 0)
    def _():
        for j in range(ATTN_KV_HEADS):
            qp = q_operand(j)

            def body(c, carry, qp=qp, j=j):
                m, l, acc = carry
                off = pl.multiple_of(c * tk, tk)
                st = _dot(k_ref[0, pl.ds(off, tk), :], qp)
                m_new = jnp.maximum(m, jnp.max(st, axis=0, keepdims=True))
                alpha = jnp.exp2(m - m_new)
                p = jnp.exp2(st - m_new)
                l = alpha * l + jnp.sum(p, axis=0, keepdims=True)
                vc = vt_ref[0, j * HEAD_DIM:(j + 1) * HEAD_DIM, pl.ds(off, tk)]
                return m_new, l, alpha * acc + _dot(vc, p.astype(BF16))

            init = (jnp.full((1, n), NEG, F32), jnp.zeros((1, n), F32), jnp.zeros((HEAD_DIM, n), F32))
            _, l, acc = lax.fori_loop(0, T // tk, body, init)
            finish(j, acc, l)


def _attention(safe, qt, k, vt, tq=256, tk=128):
    B, _, T = qt.shape
    tk = min(tk, T)
    return pl.pallas_call(
        functools.partial(_attn_kernel, tk=tk),
        grid_spec=pltpu.PrefetchScalarGridSpec(
            num_scalar_prefetch=1,
            grid=(B, T // tq),
            in_specs=[
                pl.BlockSpec((1, ATTN_WIDTH, tq), lambda b, i, s: (b, 0, i)),
                pl.BlockSpec((1, T, KV_WIDTH), lambda b, i, s: (b, 0, 0)),
                pl.BlockSpec((1, KV_WIDTH, T), lambda b, i, s: (b, 0, 0)),
            ],
            out_specs=pl.BlockSpec((1, tq, ATTN_WIDTH), lambda b, i, s: (b, i, 0)),
        ),
        out_shape=jax.ShapeDtypeStruct((B, T, ATTN_WIDTH), BF16),
        compiler_params=_params("arbitrary", "arbitrary"),
        name="attention",
    )(safe, qt, k, vt)


def _chunk_scan(x, pos, op, fill, reverse):
    rows = x.shape[0]
    s = 1
    while s < CHUNK:
        if reverse:
            shifted = jnp.where(pos < CHUNK - s, pltpu.roll(x, rows - s, 0), fill)
        else:
            shifted = jnp.where(pos >= s, pltpu.roll(x, s, 0), fill)
        x = op(x, shifted)
        s *= 2
    return x


def _chunk_pos(rows):
    return lax.broadcasted_iota(jnp.int32, (rows, LANES), 0) % CHUNK


def _chunk_row(a, reverse_dir, idx_fwd, idx_bwd):
    rows = a.shape[0]
    a3 = a.reshape(rows // CHUNK, CHUNK, LANES)
    i = idx_bwd if reverse_dir else idx_fwd
    return jnp.broadcast_to(a3[:, i:i + 1, :], a3.shape).reshape(rows, LANES)


def _stack_heads(x):
    lane = lax.broadcasted_iota(jnp.int32, x.shape, 1)
    zero = jnp.zeros_like(x)
    return jnp.concatenate([jnp.where(lane < HEAD_DIM, x, zero), jnp.where(lane >= HEAD_DIM, x, zero)], axis=0)


def _select_heads(x):
    c = x.shape[0] // 2
    lane = lax.broadcasted_iota(jnp.int32, (c, x.shape[1]), 1)
    return jnp.where(lane < HEAD_DIM, x[:c], x[c:])


def _stacked_causal_masks():
    ci = lax.broadcasted_iota(jnp.int32, (2 * CHUNK, CHUNK), 0) % CHUNK
    si = lax.broadcasted_iota(jnp.int32, (2 * CHUNK, CHUNK), 1)
    return ci >= si, ci <= si


def _pair_blockdiag(width):
    r = lax.broadcasted_iota(jnp.int32, (LANES, width), 0) // HEAD_DIM
    c = lax.broadcasted_iota(jnp.int32, (LANES, width), 1) // (width // 2)
    return r == c


def _head_rms(o, gain):
    lane = lax.broadcasted_iota(jnp.int32, o.shape, 1)
    lo = lane < HEAD_DIM
    sq = o * o
    s_lo = jnp.sum(jnp.where(lo, sq, 0.0), axis=-1, keepdims=True)
    s_hi = jnp.sum(jnp.where(lo, 0.0, sq), axis=-1, keepdims=True)
    ms = jnp.where(lo, s_lo, s_hi) * (1.0 / HEAD_DIM)
    return o * lax.rsqrt(ms + EPS) * gain


PREP_ROWS = 1024
GLA_CHUNKS_PER_STEP = 32


def _gla_kernel(q_ref, k_ref, v_ref, g_ref, small_ref, wdh_ref, wdl_ref, bd_ref, gain_ref,
                o_ref, qe_s, ke_s, dec_s, of_s, ob_s):
    T = q_ref.shape[1]
    nc = T // CHUNK
    R = min(PREP_ROWS, T)
    cpt = R // CHUNK
    pos = _chunk_pos(R)
    w_hi = jnp.concatenate([wdh_ref[0], wdh_ref[1]], axis=1)
    w_lo = jnp.concatenate([wdl_ref[0], wdl_ref[1]], axis=1)
    bias = jnp.concatenate([bd_ref[0], bd_ref[1]], axis=1)

    def prep(t, _):
        r0 = pl.multiple_of(t * R, R)
        q = q_ref[0, pl.ds(r0, R), :] * HEAD_DIM ** -0.5
        k = k_ref[0, pl.ds(r0, R), :]
        la2 = _log_sigmoid(_dot3(small_ref[0, pl.ds(r0, R), :], w_hi, w_lo) + bias) * (1.0 / GLA_TAU)
        for d in range(2):
            b = _chunk_scan(la2[:, d * LANES:(d + 1) * LANES], pos, jnp.add, 0.0, reverse=bool(d))
            b_mid = _chunk_row(b, d, CHUNK // 2 - 1, CHUNK // 2)
            b_last = _chunk_row(b, d, CHUNK - 1, 0)
            qe_s[d, pl.ds(r0, R), :] = (q * jnp.exp(b - b_mid)).astype(BF16)
            ke_s[d, pl.ds(r0, R), :] = (k * jnp.exp(b_mid - b)).astype(BF16)
            c8 = pl.ds(pl.multiple_of(t * cpt * SUBLANES, SUBLANES), cpt * SUBLANES)
            for kind, val in enumerate((b_last, b_last - b_mid, b_mid)):
                rows8 = val.reshape(cpt, CHUNK, LANES)[:, :SUBLANES, :].reshape(cpt * SUBLANES, LANES)
                dec_s[d, kind, c8, :] = jnp.exp(rows8)
        return 0

    lax.fori_loop(0, T // R, prep, 0)

    outs = (of_s, ob_s)
    masks = _stacked_causal_masks()
    blockdiag = _pair_blockdiag(LANES)
    G = min(GLA_CHUNKS_PER_STEP, nc)
    assert nc % G == 0

    def step(it, states):
        units = [(d, g) for d in range(2) for g in range(G)]
        ops = {}
        for d, g in units:
            c = it * G + g
            c = (nc - 1 - c) if d else c
            r0 = pl.multiple_of(c * CHUNK, CHUNK)
            c8 = pl.ds(pl.multiple_of(c * SUBLANES, SUBLANES), SUBLANES)
            ops[d, g] = dict(
                r0=r0,
                qe=qe_s[d, pl.ds(r0, CHUNK), :], ke=ke_s[d, pl.ds(r0, CHUNK), :],
                v=v_ref[0, pl.ds(r0, CHUNK), :].astype(BF16),
                dec=dec_s[d, 0, c8, :][0:1, :], to_end=dec_s[d, 1, c8, :][0:1, :],
                from_start=dec_s[d, 2, c8, :][0:1, :])
        a2, kv = {}, {}
        for u in units:
            o = ops[u]
            a2[u] = _dot_nt(_stack_heads(o["qe"]), o["ke"])
            kv[u] = _dot_tn(o["v"], o["ke"]) * o["to_end"]
        st_in = {}
        new_states = []
        for d in range(2):
            st = states[d]
            for g in range(G):
                st_in[d, g] = (st * ops[d, g]["from_start"]).astype(BF16)
                st = st * ops[d, g]["dec"] + jnp.where(blockdiag, kv[d, g], 0.0)
            new_states.append(st)
        for u in units:
            o = ops[u]
            a = jnp.where(masks[u[0]], a2[u], 0.0).astype(BF16)
            intra = _select_heads(_dot(a, o["v"]))
            outs[u[0]][pl.ds(o["r0"], CHUNK), :] = intra + _dot_nt(o["qe"], st_in[u])
        return tuple(new_states)

    zero = jnp.zeros((LANES, LANES), F32)
    lax.fori_loop(0, nc // G, step, (zero, zero))

    def post(t, _):
        r0 = pl.multiple_of(t * R, R)
        o = _head_rms(of_s[pl.ds(r0, R), :] + ob_s[pl.ds(r0, R), :], gain_ref[...])
        g = g_ref[0, pl.ds(r0, R), :]
        o_ref[0, pl.ds(r0, R), :] = (o * (g * _sigmoid(g))).astype(BF16)
        return 0

    lax.fori_loop(0, T // R, post, 0)


def _gla(gla, small, lp):
    B, T, _ = gla.shape
    pairs = GLA_WIDTH // LANES
    col = lambda off: (lambda b, hp: (b, 0, off * pairs + hp))
    return pl.pallas_call(
        _gla_kernel,
        grid=(B, pairs),
        in_specs=[
            pl.BlockSpec((1, T, LANES), col(0)),
            pl.BlockSpec((1, T, LANES), col(1)),
            pl.BlockSpec((1, T, LANES), col(2)),
            pl.BlockSpec((1, T, LANES), col(3)),
            pl.BlockSpec((1, T, LANES), lambda b, hp: (b, 0, 0)),
            pl.BlockSpec((2, LANES, LANES), lambda b, hp: (0, 0, hp)),
            pl.BlockSpec((2, LANES, LANES), lambda b, hp: (0, 0, hp)),
            pl.BlockSpec((2, 1, LANES), lambda b, hp: (0, 0, hp)),
            pl.BlockSpec((1, LANES), lambda b, hp: (0, 0)),
        ],
        out_specs=pl.BlockSpec((1, T, LANES), lambda b, hp: (b, 0, hp)),
        out_shape=jax.ShapeDtypeStruct((B, T, GLA_WIDTH), BF16),
        scratch_shapes=[
            pltpu.VMEM((2, T, LANES), BF16), pltpu.VMEM((2, T, LANES), BF16),
            pltpu.VMEM((2, 3, (T // CHUNK) * SUBLANES, LANES), F32),
            pltpu.VMEM((T, LANES), F32), pltpu.VMEM((T, LANES), F32),
        ],
        compiler_params=_params("arbitrary", "arbitrary"),
        name="gla",
    )(gla, gla, gla, gla, small, lp["wd_hi"], lp["wd_lo"], lp["bd"], lp["gla_gain"])


GATE_I = SMALL_GATE_LANE
GATE_F = SMALL_GATE_LANE + MLSTM_HEADS


def _gate_lane(d, hh):
    return GATE_F + 2 * MLSTM_HEADS * d + hh


MLSTM_CHUNKS_PER_STEP = 32


def _chunk_rows8(a, row):
    n = a.shape[0] // CHUNK
    a3 = a.reshape(n, CHUNK, LANES)
    full = jnp.broadcast_to(a3[:, row:row + 1, :], a3.shape)
    return full[:, :SUBLANES, :].reshape(n * SUBLANES, LANES)


def _expand_rows8(a8):
    n, w = a8.shape[0] // SUBLANES, a8.shape[1]
    a3 = a8.reshape(n, SUBLANES, w)[:, 0:1, :]
    return jnp.broadcast_to(a3, (n, CHUNK, w)).reshape(n * CHUNK, w)


def _gate_select():
    src = lax.broadcasted_iota(jnp.int32, (LANES, 2 * LANES), 0)
    dst = lax.broadcasted_iota(jnp.int32, (LANES, 2 * LANES), 1)
    want = GATE_F + 2 * MLSTM_HEADS * (dst // LANES) + (dst % LANES) // HEAD_DIM
    return jnp.where(src == want, 1.0, 0.0).astype(BF16)


def _gate_broadcast(x, sel, gate_mask, pieces=2):
    x = jnp.where(gate_mask, x, 0.0)
    if pieces == 1:
        return _dot(x.astype(BF16), sel)
    hi, lo = _split(x)
    return _dot(hi, sel) + _dot(lo, sel)


def _mlstm_kernel(q_ref, k_ref, v_ref, og_ref, small_ref, wq_ref, wk_ref, bq_ref, bk_ref,
                  gbias_ref, gain_ref, o_ref, q_s, k_s, b_s, r_s, cm_s, fl1_s, bl_s, rl_s, mf_s, mb_s,
                  wp_s, qi_s, kw_s, rt_s, st_s, of_s, ob_s):
    T = q_ref.shape[1]
    nc = T // CHUNK
    R = min(PREP_ROWS, T)
    hp = pl.program_id(1)
    G = min(MLSTM_CHUNKS_PER_STEP, nc)
    assert nc % G == 0
    cpt = R // CHUNK

    row = lax.broadcasted_iota(jnp.int32, (T, LANES), 0)
    for src, w_ref, b_ref, dst, scale in ((q_ref, wq_ref, bq_ref, q_s, 1.0),
                                          (k_ref, wk_ref, bk_ref, k_s, HEAD_DIM ** -0.5)):
        xc = src[0]
        prev = jnp.where(row >= 1, pltpu.roll(xc, 1, 0), 0.0)
        nxt = jnp.where(row < T - 1, pltpu.roll(xc, T - 1, 0), 0.0)
        y = prev * w_ref[0:1, :] + xc * w_ref[1:2, :] + nxt * w_ref[2:3, :] + b_ref[...]
        dst[...] = (y * _sigmoid(y) * scale).astype(BF16)

    pos = _chunk_pos(R)
    lane = lax.broadcasted_iota(jnp.int32, (R, LANES), 1)
    is_bwd = lane >= GATE_I + 2 * MLSTM_HEADS
    is_bwd8 = lax.broadcasted_iota(jnp.int32, (cpt * SUBLANES, LANES), 1) >= GATE_I + 2 * MLSTM_HEADS
    heads_per_pair = LANES // HEAD_DIM
    shift = (LANES - heads_per_pair * hp) % LANES
    gate_lanes = [_gate_lane(d, hh) for d in range(2) for hh in range(2)]
    gate_mask = functools.reduce(jnp.logical_or, [lane == l for l in gate_lanes])
    lane8 = lax.broadcasted_iota(jnp.int32, (cpt * SUBLANES, LANES), 1)
    gate_mask8 = functools.reduce(jnp.logical_or, [lane8 == l for l in gate_lanes])

    def prep(t, _):
        r0 = pl.multiple_of(t * R, R)
        gc = pltpu.roll(small_ref[0, pl.ds(r0, R), :] + gbias_ref[...], shift, 1)
        logf = _log_sigmoid(gc)
        pre = _chunk_scan(logf, pos, jnp.add, 0.0, False)
        b = jnp.where(is_bwd, _chunk_row(pre, 0, CHUNK - 1, CHUNK - 1) - pre + logf, pre)
        r = pltpu.roll(gc, MLSTM_HEADS, 1) - b
        cm = jnp.where(is_bwd, _chunk_scan(r, pos, jnp.maximum, NEG, True),
                       _chunk_scan(r, pos, jnp.maximum, NEG, False))
        b_s[pl.ds(r0, R), :] = b
        r_s[pl.ds(r0, R), :] = r
        cm_s[pl.ds(r0, R), :] = cm
        rt_s[:, pl.ds(r0, R)] = r.T
        c8 = pl.ds(pl.multiple_of(t * cpt * SUBLANES, SUBLANES), cpt * SUBLANES)
        bl_s[c8, :] = jnp.where(is_bwd8, _chunk_rows8(b, 0), _chunk_rows8(b, CHUNK - 1))
        rl_s[c8, :] = jnp.where(is_bwd8, _chunk_rows8(cm, 0), _chunk_rows8(cm, CHUNK - 1))
        return 0

    lax.fori_loop(0, T // R, prep, 0)

    def m_chain(n, carry):
        mf, mb = carry
        rf = pl.ds(pl.multiple_of(n * SUBLANES, SUBLANES), SUBLANES)
        rb = pl.ds(pl.multiple_of((nc - 1 - n) * SUBLANES, SUBLANES), SUBLANES)
        mf_s[rf, :] = mf
        mb_s[rb, :] = mb
        return (bl_s[rf, :] + jnp.maximum(mf, rl_s[rf, :]), bl_s[rb, :] + jnp.maximum(mb, rl_s[rb, :]))

    m0 = jnp.full((SUBLANES, LANES), NEG, F32)
    lax.fori_loop(0, nc, m_chain, (m0, m0))

    expo_s, floor_s = (b_s, cm_s), (r_s, fl1_s)
    sel = _gate_select()

    def weights(t, _):
        r0 = pl.multiple_of(t * R, R)
        c8 = pl.ds(pl.multiple_of(t * cpt * SUBLANES, SUBLANES), cpt * SUBLANES)
        rows = pl.ds(r0, R)
        m_in8 = jnp.where(is_bwd8, mb_s[c8, :], mf_s[c8, :])
        bl8 = bl_s[c8, :]
        m_out8 = bl8 + jnp.maximum(m_in8, rl_s[c8, :])
        m_in = _expand_rows8(m_in8)
        mx = jnp.maximum(m_in, cm_s[rows, :])
        wp_b = jnp.exp(_gate_broadcast(bl8 + m_in8 - m_out8, sel, gate_mask8))
        mx_b = _gate_broadcast(mx, sel, gate_mask)
        floor = jnp.exp(-_gate_broadcast(b_s[rows, :] + mx, sel, gate_mask))
        w_inter = _gate_broadcast(jnp.exp(m_in - mx), sel, gate_mask, pieces=1)
        wk = _gate_broadcast(jnp.exp(_expand_rows8(bl8 - m_out8) + r_s[rows, :]), sel, gate_mask, pieces=1)
        q = q_s[rows, :].astype(F32)
        k = k_s[rows, :].astype(F32)
        for d in range(2):
            half = slice(d * LANES, (d + 1) * LANES)
            qi_s[d, rows, :] = (q * w_inter[:, half]).astype(BF16)
            kw_s[d, rows, :] = (k * wk[:, half]).astype(BF16)
            wp_s[d, c8, :] = wp_b[:, half]
            expo_s[d][rows, :] = -mx_b[:, half]
            floor_s[d][rows, :] = floor[:, half]
        return 0

    lax.fori_loop(0, T // R, weights, 0)

    outs = (of_s, ob_s)
    ci = lax.broadcasted_iota(jnp.int32, (CHUNK, LANES), 0)
    si = lax.broadcasted_iota(jnp.int32, (CHUNK, LANES), 1) % CHUNK
    masks = (ci >= si, ci <= si)
    br = lax.broadcasted_iota(jnp.int32, (LANES, 2 * LANES), 0) // HEAD_DIM
    bc = (lax.broadcasted_iota(jnp.int32, (LANES, 2 * LANES), 1) % LANES) // HEAD_DIM
    blockdiag = br == bc
    ones = jnp.ones((CHUNK, LANES), BF16)
    st_s[...] = jnp.zeros(st_s.shape, F32)
    gate_rows = slice(GATE_I, GATE_I + 4 * MLSTM_HEADS)

    def step(it, _):
        units = [(d, g) for d in range(2) for g in range(G)]
        ops = {}
        for d in range(2):
            first = (nc - (it + 1) * G) if d else it * G
            rt = rt_s[gate_rows, pl.ds(pl.multiple_of(first * CHUNK, G * CHUNK), G * CHUNK)]
            for g in range(G):
                local = (G - 1 - g) if d else g
                c = first + local
                rows = pl.ds(pl.multiple_of(c * CHUNK, CHUNK), CHUNK)
                c8 = pl.ds(pl.multiple_of(c * SUBLANES, SUBLANES), SUBLANES)
                r_row = jnp.concatenate(
                    [rt[_gate_lane(d, hh) - GATE_I:_gate_lane(d, hh) - GATE_I + 1,
                        local * CHUNK:(local + 1) * CHUNK] for hh in range(2)], axis=1)
                wp = wp_s[d, c8, :][0:1, :]
                ops[d, g] = dict(
                    rows=rows, q=q_s[rows, :], k=k_s[rows, :], qi=qi_s[d, rows, :], kw=kw_s[d, rows, :],
                    va=jnp.concatenate([v_ref[0, rows, :].astype(BF16), ones], axis=1),
                    expo=expo_s[d][rows, :] + r_row, floor=floor_s[d][rows, :],
                    wp=jnp.concatenate([wp, wp], axis=1))
        qk, kv = {}, {}
        for u in units:
            o = ops[u]
            qk[u] = _dot_nt(o["q"], _stack_heads(o["k"]))
            kv[u] = _dot_tn(o["kw"], o["va"])
        st_in = {}
        for d in range(2):
            st = st_s[d]
            for g in range(G):
                st_in[d, g] = st.astype(BF16)
                st = st * ops[d, g]["wp"] + jnp.where(blockdiag, kv[d, g], 0.0)
            st_s[d] = st
        for u in units:
            d = u[0]
            o = ops[u]
            smat = (qk[u] * jnp.where(masks[d], jnp.exp(o["expo"]), 0.0)).astype(BF16)
            va_bd = jnp.where(blockdiag, jnp.concatenate([o["va"], o["va"]], axis=0), jnp.zeros((), BF16))
            num = _dot(jnp.concatenate([o["qi"], smat], axis=1), jnp.concatenate([st_in[u], va_bd], axis=0))
            den = jnp.maximum(jnp.abs(num[:, LANES:]), o["floor"])
            outs[d][o["rows"], :] = num[:, :LANES] / den
        return 0

    lax.fori_loop(0, nc // G, step, 0)

    def post(t, _):
        r0 = pl.multiple_of(t * R, R)
        h = _head_rms(of_s[pl.ds(r0, R), :] + ob_s[pl.ds(r0, R), :], gain_ref[...])
        o_ref[0, pl.ds(r0, R), :] = (h * _sigmoid(og_ref[0, pl.ds(r0, R), :])).astype(BF16)
        return 0

    lax.fori_loop(0, T // R, post, 0)


def _mlstm(ml, small, lp):
    B, T, _ = ml.shape
    pairs = MLSTM_WIDTH // LANES
    col = lambda off: (lambda b, hp: (b, 0, off * pairs + hp))
    const = lambda b, hp: (0, 0)
    return pl.pallas_call(
        _mlstm_kernel,
        grid=(B, pairs),
        in_specs=[
            pl.BlockSpec((1, T, LANES), col(0)),
            pl.BlockSpec((1, T, LANES), col(1)),
            pl.BlockSpec((1, T, LANES), col(2)),
            pl.BlockSpec((1, T, LANES), col(3)),
            pl.BlockSpec((1, T, LANES), lambda b, hp: (b, 0, 0)),
            pl.BlockSpec((3, LANES), lambda b, hp: (0, hp)),
            pl.BlockSpec((3, LANES), lambda b, hp: (0, pairs + hp)),
            pl.BlockSpec((1, LANES), lambda b, hp: (0, hp)),
            pl.BlockSpec((1, LANES), lambda b, hp: (0, pairs + hp)),
            pl.BlockSpec((1, LANES), const),
            pl.BlockSpec((1, LANES), const),
        ],
        out_specs=pl.BlockSpec((1, T, LANES), lambda b, hp: (b, 0, hp)),
        out_shape=jax.ShapeDtypeStruct((B, T, MLSTM_WIDTH), BF16),
        scratch_shapes=(
            [pltpu.VMEM((T, LANES), BF16)] * 2
            + [pltpu.VMEM((T, LANES), F32)] * 4
            + [pltpu.VMEM(((T // CHUNK) * SUBLANES, LANES), F32)] * 4
            + [pltpu.VMEM((2, (T // CHUNK) * SUBLANES, LANES), F32)]
            + [pltpu.VMEM((2, T, LANES), BF16)] * 2
            + [pltpu.VMEM((LANES, T), F32),
               pltpu.VMEM((2, LANES, 2 * LANES), F32),
               pltpu.VMEM((T, LANES), F32), pltpu.VMEM((T, LANES), F32)]),
        compiler_params=_params("arbitrary", "arbitrary"),
        name="mlstm",
    )(ml, ml, ml, ml, small, lp["conv_w"], lp["conv_w"], lp["conv_b"], lp["conv_b"],
      lp["gate_bias"], lp["ml_gain"])


def _first_argmax(vals, row):
    mx = jnp.max(vals, axis=0, keepdims=True)
    idx = jnp.min(jnp.where(vals == mx, row, vals.shape[0]), axis=0, keepdims=True)
    return mx, idx


def _out_route_kernel(x_ref, a_ref, gl_ref, ml_ref, w_ref, g_ref, wr_ref, br_ref,
                      x1_ref, t_ref, route_ref, routet_ref):
    tm = x_ref.shape[0]
    sub = min(tm, ROUTE_SUBTILE)
    spans = [slice(r, r + sub) for r in range(0, tm, sub)]
    x1s, logits = [], []
    for rs in spans:
        x1 = (x_ref[rs, :]
              + _dot(a_ref[rs, :], w_ref[0:ATTN_WIDTH, :])
              + _dot(gl_ref[rs, :], w_ref[ATTN_WIDTH:ATTN_WIDTH + GLA_WIDTH, :])
              + _dot(ml_ref[rs, :], w_ref[ATTN_WIDTH + GLA_WIDTH:, :]))
        x1_ref[rs, :] = x1
        x1s.append(x1)
    for rs, x1 in zip(spans, x1s):
        t_hi, t_lo = _split(_rms(x1, g_ref[...]))
        t_ref[rs, 0:t_hi.shape[1]] = t_hi
        both = _dot(t_hi, wr_ref[...])
        logits.append((both[:, :LANES] + both[:, LANES:] + _dot(t_lo, wr_ref[:, :LANES])) + br_ref[...])
    grow = lax.broadcasted_iota(jnp.int32, (SUBLANES, sub), 0)
    erow = lax.broadcasted_iota(jnp.int32, (N_EXPERTS, sub), 0)
    for rs, lg in zip(spans, logits):
        lt = lg.T
        gl = jnp.where(grow < N_GROUPS, lt[0:SUBLANES, :], -jnp.inf)
        gmax, gi = _first_argmax(gl, grow)
        g_prob = 1.0 / jnp.sum(jnp.exp(gl - gmax), axis=0, keepdims=True)
        el = jnp.where(erow // EXPERTS_PER_GROUP == gi, lt[ROUTE_W_LANE:ROUTE_W_LANE + N_EXPERTS, :], -jnp.inf)
        v1, i1 = _first_argmax(el, erow)
        v2, i2 = _first_argmax(jnp.where(erow == i1, -jnp.inf, el), erow)
        e2 = jnp.exp(v2 - v1)
        w1 = g_prob / (1.0 + e2)
        w2 = g_prob * e2 / (1.0 + e2)
        comb = jnp.where(erow == i1, w1, jnp.where(erow == i2, w2, 0.0))
        head = jnp.where(grow == 0, gi.astype(F32), 0.0)
        route_t = jnp.concatenate(
            [head, comb, jnp.zeros((LANES - SUBLANES - N_EXPERTS, sub), F32)], axis=0)
        route_ref[rs, :] = route_t.T
        routet_ref[:, rs] = head


def _dispatch_kernel(t_ref, route_ref, routet_ref, xs_ref, cws_ref, pos_ref, cnt_ref):
    nb = ROUTE_BLOCK
    gi_row = routet_ref[0:1, :]
    sub = lax.broadcasted_iota(jnp.int32, (SUBLANES, nb), 0).astype(F32)
    onehot = (sub == gi_row)
    ri = lax.broadcasted_iota(jnp.int32, (nb, nb), 0)
    cj = lax.broadcasted_iota(jnp.int32, (nb, nb), 1)
    before = (ri < cj).astype(BF16)
    rank = _dot(onehot.astype(BF16), before)
    counts = jnp.broadcast_to(jnp.sum(onehot.astype(F32), axis=-1, keepdims=True), (SUBLANES, LANES))
    padded = jnp.ceil(counts * (1.0 / ROUTE_TILE)) * ROUTE_TILE
    srow = lax.broadcasted_iota(jnp.int32, (SUBLANES, LANES), 0)
    incl = padded
    s = 1
    while s < SUBLANES:
        incl = incl + jnp.where(srow >= s, pltpu.roll(incl, s, 0), 0.0)
        s *= 2
    start = incl - padded
    pos_row = jnp.sum(jnp.where(onehot, start[:, 0:1] + rank, 0.0), axis=0, keepdims=True)
    cnt_ref[0] = counts.astype(jnp.int32)

    pos_hi = jnp.floor(pos_row * (1.0 / ROUTE_TILE))
    prow = lax.broadcasted_iota(jnp.int32, (2 * SUBLANES, nb), 0)
    pieces = jnp.where(prow == 0, pos_hi, jnp.where(prow == 1, pos_row - ROUTE_TILE * pos_hi, 0.0))
    wrow = lax.broadcasted_iota(jnp.int32, (2 * SUBLANES, LANES), 0)
    weights = jnp.where(wrow == 0, float(ROUTE_TILE), jnp.where(wrow == 1, 1.0, 0.0))
    pos_ref[...] = _dot_tn(pieces.astype(BF16), weights.astype(BF16))

    route = route_ref[...]
    lane = lax.broadcasted_iota(jnp.int32, (nb, LANES), 1)
    comb = jnp.where(lane >= ROUTE_W_LANE, route, 0.0)
    c_hi, c_lo = _split(comb)
    c_lo2 = (comb - c_hi.astype(F32) - c_lo.astype(F32)).astype(BF16)
    d_model = xs_ref.shape[1]
    t_ref[:, d_model:] = (c_hi.astype(F32) + pltpu.roll(c_lo.astype(F32), ROUTE_LO_SHIFT, 1)
                          + pltpu.roll(c_lo2.astype(F32), 2 * ROUTE_LO_SHIFT, 1)).astype(BF16)
    tb = t_ref[...]
    for r in range(TILES_PER_BLOCK):
        rows = (lax.broadcasted_iota(jnp.int32, (ROUTE_TILE, nb), 0) + r * ROUTE_TILE).astype(F32)
        perm = (rows == pos_row).astype(BF16)
        moved = _dot(perm, tb)
        xs_ref[r * ROUTE_TILE:(r + 1) * ROUTE_TILE, :] = moved[:, :d_model].astype(BF16)
        cws_ref[r * ROUTE_TILE:(r + 1) * ROUTE_TILE, :] = moved[:, d_model:]


def _route_dispatch_kernel(x_ref, a_ref, gl_ref, ml_ref, w_ref, g_ref, wr_ref, br_ref,
                           x1_ref, xs_ref, cws_ref, pos_ref, cnt_ref, t_s, route_s, routet_s):
    _out_route_kernel(x_ref, a_ref, gl_ref, ml_ref, w_ref, g_ref, wr_ref, br_ref,
                      x1_ref, t_s, route_s, routet_s)
    _dispatch_kernel(t_s, route_s, routet_s, xs_ref, cws_ref, pos_ref, cnt_ref)


def _route_dispatch(x2, attn, gla_o, ml_o, lp):
    N, D = x2.shape
    nblk = N // ROUTE_BLOCK
    const = lambda i: (0, 0)
    tok = lambda i: (i, 0)
    return pl.pallas_call(
        _route_dispatch_kernel,
        grid=(nblk,),
        in_specs=[
            pl.BlockSpec((ROUTE_BLOCK, D), tok),
            pl.BlockSpec((ROUTE_BLOCK, ATTN_WIDTH), tok),
            pl.BlockSpec((ROUTE_BLOCK, GLA_WIDTH), tok),
            pl.BlockSpec((ROUTE_BLOCK, MLSTM_WIDTH), tok),
            pl.BlockSpec((ATTN_WIDTH + GLA_WIDTH + MLSTM_WIDTH, D), const),
            pl.BlockSpec((1, D), const),
            pl.BlockSpec((D, 2 * LANES), const),
            pl.BlockSpec((1, LANES), const),
        ],
        out_specs=[
            pl.BlockSpec((ROUTE_BLOCK, D), tok),
            pl.BlockSpec((ROUTE_ROWS, D), tok),
            pl.BlockSpec((ROUTE_ROWS, LANES), tok),
            pl.BlockSpec((ROUTE_BLOCK, LANES), tok),
            pl.BlockSpec((1, SUBLANES, LANES), lambda i: (i, 0, 0)),
        ],
        out_shape=[
            jax.ShapeDtypeStruct((N, D), F32),
            jax.ShapeDtypeStruct((nblk * ROUTE_ROWS, D), BF16),
            jax.ShapeDtypeStruct((nblk * ROUTE_ROWS, LANES), F32),
            jax.ShapeDtypeStruct((N, LANES), F32),
            jax.ShapeDtypeStruct((nblk, SUBLANES, LANES), jnp.int32),
        ],
        scratch_shapes=[
            pltpu.VMEM((ROUTE_BLOCK, D + LANES), BF16),
            pltpu.VMEM((ROUTE_BLOCK, LANES), F32),
            pltpu.VMEM((SUBLANES, ROUTE_BLOCK), F32),
        ],
        compiler_params=_params("arbitrary"),
        name="route_dispatch",
    )(x2, attn, gla_o, ml_o, lp["w_out"], lp["g_ffn"], lp["w_route"], lp["b_route"])


def _tile_schedule(cnt):
    nblk = cnt.shape[0]
    ntile = (cnt + ROUTE_TILE - 1) // ROUTE_TILE
    end = jnp.cumsum(ntile, axis=1)
    r = jnp.arange(TILES_PER_BLOCK, dtype=jnp.int32)
    grp = jnp.sum(r[None, :, None] >= end[:, None, :], axis=-1)
    grp = grp.reshape(-1).astype(jnp.int32)
    tile = jnp.arange(nblk * TILES_PER_BLOCK, dtype=jnp.int32)
    order = jnp.argsort(grp * (nblk * TILES_PER_BLOCK) + tile).astype(jnp.int32)
    n_active = jnp.sum(grp < N_GROUPS).astype(jnp.int32)
    g_sorted = grp[order]
    last_group = g_sorted[jnp.maximum(n_active - 1, 0)]
    g_sorted = jnp.where(g_sorted < N_GROUPS, g_sorted, last_group)
    return order, g_sorted, n_active[None]


def _expert_kernel(trow_ref, tgrp_ref, nact_ref, xs_ref, cws_ref, wg_ref, wu_ref, wd_ref, ys_ref,
                   wg_s, wu_s, wd_s):
    i = pl.program_id(0)

    @pl.when((i == 0) | (tgrp_ref[i] != tgrp_ref[jnp.maximum(i - 1, 0)]))
    def _():
        for j in range(EXPERTS_PER_GROUP):
            wg_s[j] = wg_ref[j].astype(BF16)
            wu_s[j] = wu_ref[j].astype(BF16)
            wd_s[j] = wd_ref[j].astype(BF16)

    @pl.when(i < nact_ref[0])
    def _():
        x = xs_ref[...]
        cws = cws_ref[...]
        lane = lax.broadcasted_iota(jnp.int32, cws.shape, 1)
        first = ROUTE_W_LANE + tgrp_ref[i] * EXPERTS_PER_GROUP
        y = jnp.zeros(ys_ref.shape, F32)

        def hidden(j):
            return _dot(x, wg_s[j]), _dot(x, wu_s[j])

        h = hidden(0)
        for j in range(EXPERTS_PER_GROUP):
            h_next = hidden(j + 1) if j + 1 < EXPERTS_PER_GROUP else None
            off = lane - (first + j)
            sel = (off == 0) | (off == ROUTE_LO_SHIFT) | (off == 2 * ROUTE_LO_SHIFT)
            wj = jnp.sum(jnp.where(sel, cws, 0.0), axis=-1, keepdims=True)
            a = (h[0] * _sigmoid(h[0]) * h[1]).astype(BF16)
            y = y + wj * _dot(a, wd_s[j])
            h = h_next
        ys_ref[...] = y.astype(BF16)

    @pl.when(i >= nact_ref[0])
    def _():
        ys_ref[...] = jnp.zeros(ys_ref.shape, BF16)


def _experts(xs, cws, order, grp, n_active, lp):
    rows, D = xs.shape
    n_tiles = rows // ROUTE_TILE
    tile = lambda i, trow, tgrp, nact: (trow[i], 0)
    layer_groups = lp["layer"] * N_GROUPS
    wsel = lambda i, trow, tgrp, nact: (layer_groups + tgrp[i], 0, 0)
    once = pl.Buffered(1)
    return pl.pallas_call(
        _expert_kernel,
        grid_spec=pltpu.PrefetchScalarGridSpec(
            num_scalar_prefetch=3,
            grid=(n_tiles,),
            in_specs=[
                pl.BlockSpec((ROUTE_TILE, D), tile),
                pl.BlockSpec((ROUTE_TILE, LANES), tile),
                pl.BlockSpec((EXPERTS_PER_GROUP, D, D_EXPERT), wsel, pipeline_mode=once),
                pl.BlockSpec((EXPERTS_PER_GROUP, D, D_EXPERT), wsel, pipeline_mode=once),
                pl.BlockSpec((EXPERTS_PER_GROUP, D_EXPERT, D), wsel, pipeline_mode=once),
            ],
            out_specs=pl.BlockSpec((ROUTE_TILE, D), tile),
            scratch_shapes=[
                pltpu.VMEM((EXPERTS_PER_GROUP, D, D_EXPERT), BF16),
                pltpu.VMEM((EXPERTS_PER_GROUP, D, D_EXPERT), BF16),
                pltpu.VMEM((EXPERTS_PER_GROUP, D_EXPERT, D), BF16),
            ],
        ),
        out_shape=jax.ShapeDtypeStruct((rows, D), BF16),
        compiler_params=_params("arbitrary"),
        name="experts",
    )(order, grp, n_active, xs, cws, lp["w_gate"], lp["w_up"], lp["w_down"])


def _combine_kernel(ys_ref, pos_ref, x1_ref, p_ref, g_ref, wpg_ref, wpp_ref, gfin_ref, o_ref,
                    *, final):
    tm = x1_ref.shape[0]
    pos = pos_ref[...]
    lane = lax.broadcasted_iota(jnp.int32, (tm, LANES), 1).astype(F32)
    perm_t = jnp.concatenate(
        [(lane + r * LANES == pos).astype(BF16) for r in range(ROUTE_ROWS // LANES)], axis=1)
    x = x1_ref[...] + _dot(perm_t, ys_ref[...])
    gate = _sigmoid(_dot(_rms(x, g_ref[...]).astype(BF16), wpg_ref[...]))
    x = x + gate * _dot(p_ref[...].astype(BF16), wpp_ref[...])
    if final:
        x = _rms(x, gfin_ref[...])
    o_ref[...] = x


def _combine(ys, pos, x1, p3, lp, g_final, final=False, tm=1024):
    N, D = x1.shape
    inner = ROUTE_BLOCK // tm
    layer = lp["layer"]
    tok = lambda b, i: (b * inner + i, 0)
    const = lambda b, i: (0, 0)
    return pl.pallas_call(
        functools.partial(_combine_kernel, final=final),
        grid=(N // ROUTE_BLOCK, inner),
        in_specs=[
            pl.BlockSpec((ROUTE_ROWS, D), lambda b, i: (b, 0)),
            pl.BlockSpec((tm, LANES), tok),
            pl.BlockSpec((tm, D), tok),
            pl.BlockSpec((None, tm, p3.shape[2]), lambda b, i: (layer, b * inner + i, 0)),
            pl.BlockSpec((1, D), const),
            pl.BlockSpec((D, D), const),
            pl.BlockSpec((p3.shape[2], D), const),
            pl.BlockSpec((1, D), const),
        ],
        out_specs=pl.BlockSpec((tm, D), tok),
        out_shape=jax.ShapeDtypeStruct((N, D), F32),
        compiler_params=_params("arbitrary", "arbitrary"),
        name="combine",
    )(ys, pos, x1, p3, lp["g_ple"], lp["w_pg"], lp["w_pp"], g_final)


def _mix_out_moe(x2, attn, gla_o, ml_o, lp):
    x1, xs, cws, pos, cnt = _route_dispatch(x2, attn, gla_o, ml_o, lp)
    order, grp, n_active = _tile_schedule(cnt[:, :N_GROUPS, 0])
    return x1, _experts(xs, cws, order, grp, n_active, lp), pos


def _permute_in_cols(w):
    glr0 = ATTN_WIDTH + 2 * KV_WIDTH + 4 * GLA_WIDTH
    ml0 = glr0 + 2 * GLA_RANK
    mg0 = ml0 + 4 * MLSTM_WIDTH
    end = mg0 + 4 * MLSTM_HEADS
    assert end == w.shape[-1]
    pad = jnp.zeros(w.shape[:-1] + (IN_PERM_WIDTH - end,), BF16)
    parts = [w[..., :glr0], w[..., ml0:mg0], w[..., glr0:ml0], w[..., mg0:end]]
    return jnp.concatenate([part.astype(BF16) for part in parts] + [pad], axis=-1)


def _rope_tables(T):
    t = np.arange(T)
    inv = ROPE_THETA ** (-np.arange(0, ROPE_AXIS_DIM, 2, dtype=np.float64) / ROPE_AXIS_DIM)
    ang_r = (t // GRID_W)[None, :] * inv[:, None]
    ang_c = (t % GRID_W)[None, :] * inv[:, None]
    cos_h = np.concatenate([np.cos(ang_r), np.cos(ang_r), np.cos(ang_c), np.cos(ang_c)], axis=0)
    sin_h = np.concatenate([-np.sin(ang_r), np.sin(ang_r), -np.sin(ang_c), np.sin(ang_c)], axis=0)
    reps = LANES // HEAD_DIM
    return dict(cos_t=jnp.asarray(np.tile(cos_h, (reps, 1)), F32),
                sin_t=jnp.asarray(np.tile(sin_h, (reps, 1)), F32))


def kernel(x, p, norm_mix_g, w_in, attn_q_norm_g, attn_k_norm_g, gla_w_decay, gla_b_decay,
           gla_out_norm_g, mlstm_conv_w, mlstm_conv_b, mlstm_b_input, mlstm_b_forget,
           mlstm_out_norm_g, w_out, norm_ffn_g, w_group, b_group, w_router, b_router,
           w_expert_gate, w_expert_up, w_expert_down, norm_ple_g, w_ple_gate, w_ple_proj,
           final_norm_g):
    params = dict(
        norm_mix_g=norm_mix_g, w_in=w_in, attn_q_norm_g=attn_q_norm_g, attn_k_norm_g=attn_k_norm_g,
        gla_w_decay=gla_w_decay, gla_b_decay=gla_b_decay, gla_out_norm_g=gla_out_norm_g,
        mlstm_conv_w=mlstm_conv_w, mlstm_conv_b=mlstm_conv_b, mlstm_b_input=mlstm_b_input,
        mlstm_b_forget=mlstm_b_forget, mlstm_out_norm_g=mlstm_out_norm_g, w_out=w_out,
        norm_ffn_g=norm_ffn_g, w_group=w_group, b_group=b_group, w_router=w_router,
        b_router=b_router, w_expert_gate=w_expert_gate, w_expert_up=w_expert_up,
        w_expert_down=w_expert_down, norm_ple_g=norm_ple_g, w_ple_gate=w_ple_gate,
        w_ple_proj=w_ple_proj)
    B, T, D = x.shape
    rope = _rope_tables(T)
    N = B * T
    depth = w_in.shape[0]
    x2 = x.reshape(N, D)
    g_final = final_norm_g[None, :]
    p3 = p.reshape(depth, N, p.shape[-1])
    for i in range(depth):
        lp = _layer_params(params, i)
        qt, k, vt, gla, ml, small = _in_proj(x2, lp, rope, B, T)
        attn = _attention(lp["attn_safe"], qt, k, vt).reshape(N, ATTN_WIDTH)
        gla_o = _gla(gla, small, lp).reshape(N, GLA_WIDTH)
        ml_o = _mlstm(ml, small, lp).reshape(N, MLSTM_WIDTH)
        x1, ys, pos = _mix_out_moe(x2, attn, gla_o, ml_o, lp)
        x2 = _combine(ys, pos, x1, p3, lp, g_final, final=(i == depth - 1))
    return x2.reshape(B, T, D)


def _split_w(w):
    hi = w.astype(BF16)
    return hi, (w - hi.astype(F32)).astype(BF16)


def _stacked_experts(w):
    return w.reshape((w.shape[0] * w.shape[1],) + w.shape[2:])


def _layer_params(p, i):
    D = p["w_in"].shape[1]
    gq, gk = p["attn_q_norm_g"][i], p["attn_k_norm_g"][i]
    q_gain = jnp.tile(gq, LANES // HEAD_DIM) * (HEAD_DIM ** -0.5 * LOG2E)
    logit_bound = HEAD_DIM ** 0.5 * jnp.max(jnp.abs(gq)) * jnp.max(jnp.abs(gk))
    attn_safe = (logit_bound <= ATTN_SAFE_LOGIT).astype(jnp.int32)[None]
    wd = jnp.zeros((2, LANES, GLA_WIDTH), F32)
    wd = wd.at[0, :GLA_RANK].set(p["gla_w_decay"][i, 0]).at[1, GLA_RANK:2 * GLA_RANK].set(p["gla_w_decay"][i, 1])
    wd_hi, wd_lo = _split_w(wd)
    gate_bias = jnp.zeros((LANES,), F32).at[SMALL_GATE_LANE:SMALL_GATE_LANE + 4 * MLSTM_HEADS].set(
        jnp.concatenate([p["mlstm_b_input"][i, 0], p["mlstm_b_forget"][i, 0],
                         p["mlstm_b_input"][i, 1], p["mlstm_b_forget"][i, 1]]))
    w_route = jnp.zeros((D, LANES), F32)
    w_route = w_route.at[:, :N_GROUPS].set(p["w_group"][i])
    w_route = w_route.at[:, ROUTE_W_LANE:ROUTE_W_LANE + N_EXPERTS].set(p["w_router"][i])
    wr_hi, wr_lo = _split_w(w_route)
    b_route = jnp.zeros((LANES,), F32).at[:N_GROUPS].set(p["b_group"][i])
    b_route = b_route.at[ROUTE_W_LANE:ROUTE_W_LANE + N_EXPERTS].set(p["b_router"][i])
    return dict(
        g_mix=p["norm_mix_g"][i][None, :],
        w_in=_permute_in_cols(p["w_in"]),
        q_gain_t=jnp.broadcast_to(q_gain[:, None], (LANES, LANES)),
        k_gain_t=jnp.broadcast_to(jnp.tile(gk, LANES // HEAD_DIM)[:, None], (LANES, LANES)),
        attn_safe=attn_safe,
        wd_hi=wd_hi, wd_lo=wd_lo,
        bd=p["gla_b_decay"][i][:, None, :],
        gla_gain=jnp.tile(p["gla_out_norm_g"][i], LANES // HEAD_DIM)[None, :],
        conv_w=p["mlstm_conv_w"][i],
        conv_b=p["mlstm_conv_b"][i][None, :],
        gate_bias=gate_bias[None, :],
        ml_gain=jnp.tile(p["mlstm_out_norm_g"][i], LANES // HEAD_DIM)[None, :],
        w_out=p["w_out"][i].astype(BF16),
        g_ffn=p["norm_ffn_g"][i][None, :],
        w_route=jnp.concatenate([wr_hi, wr_lo], axis=1), b_route=b_route[None, :],
        layer=i,
        w_gate=_stacked_experts(p["w_expert_gate"]),
        w_up=_stacked_experts(p["w_expert_up"]),
        w_down=_stacked_experts(p["w_expert_down"]),
        g_ple=p["norm_ple_g"][i][None, :],
        w_pg=p["w_ple_gate"][i].astype(BF16),
        w_pp=p["w_ple_proj"][i].astype(BF16),
    )
```

```python
import functools

import jax
import jax.numpy as jnp
import numpy as np
from jax import lax
from jax.experimental import pallas as pl
from jax.experimental.pallas import tpu as pltpu

F32 = jnp.float32
BF16 = jnp.bfloat16

GRID_W = 64
HEAD_DIM = 64
ATTN_HEADS = 8
ATTN_KV_HEADS = 2
GLA_HEADS = 4
MLSTM_HEADS = 4
ATTN_WIDTH = ATTN_HEADS * HEAD_DIM
KV_WIDTH = ATTN_KV_HEADS * HEAD_DIM
GLA_WIDTH = GLA_HEADS * HEAD_DIM
MLSTM_WIDTH = MLSTM_HEADS * HEAD_DIM
GLA_RANK = 16
GLA_TAU = 16.0
CHUNK = 64
ROPE_THETA = 10000.0
ROPE_AXIS_DIM = HEAD_DIM // 2
N_GROUPS = 4
EXPERTS_PER_GROUP = 4
N_EXPERTS = N_GROUPS * EXPERTS_PER_GROUP
D_EXPERT = 512
EPS = 1e-6
NEG = -1e30

LANES = 128
SUBLANES = 8
VMEM_LIMIT_BYTES = 56 * 1024 * 1024

QK_WIDTH = ATTN_WIDTH + KV_WIDTH
OFF_V = QK_WIDTH
OFF_GLA = OFF_V + KV_WIDTH
OFF_ML = OFF_GLA + 4 * GLA_WIDTH
OFF_SMALL = OFF_ML + 4 * MLSTM_WIDTH
IN_PERM_WIDTH = OFF_SMALL + LANES
SMALL_GATE_LANE = 2 * GLA_RANK

ROUTE_BLOCK = 1024
ROUTE_TILE = 128
TILES_PER_BLOCK = (ROUTE_BLOCK + N_GROUPS * (ROUTE_TILE - 1)) // ROUTE_TILE
ROUTE_ROWS = TILES_PER_BLOCK * ROUTE_TILE
ROUTE_W_LANE = 8
ROUTE_LO_SHIFT = 32
ROUTE_SUBTILE = 256


def _dot(a, b):
    return jnp.dot(a, b, preferred_element_type=F32)


def _dot_nt(a, b):
    return lax.dot_general(a, b, (((1,), (1,)), ((), ())), preferred_element_type=F32)


def _dot_tn(a, b):
    return lax.dot_general(a, b, (((0,), (0,)), ((), ())), preferred_element_type=F32)


def _split(a):
    hi = a.astype(BF16)
    lo = (a - hi.astype(F32)).astype(BF16)
    return hi, lo


def _dot3(a, w_hi, w_lo):
    a_hi, a_lo = _split(a)
    return _dot(a_hi, w_hi) + _dot(a_lo, w_hi) + _dot(a_hi, w_lo)


def _log_sigmoid(x):
    return jnp.minimum(x, 0.0) - jnp.log(1.0 + jnp.exp(-jnp.abs(x)))


def _sigmoid(x):
    return 1.0 / (1.0 + jnp.exp(-x))


def _rms(x, g):
    return x * lax.rsqrt(jnp.mean(x * x, axis=-1, keepdims=True) + EPS) * g


def _params(*semantics):
    return pltpu.CompilerParams(dimension_semantics=semantics, vmem_limit_bytes=VMEM_LIMIT_BYTES)


def _in_proj_kernel(x_ref, g_ref, w_ref, cost_ref, sint_ref, gq_ref, gk_ref,
                    qt_ref, k_ref, vt_ref, gla_ref, ml_ref, small_ref):
    tm = x_ref.shape[0]
    h = _rms(x_ref[...], g_ref[...])
    z = _dot(h.astype(BF16), w_ref[...])

    heads = LANES // HEAD_DIM
    half = ROPE_AXIS_DIM // 2
    gq = jnp.concatenate([gq_ref[...]] * (tm // LANES), axis=1)
    cost = cost_ref[...]
    sint = sint_ref[...]
    gk = jnp.concatenate([gk_ref[...]] * (tm // LANES), axis=1)

    def norm_rope_t(c, gain):
        zt = z[:, c * LANES:(c + 1) * LANES].T
        z3 = zt.reshape(heads, HEAD_DIM, tm)
        inv = lax.rsqrt(jnp.mean(z3 * z3, axis=1, keepdims=True) + EPS)
        y = (z3 * inv).reshape(LANES, tm) * gain
        partner = jnp.concatenate(
            [y[(r ^ 1) * half:((r ^ 1) + 1) * half, :] for r in range(LANES // half)], axis=0)
        return y * cost + partner * sint

    for c in range(ATTN_WIDTH // LANES):
        qt_ref[0, c * LANES:(c + 1) * LANES, :] = norm_rope_t(c, gq).astype(BF16)
    k_ref[0] = norm_rope_t(ATTN_WIDTH // LANES, gk).T.astype(BF16)
    vt_ref[0] = z[:, OFF_V:OFF_V + KV_WIDTH].T.astype(BF16)
    gla_ref[0] = z[:, OFF_GLA:OFF_ML]
    ml_ref[0] = z[:, OFF_ML:OFF_SMALL]
    small_ref[0] = z[:, OFF_SMALL:IN_PERM_WIDTH]


def _in_proj(x2, lp, rope, B, T, tm=1024):
    N, D = x2.shape
    tpb = T // tm
    const = lambda i: (0, 0)
    tok3 = lambda i: (i // tpb, i % tpb, 0)
    tokT = lambda i: (i // tpb, 0, i % tpb)
    layer = lp["layer"]
    return pl.pallas_call(
        _in_proj_kernel,
        grid=(N // tm,),
        in_specs=[
            pl.BlockSpec((tm, D), lambda i: (i, 0)),
            pl.BlockSpec((1, D), const),
            pl.BlockSpec((None, D, IN_PERM_WIDTH), lambda i: (layer, 0, 0)),
            pl.BlockSpec((LANES, tm), lambda i: (0, i % tpb)),
            pl.BlockSpec((LANES, tm), lambda i: (0, i % tpb)),
            pl.BlockSpec((LANES, LANES), const),
            pl.BlockSpec((LANES, LANES), const),
        ],
        out_specs=[
            pl.BlockSpec((1, ATTN_WIDTH, tm), tokT),
            pl.BlockSpec((1, tm, KV_WIDTH), tok3),
            pl.BlockSpec((1, KV_WIDTH, tm), tokT),
            pl.BlockSpec((1, tm, 4 * GLA_WIDTH), tok3),
            pl.BlockSpec((1, tm, 4 * MLSTM_WIDTH), tok3),
            pl.BlockSpec((1, tm, LANES), tok3),
        ],
        out_shape=[
            jax.ShapeDtypeStruct((B, ATTN_WIDTH, T), BF16),
            jax.ShapeDtypeStruct((B, T, KV_WIDTH), BF16),
            jax.ShapeDtypeStruct((B, KV_WIDTH, T), BF16),
            jax.ShapeDtypeStruct((B, T, 4 * GLA_WIDTH), F32),
            jax.ShapeDtypeStruct((B, T, 4 * MLSTM_WIDTH), F32),
            jax.ShapeDtypeStruct((B, T, LANES), F32),
        ],
        compiler_params=_params("arbitrary"),
        name="in_proj",
    )(x2, lp["g_mix"], lp["w_in"], rope["cos_t"], rope["sin_t"], lp["q_gain_t"], lp["k_gain_t"])


ATTN_SAFE_LOGIT = 40.0
LOG2E = 1.4426950408889634


def _attn_kernel(safe_ref, qt_ref, k_ref, vt_ref, o_ref, *, tk):
    tq = qt_ref.shape[2]
    T = k_ref.shape[1]
    G = ATTN_HEADS // ATTN_KV_HEADS
    n = G * tq
    zeros = jnp.zeros((HEAD_DIM, n), BF16)

    def q_operand(j):
        base = j * G * HEAD_DIM
        qs = jnp.concatenate(
            [qt_ref[0, base + h * HEAD_DIM:base + (h + 1) * HEAD_DIM, :] for h in range(G)], axis=1)
        return jnp.concatenate([qs, zeros] if j == 0 else [zeros, qs], axis=0)

    def finish(j, acc, l):
        base = j * G * HEAD_DIM
        o = acc * (1.0 / l)
        ot = jnp.concatenate([o[:, h * tq:(h + 1) * tq] for h in range(G)], axis=0)
        o_ref[0, :, base:base + G * HEAD_DIM] = ot.T.astype(BF16)

    @pl.when(safe_ref[0] == 1)
    def _():
        qps = [q_operand(j) for j in range(ATTN_KV_HEADS)]
        units = [(c, j) for c in range(T // tk) for j in range(ATTN_KV_HEADS)]

        def scores(u):
            c, j = units[u]
            return _dot(k_ref[0, c * tk:(c + 1) * tk, :], qps[j])

        l8 = [jnp.zeros((SUBLANES, n), F32)] * ATTN_KV_HEADS
        acc = [jnp.zeros((HEAD_DIM, n), F32)] * ATTN_KV_HEADS
        st = scores(0)
        for u, (c, j) in enumerate(units):
            st_next = scores(u + 1) if u + 1 < len(units) else None
            p = jnp.exp2(st)
            l8[j] = l8[j] + jnp.sum(p.reshape(tk // SUBLANES, SUBLANES, n), axis=0)
            vc = vt_ref[0, j * HEAD_DIM:(j + 1) * HEAD_DIM, c * tk:(c + 1) * tk]
            acc[j] = acc[j] + _dot(vc, p.astype(BF16))
            st = st_next
        for j in range(ATTN_KV_HEADS):
            finish(j, acc[j], jnp.sum(l8[j], axis=0, keepdims=True))

    @pl.when(safe_ref[0] == 0)
    def _():
        for j in range(ATTN_KV_HEADS):
            qp = q_operand(j)

            def body(c, carry, qp=qp, j=j):
                m, l, acc = carry
                off = pl.multiple_of(c * tk, tk)
                st = _dot(k_ref[0, pl.ds(off, tk), :], qp)
                m_new = jnp.maximum(m, jnp.max(st, axis=0, keepdims=True))
                alpha = jnp.exp2(m - m_new)
                p = jnp.exp2(st - m_new)
                l = alpha * l + jnp.sum(p, axis=0, keepdims=True)
                vc = vt_ref[0, j * HEAD_DIM:(j + 1) * HEAD_DIM, pl.ds(off, tk)]
                return m_new, l, alpha * acc + _dot(vc, p.astype(BF16))

            init = (jnp.full((1, n), NEG, F32), jnp.zeros((1, n), F32), jnp.zeros((HEAD_DIM, n), F32))
            _, l, acc = lax.fori_loop(0, T // tk, body, init)
            finish(j, acc, l)


def _attention(safe, qt, k, vt, tq=256, tk=128):
    B, _, T = qt.shape
    tk = min(tk, T)
    return pl.pallas_call(
        functools.partial(_attn_kernel, tk=tk),
        grid_spec=pltpu.PrefetchScalarGridSpec(
            num_scalar_prefetch=1,
            grid=(B, T // tq),
            in_specs=[
                pl.BlockSpec((1, ATTN_WIDTH, tq), lambda b, i, s: (b, 0, i)),
                pl.BlockSpec((1, T, KV_WIDTH), lambda b, i, s: (b, 0, 0)),
                pl.BlockSpec((1, KV_WIDTH, T), lambda b, i, s: (b, 0, 0)),
            ],
            out_specs=pl.BlockSpec((1, tq, ATTN_WIDTH), lambda b, i, s: (b, i, 0)),
        ),
        out_shape=jax.ShapeDtypeStruct((B, T, ATTN_WIDTH), BF16),
        compiler_params=_params("arbitrary", "arbitrary"),
        name="attention",
    )(safe, qt, k, vt)


def _chunk_scan(x, pos, op, fill, reverse):
    rows = x.shape[0]
    s = 1
    while s < CHUNK:
        if reverse:
            shifted = jnp.where(pos < CHUNK - s, pltpu.roll(x, rows - s, 0), fill)
        else:
            shifted = jnp.where(pos >= s, pltpu.roll(x, s, 0), fill)
        x = op(x, shifted)
        s *= 2
    return x


def _chunk_pos(rows):
    return lax.broadcasted_iota(jnp.int32, (rows, LANES), 0) % CHUNK


def _chunk_row(a, reverse_dir, idx_fwd, idx_bwd):
    rows = a.shape[0]
    a3 = a.reshape(rows // CHUNK, CHUNK, LANES)
    i = idx_bwd if reverse_dir else idx_fwd
    return jnp.broadcast_to(a3[:, i:i + 1, :], a3.shape).reshape(rows, LANES)


def _stack_heads(x):
    lane = lax.broadcasted_iota(jnp.int32, x.shape, 1)
    zero = jnp.zeros_like(x)
    return jnp.concatenate([jnp.where(lane < HEAD_DIM, x, zero), jnp.where(lane >= HEAD_DIM, x, zero)], axis=0)


def _select_heads(x):
    c = x.shape[0] // 2
    lane = lax.broadcasted_iota(jnp.int32, (c, x.shape[1]), 1)
    return jnp.where(lane < HEAD_DIM, x[:c], x[c:])


def _stacked_causal_masks():
    ci = lax.broadcasted_iota(jnp.int32, (2 * CHUNK, CHUNK), 0) % CHUNK
    si = lax.broadcasted_iota(jnp.int32, (2 * CHUNK, CHUNK), 1)
    return ci >= si, ci <= si


def _pair_blockdiag(width):
    r = lax.broadcasted_iota(jnp.int32, (LANES, width), 0) // HEAD_DIM
    c = lax.broadcasted_iota(jnp.int32, (LANES, width), 1) // (width // 2)
    return r == c


def _head_rms(o, gain):
    lane = lax.broadcasted_iota(jnp.int32, o.shape, 1)
    lo = lane < HEAD_DIM
    sq = o * o
    s_lo = jnp.sum(jnp.where(lo, sq, 0.0), axis=-1, keepdims=True)
    s_hi = jnp.sum(jnp.where(lo, 0.0, sq), axis=-1, keepdims=True)
    ms = jnp.where(lo, s_lo, s_hi) * (1.0 / HEAD_DIM)
    return o * lax.rsqrt(ms + EPS) * gain


PREP_ROWS = 1024
GLA_CHUNKS_PER_STEP = 32


def _gla_kernel(q_ref, k_ref, v_ref, g_ref, small_ref, wdh_ref, wdl_ref, bd_ref, gain_ref,
                o_ref, qe_s, ke_s, dec_s, of_s, ob_s):
    T = q_ref.shape[1]
    nc = T // CHUNK
    R = min(PREP_ROWS, T)
    cpt = R // CHUNK
    pos = _chunk_pos(R)
    w_hi = jnp.concatenate([wdh_ref[0], wdh_ref[1]], axis=1)
    w_lo = jnp.concatenate([wdl_ref[0], wdl_ref[1]], axis=1)
    bias = jnp.concatenate([bd_ref[0], bd_ref[1]], axis=1)

    def prep(t, _):
        r0 = pl.multiple_of(t * R, R)
        q = q_ref[0, pl.ds(r0, R), :] * HEAD_DIM ** -0.5
        k = k_ref[0, pl.ds(r0, R), :]
        la2 = _log_sigmoid(_dot3(small_ref[0, pl.ds(r0, R), :], w_hi, w_lo) + bias) * (1.0 / GLA_TAU)
        for d in range(2):
            b = _chunk_scan(la2[:, d * LANES:(d + 1) * LANES], pos, jnp.add, 0.0, reverse=bool(d))
            b_mid = _chunk_row(b, d, CHUNK // 2 - 1, CHUNK // 2)
            b_last = _chunk_row(b, d, CHUNK - 1, 0)
            qe_s[d, pl.ds(r0, R), :] = (q * jnp.exp(b - b_mid)).astype(BF16)
            ke_s[d, pl.ds(r0, R), :] = (k * jnp.exp(b_mid - b)).astype(BF16)
            c8 = pl.ds(pl.multiple_of(t * cpt * SUBLANES, SUBLANES), cpt * SUBLANES)
            for kind, val in enumerate((b_last, b_last - b_mid, b_mid)):
                rows8 = val.reshape(cpt, CHUNK, LANES)[:, :SUBLANES, :].reshape(cpt * SUBLANES, LANES)
                dec_s[d, kind, c8, :] = jnp.exp(rows8)
        return 0

    lax.fori_loop(0, T // R, prep, 0)

    outs = (of_s, ob_s)
    masks = _stacked_causal_masks()
    blockdiag = _pair_blockdiag(LANES)
    G = min(GLA_CHUNKS_PER_STEP, nc)
    assert nc % G == 0

    def step(it, states):
        units = [(d, g) for d in range(2) for g in range(G)]
        ops = {}
        for d, g in units:
            c = it * G + g
            c = (nc - 1 - c) if d else c
            r0 = pl.multiple_of(c * CHUNK, CHUNK)
            c8 = pl.ds(pl.multiple_of(c * SUBLANES, SUBLANES), SUBLANES)
            ops[d, g] = dict(
                r0=r0,
                qe=qe_s[d, pl.ds(r0, CHUNK), :], ke=ke_s[d, pl.ds(r0, CHUNK), :],
                v=v_ref[0, pl.ds(r0, CHUNK), :].astype(BF16),
                dec=dec_s[d, 0, c8, :][0:1, :], to_end=dec_s[d, 1, c8, :][0:1, :],
                from_start=dec_s[d, 2, c8, :][0:1, :])
        a2, kv = {}, {}
        for u in units:
            o = ops[u]
            a2[u] = _dot_nt(_stack_heads(o["qe"]), o["ke"])
            kv[u] = _dot_tn(o["v"], o["ke"]) * o["to_end"]
        st_in = {}
        new_states = []
        for d in range(2):
            st = states[d]
            for g in range(G):
                st_in[d, g] = (st * ops[d, g]["from_start"]).astype(BF16)
                st = st * ops[d, g]["dec"] + jnp.where(blockdiag, kv[d, g], 0.0)
            new_states.append(st)
        for u in units:
            o = ops[u]
            a = jnp.where(masks[u[0]], a2[u], 0.0).astype(BF16)
            intra = _select_heads(_dot(a, o["v"]))
            outs[u[0]][pl.ds(o["r0"], CHUNK), :] = intra + _dot_nt(o["qe"], st_in[u])
        return tuple(new_states)

    zero = jnp.zeros((LANES, LANES), F32)
    lax.fori_loop(0, nc // G, step, (zero, zero))

    def post(t, _):
        r0 = pl.multiple_of(t * R, R)
        o = _head_rms(of_s[pl.ds(r0, R), :] + ob_s[pl.ds(r0, R), :], gain_ref[...])
        g = g_ref[0, pl.ds(r0, R), :]
        o_ref[0, pl.ds(r0, R), :] = (o * (g * _sigmoid(g))).astype(BF16)
        return 0

    lax.fori_loop(0, T // R, post, 0)


def _gla(gla, small, lp):
    B, T, _ = gla.shape
    pairs = GLA_WIDTH // LANES
    col = lambda off: (lambda b, hp: (b, 0, off * pairs + hp))
    return pl.pallas_call(
        _gla_kernel,
        grid=(B, pairs),
        in_specs=[
            pl.BlockSpec((1, T, LANES), col(0)),
            pl.BlockSpec((1, T, LANES), col(1)),
            pl.BlockSpec((1, T, LANES), col(2)),
            pl.BlockSpec((1, T, LANES), col(3)),
            pl.BlockSpec((1, T, LANES), lambda b, hp: (b, 0, 0)),
            pl.BlockSpec((2, LANES, LANES), lambda b, hp: (0, 0, hp)),
            pl.BlockSpec((2, LANES, LANES), lambda b, hp: (0, 0, hp)),
            pl.BlockSpec((2, 1, LANES), lambda b, hp: (0, 0, hp)),
            pl.BlockSpec((1, LANES), lambda b, hp: (0, 0)),
        ],
        out_specs=pl.BlockSpec((1, T, LANES), lambda b, hp: (b, 0, hp)),
        out_shape=jax.ShapeDtypeStruct((B, T, GLA_WIDTH), BF16),
        scratch_shapes=[
            pltpu.VMEM((2, T, LANES), BF16), pltpu.VMEM((2, T, LANES), BF16),
            pltpu.VMEM((2, 3, (T // CHUNK) * SUBLANES, LANES), F32),
            pltpu.VMEM((T, LANES), F32), pltpu.VMEM((T, LANES), F32),
        ],
        compiler_params=_params("arbitrary", "arbitrary"),
        name="gla",
    )(gla, gla, gla, gla, small, lp["wd_hi"], lp["wd_lo"], lp["bd"], lp["gla_gain"])


GATE_I = SMALL_GATE_LANE
GATE_F = SMALL_GATE_LANE + MLSTM_HEADS


def _gate_lane(d, hh):
    return GATE_F + 2 * MLSTM_HEADS * d + hh


MLSTM_CHUNKS_PER_STEP = 32


def _chunk_rows8(a, row):
    n = a.shape[0] // CHUNK
    a3 = a.reshape(n, CHUNK, LANES)
    full = jnp.broadcast_to(a3[:, row:row + 1, :], a3.shape)
    return full[:, :SUBLANES, :].reshape(n * SUBLANES, LANES)


def _expand_rows8(a8):
    n, w = a8.shape[0] // SUBLANES, a8.shape[1]
    a3 = a8.reshape(n, SUBLANES, w)[:, 0:1, :]
    return jnp.broadcast_to(a3, (n, CHUNK, w)).reshape(n * CHUNK, w)


def _gate_select():
    src = lax.broadcasted_iota(jnp.int32, (LANES, 2 * LANES), 0)
    dst = lax.broadcasted_iota(jnp.int32, (LANES, 2 * LANES), 1)
    want = GATE_F + 2 * MLSTM_HEADS * (dst // LANES) + (dst % LANES) // HEAD_DIM
    return jnp.where(src == want, 1.0, 0.0).astype(BF16)


def _gate_broadcast(x, sel, gate_mask, pieces=2):
    x = jnp.where(gate_mask, x, 0.0)
    if pieces == 1:
        return _dot(x.astype(BF16), sel)
    hi, lo = _split(x)
    return _dot(hi, sel) + _dot(lo, sel)


def _mlstm_kernel(q_ref, k_ref, v_ref, og_ref, small_ref, wq_ref, wk_ref, bq_ref, bk_ref,
                  gbias_ref, gain_ref, o_ref, q_s, k_s, b_s, r_s, cm_s, fl1_s, bl_s, rl_s, mf_s, mb_s,
                  wp_s, qi_s, kw_s, rt_s, st_s, of_s, ob_s):
    T = q_ref.shape[1]
    nc = T // CHUNK
    R = min(PREP_ROWS, T)
    hp = pl.program_id(1)
    G = min(MLSTM_CHUNKS_PER_STEP, nc)
    assert nc % G == 0
    cpt = R // CHUNK

    row = lax.broadcasted_iota(jnp.int32, (T, LANES), 0)
    for src, w_ref, b_ref, dst, scale in ((q_ref, wq_ref, bq_ref, q_s, 1.0),
                                          (k_ref, wk_ref, bk_ref, k_s, HEAD_DIM ** -0.5)):
        xc = src[0]
        prev = jnp.where(row >= 1, pltpu.roll(xc, 1, 0), 0.0)
        nxt = jnp.where(row < T - 1, pltpu.roll(xc, T - 1, 0), 0.0)
        y = prev * w_ref[0:1, :] + xc * w_ref[1:2, :] + nxt * w_ref[2:3, :] + b_ref[...]
        dst[...] = (y * _sigmoid(y) * scale).astype(BF16)

    pos = _chunk_pos(R)
    lane = lax.broadcasted_iota(jnp.int32, (R, LANES), 1)
    is_bwd = lane >= GATE_I + 2 * MLSTM_HEADS
    is_bwd8 = lax.broadcasted_iota(jnp.int32, (cpt * SUBLANES, LANES), 1) >= GATE_I + 2 * MLSTM_HEADS
    heads_per_pair = LANES // HEAD_DIM
    shift = (LANES - heads_per_pair * hp) % LANES
    gate_lanes = [_gate_lane(d, hh) for d in range(2) for hh in range(2)]
    gate_mask = functools.reduce(jnp.logical_or, [lane == l for l in gate_lanes])
    lane8 = lax.broadcasted_iota(jnp.int32, (cpt * SUBLANES, LANES), 1)
    gate_mask8 = functools.reduce(jnp.logical_or, [lane8 == l for l in gate_lanes])

    def prep(t, _):
        r0 = pl.multiple_of(t * R, R)
        gc = pltpu.roll(small_ref[0, pl.ds(r0, R), :] + gbias_ref[...], shift, 1)
        logf = _log_sigmoid(gc)
        pre = _chunk_scan(logf, pos, jnp.add, 0.0, False)
        b = jnp.where(is_bwd, _chunk_row(pre, 0, CHUNK - 1, CHUNK - 1) - pre + logf, pre)
        r = pltpu.roll(gc, MLSTM_HEADS, 1) - b
        cm = jnp.where(is_bwd, _chunk_scan(r, pos, jnp.maximum, NEG, True),
                       _chunk_scan(r, pos, jnp.maximum, NEG, False))
        b_s[pl.ds(r0, R), :] = b
        r_s[pl.ds(r0, R), :] = r
        cm_s[pl.ds(r0, R), :] = cm
        rt_s[:, pl.ds(r0, R)] = r.T
        c8 = pl.ds(pl.multiple_of(t * cpt * SUBLANES, SUBLANES), cpt * SUBLANES)
        bl_s[c8, :] = jnp.where(is_bwd8, _chunk_rows8(b, 0), _chunk_rows8(b, CHUNK - 1))
        rl_s[c8, :] = jnp.where(is_bwd8, _chunk_rows8(cm, 0), _chunk_rows8(cm, CHUNK - 1))
        return 0

    lax.fori_loop(0, T // R, prep, 0)

    def m_chain(n, carry):
        mf, mb = carry
        rf = pl.ds(pl.multiple_of(n * SUBLANES, SUBLANES), SUBLANES)
        rb = pl.ds(pl.multiple_of((nc - 1 - n) * SUBLANES, SUBLANES), SUBLANES)
        mf_s[rf, :] = mf
        mb_s[rb, :] = mb
        return (bl_s[rf, :] + jnp.maximum(mf, rl_s[rf, :]), bl_s[rb, :] + jnp.maximum(mb, rl_s[rb, :]))

    m0 = jnp.full((SUBLANES, LANES), NEG, F32)
    lax.fori_loop(0, nc, m_chain, (m0, m0))

    expo_s, floor_s = (b_s, cm_s), (r_s, fl1_s)
    sel = _gate_select()

    def weights(t, _):
        r0 = pl.multiple_of(t * R, R)
        c8 = pl.ds(pl.multiple_of(t * cpt * SUBLANES, SUBLANES), cpt * SUBLANES)
        rows = pl.ds(r0, R)
        m_in8 = jnp.where(is_bwd8, mb_s[c8, :], mf_s[c8, :])
        bl8 = bl_s[c8, :]
        m_out8 = bl8 + jnp.maximum(m_in8, rl_s[c8, :])
        m_in = _expand_rows8(m_in8)
        mx = jnp.maximum(m_in, cm_s[rows, :])
        wp_b = jnp.exp(_gate_broadcast(bl8 + m_in8 - m_out8, sel, gate_mask8))
        mx_b = _gate_broadcast(mx, sel, gate_mask)
        floor = jnp.exp(-_gate_broadcast(b_s[rows, :] + mx, sel, gate_mask))
        w_inter = _gate_broadcast(jnp.exp(m_in - mx), sel, gate_mask, pieces=1)
        wk = _gate_broadcast(jnp.exp(_expand_rows8(bl8 - m_out8) + r_s[rows, :]), sel, gate_mask, pieces=1)
        q = q_s[rows, :].astype(F32)
        k = k_s[rows, :].astype(F32)
        for d in range(2):
            half = slice(d * LANES, (d + 1) * LANES)
            qi_s[d, rows, :] = (q * w_inter[:, half]).astype(BF16)
            kw_s[d, rows, :] = (k * wk[:, half]).astype(BF16)
            wp_s[d, c8, :] = wp_b[:, half]
            expo_s[d][rows, :] = -mx_b[:, half]
            floor_s[d][rows, :] = floor[:, half]
        return 0

    lax.fori_loop(0, T // R, weights, 0)

    outs = (of_s, ob_s)
    ci = lax.broadcasted_iota(jnp.int32, (CHUNK, LANES), 0)
    si = lax.broadcasted_iota(jnp.int32, (CHUNK, LANES), 1) % CHUNK
    masks = (ci >= si, ci <= si)
    br = lax.broadcasted_iota(jnp.int32, (LANES, 2 * LANES), 0) // HEAD_DIM
    bc = (lax.broadcasted_iota(jnp.int32, (LANES, 2 * LANES), 1) % LANES) // HEAD_DIM
    blockdiag = br == bc
    ones = jnp.ones((CHUNK, LANES), BF16)
    st_s[...] = jnp.zeros(st_s.shape, F32)
    gate_rows = slice(GATE_I, GATE_I + 4 * MLSTM_HEADS)

    def step(it, _):
        units = [(d, g) for d in range(2) for g in range(G)]
        ops = {}
        for d in range(2):
            first = (nc - (it + 1) * G) if d else it * G
            rt = rt_s[gate_rows, pl.ds(pl.multiple_of(first * CHUNK, G * CHUNK), G * CHUNK)]
            for g in range(G):
                local = (G - 1 - g) if d else g
                c = first + local
                rows = pl.ds(pl.multiple_of(c * CHUNK, CHUNK), CHUNK)
                c8 = pl.ds(pl.multiple_of(c * SUBLANES, SUBLANES), SUBLANES)
                r_row = jnp.concatenate(
                    [rt[_gate_lane(d, hh) - GATE_I:_gate_lane(d, hh) - GATE_I + 1,
                        local * CHUNK:(local + 1) * CHUNK] for hh in range(2)], axis=1)
                wp = wp_s[d, c8, :][0:1, :]
                ops[d, g] = dict(
                    rows=rows, q=q_s[rows, :], k=k_s[rows, :], qi=qi_s[d, rows, :], kw=kw_s[d, rows, :],
                    va=jnp.concatenate([v_ref[0, rows, :].astype(BF16), ones], axis=1),
                    expo=expo_s[d][rows, :] + r_row, floor=floor_s[d][rows, :],
                    wp=jnp.concatenate([wp, wp], axis=1))
        qk, kv = {}, {}
        for u in units:
            o = ops[u]
            qk[u] = _dot_nt(o["q"], _stack_heads(o["k"]))
            kv[u] = _dot_tn(o["kw"], o["va"])
        st_in = {}
        for d in range(2):
            st = st_s[d]
            for g in range(G):
                st_in[d, g] = st.astype(BF16)
                st = st * ops[d, g]["wp"] + jnp.where(blockdiag, kv[d, g], 0.0)
            st_s[d] = st
        for u in units:
            d = u[0]
            o = ops[u]
            smat = (qk[u] * jnp.where(masks[d], jnp.exp(o["expo"]), 0.0)).astype(BF16)
            va_bd = jnp.where(blockdiag, jnp.concatenate([o["va"], o["va"]], axis=0), jnp.zeros((), BF16))
            num = _dot(jnp.concatenate([o["qi"], smat], axis=1), jnp.concatenate([st_in[u], va_bd], axis=0))
            den = jnp.maximum(jnp.abs(num[:, LANES:]), o["floor"])
            outs[d][o["rows"], :] = num[:, :LANES] / den
        return 0

    lax.fori_loop(0, nc // G, step, 0)

    def post(t, _):
        r0 = pl.multiple_of(t * R, R)
        h = _head_rms(of_s[pl.ds(r0, R), :] + ob_s[pl.ds(r0, R), :], gain_ref[...])
        o_ref[0, pl.ds(r0, R), :] = (h * _sigmoid(og_ref[0, pl.ds(r0, R), :])).astype(BF16)
        return 0

    lax.fori_loop(0, T // R, post, 0)


def _mlstm(ml, small, lp):
    B, T, _ = ml.shape
    pairs = MLSTM_WIDTH // LANES
    col = lambda off: (lambda b, hp: (b, 0, off * pairs + hp))
    const = lambda b, hp: (0, 0)
    return pl.pallas_call(
        _mlstm_kernel,
        grid=(B, pairs),
        in_specs=[
            pl.BlockSpec((1, T, LANES), col(0)),
            pl.BlockSpec((1, T, LANES), col(1)),
            pl.BlockSpec((1, T, LANES), col(2)),
            pl.BlockSpec((1, T, LANES), col(3)),
            pl.BlockSpec((1, T, LANES), lambda b, hp: (b, 0, 0)),
            pl.BlockSpec((3, LANES), lambda b, hp: (0, hp)),
            pl.BlockSpec((3, LANES), lambda b, hp: (0, pairs + hp)),
            pl.BlockSpec((1, LANES), lambda b, hp: (0, hp)),
            pl.BlockSpec((1, LANES), lambda b, hp: (0, pairs + hp)),
            pl.BlockSpec((1, LANES), const),
            pl.BlockSpec((1, LANES), const),
        ],
        out_specs=pl.BlockSpec((1, T, LANES), lambda b, hp: (b, 0, hp)),
        out_shape=jax.ShapeDtypeStruct((B, T, MLSTM_WIDTH), BF16),
        scratch_shapes=(
            [pltpu.VMEM((T, LANES), BF16)] * 2
            + [pltpu.VMEM((T, LANES), F32)] * 4
            + [pltpu.VMEM(((T // CHUNK) * SUBLANES, LANES), F32)] * 4
            + [pltpu.VMEM((2, (T // CHUNK) * SUBLANES, LANES), F32)]
            + [pltpu.VMEM((2, T, LANES), BF16)] * 2
            + [pltpu.VMEM((LANES, T), F32),
               pltpu.VMEM((2, LANES, 2 * LANES), F32),
               pltpu.VMEM((T, LANES), F32), pltpu.VMEM((T, LANES), F32)]),
        compiler_params=_params("arbitrary", "arbitrary"),
        name="mlstm",
    )(ml, ml, ml, ml, small, lp["conv_w"], lp["conv_w"], lp["conv_b"], lp["conv_b"],
      lp["gate_bias"], lp["ml_gain"])


def _first_argmax(vals, row):
    mx = jnp.max(vals, axis=0, keepdims=True)
    idx = jnp.min(jnp.where(vals == mx, row, vals.shape[0]), axis=0, keepdims=True)
    return mx, idx


def _out_route_kernel(x_ref, a_ref, gl_ref, ml_ref, w_ref, g_ref, wr_ref, br_ref,
                      x1_ref, t_ref, route_ref, routet_ref):
    tm = x_ref.shape[0]
    sub = min(tm, ROUTE_SUBTILE)
    spans = [slice(r, r + sub) for r in range(0, tm, sub)]
    x1s, logits = [], []
    for rs in spans:
        x1 = (x_ref[rs, :]
              + _dot(a_ref[rs, :], w_ref[0:ATTN_WIDTH, :])
              + _dot(gl_ref[rs, :], w_ref[ATTN_WIDTH:ATTN_WIDTH + GLA_WIDTH, :])
              + _dot(ml_ref[rs, :], w_ref[ATTN_WIDTH + GLA_WIDTH:, :]))
        x1_ref[rs, :] = x1
        x1s.append(x1)
    for rs, x1 in zip(spans, x1s):
        t_hi, t_lo = _split(_rms(x1, g_ref[...]))
        t_ref[rs, 0:t_hi.shape[1]] = t_hi
        both = _dot(t_hi, wr_ref[...])
        logits.append((both[:, :LANES] + both[:, LANES:] + _dot(t_lo, wr_ref[:, :LANES])) + br_ref[...])
    grow = lax.broadcasted_iota(jnp.int32, (SUBLANES, sub), 0)
    erow = lax.broadcasted_iota(jnp.int32, (N_EXPERTS, sub), 0)
    for rs, lg in zip(spans, logits):
        lt = lg.T
        gl = jnp.where(grow < N_GROUPS, lt[0:SUBLANES, :], -jnp.inf)
        gmax, gi = _first_argmax(gl, grow)
        g_prob = 1.0 / jnp.sum(jnp.exp(gl - gmax), axis=0, keepdims=True)
        el = jnp.where(erow // EXPERTS_PER_GROUP == gi, lt[ROUTE_W_LANE:ROUTE_W_LANE + N_EXPERTS, :], -jnp.inf)
        v1, i1 = _first_argmax(el, erow)
        v2, i2 = _first_argmax(jnp.where(erow == i1, -jnp.inf, el), erow)
        e2 = jnp.exp(v2 - v1)
        w1 = g_prob / (1.0 + e2)
        w2 = g_prob * e2 / (1.0 + e2)
        comb = jnp.where(erow == i1, w1, jnp.where(erow == i2, w2, 0.0))
        head = jnp.where(grow == 0, gi.astype(F32), 0.0)
        route_t = jnp.concatenate(
            [head, comb, jnp.zeros((LANES - SUBLANES - N_EXPERTS, sub), F32)], axis=0)
        route_ref[rs, :] = route_t.T
        routet_ref[:, rs] = head


def _dispatch_kernel(t_ref, route_ref, routet_ref, xs_ref, cws_ref, pos_ref, cnt_ref):
    nb = ROUTE_BLOCK
    gi_row = routet_ref[0:1, :]
    sub = lax.broadcasted_iota(jnp.int32, (SUBLANES, nb), 0).astype(F32)
    onehot = (sub == gi_row)
    ri = lax.broadcasted_iota(jnp.int32, (nb, nb), 0)
    cj = lax.broadcasted_iota(jnp.int32, (nb, nb), 1)
    before = (ri < cj).astype(BF16)
    rank = _dot(onehot.astype(BF16), before)
    counts = jnp.broadcast_to(jnp.sum(onehot.astype(F32), axis=-1, keepdims=True), (SUBLANES, LANES))
    padded = jnp.ceil(counts * (1.0 / ROUTE_TILE)) * ROUTE_TILE
    srow = lax.broadcasted_iota(jnp.int32, (SUBLANES, LANES), 0)
    incl = padded
    s = 1
    while s < SUBLANES:
        incl = incl + jnp.where(srow >= s, pltpu.roll(incl, s, 0), 0.0)
        s *= 2
    start = incl - padded
    pos_row = jnp.sum(jnp.where(onehot, start[:, 0:1] + rank, 0.0), axis=0, keepdims=True)
    cnt_ref[0] = counts.astype(jnp.int32)

    pos_hi = jnp.floor(pos_row * (1.0 / ROUTE_TILE))
    prow = lax.broadcasted_iota(jnp.int32, (2 * SUBLANES, nb), 0)
    pieces = jnp.where(prow == 0, pos_hi, jnp.where(prow == 1, pos_row - ROUTE_TILE * pos_hi, 0.0))
    wrow = lax.broadcasted_iota(jnp.int32, (2 * SUBLANES, LANES), 0)
    weights = jnp.where(wrow == 0, float(ROUTE_TILE), jnp.where(wrow == 1, 1.0, 0.0))
    pos_ref[...] = _dot_tn(pieces.astype(BF16), weights.astype(BF16))

    route = route_ref[...]
    lane = lax.broadcasted_iota(jnp.int32, (nb, LANES), 1)
    comb = jnp.where(lane >= ROUTE_W_LANE, route, 0.0)
    c_hi, c_lo = _split(comb)
    c_lo2 = (comb - c_hi.astype(F32) - c_lo.astype(F32)).astype(BF16)
    d_model = xs_ref.shape[1]
    t_ref[:, d_model:] = (c_hi.astype(F32) + pltpu.roll(c_lo.astype(F32), ROUTE_LO_SHIFT, 1)
                          + pltpu.roll(c_lo2.astype(F32), 2 * ROUTE_LO_SHIFT, 1)).astype(BF16)
    tb = t_ref[...]
    for r in range(TILES_PER_BLOCK):
        rows = (lax.broadcasted_iota(jnp.int32, (ROUTE_TILE, nb), 0) + r * ROUTE_TILE).astype(F32)
        perm = (rows == pos_row).astype(BF16)
        moved = _dot(perm, tb)
        xs_ref[r * ROUTE_TILE:(r + 1) * ROUTE_TILE, :] = moved[:, :d_model].astype(BF16)
        cws_ref[r * ROUTE_TILE:(r + 1) * ROUTE_TILE, :] = moved[:, d_model:]


def _route_dispatch_kernel(x_ref, a_ref, gl_ref, ml_ref, w_ref, g_ref, wr_ref, br_ref,
                           x1_ref, xs_ref, cws_ref, pos_ref, cnt_ref, t_s, route_s, routet_s):
    _out_route_kernel(x_ref, a_ref, gl_ref, ml_ref, w_ref, g_ref, wr_ref, br_ref,
                      x1_ref, t_s, route_s, routet_s)
    _dispatch_kernel(t_s, route_s, routet_s, xs_ref, cws_ref, pos_ref, cnt_ref)


def _route_dispatch(x2, attn, gla_o, ml_o, lp):
    N, D = x2.shape
    nblk = N // ROUTE_BLOCK
    const = lambda i: (0, 0)
    tok = lambda i: (i, 0)
    return pl.pallas_call(
        _route_dispatch_kernel,
        grid=(nblk,),
        in_specs=[
            pl.BlockSpec((ROUTE_BLOCK, D), tok),
            pl.BlockSpec((ROUTE_BLOCK, ATTN_WIDTH), tok),
            pl.BlockSpec((ROUTE_BLOCK, GLA_WIDTH), tok),
            pl.BlockSpec((ROUTE_BLOCK, MLSTM_WIDTH), tok),
            pl.BlockSpec((ATTN_WIDTH + GLA_WIDTH + MLSTM_WIDTH, D), const),
            pl.BlockSpec((1, D), const),
            pl.BlockSpec((D, 2 * LANES), const),
            pl.BlockSpec((1, LANES), const),
        ],
        out_specs=[
            pl.BlockSpec((ROUTE_BLOCK, D), tok),
            pl.BlockSpec((ROUTE_ROWS, D), tok),
            pl.BlockSpec((ROUTE_ROWS, LANES), tok),
            pl.BlockSpec((ROUTE_BLOCK, LANES), tok),
            pl.BlockSpec((1, SUBLANES, LANES), lambda i: (i, 0, 0)),
        ],
        out_shape=[
            jax.ShapeDtypeStruct((N, D), F32),
            jax.ShapeDtypeStruct((nblk * ROUTE_ROWS, D), BF16),
            jax.ShapeDtypeStruct((nblk * ROUTE_ROWS, LANES), F32),
            jax.ShapeDtypeStruct((N, LANES), F32),
            jax.ShapeDtypeStruct((nblk, SUBLANES, LANES), jnp.int32),
        ],
        scratch_shapes=[
            pltpu.VMEM((ROUTE_BLOCK, D + LANES), BF16),
            pltpu.VMEM((ROUTE_BLOCK, LANES), F32),
            pltpu.VMEM((SUBLANES, ROUTE_BLOCK), F32),
        ],
        compiler_params=_params("arbitrary"),
        name="route_dispatch",
    )(x2, attn, gla_o, ml_o, lp["w_out"], lp["g_ffn"], lp["w_route"], lp["b_route"])


def _tile_schedule(cnt):
    nblk = cnt.shape[0]
    ntile = (cnt + ROUTE_TILE - 1) // ROUTE_TILE
    end = jnp.cumsum(ntile, axis=1)
    r = jnp.arange(TILES_PER_BLOCK, dtype=jnp.int32)
    grp = jnp.sum(r[None, :, None] >= end[:, None, :], axis=-1)
    grp = grp.reshape(-1).astype(jnp.int32)
    tile = jnp.arange(nblk * TILES_PER_BLOCK, dtype=jnp.int32)
    order = jnp.argsort(grp * (nblk * TILES_PER_BLOCK) + tile).astype(jnp.int32)
    n_active = jnp.sum(grp < N_GROUPS).astype(jnp.int32)
    g_sorted = grp[order]
    last_group = g_sorted[jnp.maximum(n_active - 1, 0)]
    g_sorted = jnp.where(g_sorted < N_GROUPS, g_sorted, last_group)
    return order, g_sorted, n_active[None]


def _expert_kernel(trow_ref, tgrp_ref, nact_ref, xs_ref, cws_ref, wg_ref, wu_ref, wd_ref, ys_ref,
                   wg_s, wu_s, wd_s):
    i = pl.program_id(0)

    @pl.when((i == 0) | (tgrp_ref[i] != tgrp_ref[jnp.maximum(i - 1, 0)]))
    def _():
        for j in range(EXPERTS_PER_GROUP):
            wg_s[j] = wg_ref[j].astype(BF16)
            wu_s[j] = wu_ref[j].astype(BF16)
            wd_s[j] = wd_ref[j].astype(BF16)

    @pl.when(i < nact_ref[0])
    def _():
        x = xs_ref[...]
        cws = cws_ref[...]
        lane = lax.broadcasted_iota(jnp.int32, cws.shape, 1)
        first = ROUTE_W_LANE + tgrp_ref[i] * EXPERTS_PER_GROUP
        y = jnp.zeros(ys_ref.shape, F32)

        def hidden(j):
            return _dot(x, wg_s[j]), _dot(x, wu_s[j])

        h = hidden(0)
        for j in range(EXPERTS_PER_GROUP):
            h_next = hidden(j + 1) if j + 1 < EXPERTS_PER_GROUP else None
            off = lane - (first + j)
            sel = (off == 0) | (off == ROUTE_LO_SHIFT) | (off == 2 * ROUTE_LO_SHIFT)
            wj = jnp.sum(jnp.where(sel, cws, 0.0), axis=-1, keepdims=True)
            a = (h[0] * _sigmoid(h[0]) * h[1]).astype(BF16)
            y = y + wj * _dot(a, wd_s[j])
            h = h_next
        ys_ref[...] = y.astype(BF16)

    @pl.when(i >= nact_ref[0])
    def _():
        ys_ref[...] = jnp.zeros(ys_ref.shape, BF16)


def _experts(xs, cws, order, grp, n_active, lp):
    rows, D = xs.shape
    n_tiles = rows // ROUTE_TILE
    tile = lambda i, trow, tgrp, nact: (trow[i], 0)
    layer_groups = lp["layer"] * N_GROUPS
    wsel = lambda i, trow, tgrp, nact: (layer_groups + tgrp[i], 0, 0)
    once = pl.Buffered(1)
    return pl.pallas_call(
        _expert_kernel,
        grid_spec=pltpu.PrefetchScalarGridSpec(
            num_scalar_prefetch=3,
            grid=(n_tiles,),
            in_specs=[
                pl.BlockSpec((ROUTE_TILE, D), tile),
                pl.BlockSpec((ROUTE_TILE, LANES), tile),
                pl.BlockSpec((EXPERTS_PER_GROUP, D, D_EXPERT), wsel, pipeline_mode=once),
                pl.BlockSpec((EXPERTS_PER_GROUP, D, D_EXPERT), wsel, pipeline_mode=once),
                pl.BlockSpec((EXPERTS_PER_GROUP, D_EXPERT, D), wsel, pipeline_mode=once),
            ],
            out_specs=pl.BlockSpec((ROUTE_TILE, D), tile),
            scratch_shapes=[
                pltpu.VMEM((EXPERTS_PER_GROUP, D, D_EXPERT), BF16),
                pltpu.VMEM((EXPERTS_PER_GROUP, D, D_EXPERT), BF16),
                pltpu.VMEM((EXPERTS_PER_GROUP, D_EXPERT, D), BF16),
            ],
        ),
        out_shape=jax.ShapeDtypeStruct((rows, D), BF16),
        compiler_params=_params("arbitrary"),
        name="experts",
    )(order, grp, n_active, xs, cws, lp["w_gate"], lp["w_up"], lp["w_down"])


def _combine_kernel(ys_ref, pos_ref, x1_ref, p_ref, g_ref, wpg_ref, wpp_ref, gfin_ref, o_ref,
                    *, final):
    tm = x1_ref.shape[0]
    pos = pos_ref[...]
    lane = lax.broadcasted_iota(jnp.int32, (tm, LANES), 1).astype(F32)
    perm_t = jnp.concatenate(
        [(lane + r * LANES == pos).astype(BF16) for r in range(ROUTE_ROWS // LANES)], axis=1)
    x = x1_ref[...] + _dot(perm_t, ys_ref[...])
    gate = _sigmoid(_dot(_rms(x, g_ref[...]).astype(BF16), wpg_ref[...]))
    x = x + gate * _dot(p_ref[...].astype(BF16), wpp_ref[...])
    if final:
        x = _rms(x, gfin_ref[...])
    o_ref[...] = x


def _combine(ys, pos, x1, p3, lp, g_final, final=False, tm=1024):
    N, D = x1.shape
    inner = ROUTE_BLOCK // tm
    layer = lp["layer"]
    tok = lambda b, i: (b * inner + i, 0)
    const = lambda b, i: (0, 0)
    return pl.pallas_call(
        functools.partial(_combine_kernel, final=final),
        grid=(N // ROUTE_BLOCK, inner),
        in_specs=[
            pl.BlockSpec((ROUTE_ROWS, D), lambda b, i: (b, 0)),
            pl.BlockSpec((tm, LANES), tok),
            pl.BlockSpec((tm, D), tok),
            pl.BlockSpec((None, tm, p3.shape[2]), lambda b, i: (layer, b * inner + i, 0)),
            pl.BlockSpec((1, D), const),
            pl.BlockSpec((D, D), const),
            pl.BlockSpec((p3.shape[2], D), const),
            pl.BlockSpec((1, D), const),
        ],
        out_specs=pl.BlockSpec((tm, D), tok),
        out_shape=jax.ShapeDtypeStruct((N, D), F32),
        compiler_params=_params("arbitrary", "arbitrary"),
        name="combine",
    )(ys, pos, x1, p3, lp["g_ple"], lp["w_pg"], lp["w_pp"], g_final)


def _mix_out_moe(x2, attn, gla_o, ml_o, lp):
    x1, xs, cws, pos, cnt = _route_dispatch(x2, attn, gla_o, ml_o, lp)
    order, grp, n_active = _tile_schedule(cnt[:, :N_GROUPS, 0])
    return x1, _experts(xs, cws, order, grp, n_active, lp), pos


def _permute_in_cols(w):
    glr0 = ATTN_WIDTH + 2 * KV_WIDTH + 4 * GLA_WIDTH
    ml0 = glr0 + 2 * GLA_RANK
    mg0 = ml0 + 4 * MLSTM_WIDTH
    end = mg0 + 4 * MLSTM_HEADS
    assert end == w.shape[-1]
    pad = jnp.zeros(w.shape[:-1] + (IN_PERM_WIDTH - end,), BF16)
    parts = [w[..., :glr0], w[..., ml0:mg0], w[..., glr0:ml0], w[..., mg0:end]]
    return jnp.concatenate([part.astype(BF16) for part in parts] + [pad], axis=-1)


def _rope_tables(T):
    t = np.arange(T)
    inv = ROPE_THETA ** (-np.arange(0, ROPE_AXIS_DIM, 2, dtype=np.float64) / ROPE_AXIS_DIM)
    ang_r = (t // GRID_W)[None, :] * inv[:, None]
    ang_c = (t % GRID_W)[None, :] * inv[:, None]
    cos_h = np.concatenate([np.cos(ang_r), np.cos(ang_r), np.cos(ang_c), np.cos(ang_c)], axis=0)
    sin_h = np.concatenate([-np.sin(ang_r), np.sin(ang_r), -np.sin(ang_c), np.sin(ang_c)], axis=0)
    reps = LANES // HEAD_DIM
    return dict(cos_t=jnp.asarray(np.tile(cos_h, (reps, 1)), F32),
                sin_t=jnp.asarray(np.tile(sin_h, (reps, 1)), F32))


def kernel(x, p, norm_mix_g, w_in, attn_q_norm_g, attn_k_norm_g, gla_w_decay, gla_b_decay,
           gla_out_norm_g, mlstm_conv_w, mlstm_conv_b, mlstm_b_input, mlstm_b_forget,
           mlstm_out_norm_g, w_out, norm_ffn_g, w_group, b_group, w_router, b_router,
           w_expert_gate, w_expert_up, w_expert_down, norm_ple_g, w_ple_gate, w_ple_proj,
           final_norm_g):
    params = dict(
        norm_mix_g=norm_mix_g, w_in=w_in, attn_q_norm_g=attn_q_norm_g, attn_k_norm_g=attn_k_norm_g,
        gla_w_decay=gla_w_decay, gla_b_decay=gla_b_decay, gla_out_norm_g=gla_out_norm_g,
        mlstm_conv_w=mlstm_conv_w, mlstm_conv_b=mlstm_conv_b, mlstm_b_input=mlstm_b_input,
        mlstm_b_forget=mlstm_b_forget, mlstm_out_norm_g=mlstm_out_norm_g, w_out=w_out,
        norm_ffn_g=norm_ffn_g, w_group=w_group, b_group=b_group, w_router=w_router,
        b_router=b_router, w_expert_gate=w_expert_gate, w_expert_up=w_expert_up,
        w_expert_down=w_expert_down, norm_ple_g=norm_ple_g, w_ple_gate=w_ple_gate,
        w_ple_proj=w_ple_proj)
    B, T, D = x.shape
    rope = _rope_tables(T)
    N = B * T
    depth = w_in.shape[0]
    x2 = x.reshape(N, D)
    g_final = final_norm_g[None, :]
    p3 = p.reshape(depth, N, p.shape[-1])
    for i in range(depth):
        lp = _layer_params(params, i)
        qt, k, vt, gla, ml, small = _in_proj(x2, lp, rope, B, T)
        attn = _attention(lp["attn_safe"], qt, k, vt).reshape(N, ATTN_WIDTH)
        gla_o = _gla(gla, small, lp).reshape(N, GLA_WIDTH)
        ml_o = _mlstm(ml, small, lp).reshape(N, MLSTM_WIDTH)
        x1, ys, pos = _mix_out_moe(x2, attn, gla_o, ml_o, lp)
        x2 = _combine(ys, pos, x1, p3, lp, g_final, final=(i == depth - 1))
    return x2.reshape(B, T, D)


def _split_w(w):
    hi = w.astype(BF16)
    return hi, (w - hi.astype(F32)).astype(BF16)


def _stacked_experts(w):
    return w.reshape((w.shape[0] * w.shape[1],) + w.shape[2:])


def _layer_params(p, i):
    D = p["w_in"].shape[1]
    gq, gk = p["attn_q_norm_g"][i], p["attn_k_norm_g"][i]
    q_gain = jnp.tile(gq, LANES // HEAD_DIM) * (HEAD_DIM ** -0.5 * LOG2E)
    logit_bound = HEAD_DIM ** 0.5 * jnp.max(jnp.abs(gq)) * jnp.max(jnp.abs(gk))
    attn_safe = (logit_bound <= ATTN_SAFE_LOGIT).astype(jnp.int32)[None]
    wd = jnp.zeros((2, LANES, GLA_WIDTH), F32)
    wd = wd.at[0, :GLA_RANK].set(p["gla_w_decay"][i, 0]).at[1, GLA_RANK:2 * GLA_RANK].set(p["gla_w_decay"][i, 1])
    wd_hi, wd_lo = _split_w(wd)
    gate_bias = jnp.zeros((LANES,), F32).at[SMALL_GATE_LANE:SMALL_GATE_LANE + 4 * MLSTM_HEADS].set(
        jnp.concatenate([p["mlstm_b_input"][i, 0], p["mlstm_b_forget"][i, 0],
                         p["mlstm_b_input"][i, 1], p["mlstm_b_forget"][i, 1]]))
    w_route = jnp.zeros((D, LANES), F32)
    w_route = w_route.at[:, :N_GROUPS].set(p["w_group"][i])
    w_route = w_route.at[:, ROUTE_W_LANE:ROUTE_W_LANE + N_EXPERTS].set(p["w_router"][i])
    wr_hi, wr_lo = _split_w(w_route)
    b_route = jnp.zeros((LANES,), F32).at[:N_GROUPS].set(p["b_group"][i])
    b_route = b_route.at[ROUTE_W_LANE:ROUTE_W_LANE + N_EXPERTS].set(p["b_router"][i])
    return dict(
        g_mix=p["norm_mix_g"][i][None, :],
        w_in=_permute_in_cols(p["w_in"]),
        q_gain_t=jnp.broadcast_to(q_gain[:, None], (LANES, LANES)),
        k_gain_t=jnp.broadcast_to(jnp.tile(gk, LANES // HEAD_DIM)[:, None], (LANES, LANES)),
        attn_safe=attn_safe,
        wd_hi=wd_hi, wd_lo=wd_lo,
        bd=p["gla_b_decay"][i][:, None, :],
        gla_gain=jnp.tile(p["gla_out_norm_g"][i], LANES // HEAD_DIM)[None, :],
        conv_w=p["mlstm_conv_w"][i],
        conv_b=p["mlstm_conv_b"][i][None, :],
        gate_bias=gate_bias[None, :],
        ml_gain=jnp.tile(p["mlstm_out_norm_g"][i], LANES // HEAD_DIM)[None, :],
        w_out=p["w_out"][i].astype(BF16),
        g_ffn=p["norm_ffn_g"][i][None, :],
        w_route=jnp.concatenate([wr_hi, wr_lo], axis=1), b_route=b_route[None, :],
        layer=i,
        w_gate=_stacked_experts(p["w_expert_gate"]),
        w_up=_stacked_experts(p["w_expert_up"]),
        w_down=_stacked_experts(p["w_expert_down"]),
        g_ple=p["norm_ple_g"][i][None, :],
        w_pg=p["w_ple_gate"][i].astype(BF16),
        w_pp=p["w_ple_proj"][i].astype(BF16),
    )
```

```python
import functools

import jax
import jax.numpy as jnp
import numpy as np
from jax import lax
from jax.experimental import pallas as pl
from jax.experimental.pallas import tpu as pltpu

F32 = jnp.float32
BF16 = jnp.bfloat16

GRID_W = 64
HEAD_DIM = 64
ATTN_HEADS = 8
ATTN_KV_HEADS = 2
GLA_HEADS = 4
MLSTM_HEADS = 4
ATTN_WIDTH = ATTN_HEADS * HEAD_DIM
KV_WIDTH = ATTN_KV_HEADS * HEAD_DIM
GLA_WIDTH = GLA_HEADS * HEAD_DIM
MLSTM_WIDTH = MLSTM_HEADS * HEAD_DIM
GLA_RANK = 16
GLA_TAU = 16.0
CHUNK = 64
ROPE_THETA = 10000.0
ROPE_AXIS_DIM = HEAD_DIM // 2
N_GROUPS = 4
EXPERTS_PER_GROUP = 4
N_EXPERTS = N_GROUPS * EXPERTS_PER_GROUP
D_EXPERT = 512
EPS = 1e-6
NEG = -1e30

LANES = 128
SUBLANES = 8
VMEM_LIMIT_BYTES = 56 * 1024 * 1024

QK_WIDTH = ATTN_WIDTH + KV_WIDTH
OFF_V = QK_WIDTH
OFF_GLA = OFF_V + KV_WIDTH
OFF_ML = OFF_GLA + 4 * GLA_WIDTH
OFF_SMALL = OFF_ML + 4 * MLSTM_WIDTH
IN_PERM_WIDTH = OFF_SMALL + LANES
SMALL_GATE_LANE = 2 * GLA_RANK

ROUTE_BLOCK = 1024
ROUTE_TILE = 128
TILES_PER_BLOCK = (ROUTE_BLOCK + N_GROUPS * (ROUTE_TILE - 1)) // ROUTE_TILE
ROUTE_ROWS = TILES_PER_BLOCK * ROUTE_TILE
COMBINE_SHORT_ROWS = ROUTE_ROWS - ROUTE_TILE
ROUTE_W_LANE = 8
ROUTE_LO_SHIFT = 32
ROUTE_SUBTILE = 256


def _dot(a, b):
    return jnp.dot(a, b, preferred_element_type=F32)


def _dot_nt(a, b):
    return lax.dot_general(a, b, (((1,), (1,)), ((), ())), preferred_element_type=F32)


def _dot_tn(a, b):
    return lax.dot_general(a, b, (((0,), (0,)), ((), ())), preferred_element_type=F32)


def _split(a):
    hi = a.astype(BF16)
    lo = (a - hi.astype(F32)).astype(BF16)
    return hi, lo


def _dot3(a, w_hi, w_lo):
    a_hi, a_lo = _split(a)
    return _dot(a_hi, w_hi) + _dot(a_lo, w_hi) + _dot(a_hi, w_lo)


def _log_sigmoid(x):
    return jnp.minimum(x, 0.0) - jnp.log(1.0 + jnp.exp(-jnp.abs(x)))


def _sigmoid(x):
    return 1.0 / (1.0 + jnp.exp(-x))


def _rms(x, g):
    return x * lax.rsqrt(jnp.mean(x * x, axis=-1, keepdims=True) + EPS) * g


def _params(*semantics):
    return pltpu.CompilerParams(dimension_semantics=semantics, vmem_limit_bytes=VMEM_LIMIT_BYTES)


def _in_proj_kernel(x_ref, g_ref, w_ref, cost_ref, sint_ref, gq_ref, gk_ref,
                    qt_ref, k_ref, vt_ref, gla_ref, ml_ref, small_ref):
    tm = x_ref.shape[0]
    h = _rms(x_ref[...], g_ref[...])
    z = _dot(h.astype(BF16), w_ref[...])

    heads = LANES // HEAD_DIM
    half = ROPE_AXIS_DIM // 2
    gq = jnp.concatenate([gq_ref[...]] * (tm // LANES), axis=1)
    cost = cost_ref[...]
    sint = sint_ref[...]
    gk = jnp.concatenate([gk_ref[...]] * (tm // LANES), axis=1)

    def norm_rope_t(c, gain):
        zt = z[:, c * LANES:(c + 1) * LANES].T
        z3 = zt.reshape(heads, HEAD_DIM, tm)
        inv = lax.rsqrt(jnp.mean(z3 * z3, axis=1, keepdims=True) + EPS)
        y = (z3 * inv).reshape(LANES, tm) * gain
        partner = jnp.concatenate(
            [y[(r ^ 1) * half:((r ^ 1) + 1) * half, :] for r in range(LANES // half)], axis=0)
        return y * cost + partner * sint

    for c in range(ATTN_WIDTH // LANES):
        qt_ref[0, c * LANES:(c + 1) * LANES, :] = norm_rope_t(c, gq).astype(BF16)
    k_ref[0] = norm_rope_t(ATTN_WIDTH // LANES, gk).T.astype(BF16)
    vt_ref[0] = z[:, OFF_V:OFF_V + KV_WIDTH].T.astype(BF16)
    gla_ref[0] = z[:, OFF_GLA:OFF_ML]
    ml_ref[0] = z[:, OFF_ML:OFF_SMALL]
    small_ref[0] = z[:, OFF_SMALL:IN_PERM_WIDTH]


def _in_proj(x2, lp, rope, B, T, tm=1024):
    N, D = x2.shape
    tpb = T // tm
    const = lambda i: (0, 0)
    tok3 = lambda i: (i // tpb, i % tpb, 0)
    tokT = lambda i: (i // tpb, 0, i % tpb)
    layer = lp["layer"]
    return pl.pallas_call(
        _in_proj_kernel,
        grid=(N // tm,),
        in_specs=[
            pl.BlockSpec((tm, D), lambda i: (i, 0)),
            pl.BlockSpec((1, D), const),
            pl.BlockSpec((None, D, IN_PERM_WIDTH), lambda i: (layer, 0, 0)),
            pl.BlockSpec((LANES, tm), lambda i: (0, i % tpb)),
            pl.BlockSpec((LANES, tm), lambda i: (0, i % tpb)),
            pl.BlockSpec((LANES, LANES), const),
            pl.BlockSpec((LANES, LANES), const),
        ],
        out_specs=[
            pl.BlockSpec((1, ATTN_WIDTH, tm), tokT),
            pl.BlockSpec((1, tm, KV_WIDTH), tok3),
            pl.BlockSpec((1, KV_WIDTH, tm), tokT),
            pl.BlockSpec((1, tm, 4 * GLA_WIDTH), tok3),
            pl.BlockSpec((1, tm, 4 * MLSTM_WIDTH), tok3),
            pl.BlockSpec((1, tm, LANES), tok3),
        ],
        out_shape=[
            jax.ShapeDtypeStruct((B, ATTN_WIDTH, T), BF16),
            jax.ShapeDtypeStruct((B, T, KV_WIDTH), BF16),
            jax.ShapeDtypeStruct((B, KV_WIDTH, T), BF16),
            jax.ShapeDtypeStruct((B, T, 4 * GLA_WIDTH), F32),
            jax.ShapeDtypeStruct((B, T, 4 * MLSTM_WIDTH), F32),
            jax.ShapeDtypeStruct((B, T, LANES), F32),
        ],
        compiler_params=_params("arbitrary"),
        name="in_proj",
    )(x2, lp["g_mix"], lp["w_in"], rope["cos_t"], rope["sin_t"], lp["q_gain_t"], lp["k_gain_t"])


ATTN_SAFE_LOGIT = 40.0
LOG2E = 1.4426950408889634


def _attn_kernel(safe_ref, qt_ref, k_ref, vt_ref, o_ref, *, tk):
    tq = qt_ref.shape[2]
    T = k_ref.shape[1]
    G = ATTN_HEADS // ATTN_KV_HEADS
    n = G * tq
    zeros = jnp.zeros((HEAD_DIM, n), BF16)

    def q_operand(j):
        base = j * G * HEAD_DIM
        qs = jnp.concatenate(
            [qt_ref[0, base + h * HEAD_DIM:base + (h + 1) * HEAD_DIM, :] for h in range(G)], axis=1)
        return jnp.concatenate([qs, zeros] if j == 0 else [zeros, qs], axis=0)

    def finish(j, acc, l):
        base = j * G * HEAD_DIM
        o = acc * (1.0 / l)
        ot = jnp.concatenate([o[:, h * tq:(h + 1) * tq] for h in range(G)], axis=0)
        o_ref[0, :, base:base + G * HEAD_DIM] = ot.T.astype(BF16)

    @pl.when(safe_ref[0] == 1)
    def _():
        qps = [q_operand(j) for j in range(ATTN_KV_HEADS)]
        units = [(c, j) for c in range(T // tk) for j in range(ATTN_KV_HEADS)]

        def scores(u):
            c, j = units[u]
            return _dot(k_ref[0, c * tk:(c + 1) * tk, :], qps[j])

        l8 = [jnp.zeros((SUBLANES, n), F32)] * ATTN_KV_HEADS
        acc = [jnp.zeros((HEAD_DIM, n), F32)] * ATTN_KV_HEADS
        st = scores(0)
        for u, (c, j) in enumerate(units):
            st_next = scores(u + 1) if u + 1 < len(units) else None
            p = jnp.exp2(st)
            l8[j] = l8[j] + jnp.sum(p.reshape(tk // SUBLANES, SUBLANES, n), axis=0)
            vc = vt_ref[0, j * HEAD_DIM:(j + 1) * HEAD_DIM, c * tk:(c + 1) * tk]
            acc[j] = acc[j] + _dot(vc, p.astype(BF16))
            st = st_next
        for j in range(ATTN_KV_HEADS):
            finish(j, acc[j], jnp.sum(l8[j], axis=0, keepdims=True))

    @pl.when(safe_ref[0] == 0)
    def _():
        for j in range(ATTN_KV_HEADS):
            qp = q_operand(j)

            def body(c, carry, qp=qp, j=j):
                m, l, acc = carry
                off = pl.multiple_of(c * tk, tk)
                st = _dot(k_ref[0, pl.ds(off, tk), :], qp)
                m_new = jnp.maximum(m, jnp.max(st, axis=0, keepdims=True))
                alpha = jnp.exp2(m - m_new)
                p = jnp.exp2(st - m_new)
                l = alpha * l + jnp.sum(p, axis=0, keepdims=True)
                vc = vt_ref[0, j * HEAD_DIM:(j + 1) * HEAD_DIM, pl.ds(off, tk)]
                return m_new, l, alpha * acc + _dot(vc, p.astype(BF16))

            init = (jnp.full((1, n), NEG, F32), jnp.zeros((1, n), F32), jnp.zeros((HEAD_DIM, n), F32))
            _, l, acc = lax.fori_loop(0, T // tk, body, init)
            finish(j, acc, l)


def _attention(safe, qt, k, vt, tq=256, tk=128):
    B, _, T = qt.shape
    tk = min(tk, T)
    return pl.pallas_call(
        functools.partial(_attn_kernel, tk=tk),
        grid_spec=pltpu.PrefetchScalarGridSpec(
            num_scalar_prefetch=1,
            grid=(B, T // tq),
            in_specs=[
                pl.BlockSpec((1, ATTN_WIDTH, tq), lambda b, i, s: (b, 0, i)),
                pl.BlockSpec((1, T, KV_WIDTH), lambda b, i, s: (b, 0, 0)),
                pl.BlockSpec((1, KV_WIDTH, T), lambda b, i, s: (b, 0, 0)),
            ],
            out_specs=pl.BlockSpec((1, tq, ATTN_WIDTH), lambda b, i, s: (b, i, 0)),
        ),
        out_shape=jax.ShapeDtypeStruct((B, T, ATTN_WIDTH), BF16),
        compiler_params=_params("arbitrary", "arbitrary"),
        name="attention",
    )(safe, qt, k, vt)


def _chunk_scan(x, pos, op, fill, reverse):
    rows = x.shape[0]
    s = 1
    while s < CHUNK:
        if reverse:
            shifted = jnp.where(pos < CHUNK - s, pltpu.roll(x, rows - s, 0), fill)
        else:
            shifted = jnp.where(pos >= s, pltpu.roll(x, s, 0), fill)
        x = op(x, shifted)
        s *= 2
    return x


def _chunk_pos(rows):
    return lax.broadcasted_iota(jnp.int32, (rows, LANES), 0) % CHUNK


def _chunk_row(a, reverse_dir, idx_fwd, idx_bwd):
    rows = a.shape[0]
    a3 = a.reshape(rows // CHUNK, CHUNK, LANES)
    i = idx_bwd if reverse_dir else idx_fwd
    return jnp.broadcast_to(a3[:, i:i + 1, :], a3.shape).reshape(rows, LANES)


def _stack_heads(x):
    lane = lax.broadcasted_iota(jnp.int32, x.shape, 1)
    zero = jnp.zeros_like(x)
    return jnp.concatenate([jnp.where(lane < HEAD_DIM, x, zero), jnp.where(lane >= HEAD_DIM, x, zero)], axis=0)


def _select_heads(x):
    c = x.shape[0] // 2
    lane = lax.broadcasted_iota(jnp.int32, (c, x.shape[1]), 1)
    return jnp.where(lane < HEAD_DIM, x[:c], x[c:])


def _stacked_causal_masks():
    ci = lax.broadcasted_iota(jnp.int32, (2 * CHUNK, CHUNK), 0) % CHUNK
    si = lax.broadcasted_iota(jnp.int32, (2 * CHUNK, CHUNK), 1)
    return ci >= si, ci <= si


def _pair_blockdiag(width):
    r = lax.broadcasted_iota(jnp.int32, (LANES, width), 0) // HEAD_DIM
    c = lax.broadcasted_iota(jnp.int32, (LANES, width), 1) // (width // 2)
    return r == c


def _head_rms(o, gain):
    lane = lax.broadcasted_iota(jnp.int32, o.shape, 1)
    lo = lane < HEAD_DIM
    sq = o * o
    s_lo = jnp.sum(jnp.where(lo, sq, 0.0), axis=-1, keepdims=True)
    s_hi = jnp.sum(jnp.where(lo, 0.0, sq), axis=-1, keepdims=True)
    ms = jnp.where(lo, s_lo, s_hi) * (1.0 / HEAD_DIM)
    return o * lax.rsqrt(ms + EPS) * gain


PREP_ROWS = 1024
GLA_CHUNKS_PER_STEP = 32


def _gla_kernel(q_ref, k_ref, v_ref, g_ref, small_ref, wdh_ref, wdl_ref, bd_ref, gain_ref,
                o_ref, qe_s, ke_s, dec_s, of_s, ob_s):
    T = q_ref.shape[1]
    nc = T // CHUNK
    R = min(PREP_ROWS, T)
    cpt = R // CHUNK
    pos = _chunk_pos(R)
    w_hi = jnp.concatenate([wdh_ref[0], wdh_ref[1]], axis=1)
    w_lo = jnp.concatenate([wdl_ref[0], wdl_ref[1]], axis=1)
    bias = jnp.concatenate([bd_ref[0], bd_ref[1]], axis=1)

    def prep(t, _):
        r0 = pl.multiple_of(t * R, R)
        q = q_ref[0, pl.ds(r0, R), :] * HEAD_DIM ** -0.5
        k = k_ref[0, pl.ds(r0, R), :]
        la2 = _log_sigmoid(_dot3(small_ref[0, pl.ds(r0, R), :], w_hi, w_lo) + bias) * (1.0 / GLA_TAU)
        for d in range(2):
            b = _chunk_scan(la2[:, d * LANES:(d + 1) * LANES], pos, jnp.add, 0.0, reverse=bool(d))
            b_mid = _chunk_row(b, d, CHUNK // 2 - 1, CHUNK // 2)
            b_last = _chunk_row(b, d, CHUNK - 1, 0)
            qe_s[d, pl.ds(r0, R), :] = (q * jnp.exp(b - b_mid)).astype(BF16)
            ke_s[d, pl.ds(r0, R), :] = (k * jnp.exp(b_mid - b)).astype(BF16)
            c8 = pl.ds(pl.multiple_of(t * cpt * SUBLANES, SUBLANES), cpt * SUBLANES)
            for kind, val in enumerate((b_last, b_last - b_mid, b_mid)):
                rows8 = val.reshape(cpt, CHUNK, LANES)[:, :SUBLANES, :].reshape(cpt * SUBLANES, LANES)
                dec_s[d, kind, c8, :] = jnp.exp(rows8)
        return 0

    lax.fori_loop(0, T // R, prep, 0)

    outs = (of_s, ob_s)
    masks = _stacked_causal_masks()
    blockdiag = _pair_blockdiag(LANES)
    G = min(GLA_CHUNKS_PER_STEP, nc)
    assert nc % G == 0

    def step(it, states):
        units = [(d, g) for d in range(2) for g in range(G)]
        ops = {}
        for d, g in units:
            c = it * G + g
            c = (nc - 1 - c) if d else c
            r0 = pl.multiple_of(c * CHUNK, CHUNK)
            c8 = pl.ds(pl.multiple_of(c * SUBLANES, SUBLANES), SUBLANES)
            ops[d, g] = dict(
                r0=r0,
                qe=qe_s[d, pl.ds(r0, CHUNK), :], ke=ke_s[d, pl.ds(r0, CHUNK), :],
                v=v_ref[0, pl.ds(r0, CHUNK), :].astype(BF16),
                dec=dec_s[d, 0, c8, :][0:1, :], to_end=dec_s[d, 1, c8, :][0:1, :],
                from_start=dec_s[d, 2, c8, :][0:1, :])
        a2, kv = {}, {}
        for u in units:
            o = ops[u]
            a2[u] = _dot_nt(_stack_heads(o["qe"]), o["ke"])
            kv[u] = _dot_tn(o["v"], o["ke"]) * o["to_end"]
        st_in = {}
        new_states = []
        for d in range(2):
            st = states[d]
            for g in range(G):
                st_in[d, g] = (st * ops[d, g]["from_start"]).astype(BF16)
                st = st * ops[d, g]["dec"] + jnp.where(blockdiag, kv[d, g], 0.0)
            new_states.append(st)
        for u in units:
            o = ops[u]
            a = jnp.where(masks[u[0]], a2[u], 0.0).astype(BF16)
            intra = _select_heads(_dot(a, o["v"]))
            outs[u[0]][pl.ds(o["r0"], CHUNK), :] = intra + _dot_nt(o["qe"], st_in[u])
        return tuple(new_states)

    zero = jnp.zeros((LANES, LANES), F32)
    lax.fori_loop(0, nc // G, step, (zero, zero))

    def post(t, _):
        r0 = pl.multiple_of(t * R, R)
        o = _head_rms(of_s[pl.ds(r0, R), :] + ob_s[pl.ds(r0, R), :], gain_ref[...])
        g = g_ref[0, pl.ds(r0, R), :]
        o_ref[0, pl.ds(r0, R), :] = (o * (g * _sigmoid(g))).astype(BF16)
        return 0

    lax.fori_loop(0, T // R, post, 0)


def _gla(gla, small, lp):
    B, T, _ = gla.shape
    pairs = GLA_WIDTH // LANES
    col = lambda off: (lambda b, hp: (b, 0, off * pairs + hp))
    return pl.pallas_call(
        _gla_kernel,
        grid=(B, pairs),
        in_specs=[
            pl.BlockSpec((1, T, LANES), col(0)),
            pl.BlockSpec((1, T, LANES), col(1)),
            pl.BlockSpec((1, T, LANES), col(2)),
            pl.BlockSpec((1, T, LANES), col(3)),
            pl.BlockSpec((1, T, LANES), lambda b, hp: (b, 0, 0)),
            pl.BlockSpec((2, LANES, LANES), lambda b, hp: (0, 0, hp)),
            pl.BlockSpec((2, LANES, LANES), lambda b, hp: (0, 0, hp)),
            pl.BlockSpec((2, 1, LANES), lambda b, hp: (0, 0, hp)),
            pl.BlockSpec((1, LANES), lambda b, hp: (0, 0)),
        ],
        out_specs=pl.BlockSpec((1, T, LANES), lambda b, hp: (b, 0, hp)),
        out_shape=jax.ShapeDtypeStruct((B, T, GLA_WIDTH), BF16),
        scratch_shapes=[
            pltpu.VMEM((2, T, LANES), BF16), pltpu.VMEM((2, T, LANES), BF16),
            pltpu.VMEM((2, 3, (T // CHUNK) * SUBLANES, LANES), F32),
            pltpu.VMEM((T, LANES), F32), pltpu.VMEM((T, LANES), F32),
        ],
        compiler_params=_params("arbitrary", "arbitrary"),
        name="gla",
    )(gla, gla, gla, gla, small, lp["wd_hi"], lp["wd_lo"], lp["bd"], lp["gla_gain"])


GATE_I = SMALL_GATE_LANE
GATE_F = SMALL_GATE_LANE + MLSTM_HEADS


def _gate_lane(d, hh):
    return GATE_F + 2 * MLSTM_HEADS * d + hh


MLSTM_CHUNKS_PER_STEP = 32


def _chunk_rows8(a, row):
    n = a.shape[0] // CHUNK
    a3 = a.reshape(n, CHUNK, LANES)
    full = jnp.broadcast_to(a3[:, row:row + 1, :], a3.shape)
    return full[:, :SUBLANES, :].reshape(n * SUBLANES, LANES)


def _expand_rows8(a8):
    n, w = a8.shape[0] // SUBLANES, a8.shape[1]
    a3 = a8.reshape(n, SUBLANES, w)[:, 0:1, :]
    return jnp.broadcast_to(a3, (n, CHUNK, w)).reshape(n * CHUNK, w)


def _gate_select():
    src = lax.broadcasted_iota(jnp.int32, (LANES, 2 * LANES), 0)
    dst = lax.broadcasted_iota(jnp.int32, (LANES, 2 * LANES), 1)
    want = GATE_F + 2 * MLSTM_HEADS * (dst // LANES) + (dst % LANES) // HEAD_DIM
    return jnp.where(src == want, 1.0, 0.0).astype(BF16)


def _gate_broadcast(x, sel, gate_mask, pieces=2):
    x = jnp.where(gate_mask, x, 0.0)
    if pieces == 1:
        return _dot(x.astype(BF16), sel)
    hi, lo = _split(x)
    return _dot(hi, sel) + _dot(lo, sel)


def _mlstm_kernel(q_ref, k_ref, v_ref, og_ref, small_ref, wq_ref, wk_ref, bq_ref, bk_ref,
                  gbias_ref, gain_ref, o_ref, q_s, k_s, b_s, r_s, cm_s, fl1_s, bl_s, rl_s, mf_s, mb_s,
                  wp_s, qi_s, kw_s, rt_s, st_s, of_s, ob_s):
    T = q_ref.shape[1]
    nc = T // CHUNK
    R = min(PREP_ROWS, T)
    hp = pl.program_id(1)
    G = min(MLSTM_CHUNKS_PER_STEP, nc)
    assert nc % G == 0
    cpt = R // CHUNK

    row = lax.broadcasted_iota(jnp.int32, (T, LANES), 0)
    for src, w_ref, b_ref, dst, scale in ((q_ref, wq_ref, bq_ref, q_s, 1.0),
                                          (k_ref, wk_ref, bk_ref, k_s, HEAD_DIM ** -0.5)):
        xc = src[0]
        prev = jnp.where(row >= 1, pltpu.roll(xc, 1, 0), 0.0)
        nxt = jnp.where(row < T - 1, pltpu.roll(xc, T - 1, 0), 0.0)
        y = prev * w_ref[0:1, :] + xc * w_ref[1:2, :] + nxt * w_ref[2:3, :] + b_ref[...]
        dst[...] = (y * _sigmoid(y) * scale).astype(BF16)

    pos = _chunk_pos(R)
    lane = lax.broadcasted_iota(jnp.int32, (R, LANES), 1)
    is_bwd = lane >= GATE_I + 2 * MLSTM_HEADS
    is_bwd8 = lax.broadcasted_iota(jnp.int32, (cpt * SUBLANES, LANES), 1) >= GATE_I + 2 * MLSTM_HEADS
    heads_per_pair = LANES // HEAD_DIM
    shift = (LANES - heads_per_pair * hp) % LANES
    gate_lanes = [_gate_lane(d, hh) for d in range(2) for hh in range(2)]
    gate_mask = functools.reduce(jnp.logical_or, [lane == l for l in gate_lanes])
    lane8 = lax.broadcasted_iota(jnp.int32, (cpt * SUBLANES, LANES), 1)
    gate_mask8 = functools.reduce(jnp.logical_or, [lane8 == l for l in gate_lanes])

    def prep(t, _):
        r0 = pl.multiple_of(t * R, R)
        gc = pltpu.roll(small_ref[0, pl.ds(r0, R), :] + gbias_ref[...], shift, 1)
        logf = _log_sigmoid(gc)
        pre = _chunk_scan(logf, pos, jnp.add, 0.0, False)
        b = jnp.where(is_bwd, _chunk_row(pre, 0, CHUNK - 1, CHUNK - 1) - pre + logf, pre)
        r = pltpu.roll(gc, MLSTM_HEADS, 1) - b
        cm = jnp.where(is_bwd, _chunk_scan(r, pos, jnp.maximum, NEG, True),
                       _chunk_scan(r, pos, jnp.maximum, NEG, False))
        b_s[pl.ds(r0, R), :] = b
        r_s[pl.ds(r0, R), :] = r
        cm_s[pl.ds(r0, R), :] = cm
        rt_s[:, pl.ds(r0, R)] = r.T
        c8 = pl.ds(pl.multiple_of(t * cpt * SUBLANES, SUBLANES), cpt * SUBLANES)
        bl_s[c8, :] = jnp.where(is_bwd8, _chunk_rows8(b, 0), _chunk_rows8(b, CHUNK - 1))
        rl_s[c8, :] = jnp.where(is_bwd8, _chunk_rows8(cm, 0), _chunk_rows8(cm, CHUNK - 1))
        return 0

    lax.fori_loop(0, T // R, prep, 0)

    def m_chain(n, carry):
        mf, mb = carry
        rf = pl.ds(pl.multiple_of(n * SUBLANES, SUBLANES), SUBLANES)
        rb = pl.ds(pl.multiple_of((nc - 1 - n) * SUBLANES, SUBLANES), SUBLANES)
        mf_s[rf, :] = mf
        mb_s[rb, :] = mb
        return (bl_s[rf, :] + jnp.maximum(mf, rl_s[rf, :]), bl_s[rb, :] + jnp.maximum(mb, rl_s[rb, :]))

    m0 = jnp.full((SUBLANES, LANES), NEG, F32)
    lax.fori_loop(0, nc, m_chain, (m0, m0))

    expo_s, floor_s = (b_s, cm_s), (r_s, fl1_s)
    sel = _gate_select()

    def weights(t, _):
        r0 = pl.multiple_of(t * R, R)
        c8 = pl.ds(pl.multiple_of(t * cpt * SUBLANES, SUBLANES), cpt * SUBLANES)
        rows = pl.ds(r0, R)
        m_in8 = jnp.where(is_bwd8, mb_s[c8, :], mf_s[c8, :])
        bl8 = bl_s[c8, :]
        m_out8 = bl8 + jnp.maximum(m_in8, rl_s[c8, :])
        m_in = _expand_rows8(m_in8)
        mx = jnp.maximum(m_in, cm_s[rows, :])
        wp_b = jnp.exp(_gate_broadcast(bl8 + m_in8 - m_out8, sel, gate_mask8))
        mx_b = _gate_broadcast(mx, sel, gate_mask)
        floor = jnp.exp(-_gate_broadcast(b_s[rows, :] + mx, sel, gate_mask))
        w_inter = _gate_broadcast(jnp.exp(m_in - mx), sel, gate_mask, pieces=1)
        wk = _gate_broadcast(jnp.exp(_expand_rows8(bl8 - m_out8) + r_s[rows, :]), sel, gate_mask, pieces=1)
        q = q_s[rows, :].astype(F32)
        k = k_s[rows, :].astype(F32)
        for d in range(2):
            half = slice(d * LANES, (d + 1) * LANES)
            qi_s[d, rows, :] = (q * w_inter[:, half]).astype(BF16)
            kw_s[d, rows, :] = (k * wk[:, half]).astype(BF16)
            wp_s[d, c8, :] = wp_b[:, half]
            expo_s[d][rows, :] = -mx_b[:, half]
            floor_s[d][rows, :] = floor[:, half]
        return 0

    lax.fori_loop(0, T // R, weights, 0)

    outs = (of_s, ob_s)
    ci = lax.broadcasted_iota(jnp.int32, (CHUNK, LANES), 0)
    si = lax.broadcasted_iota(jnp.int32, (CHUNK, LANES), 1) % CHUNK
    masks = (ci >= si, ci <= si)
    br = lax.broadcasted_iota(jnp.int32, (LANES, 2 * LANES), 0) // HEAD_DIM
    bc = (lax.broadcasted_iota(jnp.int32, (LANES, 2 * LANES), 1) % LANES) // HEAD_DIM
    blockdiag = br == bc
    ones = jnp.ones((CHUNK, LANES), BF16)
    st_s[...] = jnp.zeros(st_s.shape, F32)
    gate_rows = slice(GATE_I, GATE_I + 4 * MLSTM_HEADS)

    def step(it, _):
        units = [(d, g) for d in range(2) for g in range(G)]
        ops = {}
        for d in range(2):
            first = (nc - (it + 1) * G) if d else it * G
            rt = rt_s[gate_rows, pl.ds(pl.multiple_of(first * CHUNK, G * CHUNK), G * CHUNK)]
            for g in range(G):
                local = (G - 1 - g) if d else g
                c = first + local
                rows = pl.ds(pl.multiple_of(c * CHUNK, CHUNK), CHUNK)
                c8 = pl.ds(pl.multiple_of(c * SUBLANES, SUBLANES), SUBLANES)
                r_row = jnp.concatenate(
                    [rt[_gate_lane(d, hh) - GATE_I:_gate_lane(d, hh) - GATE_I + 1,
                        local * CHUNK:(local + 1) * CHUNK] for hh in range(2)], axis=1)
                wp = wp_s[d, c8, :][0:1, :]
                ops[d, g] = dict(
                    rows=rows, q=q_s[rows, :], k=k_s[rows, :], qi=qi_s[d, rows, :], kw=kw_s[d, rows, :],
                    va=jnp.concatenate([v_ref[0, rows, :].astype(BF16), ones], axis=1),
                    expo=expo_s[d][rows, :] + r_row, floor=floor_s[d][rows, :],
                    wp=jnp.concatenate([wp, wp], axis=1))
        qk, kv = {}, {}
        for u in units:
            o = ops[u]
            qk[u] = _dot_nt(o["q"], _stack_heads(o["k"]))
            kv[u] = _dot_tn(o["kw"], o["va"])
        st_in = {}
        for d in range(2):
            st = st_s[d]
            for g in range(G):
                st_in[d, g] = st.astype(BF16)
                st = st * ops[d, g]["wp"] + jnp.where(blockdiag, kv[d, g], 0.0)
            st_s[d] = st
        for u in units:
            d = u[0]
            o = ops[u]
            smat = (qk[u] * jnp.where(masks[d], jnp.exp(o["expo"]), 0.0)).astype(BF16)
            va_bd = jnp.where(blockdiag, jnp.concatenate([o["va"], o["va"]], axis=0), jnp.zeros((), BF16))
            num = _dot(jnp.concatenate([o["qi"], smat], axis=1), jnp.concatenate([st_in[u], va_bd], axis=0))
            den = jnp.maximum(jnp.abs(num[:, LANES:]), o["floor"])
            outs[d][o["rows"], :] = num[:, :LANES] / den
        return 0

    lax.fori_loop(0, nc // G, step, 0)

    def post(t, _):
        r0 = pl.multiple_of(t * R, R)
        h = _head_rms(of_s[pl.ds(r0, R), :] + ob_s[pl.ds(r0, R), :], gain_ref[...])
        o_ref[0, pl.ds(r0, R), :] = (h * _sigmoid(og_ref[0, pl.ds(r0, R), :])).astype(BF16)
        return 0

    lax.fori_loop(0, T // R, post, 0)


def _mlstm(ml, small, lp):
    B, T, _ = ml.shape
    pairs = MLSTM_WIDTH // LANES
    col = lambda off: (lambda b, hp: (b, 0, off * pairs + hp))
    const = lambda b, hp: (0, 0)
    return pl.pallas_call(
        _mlstm_kernel,
        grid=(B, pairs),
        in_specs=[
            pl.BlockSpec((1, T, LANES), col(0)),
            pl.BlockSpec((1, T, LANES), col(1)),
            pl.BlockSpec((1, T, LANES), col(2)),
            pl.BlockSpec((1, T, LANES), col(3)),
            pl.BlockSpec((1, T, LANES), lambda b, hp: (b, 0, 0)),
            pl.BlockSpec((3, LANES), lambda b, hp: (0, hp)),
            pl.BlockSpec((3, LANES), lambda b, hp: (0, pairs + hp)),
            pl.BlockSpec((1, LANES), lambda b, hp: (0, hp)),
            pl.BlockSpec((1, LANES), lambda b, hp: (0, pairs + hp)),
            pl.BlockSpec((1, LANES), const),
            pl.BlockSpec((1, LANES), const),
        ],
        out_specs=pl.BlockSpec((1, T, LANES), lambda b, hp: (b, 0, hp)),
        out_shape=jax.ShapeDtypeStruct((B, T, MLSTM_WIDTH), BF16),
        scratch_shapes=(
            [pltpu.VMEM((T, LANES), BF16)] * 2
            + [pltpu.VMEM((T, LANES), F32)] * 4
            + [pltpu.VMEM(((T // CHUNK) * SUBLANES, LANES), F32)] * 4
            + [pltpu.VMEM((2, (T // CHUNK) * SUBLANES, LANES), F32)]
            + [pltpu.VMEM((2, T, LANES), BF16)] * 2
            + [pltpu.VMEM((LANES, T), F32),
               pltpu.VMEM((2, LANES, 2 * LANES), F32),
               pltpu.VMEM((T, LANES), F32), pltpu.VMEM((T, LANES), F32)]),
        compiler_params=_params("arbitrary", "arbitrary"),
        name="mlstm",
    )(ml, ml, ml, ml, small, lp["conv_w"], lp["conv_w"], lp["conv_b"], lp["conv_b"],
      lp["gate_bias"], lp["ml_gain"])


def _first_argmax(vals, row):
    mx = jnp.max(vals, axis=0, keepdims=True)
    idx = jnp.min(jnp.where(vals == mx, row, vals.shape[0]), axis=0, keepdims=True)
    return mx, idx


def _out_route_kernel(x_ref, a_ref, gl_ref, ml_ref, w_ref, g_ref, wr_ref, br_ref,
                      x1_ref, t_ref, route_ref, routet_ref):
    tm = x_ref.shape[0]
    sub = min(tm, ROUTE_SUBTILE)
    spans = [slice(r, r + sub) for r in range(0, tm, sub)]
    x1s, logits = [], []
    for rs in spans:
        x1 = (x_ref[rs, :]
              + _dot(a_ref[rs, :], w_ref[0:ATTN_WIDTH, :])
              + _dot(gl_ref[rs, :], w_ref[ATTN_WIDTH:ATTN_WIDTH + GLA_WIDTH, :])
              + _dot(ml_ref[rs, :], w_ref[ATTN_WIDTH + GLA_WIDTH:, :]))
        x1_ref[rs, :] = x1
        x1s.append(x1)
    for rs, x1 in zip(spans, x1s):
        t_hi, t_lo = _split(_rms(x1, g_ref[...]))
        t_ref[rs, 0:t_hi.shape[1]] = t_hi
        both = _dot(t_hi, wr_ref[...])
        logits.append((both[:, :LANES] + both[:, LANES:] + _dot(t_lo, wr_ref[:, :LANES])) + br_ref[...])
    grow = lax.broadcasted_iota(jnp.int32, (SUBLANES, sub), 0)
    erow = lax.broadcasted_iota(jnp.int32, (N_EXPERTS, sub), 0)
    for rs, lg in zip(spans, logits):
        lt = lg.T
        gl = jnp.where(grow < N_GROUPS, lt[0:SUBLANES, :], -jnp.inf)
        gmax, gi = _first_argmax(gl, grow)
        g_prob = 1.0 / jnp.sum(jnp.exp(gl - gmax), axis=0, keepdims=True)
        el = jnp.where(erow // EXPERTS_PER_GROUP == gi, lt[ROUTE_W_LANE:ROUTE_W_LANE + N_EXPERTS, :], -jnp.inf)
        v1, i1 = _first_argmax(el, erow)
        v2, i2 = _first_argmax(jnp.where(erow == i1, -jnp.inf, el), erow)
        e2 = jnp.exp(v2 - v1)
        w1 = g_prob / (1.0 + e2)
        w2 = g_prob * e2 / (1.0 + e2)
        comb = jnp.where(erow == i1, w1, jnp.where(erow == i2, w2, 0.0))
        head = jnp.where(grow == 0, gi.astype(F32), 0.0)
        route_t = jnp.concatenate(
            [head, comb, jnp.zeros((LANES - SUBLANES - N_EXPERTS, sub), F32)], axis=0)
        route_ref[rs, :] = route_t.T
        routet_ref[:, rs] = head


def _dispatch_kernel(t_ref, route_ref, routet_ref, xs_ref, cws_ref, pos_ref, cnt_ref):
    nb = ROUTE_BLOCK
    gi_row = routet_ref[0:1, :]
    sub = lax.broadcasted_iota(jnp.int32, (SUBLANES, nb), 0).astype(F32)
    onehot = (sub == gi_row)
    ri = lax.broadcasted_iota(jnp.int32, (nb, nb), 0)
    cj = lax.broadcasted_iota(jnp.int32, (nb, nb), 1)
    before = (ri < cj).astype(BF16)
    rank = _dot(onehot.astype(BF16), before)
    counts = jnp.broadcast_to(jnp.sum(onehot.astype(F32), axis=-1, keepdims=True), (SUBLANES, LANES))
    padded = jnp.ceil(counts * (1.0 / ROUTE_TILE)) * ROUTE_TILE
    srow = lax.broadcasted_iota(jnp.int32, (SUBLANES, LANES), 0)
    incl = padded
    s = 1
    while s < SUBLANES:
        incl = incl + jnp.where(srow >= s, pltpu.roll(incl, s, 0), 0.0)
        s *= 2
    start = incl - padded
    pos_row = jnp.sum(jnp.where(onehot, start[:, 0:1] + rank, 0.0), axis=0, keepdims=True)
    cnt_ref[0] = counts.astype(jnp.int32)

    pos_hi = jnp.floor(pos_row * (1.0 / ROUTE_TILE))
    prow = lax.broadcasted_iota(jnp.int32, (2 * SUBLANES, nb), 0)
    pieces = jnp.where(prow == 0, pos_hi, jnp.where(prow == 1, pos_row - ROUTE_TILE * pos_hi, 0.0))
    wrow = lax.broadcasted_iota(jnp.int32, (2 * SUBLANES, LANES), 0)
    weights = jnp.where(wrow == 0, float(ROUTE_TILE), jnp.where(wrow == 1, 1.0, 0.0))
    pos_ref[...] = _dot_tn(pieces.astype(BF16), weights.astype(BF16))

    route = route_ref[...]
    lane = lax.broadcasted_iota(jnp.int32, (nb, LANES), 1)
    comb = jnp.where(lane >= ROUTE_W_LANE, route, 0.0)
    c_hi, c_lo = _split(comb)
    c_lo2 = (comb - c_hi.astype(F32) - c_lo.astype(F32)).astype(BF16)
    d_model = xs_ref.shape[1]
    t_ref[:, d_model:] = (c_hi.astype(F32) + pltpu.roll(c_lo.astype(F32), ROUTE_LO_SHIFT, 1)
                          + pltpu.roll(c_lo2.astype(F32), 2 * ROUTE_LO_SHIFT, 1)).astype(BF16)
    tb = t_ref[...]
    for r in range(TILES_PER_BLOCK):
        rows = (lax.broadcasted_iota(jnp.int32, (ROUTE_TILE, nb), 0) + r * ROUTE_TILE).astype(F32)
        perm = (rows == pos_row).astype(BF16)
        moved = _dot(perm, tb)
        xs_ref[r * ROUTE_TILE:(r + 1) * ROUTE_TILE, :] = moved[:, :d_model].astype(BF16)
        cws_ref[r * ROUTE_TILE:(r + 1) * ROUTE_TILE, :] = moved[:, d_model:]


def _route_dispatch_kernel(x_ref, a_ref, gl_ref, ml_ref, w_ref, g_ref, wr_ref, br_ref,
                           x1_ref, xs_ref, cws_ref, pos_ref, cnt_ref, t_s, route_s, routet_s):
    _out_route_kernel(x_ref, a_ref, gl_ref, ml_ref, w_ref, g_ref, wr_ref, br_ref,
                      x1_ref, t_s, route_s, routet_s)
    _dispatch_kernel(t_s, route_s, routet_s, xs_ref, cws_ref, pos_ref, cnt_ref)


def _route_dispatch(x2, attn, gla_o, ml_o, lp):
    N, D = x2.shape
    nblk = N // ROUTE_BLOCK
    const = lambda i: (0, 0)
    tok = lambda i: (i, 0)
    return pl.pallas_call(
        _route_dispatch_kernel,
        grid=(nblk,),
        in_specs=[
            pl.BlockSpec((ROUTE_BLOCK, D), tok),
            pl.BlockSpec((ROUTE_BLOCK, ATTN_WIDTH), tok),
            pl.BlockSpec((ROUTE_BLOCK, GLA_WIDTH), tok),
            pl.BlockSpec((ROUTE_BLOCK, MLSTM_WIDTH), tok),
            pl.BlockSpec((ATTN_WIDTH + GLA_WIDTH + MLSTM_WIDTH, D), const),
            pl.BlockSpec((1, D), const),
            pl.BlockSpec((D, 2 * LANES), const),
            pl.BlockSpec((1, LANES), const),
        ],
        out_specs=[
            pl.BlockSpec((ROUTE_BLOCK, D), tok),
            pl.BlockSpec((ROUTE_ROWS, D), tok),
            pl.BlockSpec((ROUTE_ROWS, LANES), tok),
            pl.BlockSpec((ROUTE_BLOCK, LANES), tok),
            pl.BlockSpec((1, SUBLANES, LANES), lambda i: (i, 0, 0)),
        ],
        out_shape=[
            jax.ShapeDtypeStruct((N, D), F32),
            jax.ShapeDtypeStruct((nblk * ROUTE_ROWS, D), BF16),
            jax.ShapeDtypeStruct((nblk * ROUTE_ROWS, LANES), F32),
            jax.ShapeDtypeStruct((N, LANES), F32),
            jax.ShapeDtypeStruct((nblk, SUBLANES, LANES), jnp.int32),
        ],
        scratch_shapes=[
            pltpu.VMEM((ROUTE_BLOCK, D + LANES), BF16),
            pltpu.VMEM((ROUTE_BLOCK, LANES), F32),
            pltpu.VMEM((SUBLANES, ROUTE_BLOCK), F32),
        ],
        compiler_params=_params("arbitrary"),
        name="route_dispatch",
    )(x2, attn, gla_o, ml_o, lp["w_out"], lp["g_ffn"], lp["w_route"], lp["b_route"])


def _tile_schedule(cnt):
    nblk = cnt.shape[0]
    ntile = (cnt + ROUTE_TILE - 1) // ROUTE_TILE
    end = jnp.cumsum(ntile, axis=1)
    r = jnp.arange(TILES_PER_BLOCK, dtype=jnp.int32)
    grp = jnp.sum(r[None, :, None] >= end[:, None, :], axis=-1)
    grp = grp.reshape(-1).astype(jnp.int32)
    tile = jnp.arange(nblk * TILES_PER_BLOCK, dtype=jnp.int32)
    order = jnp.argsort(grp * (nblk * TILES_PER_BLOCK) + tile).astype(jnp.int32)
    n_active = jnp.sum(grp < N_GROUPS).astype(jnp.int32)
    g_sorted = grp[order]
    last_group = g_sorted[jnp.maximum(n_active - 1, 0)]
    g_sorted = jnp.where(g_sorted < N_GROUPS, g_sorted, last_group)
    return order, g_sorted, n_active[None]


def _expert_kernel(trow_ref, tgrp_ref, nact_ref, xs_ref, cws_ref, wg_ref, wu_ref, wd_ref, ys_ref,
                   wg_s, wu_s, wd_s):
    i = pl.program_id(0)

    @pl.when((i == 0) | (tgrp_ref[i] != tgrp_ref[jnp.maximum(i - 1, 0)]))
    def _():
        for j in range(EXPERTS_PER_GROUP):
            wg_s[j] = wg_ref[j].astype(BF16)
            wu_s[j] = wu_ref[j].astype(BF16)
            wd_s[j] = wd_ref[j].astype(BF16)

    @pl.when(i < nact_ref[0])
    def _():
        x = xs_ref[...]
        cws = cws_ref[...]
        lane = lax.broadcasted_iota(jnp.int32, cws.shape, 1)
        first = ROUTE_W_LANE + tgrp_ref[i] * EXPERTS_PER_GROUP
        y = jnp.zeros(ys_ref.shape, F32)

        def hidden(j):
            return _dot(x, wg_s[j]), _dot(x, wu_s[j])

        h = hidden(0)
        for j in range(EXPERTS_PER_GROUP):
            h_next = hidden(j + 1) if j + 1 < EXPERTS_PER_GROUP else None
            off = lane - (first + j)
            sel = (off == 0) | (off == ROUTE_LO_SHIFT) | (off == 2 * ROUTE_LO_SHIFT)
            wj = jnp.sum(jnp.where(sel, cws, 0.0), axis=-1, keepdims=True)
            a = (h[0] * _sigmoid(h[0]) * h[1]).astype(BF16)
            y = y + wj * _dot(a, wd_s[j])
            h = h_next
        ys_ref[...] = y.astype(BF16)

    @pl.when(i >= nact_ref[0])
    def _():
        ys_ref[...] = jnp.zeros(ys_ref.shape, BF16)


def _experts(xs, cws, order, grp, n_active, lp):
    rows, D = xs.shape
    n_tiles = rows // ROUTE_TILE
    tile = lambda i, trow, tgrp, nact: (trow[i], 0)
    layer_groups = lp["layer"] * N_GROUPS
    wsel = lambda i, trow, tgrp, nact: (layer_groups + tgrp[i], 0, 0)
    once = pl.Buffered(1)
    return pl.pallas_call(
        _expert_kernel,
        grid_spec=pltpu.PrefetchScalarGridSpec(
            num_scalar_prefetch=3,
            grid=(n_tiles,),
            in_specs=[
                pl.BlockSpec((ROUTE_TILE, D), tile),
                pl.BlockSpec((ROUTE_TILE, LANES), tile),
                pl.BlockSpec((EXPERTS_PER_GROUP, D, D_EXPERT), wsel, pipeline_mode=once),
                pl.BlockSpec((EXPERTS_PER_GROUP, D, D_EXPERT), wsel, pipeline_mode=once),
                pl.BlockSpec((EXPERTS_PER_GROUP, D_EXPERT, D), wsel, pipeline_mode=once),
            ],
            out_specs=pl.BlockSpec((ROUTE_TILE, D), tile),
            scratch_shapes=[
                pltpu.VMEM((EXPERTS_PER_GROUP, D, D_EXPERT), BF16),
                pltpu.VMEM((EXPERTS_PER_GROUP, D, D_EXPERT), BF16),
                pltpu.VMEM((EXPERTS_PER_GROUP, D_EXPERT, D), BF16),
            ],
        ),
        out_shape=jax.ShapeDtypeStruct((rows, D), BF16),
        compiler_params=_params("arbitrary"),
        name="experts",
    )(order, grp, n_active, xs, cws, lp["w_gate"], lp["w_up"], lp["w_down"])


def _combine_kernel(used_ref, ys_ref, pos_ref, x1_ref, p_ref, g_ref, wpg_ref, wpp_ref, gfin_ref, o_ref,
                    *, final):
    tm = x1_ref.shape[0]
    pos = pos_ref[...]
    lane = lax.broadcasted_iota(jnp.int32, (tm, LANES), 1).astype(F32)

    def unsort(rows):
        perm_t = jnp.concatenate(
            [(lane + r * LANES == pos).astype(BF16) for r in range(rows // LANES)], axis=1)
        o_ref[...] = x1_ref[...] + _dot(perm_t, ys_ref[0:rows, :])

    short = used_ref[pl.program_id(0)] * ROUTE_TILE <= COMBINE_SHORT_ROWS
    pl.when(short)(lambda: unsort(COMBINE_SHORT_ROWS))
    pl.when(jnp.logical_not(short))(lambda: unsort(ROUTE_ROWS))
    x = o_ref[...]
    gate = _sigmoid(_dot(_rms(x, g_ref[...]).astype(BF16), wpg_ref[...]))
    x = x + gate * _dot(p_ref[...].astype(BF16), wpp_ref[...])
    if final:
        x = _rms(x, gfin_ref[...])
    o_ref[...] = x


def _combine(used, ys, pos, x1, p3, lp, g_final, final=False, tm=1024):
    N, D = x1.shape
    inner = ROUTE_BLOCK // tm
    layer = lp["layer"]
    tok = lambda b, i, u: (b * inner + i, 0)
    const = lambda b, i, u: (0, 0)
    return pl.pallas_call(
        functools.partial(_combine_kernel, final=final),
        grid_spec=pltpu.PrefetchScalarGridSpec(
            num_scalar_prefetch=1,
            grid=(N // ROUTE_BLOCK, inner),
            in_specs=[
                pl.BlockSpec((ROUTE_ROWS, D), lambda b, i, u: (b, 0)),
                pl.BlockSpec((tm, LANES), tok),
                pl.BlockSpec((tm, D), tok),
                pl.BlockSpec((None, tm, p3.shape[2]), lambda b, i, u: (layer, b * inner + i, 0)),
                pl.BlockSpec((1, D), const),
                pl.BlockSpec((D, D), const),
                pl.BlockSpec((p3.shape[2], D), const),
                pl.BlockSpec((1, D), const),
            ],
            out_specs=pl.BlockSpec((tm, D), tok),
        ),
        out_shape=jax.ShapeDtypeStruct((N, D), F32),
        compiler_params=_params("arbitrary", "arbitrary"),
        name="combine",
    )(used, ys, pos, x1, p3, lp["g_ple"], lp["w_pg"], lp["w_pp"], g_final)


def _mix_out_moe(x2, attn, gla_o, ml_o, lp):
    x1, xs, cws, pos, cnt = _route_dispatch(x2, attn, gla_o, ml_o, lp)
    counts = cnt[:, :N_GROUPS, 0]
    order, grp, n_active = _tile_schedule(counts)
    used = jnp.sum((counts + ROUTE_TILE - 1) // ROUTE_TILE, axis=1).astype(jnp.int32)
    return x1, _experts(xs, cws, order, grp, n_active, lp), pos, used


def _permute_in_cols(w):
    glr0 = ATTN_WIDTH + 2 * KV_WIDTH + 4 * GLA_WIDTH
    ml0 = glr0 + 2 * GLA_RANK
    mg0 = ml0 + 4 * MLSTM_WIDTH
    end = mg0 + 4 * MLSTM_HEADS
    assert end == w.shape[-1]
    pad = jnp.zeros(w.shape[:-1] + (IN_PERM_WIDTH - end,), BF16)
    parts = [w[..., :glr0], w[..., ml0:mg0], w[..., glr0:ml0], w[..., mg0:end]]
    return jnp.concatenate([part.astype(BF16) for part in parts] + [pad], axis=-1)


def _rope_tables(T):
    t = np.arange(T)
    inv = ROPE_THETA ** (-np.arange(0, ROPE_AXIS_DIM, 2, dtype=np.float64) / ROPE_AXIS_DIM)
    ang_r = (t // GRID_W)[None, :] * inv[:, None]
    ang_c = (t % GRID_W)[None, :] * inv[:, None]
    cos_h = np.concatenate([np.cos(ang_r), np.cos(ang_r), np.cos(ang_c), np.cos(ang_c)], axis=0)
    sin_h = np.concatenate([-np.sin(ang_r), np.sin(ang_r), -np.sin(ang_c), np.sin(ang_c)], axis=0)
    reps = LANES // HEAD_DIM
    return dict(cos_t=jnp.asarray(np.tile(cos_h, (reps, 1)), F32),
                sin_t=jnp.asarray(np.tile(sin_h, (reps, 1)), F32))


def kernel(x, p, norm_mix_g, w_in, attn_q_norm_g, attn_k_norm_g, gla_w_decay, gla_b_decay,
           gla_out_norm_g, mlstm_conv_w, mlstm_conv_b, mlstm_b_input, mlstm_b_forget,
           mlstm_out_norm_g, w_out, norm_ffn_g, w_group, b_group, w_router, b_router,
           w_expert_gate, w_expert_up, w_expert_down, norm_ple_g, w_ple_gate, w_ple_proj,
           final_norm_g):
    params = dict(
        norm_mix_g=norm_mix_g, w_in=w_in, attn_q_norm_g=attn_q_norm_g, attn_k_norm_g=attn_k_norm_g,
        gla_w_decay=gla_w_decay, gla_b_decay=gla_b_decay, gla_out_norm_g=gla_out_norm_g,
        mlstm_conv_w=mlstm_conv_w, mlstm_conv_b=mlstm_conv_b, mlstm_b_input=mlstm_b_input,
        mlstm_b_forget=mlstm_b_forget, mlstm_out_norm_g=mlstm_out_norm_g, w_out=w_out,
        norm_ffn_g=norm_ffn_g, w_group=w_group, b_group=b_group, w_router=w_router,
        b_router=b_router, w_expert_gate=w_expert_gate, w_expert_up=w_expert_up,
        w_expert_down=w_expert_down, norm_ple_g=norm_ple_g, w_ple_gate=w_ple_gate,
        w_ple_proj=w_ple_proj)
    B, T, D = x.shape
    rope = _rope_tables(T)
    N = B * T
    depth = w_in.shape[0]
    x2 = x.reshape(N, D)
    g_final = final_norm_g[None, :]
    p3 = p.reshape(depth, N, p.shape[-1])
    for i in range(depth):
        lp = _layer_params(params, i)
        qt, k, vt, gla, ml, small = _in_proj(x2, lp, rope, B, T)
        attn = _attention(lp["attn_safe"], qt, k, vt).reshape(N, ATTN_WIDTH)
        gla_o = _gla(gla, small, lp).reshape(N, GLA_WIDTH)
        ml_o = _mlstm(ml, small, lp).reshape(N, MLSTM_WIDTH)
        x1, ys, pos, used = _mix_out_moe(x2, attn, gla_o, ml_o, lp)
        x2 = _combine(used, ys, pos, x1, p3, lp, g_final, final=(i == depth - 1))
    return x2.reshape(B, T, D)


def _split_w(w):
    hi = w.astype(BF16)
    return hi, (w - hi.astype(F32)).astype(BF16)


def _stacked_experts(w):
    return w.reshape((w.shape[0] * w.shape[1],) + w.shape[2:])


def _layer_params(p, i):
    D = p["w_in"].shape[1]
    gq, gk = p["attn_q_norm_g"][i], p["attn_k_norm_g"][i]
    q_gain = jnp.tile(gq, LANES // HEAD_DIM) * (HEAD_DIM ** -0.5 * LOG2E)
    logit_bound = HEAD_DIM ** 0.5 * jnp.max(jnp.abs(gq)) * jnp.max(jnp.abs(gk))
    attn_safe = (logit_bound <= ATTN_SAFE_LOGIT).astype(jnp.int32)[None]
    wd = jnp.zeros((2, LANES, GLA_WIDTH), F32)
    wd = wd.at[0, :GLA_RANK].set(p["gla_w_decay"][i, 0]).at[1, GLA_RANK:2 * GLA_RANK].set(p["gla_w_decay"][i, 1])
    wd_hi, wd_lo = _split_w(wd)
    gate_bias = jnp.zeros((LANES,), F32).at[SMALL_GATE_LANE:SMALL_GATE_LANE + 4 * MLSTM_HEADS].set(
        jnp.concatenate([p["mlstm_b_input"][i, 0], p["mlstm_b_forget"][i, 0],
                         p["mlstm_b_input"][i, 1], p["mlstm_b_forget"][i, 1]]))
    w_route = jnp.zeros((D, LANES), F32)
    w_route = w_route.at[:, :N_GROUPS].set(p["w_group"][i])
    w_route = w_route.at[:, ROUTE_W_LANE:ROUTE_W_LANE + N_EXPERTS].set(p["w_router"][i])
    wr_hi, wr_lo = _split_w(w_route)
    b_route = jnp.zeros((LANES,), F32).at[:N_GROUPS].set(p["b_group"][i])
    b_route = b_route.at[ROUTE_W_LANE:ROUTE_W_LANE + N_EXPERTS].set(p["b_router"][i])
    return dict(
        g_mix=p["norm_mix_g"][i][None, :],
        w_in=_permute_in_cols(p["w_in"]),
        q_gain_t=jnp.broadcast_to(q_gain[:, None], (LANES, LANES)),
        k_gain_t=jnp.broadcast_to(jnp.tile(gk, LANES // HEAD_DIM)[:, None], (LANES, LANES)),
        attn_safe=attn_safe,
        wd_hi=wd_hi, wd_lo=wd_lo,
        bd=p["gla_b_decay"][i][:, None, :],
        gla_gain=jnp.tile(p["gla_out_norm_g"][i], LANES // HEAD_DIM)[None, :],
        conv_w=p["mlstm_conv_w"][i],
        conv_b=p["mlstm_conv_b"][i][None, :],
        gate_bias=gate_bias[None, :],
        ml_gain=jnp.tile(p["mlstm_out_norm_g"][i], LANES // HEAD_DIM)[None, :],
        w_out=p["w_out"][i].astype(BF16),
        g_ffn=p["norm_ffn_g"][i][None, :],
        w_route=jnp.concatenate([wr_hi, wr_lo], axis=1), b_route=b_route[None, :],
        layer=i,
        w_gate=_stacked_experts(p["w_expert_gate"]),
        w_up=_stacked_experts(p["w_expert_up"]),
        w_down=_stacked_experts(p["w_expert_down"]),
        g_ple=p["norm_ple_g"][i][None, :],
        w_pg=p["w_ple_gate"][i].astype(BF16),
        w_pp=p["w_ple_proj"][i].astype(BF16),
    )
```
